```python
import math
import jax, jax.numpy as jnp
from jax import lax
import numpy as np

D_MODEL = 1024
BATCH = 8
SEQ = 4096
DEPTH = 1

GRID_W = 64
CTX_LEN = 256
D_MIX = D_MODEL
D_RET = D_MIX // 2
D_HYENA = D_MIX - D_RET
RET_HEADS = 4
RET_HEAD_DIM = D_RET // RET_HEADS
RET_CHUNK = 128
ROPE_BASE = 10000.0
HYENA_PROJ = 3
HYENA_SHORT = 3
FILTER_EMB = 33
FILTER_HIDDEN = 64
FILTER_DECAY_FAST = 0.3
FILTER_DECAY_SLOW = 1.5
FILTER_DECAY_TARGET = 1e-2
D_IN = 4 * D_RET + HYENA_PROJ * D_HYENA
D_FF = ((-(-8 * D_MODEL // 3)) + 255) // 256 * 256
N_MOD = 6
EPS = 1e-6

kernel_name = "hybrid_retention_hyena_dit_block"

F32 = jnp.float32


def rms_norm(x, gain):
    xf = x.astype(F32)
    y = xf * lax.rsqrt(jnp.mean(xf * xf, axis=-1, keepdims=True) + EPS)
    return (y * gain.astype(F32)).astype(x.dtype)


def modulate(h, shift, scale):
    return h * (1 + scale) + shift


def to_heads(t):
    b, l, _ = t.shape
    return t.reshape(b, l, RET_HEADS, RET_HEAD_DIM).transpose(0, 2, 1, 3)


def flip_seq(t):
    return jnp.flip(t, axis=2)


def rope_tables(rows, cols):
    n = RET_HEAD_DIM // 4
    inv = ROPE_BASE ** (-jnp.arange(n, dtype=F32) / n)
    ang = jnp.concatenate([rows[:, None] * inv, cols[:, None] * inv], axis=-1)
    return jnp.cos(ang), jnp.sin(ang)


def apply_rope(t, cos, sin):
    half = RET_HEAD_DIM // 2
    t = t.astype(F32)
    t1, t2 = t[..., :half], t[..., half:]
    return jnp.concatenate([t1 * cos - t2 * sin, t1 * sin + t2 * cos], axis=-1)


def log_gamma_from(ret_decay):
    return jnp.log1p(-jnp.exp(ret_decay.astype(F32)))


def retention_chunked(q, k, v, log_gamma, s0):
    q, k, v = q.astype(F32), k.astype(F32), v.astype(F32)
    b, h, n, dh = q.shape
    nc = n // RET_CHUNK
    qc = q.reshape(b, h, nc, RET_CHUNK, dh)
    kc = k.reshape(b, h, nc, RET_CHUNK, dh)
    vc = v.reshape(b, h, nc, RET_CHUNK, dh)
    idx = jnp.arange(RET_CHUNK, dtype=F32)
    diff = idx[:, None] - idx[None, :]
    dmask = jnp.where(diff >= 0, jnp.exp(log_gamma[:, None, None] * jnp.maximum(diff, 0.0)), 0.0)
    scores = jnp.einsum('bhnid,bhnjd->bhnij', qc, kc) * dmask[None, :, None]
    inner = jnp.einsum('bhnij,bhnje->bhnie', scores, vc)
    k_w = jnp.exp(log_gamma[:, None] * (RET_CHUNK - 1 - idx))
    kv = jnp.einsum('bhnjd,bhnje,hj->nbhde', kc, vc, k_w)
    chunk_decay = jnp.exp(log_gamma * RET_CHUNK)[None, :, None, None]

    def step(s, kv_n):
        return s * chunk_decay + kv_n, s

    s_final, s_prev = lax.scan(step, s0.astype(F32), kv)
    q_w = jnp.exp(log_gamma[:, None] * (idx + 1))
    cross = jnp.einsum('bhnid,nbhde,hi->bhnie', qc, s_prev, q_w)
    return (inner + cross).reshape(b, h, n, dh), s_final


def retention_state(k, v, log_gamma):
    n = k.shape[2]
    w = jnp.exp(log_gamma[:, None] * (n - 1 - jnp.arange(n, dtype=F32)))
    return jnp.einsum('bhnd,bhne,hn->bhde', k.astype(F32), v.astype(F32), w)


def retention_out(o_fwd, o_bwd, g, gn_gain):
    o = o_fwd + o_bwd
    mu = jnp.mean(o, axis=-1, keepdims=True)
    var = jnp.mean(jnp.square(o - mu), axis=-1, keepdims=True)
    o = (o - mu) * lax.rsqrt(var + EPS)
    b, h, l, dh = o.shape
    o = o.transpose(0, 2, 1, 3).reshape(b, l, h * dh)
    return (o * gn_gain.astype(F32) * jax.nn.silu(g.astype(F32))).astype(g.dtype)


def short_conv(u, w, bias):
    pad = HYENA_SHORT // 2
    y = lax.conv_general_dilated(u, w.astype(u.dtype)[:, None, :], window_strides=(1,),
                                 padding=((pad, pad),), dimension_numbers=('NWC', 'WIO', 'NWC'),
                                 feature_group_count=u.shape[-1])
    return y + bias.astype(u.dtype)


def hyena_filter(seq_len, w1, b1, w2, b2, w3, b3, w4, freq):
    t = jnp.linspace(0.0, 1.0, seq_len, dtype=F32)[:, None]
    bands = (FILTER_EMB - 1) // 2
    f = jnp.linspace(1e-4, bands - 1, bands, dtype=F32)[None, :]
    wpos = 2.0 * math.pi * jnp.arange(seq_len, dtype=F32)[:, None] / seq_len
    emb = jnp.concatenate([t, jnp.cos(f * wpos), -jnp.sin(f * wpos)], axis=-1)
    fr = freq.astype(F32)
    h = jnp.sin(fr * (emb @ w1.astype(F32) + b1.astype(F32)))
    h = jnp.sin(fr * (h @ w2.astype(F32) + b2.astype(F32)))
    h = jnp.sin(fr * (h @ w3.astype(F32) + b3.astype(F32)))
    h = (h @ w4.astype(F32)).reshape(seq_len, 2, D_HYENA)
    max_decay = math.log(FILTER_DECAY_TARGET) / FILTER_DECAY_FAST
    min_decay = math.log(FILTER_DECAY_TARGET) / FILTER_DECAY_SLOW
    deltas = jnp.linspace(min_decay, max_decay, D_HYENA, dtype=F32)
    window = jnp.exp(-t * jnp.abs(deltas)[None, :])
    h = h * window[:, None, :]
    h_fwd, h_bwd = h[:, 0], h[:, 1]
    kern = jnp.concatenate([h_fwd, jnp.zeros((1, D_HYENA), F32), jnp.flip(h_bwd[1:], axis=0)], axis=0)
    kern = kern / (jnp.sum(jnp.abs(kern), axis=0, keepdims=True) + EPS)
    return jnp.fft.rfft(kern, axis=0)


def fft_long_conv(u, kern_f, bias):
    seq_len = u.shape[1]
    u_f = jnp.fft.rfft(u, n=2 * seq_len, axis=1)
    y = jnp.fft.irfft(u_f * kern_f[None], n=2 * seq_len, axis=1)[:, :seq_len]
    return y + u * bias.astype(F32)


def hyena_mix(hy, short_w, short_b, w1, b1, w2, b2, w3, b3, w4, freq, bias, out_gain):
    seq_len = hy.shape[1]
    u = short_conv(hy.astype(F32), short_w, short_b)
    x0, x1, v = u[..., :D_HYENA], u[..., D_HYENA:2 * D_HYENA], u[..., 2 * D_HYENA:]
    kern_f = hyena_filter(seq_len, w1, b1, w2, b2, w3, b3, w4, freq)
    y = x0 * fft_long_conv(v * x1, kern_f, bias)
    return rms_norm(y, out_gain).astype(hy.dtype)


def swiglu(h, w_gate_up, w_down):
    gu = h @ w_gate_up
    return (jax.nn.silu(gu[..., :D_FF]) * gu[..., D_FF:]) @ w_down


def setup_inputs(seed: int = 0) -> dict:
    key = jax.random.key(seed)
    ks = jax.random.split(key, 28)
    nrm = lambda k, shape, s: jax.random.normal(k, shape, F32) * s
    ln2 = math.log(2.0)
    decay_init = -(5.0 + jnp.arange(RET_HEADS, dtype=F32)) * ln2
    return {
        "x": nrm(ks[0], (BATCH, SEQ, D_MODEL), 1.0),
        "c": nrm(ks[1], (BATCH, D_MODEL), 1.0),
        "ctx": nrm(ks[2], (BATCH, CTX_LEN, D_MODEL), 1.0),
        "c_ctx": nrm(ks[3], (D_MODEL,), 1.0),
        "w_mod": nrm(ks[4], (DEPTH, D_MODEL, N_MOD * D_MODEL), 0.5 * D_MODEL ** -0.5),
        "b_mod": nrm(ks[5], (DEPTH, N_MOD * D_MODEL), 0.02),
        "norm1": 1.0 + nrm(ks[6], (DEPTH, D_MODEL), 0.02),
        "norm2": 1.0 + nrm(ks[7], (DEPTH, D_MODEL), 0.02),
        "w_in": nrm(ks[8], (DEPTH, D_MODEL, D_IN), D_MODEL ** -0.5),
        "ret_decay": decay_init[None, None, :] + nrm(ks[9], (DEPTH, 2, RET_HEADS), 0.05),
        "ret_gn_gain": 1.0 + nrm(ks[10], (DEPTH, D_RET), 0.02),
        "hy_short_w": nrm(ks[11], (DEPTH, HYENA_SHORT, HYENA_PROJ * D_HYENA), HYENA_SHORT ** -0.5),
        "hy_short_b": nrm(ks[12], (DEPTH, HYENA_PROJ * D_HYENA), 0.02),
        "hy_w1": nrm(ks[13], (DEPTH, FILTER_EMB, FILTER_HIDDEN), FILTER_EMB ** -0.5),
        "hy_b1": nrm(ks[14], (DEPTH, FILTER_HIDDEN), 0.1),
        "hy_w2": nrm(ks[15], (DEPTH, FILTER_HIDDEN, FILTER_HIDDEN), FILTER_HIDDEN ** -0.5),
        "hy_b2": nrm(ks[16], (DEPTH, FILTER_HIDDEN), 0.1),
        "hy_w3": nrm(ks[17], (DEPTH, FILTER_HIDDEN, FILTER_HIDDEN), FILTER_HIDDEN ** -0.5),
        "hy_b3": nrm(ks[18], (DEPTH, FILTER_HIDDEN), 0.1),
        "hy_w4": nrm(ks[19], (DEPTH, FILTER_HIDDEN, 2 * D_HYENA), FILTER_HIDDEN ** -0.5),
        "hy_freq": 1.0 + nrm(ks[20], (DEPTH, FILTER_HIDDEN), 0.02),
        "hy_bias": nrm(ks[21], (DEPTH, D_HYENA), 1.0),
        "hy_out_norm": 1.0 + nrm(ks[22], (DEPTH, D_HYENA), 0.02),
        "w_out": nrm(ks[23], (DEPTH, D_MIX, D_MODEL), D_MIX ** -0.5),
        "w_gate_up": nrm(ks[24], (DEPTH, D_MODEL, 2 * D_FF), D_MODEL ** -0.5),
        "w_down": nrm(ks[25], (DEPTH, D_FF, D_MODEL), D_FF ** -0.5),
        "final_norm": 1.0 + nrm(ks[26], (D_MODEL,), 0.02),
    }


def reference(x, c, ctx, c_ctx, w_mod, b_mod, norm1, norm2, w_in, ret_decay, ret_gn_gain,
              hy_short_w, hy_short_b, hy_w1, hy_b1, hy_w2, hy_b2, hy_w3, hy_b3, hy_w4,
              hy_freq, hy_bias, hy_out_norm, w_out, w_gate_up, w_down, final_norm):
    b, seq_len, _ = x.shape
    rows_n = seq_len // GRID_W
    rows = jnp.repeat(jnp.arange(rows_n, dtype=F32), GRID_W)
    cols = jnp.tile(jnp.arange(GRID_W, dtype=F32), rows_n)
    cos, sin = rope_tables(rows, cols)
    k_scale = RET_HEAD_DIM ** -0.5

    for layer in range(DEPTH):
        last = layer == DEPTH - 1
        mod = (jax.nn.silu(c) @ w_mod[layer] + b_mod[layer]).reshape(b, N_MOD, 1, D_MODEL)
        mod_c = (jax.nn.silu(c_ctx) @ w_mod[layer] + b_mod[layer]).reshape(N_MOD, D_MODEL)
        lg = log_gamma_from(ret_decay[layer])
        hy_p = (hy_short_w[layer], hy_short_b[layer], hy_w1[layer], hy_b1[layer], hy_w2[layer],
                hy_b2[layer], hy_w3[layer], hy_b3[layer], hy_w4[layer], hy_freq[layer],
                hy_bias[layer], hy_out_norm[layer])

        hc = modulate(rms_norm(ctx, norm1[layer]), mod_c[0], mod_c[1])
        if last:
            kvc = hc @ w_in[layer][:, D_RET:3 * D_RET]
            kc = to_heads(kvc[..., :D_RET]) * k_scale
            vc = to_heads(kvc[..., D_RET:])
            s_fwd = retention_state(kc, vc, lg[0])
            s_bwd = retention_state(flip_seq(kc), flip_seq(vc), lg[1])
        else:
            pc = hc @ w_in[layer]
            qc = to_heads(pc[..., :D_RET])
            kc = to_heads(pc[..., D_RET:2 * D_RET]) * k_scale
            vc = to_heads(pc[..., 2 * D_RET:3 * D_RET])
            zero = jnp.zeros((b, RET_HEADS, RET_HEAD_DIM, RET_HEAD_DIM), F32)
            oc_f, s_fwd = retention_chunked(qc, kc, vc, lg[0], zero)
            oc_b, s_bwd = retention_chunked(flip_seq(qc), flip_seq(kc), flip_seq(vc), lg[1], zero)
            ret_c = retention_out(oc_f, flip_seq(oc_b), pc[..., 3 * D_RET:4 * D_RET], ret_gn_gain[layer])
            hy_c = hyena_mix(pc[..., 4 * D_RET:], *hy_p)
            ctx_mix = jnp.concatenate([ret_c, hy_c], axis=-1) @ w_out[layer]

        h = modulate(rms_norm(x, norm1[layer]), mod[:, 0], mod[:, 1])
        p = h @ w_in[layer]
        q = apply_rope(to_heads(p[..., :D_RET]), cos, sin)
        k = apply_rope(to_heads(p[..., D_RET:2 * D_RET]), cos, sin) * k_scale
        v = to_heads(p[..., 2 * D_RET:3 * D_RET])
        o_f, _ = retention_chunked(q, k, v, lg[0], s_fwd)
        o_b, _ = retention_chunked(flip_seq(q), flip_seq(k), flip_seq(v), lg[1], s_bwd)
        ret_x = retention_out(o_f, flip_seq(o_b), p[..., 3 * D_RET:4 * D_RET], ret_gn_gain[layer])
        hy_x = hyena_mix(p[..., 4 * D_RET:], *hy_p)
        mix = jnp.concatenate([ret_x, hy_x], axis=-1) @ w_out[layer]
        x = x + (mod[:, 2] * mix).astype(x.dtype)

        h2 = modulate(rms_norm(x, norm2[layer]), mod[:, 3], mod[:, 4])
        x = x + (mod[:, 5] * swiglu(h2, w_gate_up[layer], w_down[layer])).astype(x.dtype)

        if not last:
            ctx = ctx + (mod_c[2] * ctx_mix).astype(ctx.dtype)
            hc2 = modulate(rms_norm(ctx, norm2[layer]), mod_c[3], mod_c[4])
            ctx = ctx + (mod_c[5] * swiglu(hc2, w_gate_up[layer], w_down[layer])).astype(ctx.dtype)

    return rms_norm(x, final_norm)
```

```python
import functools
import math

import jax
import jax.numpy as jnp
import numpy as np
from jax import lax
from jax.experimental import pallas as pl
from jax.experimental.pallas import tpu as pltpu

F32 = jnp.float32
BF16 = jnp.bfloat16
HIGHEST = lax.Precision.HIGHEST

RET_HEADS = 4
GRID_W = 64
ROPE_BASE = 10000.0
N_MOD = 6
HYENA_PROJ = 3
FILTER_DECAY_FAST = 0.3
FILTER_DECAY_SLOW = 1.5
FILTER_DECAY_TARGET = 1e-2
EPS = 1e-6

LANES = 128
SUBLANES = 8
VMEM_LIMIT_BYTES = 56 * 1024 * 1024

RET_CHUNK = 256
ROW_TILE = 512
HY_CT = 128
FILT_ROWS = 512
FFT_N1 = 64
K1P = 40


def _silu(v):
    return v / (1.0 + jnp.exp(-v))


def _cparams(sem, vmem=VMEM_LIMIT_BYTES):
    return pltpu.CompilerParams(dimension_semantics=sem, vmem_limit_bytes=vmem)


def _const_spec(shape):
    nd = len(shape)
    return pl.BlockSpec(shape, lambda *_: (0,) * nd, pipeline_mode=pl.Buffered(1))


def _mod_kernel(c_ref, w_ref, b_ref, o_ref):
    s = _silu(c_ref[...])
    o_ref[...] = jnp.dot(s, w_ref[...], precision=HIGHEST, preferred_element_type=F32) + b_ref[...]


def _mod_call(c_rows, w_mod, b_mod):
    rows, d = c_rows.shape
    n = w_mod.shape[1]
    tn = 1536
    return pl.pallas_call(
        _mod_kernel,
        grid=(n // tn,),
        in_specs=[pl.BlockSpec((rows, d), lambda j: (0, 0)),
                  pl.BlockSpec((d, tn), lambda j: (0, j)),
                  pl.BlockSpec((1, tn), lambda j: (0, j))],
        out_specs=pl.BlockSpec((rows, tn), lambda j: (0, j)),
        out_shape=jax.ShapeDtypeStruct((rows, n), F32),
        compiler_params=_cparams(("arbitrary",)),
        name="mod",
    )(c_rows, w_mod, b_mod)


def _ctx_kernel(ctx_ref, n1_ref, sh_ref, sc_ref, w_ref, dec_ref, s_ref, *, heads, dh):
    xc = ctx_ref[...]
    n_ctx = xc.shape[0]
    ms = jnp.mean(xc * xc, axis=-1, keepdims=True)
    hc = (xc * lax.rsqrt(ms + EPS) * n1_ref[...]) * (1.0 + sc_ref[...]) + sh_ref[...]
    kv = jnp.dot(hc.astype(BF16), w_ref[...], preferred_element_type=F32)
    lg = jnp.log1p(-jnp.exp(dec_ref[...]))
    pos = lax.broadcasted_iota(jnp.int32, (n_ctx, dh), 0).astype(F32)
    k_scale = dh ** -0.5
    d_ret = heads * dh
    tdims = (((0,), (0,)), ((), ()))
    for h in range(heads):
        kh = kv[:, h * dh:(h + 1) * dh] * k_scale
        vh = kv[:, d_ret + h * dh:d_ret + (h + 1) * dh].astype(BF16)
        wf = jnp.exp(lg[h:h + 1, :] * (n_ctx - 1.0 - pos))
        wb = jnp.exp(lg[heads + h:heads + h + 1, :] * pos)
        s_ref[h] = lax.dot_general((kh * wf).astype(BF16), vh, tdims, preferred_element_type=F32)
        s_ref[heads + h] = lax.dot_general((kh * wb).astype(BF16), vh, tdims, preferred_element_type=F32)


def _ctx_call(ctx, norm1, shift_c, scale_c, w_kv, dec, heads, dh):
    b, n_ctx, d = ctx.shape
    return pl.pallas_call(
        functools.partial(_ctx_kernel, heads=heads, dh=dh),
        grid=(b,),
        in_specs=[pl.BlockSpec((None, n_ctx, d), lambda i: (i, 0, 0)),
                  pl.BlockSpec((1, d), lambda i: (0, 0)),
                  pl.BlockSpec((1, d), lambda i: (0, 0)),
                  pl.BlockSpec((1, d), lambda i: (0, 0)),
                  pl.BlockSpec(w_kv.shape, lambda i: (0, 0)),
                  pl.BlockSpec(dec.shape, lambda i: (0, 0))],
        out_specs=pl.BlockSpec((None, 2 * heads, dh, dh), lambda i: (i, 0, 0, 0)),
        out_shape=jax.ShapeDtypeStruct((b, 2 * heads, dh, dh), F32),
        compiler_params=_cparams(("arbitrary",)),
        name="ctx_state",
    )(ctx, norm1, shift_c, scale_c, w_kv, dec)


def _inproj_kernel(x_ref, mod_ref, n1_ref, w_ref, o_ref, *, col_chunk):
    x = x_ref[...]
    ms = jnp.mean(x * x, axis=-1, keepdims=True)
    h = (x * lax.rsqrt(ms + EPS) * n1_ref[...]) * (1.0 + mod_ref[1:2, :]) + mod_ref[0:1, :]
    hb = h.astype(BF16)
    for j in range(w_ref.shape[1] // col_chunk):
        sl = slice(j * col_chunk, (j + 1) * col_chunk)
        o_ref[:, sl] = jnp.dot(hb, w_ref[:, sl], preferred_element_type=F32).astype(o_ref.dtype)


def _inproj_call(x, mod, norm1, w_in):
    b, l, d = x.shape
    d_in = w_in.shape[1]
    tm = ROW_TILE
    return pl.pallas_call(
        functools.partial(_inproj_kernel, col_chunk=512),
        grid=(b, l // tm),
        in_specs=[pl.BlockSpec((None, tm, d), lambda i, t: (i, t, 0)),
                  pl.BlockSpec((None, N_MOD, d), lambda i, t: (i, 0, 0)),
                  pl.BlockSpec((1, d), lambda i, t: (0, 0)),
                  _const_spec(w_in.shape)],
        out_specs=pl.BlockSpec((None, tm, d_in), lambda i, t: (i, t, 0)),
        out_shape=jax.ShapeDtypeStruct((b, l, d_in), BF16),
        compiler_params=_cparams(("arbitrary", "arbitrary")),
        name="in_proj",
    )(x, mod, norm1, w_in)


def _ret_kernel(q_ref, k_ref, v_ref, g_ref, cc_ref, ss_ref, sf0_ref, sb0_ref, dec_ref, gain_ref,
                o_ref, kr_s, kvf_s, kvb_s, sf_s, sb_s, *, heads, chunk):
    l, dh = q_ref.shape
    nc = l // chunk
    h = pl.program_id(1)
    lgf = jnp.log1p(-jnp.exp(dec_ref[pl.ds(h, 1), :]))
    lgb = jnp.log1p(-jnp.exp(dec_ref[pl.ds(h + heads, 1), :]))
    il = lax.broadcasted_iota(jnp.int32, (chunk, dh), 0).astype(F32)
    kw_f = jnp.exp(lgf * (chunk - 1.0 - il))
    kw_b = jnp.exp(lgb * il)
    qw_f = jnp.exp(lgf * (il + 1.0))
    qw_b = jnp.exp(lgb * (chunk - il))
    cd_f = jnp.exp(lgf * float(chunk))
    cd_b = jnp.exp(lgb * float(chunk))
    reps = chunk // dh
    lgf_c = jnp.concatenate([lgf] * reps, axis=1)
    lgb_c = jnp.concatenate([lgb] * reps, axis=1)
    ii = lax.broadcasted_iota(jnp.int32, (chunk, chunk), 0)
    jj = lax.broadcasted_iota(jnp.int32, (chunk, chunk), 1)
    diff = (ii - jj).astype(F32)
    dmask = (jnp.where(diff >= 0, jnp.exp(lgf_c * jnp.maximum(diff, 0.0)), 0.0)
             + jnp.where(diff <= 0, jnp.exp(lgb_c * jnp.maximum(-diff, 0.0)), 0.0))
    k_scale = dh ** -0.5
    tdims = (((0,), (0,)), ((), ()))
    ntdims = (((1,), (1,)), ((), ()))

    def rope(t, r0):
        return t * cc_ref[pl.ds(r0, chunk), :] + pltpu.roll(t, dh // 2, 1) * ss_ref[pl.ds(r0, chunk), :]

    def chunk_states(n, carry):
        r0 = pl.multiple_of(n * chunk, chunk)
        kr = rope(k_ref[pl.ds(r0, chunk), :].astype(F32), r0) * k_scale
        kr_s[pl.ds(r0, chunk), :] = kr.astype(BF16)
        vv = v_ref[pl.ds(r0, chunk), :]
        kvf_s[n] = lax.dot_general((kr * kw_f).astype(BF16), vv, tdims, preferred_element_type=F32)
        kvb_s[n] = lax.dot_general((kr * kw_b).astype(BF16), vv, tdims, preferred_element_type=F32)
        return carry

    lax.fori_loop(0, nc, chunk_states, 0)

    def scan_f(n, s):
        sf_s[n] = s
        return s * cd_f + kvf_s[n]

    lax.fori_loop(0, nc, scan_f, sf0_ref[...])

    def scan_b(m, s):
        n = nc - 1 - m
        sb_s[n] = s
        return s * cd_b + kvb_s[n]

    lax.fori_loop(0, nc, scan_b, sb0_ref[...])

    gain = gain_ref[...]

    def chunk_out(n, carry):
        r0 = pl.multiple_of(n * chunk, chunk)
        qr = rope(q_ref[pl.ds(r0, chunk), :].astype(F32), r0)
        sc = lax.dot_general(qr.astype(BF16), kr_s[pl.ds(r0, chunk), :], ntdims,
                             preferred_element_type=F32)
        o = jnp.dot((sc * dmask).astype(BF16), v_ref[pl.ds(r0, chunk), :], preferred_element_type=F32)
        o = o + jnp.dot((qr * qw_f).astype(BF16), sf_s[n].astype(BF16), preferred_element_type=F32)
        o = o + jnp.dot((qr * qw_b).astype(BF16), sb_s[n].astype(BF16), preferred_element_type=F32)
        mu = jnp.mean(o, axis=-1, keepdims=True)
        d = o - mu
        var = jnp.mean(d * d, axis=-1, keepdims=True)
        gg = g_ref[pl.ds(r0, chunk), :].astype(F32)
        o_ref[pl.ds(r0, chunk), :] = (d * lax.rsqrt(var + EPS) * gain * _silu(gg)).astype(o_ref.dtype)
        return carry

    lax.fori_loop(0, nc, chunk_out, 0)


def _ret_call(p, cc, ss, s0, dec, gn_gain, heads, dh):
    b, l, _ = p.shape
    chunk = RET_CHUNK
    nc = l // chunk
    seq = lambda off: pl.BlockSpec((None, l, dh), lambda i, h: (i, 0, off + h))
    return pl.pallas_call(
        functools.partial(_ret_kernel, heads=heads, chunk=chunk),
        grid=(b, heads),
        in_specs=[seq(0), seq(heads), seq(2 * heads), seq(3 * heads),
                  _const_spec(cc.shape), _const_spec(ss.shape),
                  pl.BlockSpec((None, None, dh, dh), lambda i, h: (i, h, 0, 0)),
                  pl.BlockSpec((None, None, dh, dh), lambda i, h: (i, heads + h, 0, 0)),
                  pl.BlockSpec(dec.shape, lambda i, h: (0, 0)),
                  pl.BlockSpec((1, dh), lambda i, h: (0, h))],
        out_specs=pl.BlockSpec((None, l, dh), lambda i, h: (i, 0, h)),
        out_shape=jax.ShapeDtypeStruct((b, l, heads * dh), BF16),
        scratch_shapes=[pltpu.VMEM((l, dh), BF16),
                        pltpu.VMEM((nc, dh, dh), F32), pltpu.VMEM((nc, dh, dh), F32),
                        pltpu.VMEM((nc, dh, dh), F32), pltpu.VMEM((nc, dh, dh), F32)],
        compiler_params=_cparams(("arbitrary", "arbitrary")),
        name="retention",
    )(p, p, p, p, cc, ss, s0, s0, dec, gn_gain)


def _filt_time_kernel(emb_ref, embr_ref, w1_ref, b1_ref, w2_ref, b2_ref, w3_ref, b3_ref, w4_ref,
                      fr_ref, dl_ref, kern_ref, s_ref, *, seq_len):
    i = pl.program_id(0)
    rows, c = kern_ref.shape[1], kern_ref.shape[2]
    fr = fr_ref[...]
    hdot = functools.partial(jnp.dot, precision=HIGHEST, preferred_element_type=F32)

    def mlp(e):
        z = jnp.sin(fr * (hdot(e, w1_ref[...]) + b1_ref[...]))
        z = jnp.sin(fr * (hdot(z, w2_ref[...]) + b2_ref[...]))
        return jnp.sin(fr * (hdot(z, w3_ref[...]) + b3_ref[...]))

    pos = (i * rows + lax.broadcasted_iota(jnp.int32, (rows, c), 0)).astype(F32)
    inv = 1.0 / (seq_len - 1.0)
    adl = dl_ref[...]
    hf = hdot(mlp(emb_ref[...]), w4_ref[:, :c]) * jnp.exp(-(pos * inv) * adl)
    hr = hdot(mlp(embr_ref[...]), w4_ref[:, c:]) * jnp.exp(-((seq_len - pos) * inv) * adl)
    hr = jnp.where(pos == 0.0, 0.0, hr)
    kern_ref[0] = hf
    kern_ref[1] = hr
    part = jnp.sum(jnp.abs(hf) + jnp.abs(hr), axis=0, keepdims=True)

    @pl.when(i == 0)
    def _():
        s_ref[...] = part

    @pl.when(i != 0)
    def _():
        s_ref[...] += part


def _filt_time_call(emb, emb_rev, w1, b1, w2, b2, w3, b3, w4, freq, absdelta, seq_len):
    c = absdelta.shape[1]
    rows = FILT_ROWS
    small = lambda a: pl.BlockSpec(a.shape, lambda i: (0,) * a.ndim)
    return pl.pallas_call(
        functools.partial(_filt_time_kernel, seq_len=seq_len),
        grid=(seq_len // rows,),
        in_specs=[pl.BlockSpec((rows, emb.shape[1]), lambda i: (i, 0)),
                  pl.BlockSpec((rows, emb.shape[1]), lambda i: (i, 0)),
                  small(w1), small(b1), small(w2), small(b2), small(w3), small(b3), small(w4),
                  small(freq), small(absdelta)],
        out_specs=[pl.BlockSpec((2, rows, c), lambda i: (0, i, 0)),
                   pl.BlockSpec((1, c), lambda i: (0, 0))],
        out_shape=[jax.ShapeDtypeStruct((2, seq_len, c), F32),
                   jax.ShapeDtypeStruct((1, c), F32)],
        compiler_params=_cparams(("arbitrary",)),
        name="filt_time",
    )(emb, emb_rev, w1, b1, w2, b2, w3, b3, w4, freq, absdelta)


@functools.lru_cache(maxsize=None)
def _fft_tables(seq_len):
    n = 2 * seq_len
    n1s = FFT_N1
    n2s = n // n1s
    half = n1s // 2
    k1n = half + 1
    n1 = np.arange(half)
    k1 = np.arange(k1n)
    th = 2.0 * np.pi * (np.outer(k1, n1) % n1s) / n1s
    fa = np.zeros((2 * K1P, half))
    fa[:k1n] = np.cos(th)
    fa[K1P:K1P + k1n] = -np.sin(th)
    sgn = np.ones(2 * K1P)
    sgn[:k1n] = (-1.0) ** k1
    sgn[K1P:K1P + k1n] = (-1.0) ** k1
    herm = np.where((k1 == 0) | (k1 == half), 1.0, 2.0)
    fai = np.zeros((half, 2 * K1P))
    fai[:, :k1n] = (np.cos(th) * herm[:, None]).T / n
    fai[:, K1P:K1P + k1n] = (-np.sin(th) * herm[:, None]).T / n
    eye = np.eye(SUBLANES)
    fk = np.kron(fa, eye)
    fks = np.kron(fa * sgn[:, None], eye)
    fki = np.kron(fai, eye)
    k2 = np.arange(n2s)
    n2 = np.arange(n2s)
    m = np.zeros((k1n, 2 * n2s, 2 * n2s))
    for a in range(k1n):
        ang = 2.0 * np.pi * (np.outer(a + n1s * k2, n2) % n) / n
        gr, gi = np.cos(ang), -np.sin(ang)
        m[a] = np.block([[gr, -gi], [gi, gr]])
    mt = np.transpose(m, (0, 2, 1))
    return dict(fk=fk, fkf=np.concatenate([fk, fks], axis=1), fki=fki, m=m, mt=mt,
                k1n=k1n, n2s=n2s, half=half)


def _stage_n1(src_refs, mat, dst_ref, n2s):
    rows = dst_ref.shape[0]
    for j in range(n2s // SUBLANES):
        sl = slice(j * SUBLANES, (j + 1) * SUBLANES)
        parts = [r[:, sl, :] for r in src_refs]
        xg = jnp.concatenate([p.reshape(p.shape[0] * SUBLANES, p.shape[2]) for p in parts], axis=0)
        a = jnp.dot(mat, xg.astype(BF16), preferred_element_type=F32)
        dst_ref[:, sl, :] = a.reshape(rows, SUBLANES, a.shape[1])


def _filt_spec_kernel(kern_ref, s_ref, fkf_ref, m_ref, kf_ref, a_s, *, k1n, n2s):
    _stage_n1([kern_ref.at[0], kern_ref.at[1]], fkf_ref[...], a_s, n2s)
    inv = 1.0 / (s_ref[...] + EPS)

    def body(k1, carry):
        a = jnp.concatenate([a_s[k1], a_s[K1P + k1]], axis=0)
        kf_ref[k1] = jnp.dot(m_ref[k1], a.astype(BF16), preferred_element_type=F32) * inv
        return carry

    lax.fori_loop(0, k1n, body, 0)


def _filt_spec_call(kern4, s, tabs):
    _, half, n2s, c = kern4.shape
    ct = HY_CT
    k1n = tabs["k1n"]
    return pl.pallas_call(
        functools.partial(_filt_spec_kernel, k1n=k1n, n2s=n2s),
        grid=(c // ct,),
        in_specs=[pl.BlockSpec((2, half, n2s, ct), lambda j: (0, 0, 0, j)),
                  pl.BlockSpec((1, ct), lambda j: (0, j)),
                  _const_spec(tabs["fkf"].shape), _const_spec(tabs["m"].shape)],
        out_specs=pl.BlockSpec((k1n, 2 * n2s, ct), lambda j: (0, 0, j)),
        out_shape=jax.ShapeDtypeStruct((k1n, 2 * n2s, c), F32),
        scratch_shapes=[pltpu.VMEM((2 * K1P, n2s, ct), F32)],
        compiler_params=_cparams(("arbitrary",)),
        name="filt_spec",
    )(kern4, s, tabs["fkf"], tabs["m"])


def _hyena_kernel(x0_ref, x1_ref, v_ref, sw0_ref, sw1_ref, sw2_ref, sb0_ref, sb1_ref, sb2_ref,
                  bias_ref, kf_ref, fk_ref, fki_ref, m_ref, mt_ref, o_ref, w_s, x0_s, a_s,
                  *, k1n, n2s):
    half = w_s.shape[0]
    ct = w_s.shape[2]
    row = lax.broadcasted_iota(jnp.int32, (n2s, ct), 0)

    def short_conv(u_ref, w_ref, b_ref, c):
        u = u_ref[c].astype(F32)
        if c > 0:
            prev = u_ref[c - 1, n2s - 1:n2s, :].astype(F32)
        else:
            prev = jnp.zeros((1, ct), F32)
        if c < half - 1:
            nxt = u_ref[c + 1, 0:1, :].astype(F32)
        else:
            nxt = jnp.zeros((1, ct), F32)
        up = jnp.where(row == 0, prev, pltpu.roll(u, 1, 0))
        dn = jnp.where(row == n2s - 1, nxt, pltpu.roll(u, n2s - 1, 0))
        return w_ref[0:1, :] * up + w_ref[1:2, :] * u + w_ref[2:3, :] * dn + b_ref[...]

    for c in range(half):
        x0_s[c] = short_conv(x0_ref, sw0_ref, sb0_ref, c)
        w_s[c] = short_conv(x1_ref, sw1_ref, sb1_ref, c) * short_conv(v_ref, sw2_ref, sb2_ref, c)

    _stage_n1([w_s], fk_ref[...], a_s, n2s)

    def body(k1, carry):
        a = jnp.concatenate([a_s[k1], a_s[K1P + k1]], axis=0)
        x = jnp.dot(m_ref[k1], a.astype(BF16), preferred_element_type=F32)
        xr, xi = x[:n2s], x[n2s:]
        kr, ki = kf_ref[k1, :n2s, :], kf_ref[k1, n2s:, :]
        y = jnp.concatenate([xr * kr - xi * ki, xr * ki + xi * kr], axis=0)
        z = jnp.dot(mt_ref[k1], y.astype(BF16), preferred_element_type=F32)
        a_s[k1] = z[:n2s]
        a_s[K1P + k1] = z[n2s:]
        return carry

    lax.fori_loop(0, k1n, body, 0)

    fki = fki_ref[...]
    bias = bias_ref[...]
    for j in range(n2s // SUBLANES):
        sl = slice(j * SUBLANES, (j + 1) * SUBLANES)
        zg = a_s[:, sl, :].reshape(2 * K1P * SUBLANES, ct)
        y = jnp.dot(fki, zg.astype(BF16), preferred_element_type=F32).reshape(half, SUBLANES, ct)
        o_ref[:, sl, :] = (x0_s[:, sl, :] * (y + w_s[:, sl, :] * bias)).astype(o_ref.dtype)


def _hyena_call(p5, short_w, short_b, bias, kf, tabs, d_ret4, c):
    b, half, n2s, _ = p5.shape
    ct = HY_CT
    k1n = tabs["k1n"]
    nct = c // ct
    base = d_ret4 // ct
    seq = lambda proj: pl.BlockSpec((None, half, n2s, ct), lambda j, i: (i, 0, 0, base + proj * nct + j))
    sw = lambda proj: pl.BlockSpec((short_w.shape[0], ct), lambda j, i: (0, proj * nct + j))
    sb = lambda proj: pl.BlockSpec((1, ct), lambda j, i: (0, proj * nct + j))
    return pl.pallas_call(
        functools.partial(_hyena_kernel, k1n=k1n, n2s=n2s),
        grid=(nct, b),
        in_specs=[seq(0), seq(1), seq(2), sw(0), sw(1), sw(2), sb(0), sb(1), sb(2),
                  pl.BlockSpec((1, ct), lambda j, i: (0, j)),
                  pl.BlockSpec((k1n, 2 * n2s, ct), lambda j, i: (0, 0, j)),
                  _const_spec(tabs["fk"].shape), _const_spec(tabs["fki"].shape),
                  _const_spec(tabs["m"].shape), _const_spec(tabs["mt"].shape)],
        out_specs=pl.BlockSpec((None, half, n2s, ct), lambda j, i: (i, 0, 0, j)),
        out_shape=jax.ShapeDtypeStruct((b, half, n2s, c), BF16),
        scratch_shapes=[pltpu.VMEM((half, n2s, ct), F32), pltpu.VMEM((half, n2s, ct), F32),
                        pltpu.VMEM((2 * K1P, n2s, ct), F32)],
        compiler_params=_cparams(("arbitrary", "arbitrary")),
        name="hyena",
    )(p5, p5, p5, short_w, short_w, short_w, short_b, short_b, short_b, bias, kf,
      tabs["fk"], tabs["fki"], tabs["m"], tabs["mt"])


def _out_ffn_kernel(ret_ref, hy_ref, x_ref, mod_ref, hg_ref, n2_ref, fn_ref, wo_ref, wgu_ref, wd_ref,
                    o_ref, *, d_ret, d_ff, ff_chunk):
    hy = hy_ref[...].astype(F32)
    hms = jnp.mean(hy * hy, axis=-1, keepdims=True)
    hyn = (hy * lax.rsqrt(hms + EPS) * hg_ref[...]).astype(BF16)
    mix = jnp.dot(ret_ref[...], wo_ref[:d_ret, :], preferred_element_type=F32)
    mix = mix + jnp.dot(hyn, wo_ref[d_ret:, :], preferred_element_type=F32)
    x1 = x_ref[...] + mod_ref[2:3, :] * mix
    ms = jnp.mean(x1 * x1, axis=-1, keepdims=True)
    h2 = ((x1 * lax.rsqrt(ms + EPS) * n2_ref[...]) * (1.0 + mod_ref[4:5, :]) + mod_ref[3:4, :]).astype(BF16)
    acc = jnp.zeros(x1.shape, F32)
    for j in range(d_ff // ff_chunk):
        g = jnp.dot(h2, wgu_ref[:, j * ff_chunk:(j + 1) * ff_chunk], preferred_element_type=F32)
        u = jnp.dot(h2, wgu_ref[:, d_ff + j * ff_chunk:d_ff + (j + 1) * ff_chunk], preferred_element_type=F32)
        a = (_silu(g) * u).astype(BF16)
        acc = acc + jnp.dot(a, wd_ref[j * ff_chunk:(j + 1) * ff_chunk, :], preferred_element_type=F32)
    x2 = x1 + mod_ref[5:6, :] * acc
    ms2 = jnp.mean(x2 * x2, axis=-1, keepdims=True)
    o_ref[...] = x2 * lax.rsqrt(ms2 + EPS) * fn_ref[...]


def _out_ffn_call(ret, hy, x, mod, hy_gain, norm2, final_norm, w_out, w_gu, w_down):
    b, l, d = x.shape
    d_ret = ret.shape[2]
    d_hy = hy.shape[2]
    d_ff = w_down.shape[0]
    tm = ROW_TILE
    return pl.pallas_call(
        functools.partial(_out_ffn_kernel, d_ret=d_ret, d_ff=d_ff, ff_chunk=d_ff // 2),
        grid=(b, l // tm),
        in_specs=[pl.BlockSpec((None, tm, d_ret), lambda i, t: (i, t, 0)),
                  pl.BlockSpec((None, tm, d_hy), lambda i, t: (i, t, 0)),
                  pl.BlockSpec((None, tm, d), lambda i, t: (i, t, 0)),
                  pl.BlockSpec((None, N_MOD, d), lambda i, t: (i, 0, 0)),
                  pl.BlockSpec((1, d_hy), lambda i, t: (0, 0)),
                  pl.BlockSpec((1, d), lambda i, t: (0, 0)),
                  pl.BlockSpec((1, d), lambda i, t: (0, 0)),
                  _const_spec(w_out.shape), _const_spec(w_gu.shape), _const_spec(w_down.shape)],
        out_specs=pl.BlockSpec((None, tm, d), lambda i, t: (i, t, 0)),
        out_shape=jax.ShapeDtypeStruct((b, l, d), x.dtype),
        compiler_params=_cparams(("arbitrary", "arbitrary")),
        name="out_ffn",
    )(ret, hy, x, mod, hy_gain, norm2, final_norm, w_out, w_gu, w_down)


@functools.lru_cache(maxsize=None)
def _rope_tables(seq_len, dh):
    n = dh // 4
    t = np.arange(seq_len)
    inv = ROPE_BASE ** (-np.arange(n, dtype=np.float64) / n)
    ang = np.concatenate([(t // GRID_W)[:, None] * inv, (t % GRID_W)[:, None] * inv], axis=-1)
    cc = np.concatenate([np.cos(ang), np.cos(ang)], axis=-1)
    ss = np.concatenate([-np.sin(ang), np.sin(ang)], axis=-1)
    return cc.astype(np.float32), ss.astype(np.float32)


@functools.lru_cache(maxsize=None)
def _filter_tables(seq_len, emb_dim, emb_pad, channels):
    t = np.linspace(0.0, 1.0, seq_len)[:, None]
    bands = (emb_dim - 1) // 2
    f = np.linspace(1e-4, bands - 1, bands)[None, :]
    wpos = 2.0 * np.pi * np.arange(seq_len)[:, None] / seq_len
    emb = np.concatenate([t, np.cos(f * wpos), -np.sin(f * wpos)], axis=-1)
    emb = np.pad(emb, ((0, 0), (0, emb_pad - emb_dim)))
    emb_rev = np.concatenate([emb[:1], emb[:0:-1]], axis=0)
    max_decay = math.log(FILTER_DECAY_TARGET) / FILTER_DECAY_FAST
    min_decay = math.log(FILTER_DECAY_TARGET) / FILTER_DECAY_SLOW
    absdelta = np.abs(np.linspace(min_decay, max_decay, channels))[None, :]
    return emb.astype(np.float32), emb_rev.astype(np.float32), absdelta.astype(np.float32)


def kernel(x, c, ctx, c_ctx, w_mod, b_mod, norm1, norm2, w_in, ret_decay, ret_gn_gain, hy_short_w,
           hy_short_b, hy_w1, hy_b1, hy_w2, hy_b2, hy_w3, hy_b3, hy_w4, hy_freq, hy_bias, hy_out_norm,
           w_out, w_gate_up, w_down, final_norm):
    b, seq_len, d = x.shape
    assert w_mod.shape[0] == 1, "single-layer block"
    heads = RET_HEADS
    d_ret = ret_gn_gain.shape[1]
    dh = d_ret // heads
    d_hy = hy_bias.shape[1]
    assert dh == LANES and d_hy % HY_CT == 0 and seq_len % (FFT_N1 // 2) == 0

    rows = -(-(b + 1) // SUBLANES) * SUBLANES
    c_rows = jnp.zeros((rows, d), F32).at[:b].set(c).at[b].set(c_ctx)
    mod_all = _mod_call(c_rows, w_mod[0], b_mod[0][None, :])
    mod = mod_all[:b].reshape(b, N_MOD, d)
    mod_c = mod_all[b].reshape(N_MOD, d)

    w_in_b = w_in[0].astype(BF16)
    dec = jnp.broadcast_to(ret_decay[0].reshape(2 * heads, 1), (2 * heads, LANES))
    n1g = norm1[0][None, :]

    s0 = _ctx_call(ctx, n1g, mod_c[0:1], mod_c[1:2], w_in_b[:, d_ret:3 * d_ret], dec, heads, dh)
    p = _inproj_call(x, mod, n1g, w_in_b)

    cc, ss = (jnp.asarray(t) for t in _rope_tables(seq_len, dh))
    ret = _ret_call(p, cc, ss, s0, dec, ret_gn_gain, heads, dh)

    tabs = dict(_fft_tables(seq_len))
    for name in ("fk", "fkf", "fki", "m", "mt"):
        tabs[name] = jnp.asarray(tabs[name], dtype=BF16)
    half, n2s = tabs["half"], tabs["n2s"]
    emb_dim = hy_w1.shape[1]
    emb_pad = -(-emb_dim // SUBLANES) * SUBLANES
    emb, emb_rev, absdelta = (jnp.asarray(t) for t in _filter_tables(seq_len, emb_dim, emb_pad, d_hy))
    w1p = jnp.pad(hy_w1[0], ((0, emb_pad - emb_dim), (0, 0)))
    kern, ksum = _filt_time_call(emb, emb_rev, w1p, hy_b1, hy_w2[0], hy_b2, hy_w3[0], hy_b3, hy_w4[0],
                                 hy_freq, absdelta, seq_len)
    kf = _filt_spec_call(kern.reshape(2, half, n2s, d_hy), ksum, tabs)
    hy = _hyena_call(p.reshape(b, half, n2s, p.shape[2]), hy_short_w[0], hy_short_b, hy_bias, kf, tabs,
                     4 * d_ret, d_hy)
    hy = hy.reshape(b, seq_len, d_hy)

    return _out_ffn_call(ret, hy, x, mod, hy_out_norm, norm2[0][None, :], final_norm[None, :],
                         w_out[0].astype(BF16), w_gate_up[0].astype(BF16), w_down[0].astype(BF16))
```

```python
import functools
import math

import jax
import jax.numpy as jnp
import numpy as np
from jax import lax
from jax.experimental import pallas as pl
from jax.experimental.pallas import tpu as pltpu

F32 = jnp.float32
BF16 = jnp.bfloat16
HIGHEST = lax.Precision.HIGHEST

RET_HEADS = 4
GRID_W = 64
ROPE_BASE = 10000.0
N_MOD = 6
HYENA_PROJ = 3
FILTER_DECAY_FAST = 0.3
FILTER_DECAY_SLOW = 1.5
FILTER_DECAY_TARGET = 1e-2
EPS = 1e-6

LANES = 128
SUBLANES = 8
VMEM_LIMIT_BYTES = 56 * 1024 * 1024

RET_CHUNK = 256
ROW_TILE = 512
HY_CT = 128
FILT_ROWS = 512
FFT_N1 = 64
K1P = 40


def _silu(v):
    return v / (1.0 + jnp.exp(-v))


def _cparams(sem, vmem=VMEM_LIMIT_BYTES):
    return pltpu.CompilerParams(dimension_semantics=sem, vmem_limit_bytes=vmem)


def _const_spec(shape):
    nd = len(shape)
    return pl.BlockSpec(shape, lambda *_: (0,) * nd, pipeline_mode=pl.Buffered(1))


def _mod_kernel(c_ref, w_ref, b_ref, o_ref):
    s = _silu(c_ref[...])
    o_ref[...] = jnp.dot(s, w_ref[...], precision=HIGHEST, preferred_element_type=F32) + b_ref[...]


def _mod_call(c_rows, w_mod, b_mod):
    rows, d = c_rows.shape
    n = w_mod.shape[1]
    tn = 1536
    return pl.pallas_call(
        _mod_kernel,
        grid=(n // tn,),
        in_specs=[pl.BlockSpec((rows, d), lambda j: (0, 0)),
                  pl.BlockSpec((d, tn), lambda j: (0, j)),
                  pl.BlockSpec((1, tn), lambda j: (0, j))],
        out_specs=pl.BlockSpec((rows, tn), lambda j: (0, j)),
        out_shape=jax.ShapeDtypeStruct((rows, n), F32),
        compiler_params=_cparams(("arbitrary",)),
        name="mod",
    )(c_rows, w_mod, b_mod)


def _ctx_kernel(ctx_ref, n1_ref, sh_ref, sc_ref, w_ref, dec_ref, s_ref, *, heads, dh):
    xc = ctx_ref[...]
    n_ctx = xc.shape[0]
    ms = jnp.mean(xc * xc, axis=-1, keepdims=True)
    hc = (xc * lax.rsqrt(ms + EPS) * n1_ref[...]) * (1.0 + sc_ref[...]) + sh_ref[...]
    kv = jnp.dot(hc.astype(BF16), w_ref[...], preferred_element_type=F32)
    lg = jnp.log1p(-jnp.exp(dec_ref[...]))
    pos = lax.broadcasted_iota(jnp.int32, (n_ctx, dh), 0).astype(F32)
    k_scale = dh ** -0.5
    d_ret = heads * dh
    tdims = (((0,), (0,)), ((), ()))
    for h in range(heads):
        kh = kv[:, h * dh:(h + 1) * dh] * k_scale
        vh = kv[:, d_ret + h * dh:d_ret + (h + 1) * dh].astype(BF16)
        wf = jnp.exp(lg[h:h + 1, :] * (n_ctx - 1.0 - pos))
        wb = jnp.exp(lg[heads + h:heads + h + 1, :] * pos)
        s_ref[h] = lax.dot_general((kh * wf).astype(BF16), vh, tdims, preferred_element_type=F32)
        s_ref[heads + h] = lax.dot_general((kh * wb).astype(BF16), vh, tdims, preferred_element_type=F32)


def _ctx_call(ctx, norm1, shift_c, scale_c, w_kv, dec, heads, dh):
    b, n_ctx, d = ctx.shape
    return pl.pallas_call(
        functools.partial(_ctx_kernel, heads=heads, dh=dh),
        grid=(b,),
        in_specs=[pl.BlockSpec((None, n_ctx, d), lambda i: (i, 0, 0)),
                  pl.BlockSpec((1, d), lambda i: (0, 0)),
                  pl.BlockSpec((1, d), lambda i: (0, 0)),
                  pl.BlockSpec((1, d), lambda i: (0, 0)),
                  pl.BlockSpec(w_kv.shape, lambda i: (0, 0)),
                  pl.BlockSpec(dec.shape, lambda i: (0, 0))],
        out_specs=pl.BlockSpec((None, 2 * heads, dh, dh), lambda i: (i, 0, 0, 0)),
        out_shape=jax.ShapeDtypeStruct((b, 2 * heads, dh, dh), F32),
        compiler_params=_cparams(("arbitrary",)),
        name="ctx_state",
    )(ctx, norm1, shift_c, scale_c, w_kv, dec)


def _inproj_kernel(x_ref, mod_ref, n1_ref, w_ref, o_ref, *, col_chunk):
    x = x_ref[...]
    ms = jnp.mean(x * x, axis=-1, keepdims=True)
    h = (x * lax.rsqrt(ms + EPS) * n1_ref[...]) * (1.0 + mod_ref[1:2, :]) + mod_ref[0:1, :]
    hb = h.astype(BF16)
    for j in range(w_ref.shape[1] // col_chunk):
        sl = slice(j * col_chunk, (j + 1) * col_chunk)
        o_ref[:, sl] = jnp.dot(hb, w_ref[:, sl], preferred_element_type=F32).astype(o_ref.dtype)


def _inproj_call(x, mod, norm1, w_in):
    b, l, d = x.shape
    d_in = w_in.shape[1]
    tm = ROW_TILE
    return pl.pallas_call(
        functools.partial(_inproj_kernel, col_chunk=512),
        grid=(b, l // tm),
        in_specs=[pl.BlockSpec((None, tm, d), lambda i, t: (i, t, 0)),
                  pl.BlockSpec((None, N_MOD, d), lambda i, t: (i, 0, 0)),
                  pl.BlockSpec((1, d), lambda i, t: (0, 0)),
                  _const_spec(w_in.shape)],
        out_specs=pl.BlockSpec((None, tm, d_in), lambda i, t: (i, t, 0)),
        out_shape=jax.ShapeDtypeStruct((b, l, d_in), BF16),
        compiler_params=_cparams(("arbitrary", "arbitrary")),
        name="in_proj",
    )(x, mod, norm1, w_in)


def _ret_kernel(q_ref, k_ref, v_ref, g_ref, cc_ref, ss_ref, sf0_ref, sb0_ref, dec_ref, gain_ref,
                o_ref, kr_s, kvf_s, kvb_s, sf_s, sb_s, *, heads, chunk):
    l, dh = q_ref.shape
    nc = l // chunk
    h = pl.program_id(1)
    lgf = jnp.log1p(-jnp.exp(dec_ref[pl.ds(h, 1), :]))
    lgb = jnp.log1p(-jnp.exp(dec_ref[pl.ds(h + heads, 1), :]))
    il = lax.broadcasted_iota(jnp.int32, (chunk, dh), 0).astype(F32)
    kw_f = jnp.exp(lgf * (chunk - 1.0 - il))
    kw_b = jnp.exp(lgb * il)
    qw_f = jnp.exp(lgf * (il + 1.0))
    qw_b = jnp.exp(lgb * (chunk - il))
    cd_f = jnp.exp(lgf * float(chunk))
    cd_b = jnp.exp(lgb * float(chunk))
    reps = chunk // dh
    lgf_c = jnp.concatenate([lgf] * reps, axis=1)
    lgb_c = jnp.concatenate([lgb] * reps, axis=1)
    ii = lax.broadcasted_iota(jnp.int32, (chunk, chunk), 0)
    jj = lax.broadcasted_iota(jnp.int32, (chunk, chunk), 1)
    diff = (ii - jj).astype(F32)
    dmask = (jnp.where(diff >= 0, jnp.exp(lgf_c * jnp.maximum(diff, 0.0)), 0.0)
             + jnp.where(diff <= 0, jnp.exp(lgb_c * jnp.maximum(-diff, 0.0)), 0.0))
    k_scale = dh ** -0.5
    tdims = (((0,), (0,)), ((), ()))
    ntdims = (((1,), (1,)), ((), ()))

    def rope(t, r0):
        return t * cc_ref[pl.ds(r0, chunk), :] + pltpu.roll(t, dh // 2, 1) * ss_ref[pl.ds(r0, chunk), :]

    def chunk_states(n, carry):
        r0 = pl.multiple_of(n * chunk, chunk)
        kr = rope(k_ref[pl.ds(r0, chunk), :].astype(F32), r0) * k_scale
        kr_s[pl.ds(r0, chunk), :] = kr.astype(BF16)
        vv = v_ref[pl.ds(r0, chunk), :]
        kvf_s[n] = lax.dot_general((kr * kw_f).astype(BF16), vv, tdims, preferred_element_type=F32)
        kvb_s[n] = lax.dot_general((kr * kw_b).astype(BF16), vv, tdims, preferred_element_type=F32)
        return carry

    lax.fori_loop(0, nc, chunk_states, 0, unroll=4)

    def scan_f(n, s):
        sf_s[n] = s
        return s * cd_f + kvf_s[n]

    lax.fori_loop(0, nc, scan_f, sf0_ref[...])

    def scan_b(m, s):
        n = nc - 1 - m
        sb_s[n] = s
        return s * cd_b + kvb_s[n]

    lax.fori_loop(0, nc, scan_b, sb0_ref[...])

    gain = gain_ref[...]

    def chunk_out(n, carry):
        r0 = pl.multiple_of(n * chunk, chunk)
        qr = rope(q_ref[pl.ds(r0, chunk), :].astype(F32), r0)
        sc = lax.dot_general(qr.astype(BF16), kr_s[pl.ds(r0, chunk), :], ntdims,
                             preferred_element_type=F32)
        o = jnp.dot((sc * dmask).astype(BF16), v_ref[pl.ds(r0, chunk), :], preferred_element_type=F32)
        o = o + jnp.dot((qr * qw_f).astype(BF16), sf_s[n].astype(BF16), preferred_element_type=F32)
        o = o + jnp.dot((qr * qw_b).astype(BF16), sb_s[n].astype(BF16), preferred_element_type=F32)
        mu = jnp.mean(o, axis=-1, keepdims=True)
        d = o - mu
        var = jnp.mean(d * d, axis=-1, keepdims=True)
        gg = g_ref[pl.ds(r0, chunk), :].astype(F32)
        o_ref[pl.ds(r0, chunk), :] = (d * lax.rsqrt(var + EPS) * gain * _silu(gg)).astype(o_ref.dtype)
        return carry

    lax.fori_loop(0, nc, chunk_out, 0, unroll=4)


def _ret_call(p, cc, ss, s0, dec, gn_gain, heads, dh):
    b, l, _ = p.shape
    chunk = RET_CHUNK
    nc = l // chunk
    seq = lambda off: pl.BlockSpec((None, l, dh), lambda i, h: (i, 0, off + h))
    return pl.pallas_call(
        functools.partial(_ret_kernel, heads=heads, chunk=chunk),
        grid=(b, heads),
        in_specs=[seq(0), seq(heads), seq(2 * heads), seq(3 * heads),
                  _const_spec(cc.shape), _const_spec(ss.shape),
                  pl.BlockSpec((None, None, dh, dh), lambda i, h: (i, h, 0, 0)),
                  pl.BlockSpec((None, None, dh, dh), lambda i, h: (i, heads + h, 0, 0)),
                  pl.BlockSpec(dec.shape, lambda i, h: (0, 0)),
                  pl.BlockSpec((1, dh), lambda i, h: (0, h))],
        out_specs=pl.BlockSpec((None, l, dh), lambda i, h: (i, 0, h)),
        out_shape=jax.ShapeDtypeStruct((b, l, heads * dh), BF16),
        scratch_shapes=[pltpu.VMEM((l, dh), BF16),
                        pltpu.VMEM((nc, dh, dh), F32), pltpu.VMEM((nc, dh, dh), F32),
                        pltpu.VMEM((nc, dh, dh), F32), pltpu.VMEM((nc, dh, dh), F32)],
        compiler_params=_cparams(("arbitrary", "arbitrary")),
        name="retention",
    )(p, p, p, p, cc, ss, s0, s0, dec, gn_gain)


def _filt_time_kernel(emb_ref, embr_ref, w1_ref, b1_ref, w2_ref, b2_ref, w3_ref, b3_ref, w4_ref,
                      fr_ref, dl_ref, kern_ref, s_ref, *, seq_len):
    i = pl.program_id(0)
    rows, c = kern_ref.shape[1], kern_ref.shape[2]
    fr = fr_ref[...]
    hdot = functools.partial(jnp.dot, precision=HIGHEST, preferred_element_type=F32)

    def mlp(e):
        z = jnp.sin(fr * (hdot(e, w1_ref[...]) + b1_ref[...]))
        z = jnp.sin(fr * (hdot(z, w2_ref[...]) + b2_ref[...]))
        return jnp.sin(fr * (hdot(z, w3_ref[...]) + b3_ref[...]))

    pos = (i * rows + lax.broadcasted_iota(jnp.int32, (rows, c), 0)).astype(F32)
    inv = 1.0 / (seq_len - 1.0)
    adl = dl_ref[...]
    hf = hdot(mlp(emb_ref[...]), w4_ref[:, :c]) * jnp.exp(-(pos * inv) * adl)
    hr = hdot(mlp(embr_ref[...]), w4_ref[:, c:]) * jnp.exp(-((seq_len - pos) * inv) * adl)
    hr = jnp.where(pos == 0.0, 0.0, hr)
    kern_ref[0] = hf
    kern_ref[1] = hr
    part = jnp.sum(jnp.abs(hf) + jnp.abs(hr), axis=0, keepdims=True)

    @pl.when(i == 0)
    def _():
        s_ref[...] = part

    @pl.when(i != 0)
    def _():
        s_ref[...] += part


def _filt_time_call(emb, emb_rev, w1, b1, w2, b2, w3, b3, w4, freq, absdelta, seq_len):
    c = absdelta.shape[1]
    rows = FILT_ROWS
    small = lambda a: pl.BlockSpec(a.shape, lambda i: (0,) * a.ndim)
    return pl.pallas_call(
        functools.partial(_filt_time_kernel, seq_len=seq_len),
        grid=(seq_len // rows,),
        in_specs=[pl.BlockSpec((rows, emb.shape[1]), lambda i: (i, 0)),
                  pl.BlockSpec((rows, emb.shape[1]), lambda i: (i, 0)),
                  small(w1), small(b1), small(w2), small(b2), small(w3), small(b3), small(w4),
                  small(freq), small(absdelta)],
        out_specs=[pl.BlockSpec((2, rows, c), lambda i: (0, i, 0)),
                   pl.BlockSpec((1, c), lambda i: (0, 0))],
        out_shape=[jax.ShapeDtypeStruct((2, seq_len, c), F32),
                   jax.ShapeDtypeStruct((1, c), F32)],
        compiler_params=_cparams(("arbitrary",)),
        name="filt_time",
    )(emb, emb_rev, w1, b1, w2, b2, w3, b3, w4, freq, absdelta)


@functools.lru_cache(maxsize=None)
def _fft_tables(seq_len):
    n = 2 * seq_len
    n1s = FFT_N1
    n2s = n // n1s
    half = n1s // 2
    k1n = half + 1
    n1 = np.arange(half)
    k1 = np.arange(k1n)
    th = 2.0 * np.pi * (np.outer(k1, n1) % n1s) / n1s
    fa = np.zeros((2 * K1P, half))
    fa[:k1n] = np.cos(th)
    fa[K1P:K1P + k1n] = -np.sin(th)
    sgn = np.ones(2 * K1P)
    sgn[:k1n] = (-1.0) ** k1
    sgn[K1P:K1P + k1n] = (-1.0) ** k1
    herm = np.where((k1 == 0) | (k1 == half), 1.0, 2.0)
    fai = np.zeros((half, 2 * K1P))
    fai[:, :k1n] = (np.cos(th) * herm[:, None]).T / n
    fai[:, K1P:K1P + k1n] = (-np.sin(th) * herm[:, None]).T / n
    eye = np.eye(SUBLANES)
    fk = np.kron(fa, eye)
    fks = np.kron(fa * sgn[:, None], eye)
    fki = np.kron(fai, eye)
    k2 = np.arange(n2s)
    n2 = np.arange(n2s)
    m = np.zeros((k1n, 2 * n2s, 2 * n2s))
    for a in range(k1n):
        ang = 2.0 * np.pi * (np.outer(a + n1s * k2, n2) % n) / n
        gr, gi = np.cos(ang), -np.sin(ang)
        m[a] = np.block([[gr, -gi], [gi, gr]])
    mt = np.transpose(m, (0, 2, 1))
    return dict(fk=fk, fkf=np.concatenate([fk, fks], axis=1), fki=fki, m=m, mt=mt,
                k1n=k1n, n2s=n2s, half=half)


def _stage_n1(src_refs, mat, dst_ref, n2s):
    rows = dst_ref.shape[0]
    for j in range(n2s // SUBLANES):
        sl = slice(j * SUBLANES, (j + 1) * SUBLANES)
        parts = [r[:, sl, :] for r in src_refs]
        xg = jnp.concatenate([p.reshape(p.shape[0] * SUBLANES, p.shape[2]) for p in parts], axis=0)
        a = jnp.dot(mat, xg.astype(BF16), preferred_element_type=F32)
        dst_ref[:, sl, :] = a.reshape(rows, SUBLANES, a.shape[1])


def _filt_spec_kernel(kern_ref, s_ref, fkf_ref, m_ref, kf_ref, a_s, *, k1n, n2s):
    _stage_n1([kern_ref.at[0], kern_ref.at[1]], fkf_ref[...], a_s, n2s)
    inv = 1.0 / (s_ref[...] + EPS)

    def body(k1, carry):
        a = jnp.concatenate([a_s[k1], a_s[K1P + k1]], axis=0)
        kf_ref[k1] = jnp.dot(m_ref[k1], a.astype(BF16), preferred_element_type=F32) * inv
        return carry

    lax.fori_loop(0, k1n, body, 0)


def _filt_spec_call(kern4, s, tabs):
    _, half, n2s, c = kern4.shape
    ct = HY_CT
    k1n = tabs["k1n"]
    return pl.pallas_call(
        functools.partial(_filt_spec_kernel, k1n=k1n, n2s=n2s),
        grid=(c // ct,),
        in_specs=[pl.BlockSpec((2, half, n2s, ct), lambda j: (0, 0, 0, j)),
                  pl.BlockSpec((1, ct), lambda j: (0, j)),
                  _const_spec(tabs["fkf"].shape), _const_spec(tabs["m"].shape)],
        out_specs=pl.BlockSpec((k1n, 2 * n2s, ct), lambda j: (0, 0, j)),
        out_shape=jax.ShapeDtypeStruct((k1n, 2 * n2s, c), F32),
        scratch_shapes=[pltpu.VMEM((2 * K1P, n2s, ct), F32)],
        compiler_params=_cparams(("arbitrary",)),
        name="filt_spec",
    )(kern4, s, tabs["fkf"], tabs["m"])


def _hyena_kernel(x0_ref, x1_ref, v_ref, sw0_ref, sw1_ref, sw2_ref, sb0_ref, sb1_ref, sb2_ref,
                  bias_ref, kf_ref, fk_ref, fki_ref, m_ref, mt_ref, o_ref, w_s, x0_s, a_s,
                  *, k1n, n2s):
    half = w_s.shape[0]
    ct = w_s.shape[2]
    row = lax.broadcasted_iota(jnp.int32, (n2s, ct), 0)

    def short_conv(u_ref, w_ref, b_ref, c):
        u = u_ref[c].astype(F32)
        if c > 0:
            prev = u_ref[c - 1, n2s - 1:n2s, :].astype(F32)
        else:
            prev = jnp.zeros((1, ct), F32)
        if c < half - 1:
            nxt = u_ref[c + 1, 0:1, :].astype(F32)
        else:
            nxt = jnp.zeros((1, ct), F32)
        up = jnp.where(row == 0, prev, pltpu.roll(u, 1, 0))
        dn = jnp.where(row == n2s - 1, nxt, pltpu.roll(u, n2s - 1, 0))
        return w_ref[0:1, :] * up + w_ref[1:2, :] * u + w_ref[2:3, :] * dn + b_ref[...]

    for c in range(half):
        x0_s[c] = short_conv(x0_ref, sw0_ref, sb0_ref, c)
        w_s[c] = short_conv(x1_ref, sw1_ref, sb1_ref, c) * short_conv(v_ref, sw2_ref, sb2_ref, c)

    _stage_n1([w_s], fk_ref[...], a_s, n2s)

    for k1 in range(k1n):
        a = jnp.concatenate([a_s[k1], a_s[K1P + k1]], axis=0)
        x = jnp.dot(m_ref[k1], a.astype(BF16), preferred_element_type=F32)
        xr, xi = x[:n2s], x[n2s:]
        kr, ki = kf_ref[k1, :n2s, :], kf_ref[k1, n2s:, :]
        y = jnp.concatenate([xr * kr - xi * ki, xr * ki + xi * kr], axis=0)
        z = jnp.dot(mt_ref[k1], y.astype(BF16), preferred_element_type=F32)
        a_s[k1] = z[:n2s]
        a_s[K1P + k1] = z[n2s:]

    fki = fki_ref[...]
    bias = bias_ref[...]
    for j in range(n2s // SUBLANES):
        sl = slice(j * SUBLANES, (j + 1) * SUBLANES)
        zg = a_s[:, sl, :].reshape(2 * K1P * SUBLANES, ct)
        y = jnp.dot(fki, zg.astype(BF16), preferred_element_type=F32).reshape(half, SUBLANES, ct)
        o_ref[:, sl, :] = (x0_s[:, sl, :] * (y + w_s[:, sl, :] * bias)).astype(o_ref.dtype)


def _hyena_call(p5, short_w, short_b, bias, kf, tabs, d_ret4, c):
    b, half, n2s, _ = p5.shape
    ct = HY_CT
    k1n = tabs["k1n"]
    nct = c // ct
    base = d_ret4 // ct
    seq = lambda proj: pl.BlockSpec((None, half, n2s, ct), lambda j, i: (i, 0, 0, base + proj * nct + j))
    sw = lambda proj: pl.BlockSpec((short_w.shape[0], ct), lambda j, i: (0, proj * nct + j))
    sb = lambda proj: pl.BlockSpec((1, ct), lambda j, i: (0, proj * nct + j))
    return pl.pallas_call(
        functools.partial(_hyena_kernel, k1n=k1n, n2s=n2s),
        grid=(nct, b),
        in_specs=[seq(0), seq(1), seq(2), sw(0), sw(1), sw(2), sb(0), sb(1), sb(2),
                  pl.BlockSpec((1, ct), lambda j, i: (0, j)),
                  pl.BlockSpec((k1n, 2 * n2s, ct), lambda j, i: (0, 0, j)),
                  _const_spec(tabs["fk"].shape), _const_spec(tabs["fki"].shape),
                  _const_spec(tabs["m"].shape), _const_spec(tabs["mt"].shape)],
        out_specs=pl.BlockSpec((None, half, n2s, ct), lambda j, i: (i, 0, 0, j)),
        out_shape=jax.ShapeDtypeStruct((b, half, n2s, c), BF16),
        scratch_shapes=[pltpu.VMEM((half, n2s, ct), F32), pltpu.VMEM((half, n2s, ct), F32),
                        pltpu.VMEM((2 * K1P, n2s, ct), F32)],
        compiler_params=_cparams(("arbitrary", "arbitrary")),
        name="hyena",
    )(p5, p5, p5, short_w, short_w, short_w, short_b, short_b, short_b, bias, kf,
      tabs["fk"], tabs["fki"], tabs["m"], tabs["mt"])


def _out_ffn_kernel(ret_ref, hy_ref, x_ref, mod_ref, hg_ref, n2_ref, fn_ref, wo_ref, wgu_ref, wd_ref,
                    o_ref, *, d_ret, d_ff, ff_chunk):
    hy = hy_ref[...].astype(F32)
    hms = jnp.mean(hy * hy, axis=-1, keepdims=True)
    hyn = (hy * lax.rsqrt(hms + EPS) * hg_ref[...]).astype(BF16)
    mix = jnp.dot(ret_ref[...], wo_ref[:d_ret, :], preferred_element_type=F32)
    mix = mix + jnp.dot(hyn, wo_ref[d_ret:, :], preferred_element_type=F32)
    x1 = x_ref[...] + mod_ref[2:3, :] * mix
    ms = jnp.mean(x1 * x1, axis=-1, keepdims=True)
    h2 = ((x1 * lax.rsqrt(ms + EPS) * n2_ref[...]) * (1.0 + mod_ref[4:5, :]) + mod_ref[3:4, :]).astype(BF16)
    acc = jnp.zeros(x1.shape, F32)
    for j in range(d_ff // ff_chunk):
        g = jnp.dot(h2, wgu_ref[:, j * ff_chunk:(j + 1) * ff_chunk], preferred_element_type=F32)
        u = jnp.dot(h2, wgu_ref[:, d_ff + j * ff_chunk:d_ff + (j + 1) * ff_chunk], preferred_element_type=F32)
        a = (_silu(g) * u).astype(BF16)
        acc = acc + jnp.dot(a, wd_ref[j * ff_chunk:(j + 1) * ff_chunk, :], preferred_element_type=F32)
    x2 = x1 + mod_ref[5:6, :] * acc
    ms2 = jnp.mean(x2 * x2, axis=-1, keepdims=True)
    o_ref[...] = x2 * lax.rsqrt(ms2 + EPS) * fn_ref[...]


def _out_ffn_call(ret, hy, x, mod, hy_gain, norm2, final_norm, w_out, w_gu, w_down):
    b, l, d = x.shape
    d_ret = ret.shape[2]
    d_hy = hy.shape[2]
    d_ff = w_down.shape[0]
    tm = ROW_TILE
    return pl.pallas_call(
        functools.partial(_out_ffn_kernel, d_ret=d_ret, d_ff=d_ff, ff_chunk=d_ff // 2),
        grid=(b, l // tm),
        in_specs=[pl.BlockSpec((None, tm, d_ret), lambda i, t: (i, t, 0)),
                  pl.BlockSpec((None, tm, d_hy), lambda i, t: (i, t, 0)),
                  pl.BlockSpec((None, tm, d), lambda i, t: (i, t, 0)),
                  pl.BlockSpec((None, N_MOD, d), lambda i, t: (i, 0, 0)),
                  pl.BlockSpec((1, d_hy), lambda i, t: (0, 0)),
                  pl.BlockSpec((1, d), lambda i, t: (0, 0)),
                  pl.BlockSpec((1, d), lambda i, t: (0, 0)),
                  _const_spec(w_out.shape), _const_spec(w_gu.shape), _const_spec(w_down.shape)],
        out_specs=pl.BlockSpec((None, tm, d), lambda i, t: (i, t, 0)),
        out_shape=jax.ShapeDtypeStruct((b, l, d), x.dtype),
        compiler_params=_cparams(("arbitrary", "arbitrary")),
        name="out_ffn",
    )(ret, hy, x, mod, hy_gain, norm2, final_norm, w_out, w_gu, w_down)


@functools.lru_cache(maxsize=None)
def _rope_tables(seq_len, dh):
    n = dh // 4
    t = np.arange(seq_len)
    inv = ROPE_BASE ** (-np.arange(n, dtype=np.float64) / n)
    ang = np.concatenate([(t // GRID_W)[:, None] * inv, (t % GRID_W)[:, None] * inv], axis=-1)
    cc = np.concatenate([np.cos(ang), np.cos(ang)], axis=-1)
    ss = np.concatenate([-np.sin(ang), np.sin(ang)], axis=-1)
    return cc.astype(np.float32), ss.astype(np.float32)


@functools.lru_cache(maxsize=None)
def _filter_tables(seq_len, emb_dim, emb_pad, channels):
    t = np.linspace(0.0, 1.0, seq_len)[:, None]
    bands = (emb_dim - 1) // 2
    f = np.linspace(1e-4, bands - 1, bands)[None, :]
    wpos = 2.0 * np.pi * np.arange(seq_len)[:, None] / seq_len
    emb = np.concatenate([t, np.cos(f * wpos), -np.sin(f * wpos)], axis=-1)
    emb = np.pad(emb, ((0, 0), (0, emb_pad - emb_dim)))
    emb_rev = np.concatenate([emb[:1], emb[:0:-1]], axis=0)
    max_decay = math.log(FILTER_DECAY_TARGET) / FILTER_DECAY_FAST
    min_decay = math.log(FILTER_DECAY_TARGET) / FILTER_DECAY_SLOW
    absdelta = np.abs(np.linspace(min_decay, max_decay, channels))[None, :]
    return emb.astype(np.float32), emb_rev.astype(np.float32), absdelta.astype(np.float32)


def kernel(x, c, ctx, c_ctx, w_mod, b_mod, norm1, norm2, w_in, ret_decay, ret_gn_gain, hy_short_w,
           hy_short_b, hy_w1, hy_b1, hy_w2, hy_b2, hy_w3, hy_b3, hy_w4, hy_freq, hy_bias, hy_out_norm,
           w_out, w_gate_up, w_down, final_norm):
    b, seq_len, d = x.shape
    assert w_mod.shape[0] == 1, "single-layer block"
    heads = RET_HEADS
    d_ret = ret_gn_gain.shape[1]
    dh = d_ret // heads
    d_hy = hy_bias.shape[1]
    assert dh == LANES and d_hy % HY_CT == 0 and seq_len % (FFT_N1 // 2) == 0

    rows = -(-(b + 1) // SUBLANES) * SUBLANES
    c_rows = jnp.zeros((rows, d), F32).at[:b].set(c).at[b].set(c_ctx)
    mod_all = _mod_call(c_rows, w_mod[0], b_mod[0][None, :])
    mod = mod_all[:b].reshape(b, N_MOD, d)
    mod_c = mod_all[b].reshape(N_MOD, d)

    w_in_b = w_in[0].astype(BF16)
    dec = jnp.broadcast_to(ret_decay[0].reshape(2 * heads, 1), (2 * heads, LANES))
    n1g = norm1[0][None, :]

    s0 = _ctx_call(ctx, n1g, mod_c[0:1], mod_c[1:2], w_in_b[:, d_ret:3 * d_ret], dec, heads, dh)
    p = _inproj_call(x, mod, n1g, w_in_b)

    cc, ss = (jnp.asarray(t) for t in _rope_tables(seq_len, dh))
    ret = _ret_call(p, cc, ss, s0, dec, ret_gn_gain, heads, dh)

    tabs = dict(_fft_tables(seq_len))
    for name in ("fk", "fkf", "fki", "m", "mt"):
        tabs[name] = jnp.asarray(tabs[name], dtype=F32).astype(BF16)
    half, n2s = tabs["half"], tabs["n2s"]
    emb_dim = hy_w1.shape[1]
    emb_pad = -(-emb_dim // SUBLANES) * SUBLANES
    emb, emb_rev, absdelta = (jnp.asarray(t) for t in _filter_tables(seq_len, emb_dim, emb_pad, d_hy))
    w1p = jnp.pad(hy_w1[0], ((0, emb_pad - emb_dim), (0, 0)))
    kern, ksum = _filt_time_call(emb, emb_rev, w1p, hy_b1, hy_w2[0], hy_b2, hy_w3[0], hy_b3, hy_w4[0],
                                 hy_freq, absdelta, seq_len)
    kf = _filt_spec_call(kern.reshape(2, half, n2s, d_hy), ksum, tabs)
    hy = _hyena_call(p.reshape(b, half, n2s, p.shape[2]), hy_short_w[0], hy_short_b, hy_bias, kf, tabs,
                     4 * d_ret, d_hy)
    hy = hy.reshape(b, seq_len, d_hy)

    return _out_ffn_call(ret, hy, x, mod, hy_out_norm, norm2[0][None, :], final_norm[None, :],
                         w_out[0].astype(BF16), w_gate_up[0].astype(BF16), w_down[0].astype(BF16))
```

```python
import functools
import math

import jax
import jax.numpy as jnp
import numpy as np
from jax import lax
from jax.experimental import pallas as pl
from jax.experimental.pallas import tpu as pltpu

F32 = jnp.float32
BF16 = jnp.bfloat16
HIGHEST = lax.Precision.HIGHEST

RET_HEADS = 4
GRID_W = 64
ROPE_BASE = 10000.0
N_MOD = 6
HYENA_PROJ = 3
FILTER_DECAY_FAST = 0.3
FILTER_DECAY_SLOW = 1.5
FILTER_DECAY_TARGET = 1e-2
EPS = 1e-6

LANES = 128
SUBLANES = 8
VMEM_LIMIT_BYTES = 56 * 1024 * 1024

RET_CHUNK = 256
ROW_TILE = 512
HY_CT = 256
FILT_ROWS = 512
FFT_N1 = 64


def _silu(v):
    return v / (1.0 + jnp.exp(-v))


def _cparams(sem, vmem=VMEM_LIMIT_BYTES):
    return pltpu.CompilerParams(dimension_semantics=sem, vmem_limit_bytes=vmem)


def _const_spec(shape):
    nd = len(shape)
    return pl.BlockSpec(shape, lambda *_: (0,) * nd, pipeline_mode=pl.Buffered(1))


def _mod_kernel(c_ref, w_ref, b_ref, o_ref):
    s = _silu(c_ref[...])
    o_ref[...] = jnp.dot(s, w_ref[...], precision=HIGHEST, preferred_element_type=F32) + b_ref[...]


def _mod_call(c_rows, w_mod, b_mod):
    rows, d = c_rows.shape
    n = w_mod.shape[1]
    tn = 1536
    return pl.pallas_call(
        _mod_kernel,
        grid=(n // tn,),
        in_specs=[pl.BlockSpec((rows, d), lambda j: (0, 0)),
                  pl.BlockSpec((d, tn), lambda j: (0, j)),
                  pl.BlockSpec((1, tn), lambda j: (0, j))],
        out_specs=pl.BlockSpec((rows, tn), lambda j: (0, j)),
        out_shape=jax.ShapeDtypeStruct((rows, n), F32),
        compiler_params=_cparams(("arbitrary",)),
        name="mod",
    )(c_rows, w_mod, b_mod)


def _ctx_kernel(ctx_ref, n1_ref, sh_ref, sc_ref, w_ref, dec_ref, s_ref, *, heads, dh):
    xc = ctx_ref[...]
    n_ctx = xc.shape[0]
    ms = jnp.mean(xc * xc, axis=-1, keepdims=True)
    hc = (xc * lax.rsqrt(ms + EPS) * n1_ref[...]) * (1.0 + sc_ref[...]) + sh_ref[...]
    kv = jnp.dot(hc.astype(BF16), w_ref[...], preferred_element_type=F32)
    lg = jnp.log1p(-jnp.exp(dec_ref[...]))
    pos = lax.broadcasted_iota(jnp.int32, (n_ctx, dh), 0).astype(F32)
    k_scale = dh ** -0.5
    d_ret = heads * dh
    tdims = (((0,), (0,)), ((), ()))
    for h in range(heads):
        kh = kv[:, h * dh:(h + 1) * dh] * k_scale
        vh = kv[:, d_ret + h * dh:d_ret + (h + 1) * dh].astype(BF16)
        wf = jnp.exp(lg[h:h + 1, :] * (n_ctx - 1.0 - pos))
        wb = jnp.exp(lg[heads + h:heads + h + 1, :] * pos)
        s_ref[h] = lax.dot_general((kh * wf).astype(BF16), vh, tdims, preferred_element_type=F32)
        s_ref[heads + h] = lax.dot_general((kh * wb).astype(BF16), vh, tdims, preferred_element_type=F32)


def _ctx_call(ctx, norm1, shift_c, scale_c, w_kv, dec, heads, dh):
    b, n_ctx, d = ctx.shape
    return pl.pallas_call(
        functools.partial(_ctx_kernel, heads=heads, dh=dh),
        grid=(b,),
        in_specs=[pl.BlockSpec((None, n_ctx, d), lambda i: (i, 0, 0)),
                  pl.BlockSpec((1, d), lambda i: (0, 0)),
                  pl.BlockSpec((1, d), lambda i: (0, 0)),
                  pl.BlockSpec((1, d), lambda i: (0, 0)),
                  pl.BlockSpec(w_kv.shape, lambda i: (0, 0)),
                  pl.BlockSpec(dec.shape, lambda i: (0, 0))],
        out_specs=pl.BlockSpec((None, 2 * heads, dh, dh), lambda i: (i, 0, 0, 0)),
        out_shape=jax.ShapeDtypeStruct((b, 2 * heads, dh, dh), F32),
        compiler_params=_cparams(("arbitrary",)),
        name="ctx_state",
    )(ctx, norm1, shift_c, scale_c, w_kv, dec)


def _inproj_kernel(x_ref, mod_ref, n1_ref, w_ref, cc_ref, ss_ref, sw_ref, sb_ref,
                   qkvg_ref, x0_ref, u_ref, hy_s, tail_s, *, heads, dh, d_hy, nt):
    t = pl.program_id(1)
    tm = x_ref.shape[0]
    d_ret = heads * dh

    @pl.when((pl.program_id(0) == 0) & (t == 0))
    def _():
        hy_s[...] = jnp.zeros(hy_s.shape, F32)
        tail_s[...] = jnp.zeros(tail_s.shape, F32)

    r8 = lax.broadcasted_iota(jnp.int32, (SUBLANES, d_hy), 0)

    def conv(cur, prev_row, next_row, sl):
        w0, w1, w2, bb = sw_ref[0:1, sl], sw_ref[1:2, sl], sw_ref[2:3, sl], sb_ref[:, sl]
        up = pltpu.roll(cur, 1, 0)
        dn = pltpu.roll(cur, tm - 1, 0)
        body = w0 * up + w1 * cur + w2 * dn + bb
        e = tm - SUBLANES
        head = (w0 * jnp.where(r8 == 0, prev_row, up[:SUBLANES]) + w1 * cur[:SUBLANES]
                + w2 * dn[:SUBLANES] + bb)
        tail = (w0 * up[e:] + w1 * cur[e:]
                + w2 * jnp.where(r8 == SUBLANES - 1, next_row, dn[e:]) + bb)
        return jnp.concatenate([head, body[SUBLANES:e], tail], axis=0)

    @pl.when(t < nt)
    def _():
        x = x_ref[...]
        ms = jnp.mean(x * x, axis=-1, keepdims=True)
        h = (x * lax.rsqrt(ms + EPS) * n1_ref[...]) * (1.0 + mod_ref[1:2, :]) + mod_ref[0:1, :]
        hb = h.astype(BF16)
        outs = []
        for j in range(HYENA_PROJ):
            sl = slice(j * d_hy, (j + 1) * d_hy)
            cur = hy_s[:, sl]
            new = jnp.dot(hb, w_ref[:, 4 * d_ret + j * d_hy:4 * d_ret + (j + 1) * d_hy],
                          preferred_element_type=F32)
            outs.append(conv(cur, tail_s[SUBLANES - 1:SUBLANES, sl], new[0:1], sl))
            tail_s[:, sl] = jnp.where(t == 0, 0.0, cur[tm - SUBLANES:])
            hy_s[:, sl] = new
        x0_ref[...] = outs[0]
        u_ref[...] = outs[1] * outs[2]
        cc = cc_ref[...]
        ss = ss_ref[...]
        k_scale = dh ** -0.5
        q = jnp.dot(hb, w_ref[:, :d_ret], preferred_element_type=F32)
        k = jnp.dot(hb, w_ref[:, d_ret:2 * d_ret], preferred_element_type=F32)
        for j in range(heads):
            sl = slice(j * dh, (j + 1) * dh)
            qj, kj = q[:, sl], k[:, sl]
            qkvg_ref[:, sl] = (qj * cc + pltpu.roll(qj, dh // 2, 1) * ss).astype(BF16)
            qkvg_ref[:, d_ret + j * dh:d_ret + (j + 1) * dh] = (
                (kj * cc + pltpu.roll(kj, dh // 2, 1) * ss) * k_scale).astype(BF16)
        sl = slice(2 * d_ret, 3 * d_ret)
        qkvg_ref[:, sl] = jnp.dot(hb, w_ref[:, sl], preferred_element_type=F32).astype(BF16)
        sl = slice(3 * d_ret, 4 * d_ret)
        qkvg_ref[:, sl] = _silu(jnp.dot(hb, w_ref[:, sl], preferred_element_type=F32)).astype(BF16)

    @pl.when(t == nt)
    def _():
        zero = jnp.zeros((1, d_hy), F32)
        outs = []
        for j in range(HYENA_PROJ):
            sl = slice(j * d_hy, (j + 1) * d_hy)
            outs.append(conv(hy_s[:, sl], tail_s[SUBLANES - 1:SUBLANES, sl], zero, sl))
        x0_ref[...] = outs[0]
        u_ref[...] = outs[1] * outs[2]


def _inproj_call(x, mod, norm1, w_in, cc, ss, short_w, short_b, heads, dh, d_hy):
    b, l, d = x.shape
    tm = ROW_TILE
    nt = l // tm
    d_ret = heads * dh
    cur = lambda i, t: (i, jnp.minimum(t, nt - 1), 0)
    lag = lambda i, t: (i, jnp.maximum(t - 1, 0), 0)
    return pl.pallas_call(
        functools.partial(_inproj_kernel, heads=heads, dh=dh, d_hy=d_hy, nt=nt),
        grid=(b, nt + 1),
        in_specs=[pl.BlockSpec((None, tm, d), cur),
                  pl.BlockSpec((None, N_MOD, d), lambda i, t: (i, 0, 0)),
                  pl.BlockSpec((1, d), lambda i, t: (0, 0)),
                  _const_spec(w_in.shape),
                  pl.BlockSpec((tm, dh), lambda i, t: (jnp.minimum(t, nt - 1), 0)),
                  pl.BlockSpec((tm, dh), lambda i, t: (jnp.minimum(t, nt - 1), 0)),
                  pl.BlockSpec(short_w.shape, lambda i, t: (0, 0)),
                  pl.BlockSpec(short_b.shape, lambda i, t: (0, 0))],
        out_specs=[pl.BlockSpec((None, tm, 4 * d_ret), cur),
                   pl.BlockSpec((None, tm, d_hy), lag),
                   pl.BlockSpec((None, tm, d_hy), lag)],
        out_shape=[jax.ShapeDtypeStruct((b, l, 4 * d_ret), BF16),
                   jax.ShapeDtypeStruct((b, l, d_hy), F32),
                   jax.ShapeDtypeStruct((b, l, d_hy), F32)],
        scratch_shapes=[pltpu.VMEM((tm, HYENA_PROJ * d_hy), F32),
                        pltpu.VMEM((SUBLANES, HYENA_PROJ * d_hy), F32)],
        compiler_params=_cparams(("arbitrary", "arbitrary")),
        name="in_proj",
    )(x, mod, norm1, w_in, cc, ss, short_w, short_b)


def _ret_kernel(q_ref, k_ref, v_ref, g_ref, sf0_ref, sb0_ref, dec_ref, gain_ref,
                o_ref, kvf_s, kvb_s, sf_s, sb_s, *, heads, chunk):
    l, dh = q_ref.shape
    nc = l // chunk
    h = pl.program_id(1)
    lgf = jnp.log1p(-jnp.exp(dec_ref[pl.ds(h, 1), :]))
    lgb = jnp.log1p(-jnp.exp(dec_ref[pl.ds(h + heads, 1), :]))
    il = lax.broadcasted_iota(jnp.int32, (chunk, dh), 0).astype(F32)
    kw_f = jnp.exp(lgf * (chunk - 1.0 - il))
    kw_b = jnp.exp(lgb * il)
    qw_f = jnp.exp(lgf * (il + 1.0))
    qw_b = jnp.exp(lgb * (chunk - il))
    cd_f = jnp.exp(lgf * float(chunk))
    cd_b = jnp.exp(lgb * float(chunk))
    reps = chunk // dh
    lgf_c = jnp.concatenate([lgf] * reps, axis=1)
    lgb_c = jnp.concatenate([lgb] * reps, axis=1)
    ii = lax.broadcasted_iota(jnp.int32, (chunk, chunk), 0)
    jj = lax.broadcasted_iota(jnp.int32, (chunk, chunk), 1)
    diff = (ii - jj).astype(F32)
    dmask = (jnp.where(diff >= 0, jnp.exp(lgf_c * jnp.maximum(diff, 0.0)), 0.0)
             + jnp.where(diff <= 0, jnp.exp(lgb_c * jnp.maximum(-diff, 0.0)), 0.0))
    tdims = (((0,), (0,)), ((), ()))
    ntdims = (((1,), (1,)), ((), ()))

    def chunk_states(n, carry):
        r0 = pl.multiple_of(n * chunk, chunk)
        kr = k_ref[pl.ds(r0, chunk), :].astype(F32)
        vv = v_ref[pl.ds(r0, chunk), :]
        kvf_s[n] = lax.dot_general((kr * kw_f).astype(BF16), vv, tdims, preferred_element_type=F32)
        kvb_s[n] = lax.dot_general((kr * kw_b).astype(BF16), vv, tdims, preferred_element_type=F32)
        return carry

    lax.fori_loop(0, nc, chunk_states, 0, unroll=4)

    def scan_f(n, s):
        sf_s[n] = s
        return s * cd_f + kvf_s[n]

    lax.fori_loop(0, nc, scan_f, sf0_ref[...])

    def scan_b(m, s):
        n = nc - 1 - m
        sb_s[n] = s
        return s * cd_b + kvb_s[n]

    lax.fori_loop(0, nc, scan_b, sb0_ref[...])

    gain = gain_ref[...]

    def chunk_out(n, carry):
        r0 = pl.multiple_of(n * chunk, chunk)
        qb = q_ref[pl.ds(r0, chunk), :]
        qr = qb.astype(F32)
        sc = lax.dot_general(qb, k_ref[pl.ds(r0, chunk), :], ntdims, preferred_element_type=F32)
        o = jnp.dot((sc * dmask).astype(BF16), v_ref[pl.ds(r0, chunk), :], preferred_element_type=F32)
        o = o + jnp.dot((qr * qw_f).astype(BF16), sf_s[n].astype(BF16), preferred_element_type=F32)
        o = o + jnp.dot((qr * qw_b).astype(BF16), sb_s[n].astype(BF16), preferred_element_type=F32)
        mu = jnp.mean(o, axis=-1, keepdims=True)
        d = o - mu
        var = jnp.mean(d * d, axis=-1, keepdims=True)
        gg = g_ref[pl.ds(r0, chunk), :].astype(F32)
        o_ref[pl.ds(r0, chunk), :] = (d * lax.rsqrt(var + EPS) * gain * gg).astype(o_ref.dtype)
        return carry

    lax.fori_loop(0, nc, chunk_out, 0, unroll=4)


def _ret_call(qkvg, s0, dec, gn_gain, heads, dh):
    b, l, _ = qkvg.shape
    chunk = RET_CHUNK
    nc = l // chunk
    seq = lambda off: pl.BlockSpec((None, l, dh), lambda i, h: (i, 0, off + h))
    return pl.pallas_call(
        functools.partial(_ret_kernel, heads=heads, chunk=chunk),
        grid=(b, heads),
        in_specs=[seq(0), seq(heads), seq(2 * heads), seq(3 * heads),
                  pl.BlockSpec((None, None, dh, dh), lambda i, h: (i, h, 0, 0)),
                  pl.BlockSpec((None, None, dh, dh), lambda i, h: (i, heads + h, 0, 0)),
                  pl.BlockSpec(dec.shape, lambda i, h: (0, 0)),
                  pl.BlockSpec((1, dh), lambda i, h: (0, h))],
        out_specs=pl.BlockSpec((None, l, dh), lambda i, h: (i, 0, h)),
        out_shape=jax.ShapeDtypeStruct((b, l, heads * dh), BF16),
        scratch_shapes=[pltpu.VMEM((nc, dh, dh), F32), pltpu.VMEM((nc, dh, dh), F32),
                        pltpu.VMEM((nc, dh, dh), F32), pltpu.VMEM((nc, dh, dh), F32)],
        compiler_params=_cparams(("arbitrary", "arbitrary")),
        name="retention",
    )(qkvg, qkvg, qkvg, qkvg, s0, s0, dec, gn_gain)


def _filt_time_kernel(emb_ref, w1_ref, b1_ref, w2_ref, b2_ref, w3_ref, b3_ref, w4f_ref, w4b_ref,
                      fr_ref, dl_ref, kern_ref, s_ref, *, seq_len):
    i = pl.program_id(0)
    rows, c = kern_ref.shape[1], kern_ref.shape[2]
    fr = fr_ref[...]
    hdot = functools.partial(jnp.dot, precision=HIGHEST, preferred_element_type=F32)
    z = jnp.sin(fr * (hdot(emb_ref[...], w1_ref[...]) + b1_ref[...]))
    z = jnp.sin(fr * (hdot(z, w2_ref[...]) + b2_ref[...]))
    z = jnp.sin(fr * (hdot(z, w3_ref[...]) + b3_ref[...]))
    pos = (i * rows + lax.broadcasted_iota(jnp.int32, (rows, c), 0)).astype(F32)
    inv = 1.0 / (seq_len - 1.0)
    adl = dl_ref[...]
    hf = hdot(z, w4f_ref[...]) * jnp.exp(-(pos * inv) * adl)
    hr = hdot(z, w4b_ref[...]) * jnp.exp(-((seq_len - pos) * inv) * adl)
    hr = jnp.where(pos == 0.0, 0.0, hr)
    kern_ref[0] = hf
    kern_ref[1] = hr
    part = jnp.sum(jnp.abs(hf) + jnp.abs(hr), axis=0, keepdims=True)

    @pl.when(i == 0)
    def _():
        s_ref[...] = part

    @pl.when(i != 0)
    def _():
        s_ref[...] += part


def _filt_time_call(emb2, w1, b1, w2, b2, w3, b3, w4f, w4b, freq, absdelta, seq_len):
    c = absdelta.shape[1]
    rows = FILT_ROWS
    small = lambda a: pl.BlockSpec(a.shape, lambda i: (0,) * a.ndim)
    return pl.pallas_call(
        functools.partial(_filt_time_kernel, seq_len=seq_len),
        grid=(seq_len // rows,),
        in_specs=[pl.BlockSpec((rows, emb2.shape[1]), lambda i: (i, 0)),
                  small(w1), small(b1), small(w2), small(b2), small(w3), small(b3), small(w4f), small(w4b),
                  small(freq), small(absdelta)],
        out_specs=[pl.BlockSpec((2, rows, c), lambda i: (0, i, 0)),
                   pl.BlockSpec((1, c), lambda i: (0, 0))],
        out_shape=[jax.ShapeDtypeStruct((2, seq_len, c), F32),
                   jax.ShapeDtypeStruct((1, c), F32)],
        compiler_params=_cparams(("arbitrary",)),
        name="filt_time",
    )(emb2, w1, b1, w2, b2, w3, b3, w4f, w4b, freq, absdelta)


@functools.lru_cache(maxsize=None)
def _fft_tables(seq_len):
    n = 2 * seq_len
    n1s = FFT_N1
    n2s = n // n1s
    half = n1s // 2
    k1n = half + 1
    n1 = np.arange(half)
    k1 = np.arange(k1n)
    th = 2.0 * np.pi * (np.outer(k1, n1) % n1s) / n1s
    herm = np.where((k1 == 0) | (k1 == half), 1.0, 2.0)
    fa = np.concatenate([np.cos(th), -np.sin(th)[1:half]], axis=0)
    sgn = np.concatenate([(-1.0) ** k1, (-1.0) ** k1[1:half]])
    fai = np.concatenate([np.cos(th) * herm[:, None], (-np.sin(th) * herm[:, None])[1:half]], axis=0).T / n
    eye = np.eye(SUBLANES)
    fk = np.kron(fa, eye)
    fks = np.kron(fa * sgn[:, None], eye)
    fki = np.kron(fai, eye)
    k2 = np.arange(n2s)
    n2 = np.arange(n2s)
    m = np.zeros((k1n, 2 * n2s, 2 * n2s))
    for a in range(k1n):
        ang = 2.0 * np.pi * (np.outer(a + n1s * k2, n2) % n) / n
        gr, gi = np.cos(ang), -np.sin(ang)
        m[a] = np.block([[gr, -gi], [gi, gr]])
    mt = np.transpose(m, (0, 2, 1))
    return dict(fk=fk, fkf=np.concatenate([fk, fks], axis=1), fki=fki, m=m, mt=mt,
                k1n=k1n, n2s=n2s, half=half)


def _stage_n1(src_refs, mat, dst_ref, n2s):
    rows = dst_ref.shape[0]
    for j in range(n2s // SUBLANES):
        sl = slice(j * SUBLANES, (j + 1) * SUBLANES)
        parts = [r[:, sl, :] for r in src_refs]
        xg = jnp.concatenate([p.reshape(p.shape[0] * SUBLANES, p.shape[2]) for p in parts], axis=0)
        a = jnp.dot(mat, xg.astype(BF16), preferred_element_type=F32)
        dst_ref[:, sl, :] = a.reshape(rows, SUBLANES, a.shape[1])


def _n2_input(a_s, k1, half, n2s):
    if 0 < k1 < half:
        return jnp.concatenate([a_s[k1], a_s[half + k1]], axis=0)
    return a_s[k1]


def _filt_spec_kernel(kern_ref, s_ref, fkf_ref, m_ref, kf_ref, a_s, *, k1n, n2s):
    half = k1n - 1
    _stage_n1([kern_ref.at[0], kern_ref.at[1]], fkf_ref[...], a_s, n2s)
    inv = 1.0 / (s_ref[...] + EPS)
    for k1 in range(k1n):
        a = _n2_input(a_s, k1, half, n2s).astype(BF16)
        kf_ref[k1] = jnp.dot(m_ref[k1, :, :a.shape[0]], a, preferred_element_type=F32) * inv


def _filt_spec_call(kern4, s, tabs):
    _, half, n2s, c = kern4.shape
    ct = HY_CT
    k1n = tabs["k1n"]
    return pl.pallas_call(
        functools.partial(_filt_spec_kernel, k1n=k1n, n2s=n2s),
        grid=(c // ct,),
        in_specs=[pl.BlockSpec((2, half, n2s, ct), lambda j: (0, 0, 0, j)),
                  pl.BlockSpec((1, ct), lambda j: (0, j)),
                  _const_spec(tabs["fkf"].shape), _const_spec(tabs["m"].shape)],
        out_specs=pl.BlockSpec((k1n, 2 * n2s, ct), lambda j: (0, 0, j)),
        out_shape=jax.ShapeDtypeStruct((k1n, 2 * n2s, c), F32),
        scratch_shapes=[pltpu.VMEM((2 * half, n2s, ct), F32)],
        compiler_params=_cparams(("arbitrary",)),
        name="filt_spec",
    )(kern4, s, tabs["fkf"], tabs["m"])


def _hyena_kernel(u_ref, x0_ref, bias_ref, kf_ref, fk_ref, fki_ref, m_ref, mt_ref, o_ref, a_s,
                  *, k1n, n2s):
    half = k1n - 1
    ct = u_ref.shape[2]
    _stage_n1([u_ref], fk_ref[...], a_s, n2s)

    for k1 in range(k1n):
        a = _n2_input(a_s, k1, half, n2s).astype(BF16)
        x = jnp.dot(m_ref[k1, :, :a.shape[0]], a, preferred_element_type=F32)
        xr, xi = x[:n2s], x[n2s:]
        kr, ki = kf_ref[k1, :n2s, :], kf_ref[k1, n2s:, :]
        y = jnp.concatenate([xr * kr - xi * ki, xr * ki + xi * kr], axis=0).astype(BF16)
        if 0 < k1 < half:
            z = jnp.dot(mt_ref[k1], y, preferred_element_type=F32)
            a_s[k1] = z[:n2s]
            a_s[half + k1] = z[n2s:]
        else:
            a_s[k1] = jnp.dot(mt_ref[k1, :n2s, :], y, preferred_element_type=F32)

    fki = fki_ref[...]
    bias = bias_ref[...]
    pair = 2 * SUBLANES
    for j in range(n2s // pair):
        ys = []
        for jj in range(2):
            sl = slice(j * pair + jj * SUBLANES, j * pair + (jj + 1) * SUBLANES)
            zg = a_s[:, sl, :].reshape(2 * half * SUBLANES, ct)
            ys.append(jnp.dot(fki, zg.astype(BF16), preferred_element_type=F32).reshape(half, SUBLANES, ct))
        y = jnp.concatenate(ys, axis=1)
        sl = slice(j * pair, (j + 1) * pair)
        o_ref[:, sl, :] = (x0_ref[:, sl, :] * (y + u_ref[:, sl, :] * bias)).astype(o_ref.dtype)


def _hyena_call(u4, x04, bias, kf, tabs):
    b, half, n2s, c = u4.shape
    ct = HY_CT
    k1n = tabs["k1n"]
    seq = pl.BlockSpec((None, half, n2s, ct), lambda j, i: (i, 0, 0, j))
    return pl.pallas_call(
        functools.partial(_hyena_kernel, k1n=k1n, n2s=n2s),
        grid=(c // ct, b),
        in_specs=[seq, seq,
                  pl.BlockSpec((1, ct), lambda j, i: (0, j)),
                  pl.BlockSpec((k1n, 2 * n2s, ct), lambda j, i: (0, 0, j), pipeline_mode=pl.Buffered(1)),
                  _const_spec(tabs["fk"].shape), _const_spec(tabs["fki"].shape),
                  _const_spec(tabs["m"].shape), _const_spec(tabs["mt"].shape)],
        out_specs=pl.BlockSpec((None, half, n2s, ct), lambda j, i: (i, 0, 0, j)),
        out_shape=jax.ShapeDtypeStruct((b, half, n2s, c), BF16),
        scratch_shapes=[pltpu.VMEM((2 * half, n2s, ct), F32)],
        compiler_params=_cparams(("arbitrary", "arbitrary")),
        name="hyena",
    )(u4, x04, bias, kf, tabs["fk"], tabs["fki"], tabs["m"], tabs["mt"])


def _out_ffn_kernel(ret_ref, hy_ref, x_ref, mod_ref, hg_ref, n2_ref, fn_ref, wo_ref, wgu_ref, wd_ref,
                    o_ref, *, d_ret, d_ff, ff_chunk):
    hy = hy_ref[...].astype(F32)
    hms = jnp.mean(hy * hy, axis=-1, keepdims=True)
    hyn = (hy * lax.rsqrt(hms + EPS) * hg_ref[...]).astype(BF16)
    mix = jnp.dot(ret_ref[...], wo_ref[:d_ret, :], preferred_element_type=F32)
    mix = mix + jnp.dot(hyn, wo_ref[d_ret:, :], preferred_element_type=F32)
    x1 = x_ref[...] + mod_ref[2:3, :] * mix
    ms = jnp.mean(x1 * x1, axis=-1, keepdims=True)
    h2 = ((x1 * lax.rsqrt(ms + EPS) * n2_ref[...]) * (1.0 + mod_ref[4:5, :]) + mod_ref[3:4, :]).astype(BF16)
    acc = jnp.zeros(x1.shape, F32)
    for j in range(d_ff // ff_chunk):
        g = jnp.dot(h2, wgu_ref[:, j * ff_chunk:(j + 1) * ff_chunk], preferred_element_type=F32)
        u = jnp.dot(h2, wgu_ref[:, d_ff + j * ff_chunk:d_ff + (j + 1) * ff_chunk], preferred_element_type=F32)
        a = (_silu(g) * u).astype(BF16)
        acc = acc + jnp.dot(a, wd_ref[j * ff_chunk:(j + 1) * ff_chunk, :], preferred_element_type=F32)
    x2 = x1 + mod_ref[5:6, :] * acc
    ms2 = jnp.mean(x2 * x2, axis=-1, keepdims=True)
    o_ref[...] = x2 * lax.rsqrt(ms2 + EPS) * fn_ref[...]


def _out_ffn_call(ret, hy, x, mod, hy_gain, norm2, final_norm, w_out, w_gu, w_down):
    b, l, d = x.shape
    d_ret = ret.shape[2]
    d_hy = hy.shape[2]
    d_ff = w_down.shape[0]
    tm = ROW_TILE
    return pl.pallas_call(
        functools.partial(_out_ffn_kernel, d_ret=d_ret, d_ff=d_ff, ff_chunk=d_ff // 2),
        grid=(b, l // tm),
        in_specs=[pl.BlockSpec((None, tm, d_ret), lambda i, t: (i, t, 0)),
                  pl.BlockSpec((None, tm, d_hy), lambda i, t: (i, t, 0)),
                  pl.BlockSpec((None, tm, d), lambda i, t: (i, t, 0)),
                  pl.BlockSpec((None, N_MOD, d), lambda i, t: (i, 0, 0)),
                  pl.BlockSpec((1, d_hy), lambda i, t: (0, 0)),
                  pl.BlockSpec((1, d), lambda i, t: (0, 0)),
                  pl.BlockSpec((1, d), lambda i, t: (0, 0)),
                  _const_spec(w_out.shape), _const_spec(w_gu.shape), _const_spec(w_down.shape)],
        out_specs=pl.BlockSpec((None, tm, d), lambda i, t: (i, t, 0)),
        out_shape=jax.ShapeDtypeStruct((b, l, d), x.dtype),
        compiler_params=_cparams(("arbitrary", "arbitrary")),
        name="out_ffn",
    )(ret, hy, x, mod, hy_gain, norm2, final_norm, w_out, w_gu, w_down)


@functools.lru_cache(maxsize=None)
def _rope_tables(seq_len, dh):
    n = dh // 4
    t = np.arange(seq_len)
    inv = ROPE_BASE ** (-np.arange(n, dtype=np.float64) / n)
    ang = np.concatenate([(t // GRID_W)[:, None] * inv, (t % GRID_W)[:, None] * inv], axis=-1)
    cc = np.concatenate([np.cos(ang), np.cos(ang)], axis=-1)
    ss = np.concatenate([-np.sin(ang), np.sin(ang)], axis=-1)
    return cc.astype(np.float32), ss.astype(np.float32)


@functools.lru_cache(maxsize=None)
def _filter_tables(seq_len, emb_dim, emb_pad, channels):
    t = np.linspace(0.0, 1.0, seq_len)[:, None]
    bands = (emb_dim - 1) // 2
    f = np.linspace(1e-4, bands - 1, bands)[None, :]
    wpos = 2.0 * np.pi * np.arange(seq_len)[:, None] / seq_len
    emb = np.concatenate([t, np.cos(f * wpos), -np.sin(f * wpos)], axis=-1)
    emb = np.pad(emb, ((0, 0), (0, emb_pad - emb_dim)))
    emb_rev = np.concatenate([emb[:1], emb[:0:-1]], axis=0)
    emb2 = np.concatenate([emb, emb_rev], axis=1)
    max_decay = math.log(FILTER_DECAY_TARGET) / FILTER_DECAY_FAST
    min_decay = math.log(FILTER_DECAY_TARGET) / FILTER_DECAY_SLOW
    absdelta = np.abs(np.linspace(min_decay, max_decay, channels))[None, :]
    return emb2.astype(np.float32), absdelta.astype(np.float32)


def _block_diag2(w):
    z = jnp.zeros_like(w)
    return jnp.concatenate([jnp.concatenate([w, z], axis=1), jnp.concatenate([z, w], axis=1)], axis=0)


def kernel(x, c, ctx, c_ctx, w_mod, b_mod, norm1, norm2, w_in, ret_decay, ret_gn_gain, hy_short_w,
           hy_short_b, hy_w1, hy_b1, hy_w2, hy_b2, hy_w3, hy_b3, hy_w4, hy_freq, hy_bias, hy_out_norm,
           w_out, w_gate_up, w_down, final_norm):
    b, seq_len, d = x.shape
    assert w_mod.shape[0] == 1, "single-layer block"
    heads = RET_HEADS
    d_ret = ret_gn_gain.shape[1]
    dh = d_ret // heads
    d_hy = hy_bias.shape[1]
    assert dh == LANES and d_hy % HY_CT == 0 and seq_len % (FFT_N1 // 2) == 0

    rows = -(-(b + 1) // SUBLANES) * SUBLANES
    c_rows = jnp.zeros((rows, d), F32).at[:b].set(c).at[b].set(c_ctx)
    mod_all = _mod_call(c_rows, w_mod[0], b_mod[0][None, :])
    mod = mod_all[:b].reshape(b, N_MOD, d)
    mod_c = mod_all[b].reshape(N_MOD, d)

    w_in_b = w_in[0].astype(BF16)
    dec = jnp.broadcast_to(ret_decay[0].reshape(2 * heads, 1), (2 * heads, LANES))
    n1g = norm1[0][None, :]

    s0 = _ctx_call(ctx, n1g, mod_c[0:1], mod_c[1:2], w_in_b[:, d_ret:3 * d_ret], dec, heads, dh)
    cc, ss = (jnp.asarray(t) for t in _rope_tables(seq_len, dh))
    qkvg, x0c, u = _inproj_call(x, mod, n1g, w_in_b, cc, ss, hy_short_w[0], hy_short_b, heads, dh, d_hy)
    ret = _ret_call(qkvg, s0, dec, ret_gn_gain, heads, dh)

    tabs = dict(_fft_tables(seq_len))
    for name in ("fk", "fkf", "fki", "m", "mt"):
        tabs[name] = jnp.asarray(tabs[name], dtype=F32).astype(BF16)
    half, n2s = tabs["half"], tabs["n2s"]
    emb_dim = hy_w1.shape[1]
    emb_pad = -(-emb_dim // SUBLANES) * SUBLANES
    emb2, absdelta = (jnp.asarray(t) for t in _filter_tables(seq_len, emb_dim, emb_pad, d_hy))
    w1p = jnp.pad(hy_w1[0], ((0, emb_pad - emb_dim), (0, 0)))
    two = lambda a: jnp.concatenate([a, a], axis=1)
    w4 = hy_w4[0]
    zero4 = jnp.zeros((w4.shape[0], d_hy), F32)
    kern, ksum = _filt_time_call(
        emb2, _block_diag2(w1p), two(hy_b1), _block_diag2(hy_w2[0]), two(hy_b2), _block_diag2(hy_w3[0]),
        two(hy_b3), jnp.concatenate([w4[:, :d_hy], zero4], axis=0), jnp.concatenate([zero4, w4[:, d_hy:]], axis=0),
        two(hy_freq), absdelta, seq_len)
    kf = _filt_spec_call(kern.reshape(2, half, n2s, d_hy), ksum, tabs)
    hy = _hyena_call(u.reshape(b, half, n2s, d_hy), x0c.reshape(b, half, n2s, d_hy), hy_bias, kf, tabs)
    hy = hy.reshape(b, seq_len, d_hy)

    return _out_ffn_call(ret, hy, x, mod, hy_out_norm, norm2[0][None, :], final_norm[None, :],
                         w_out[0].astype(BF16), w_gate_up[0].astype(BF16), w_down[0].astype(BF16))
```

```python
import functools
import math

import jax
import jax.numpy as jnp
import numpy as np
from jax import lax
from jax.experimental import pallas as pl
from jax.experimental.pallas import tpu as pltpu

F32 = jnp.float32
BF16 = jnp.bfloat16
HIGHEST = lax.Precision.HIGHEST

RET_HEADS = 4
GRID_W = 64
ROPE_BASE = 10000.0
N_MOD = 6
HYENA_PROJ = 3
FILTER_DECAY_FAST = 0.3
FILTER_DECAY_SLOW = 1.5
FILTER_DECAY_TARGET = 1e-2
EPS = 1e-6

LANES = 128
SUBLANES = 8
MXU_WIDTH = 256
VMEM_LIMIT_BYTES = 56 * 1024 * 1024

RET_CHUNK = 256
ROW_TILE = 512
IN_ROW_TILE = 512
HY_CT = 256
FILT_ROWS = 1024
FFT_N1 = 64


def _silu(v):
    return v / (1.0 + jnp.exp(-v))


def _cparams(sem, vmem=VMEM_LIMIT_BYTES):
    return pltpu.CompilerParams(dimension_semantics=sem, vmem_limit_bytes=vmem)


def _const_spec(shape):
    nd = len(shape)
    return pl.BlockSpec(shape, lambda *_: (0,) * nd, pipeline_mode=pl.Buffered(1))


def _mod_kernel(c_ref, w_ref, b_ref, o_ref):
    s = _silu(c_ref[...])
    o_ref[...] = jnp.dot(s, w_ref[...], precision=HIGHEST, preferred_element_type=F32) + b_ref[...]


def _mod_call(c_rows, w_mod, b_mod):
    rows, d = c_rows.shape
    n = w_mod.shape[1]
    tn = 1536
    return pl.pallas_call(
        _mod_kernel,
        grid=(n // tn,),
        in_specs=[pl.BlockSpec((rows, d), lambda j: (0, 0)),
                  pl.BlockSpec((d, tn), lambda j: (0, j)),
                  pl.BlockSpec((1, tn), lambda j: (0, j))],
        out_specs=pl.BlockSpec((rows, tn), lambda j: (0, j)),
        out_shape=jax.ShapeDtypeStruct((rows, n), F32),
        compiler_params=_cparams(("arbitrary",)),
        name="mod",
    )(c_rows, w_mod, b_mod)


def _ctx_kernel(ctx_ref, n1_ref, sh_ref, sc_ref, w_ref, dec_ref, s_ref, *, heads, dh):
    xc = ctx_ref[...]
    n_ctx = xc.shape[0]
    ms = jnp.mean(xc * xc, axis=-1, keepdims=True)
    hc = (xc * lax.rsqrt(ms + EPS) * n1_ref[...]) * (1.0 + sc_ref[...]) + sh_ref[...]
    kv = jnp.dot(hc.astype(BF16), w_ref[...], preferred_element_type=F32)
    lg = jnp.log1p(-jnp.exp(dec_ref[...]))
    pos = lax.broadcasted_iota(jnp.int32, (n_ctx, dh), 0).astype(F32)
    k_scale = dh ** -0.5
    d_ret = heads * dh
    tdims = (((0,), (0,)), ((), ()))
    for h in range(heads):
        kh = kv[:, h * dh:(h + 1) * dh] * k_scale
        vh = kv[:, d_ret + h * dh:d_ret + (h + 1) * dh].astype(BF16)
        wf = jnp.exp(lg[h:h + 1, :] * (n_ctx - 1.0 - pos))
        wb = jnp.exp(lg[heads + h:heads + h + 1, :] * pos)
        s_ref[h] = lax.dot_general((kh * wf).astype(BF16), vh, tdims, preferred_element_type=F32)
        s_ref[heads + h] = lax.dot_general((kh * wb).astype(BF16), vh, tdims, preferred_element_type=F32)


def _ctx_call(ctx, norm1, shift_c, scale_c, w_kv, dec, heads, dh):
    b, n_ctx, d = ctx.shape
    return pl.pallas_call(
        functools.partial(_ctx_kernel, heads=heads, dh=dh),
        grid=(b,),
        in_specs=[pl.BlockSpec((None, n_ctx, d), lambda i: (i, 0, 0)),
                  pl.BlockSpec((1, d), lambda i: (0, 0)),
                  pl.BlockSpec((1, d), lambda i: (0, 0)),
                  pl.BlockSpec((1, d), lambda i: (0, 0)),
                  pl.BlockSpec(w_kv.shape, lambda i: (0, 0)),
                  pl.BlockSpec(dec.shape, lambda i: (0, 0))],
        out_specs=pl.BlockSpec((None, 2 * heads, dh, dh), lambda i: (i, 0, 0, 0)),
        out_shape=jax.ShapeDtypeStruct((b, 2 * heads, dh, dh), F32),
        compiler_params=_cparams(("arbitrary",)),
        name="ctx_state",
    )(ctx, norm1, shift_c, scale_c, w_kv, dec)


def _inproj_kernel(x_ref, mod_ref, n1_ref, w_ref, cc_ref, ss_ref, sw_ref, sb_ref,
                   qkvg_ref, x0_ref, u_ref, hy_s, tail_s, *, heads, dh, d_hy, nt):
    t = pl.program_id(1)
    tm = x_ref.shape[0]
    d_ret = heads * dh

    @pl.when((pl.program_id(0) == 0) & (t == 0))
    def _():
        hy_s[...] = jnp.zeros(hy_s.shape, F32)
        tail_s[...] = jnp.zeros(tail_s.shape, F32)

    r8 = lax.broadcasted_iota(jnp.int32, (SUBLANES, d_hy), 0)

    def conv(cur, prev_row, next_row, sl):
        w0, w1, w2, bb = sw_ref[0:1, sl], sw_ref[1:2, sl], sw_ref[2:3, sl], sb_ref[:, sl]
        up = pltpu.roll(cur, 1, 0)
        dn = pltpu.roll(cur, tm - 1, 0)
        body = w0 * up + w1 * cur + w2 * dn + bb
        e = tm - SUBLANES
        head = (w0 * jnp.where(r8 == 0, prev_row, up[:SUBLANES]) + w1 * cur[:SUBLANES]
                + w2 * dn[:SUBLANES] + bb)
        tail = (w0 * up[e:] + w1 * cur[e:]
                + w2 * jnp.where(r8 == SUBLANES - 1, next_row, dn[e:]) + bb)
        return jnp.concatenate([head, body[SUBLANES:e], tail], axis=0)

    @pl.when(t < nt)
    def _():
        x = x_ref[...]
        ms = jnp.mean(x * x, axis=-1, keepdims=True)
        h = (x * lax.rsqrt(ms + EPS) * n1_ref[...]) * (1.0 + mod_ref[1:2, :]) + mod_ref[0:1, :]
        hb = h.astype(BF16)
        outs = []
        for j in range(HYENA_PROJ):
            sl = slice(j * d_hy, (j + 1) * d_hy)
            cur = hy_s[:, sl]
            new = jnp.dot(hb, w_ref[:, 4 * d_ret + j * d_hy:4 * d_ret + (j + 1) * d_hy],
                          preferred_element_type=F32)
            outs.append(conv(cur, tail_s[SUBLANES - 1:SUBLANES, sl], new[0:1], sl))
            tail_s[:, sl] = jnp.where(t == 0, 0.0, cur[tm - SUBLANES:])
            hy_s[:, sl] = new
        x0_ref[...] = outs[0]
        u_ref[...] = outs[1] * outs[2]
        cc = cc_ref[...]
        ss = ss_ref[...]
        k_scale = dh ** -0.5
        q = jnp.dot(hb, w_ref[:, :d_ret], preferred_element_type=F32)
        k = jnp.dot(hb, w_ref[:, d_ret:2 * d_ret], preferred_element_type=F32)
        for j in range(heads):
            sl = slice(j * dh, (j + 1) * dh)
            qj, kj = q[:, sl], k[:, sl]
            qkvg_ref[:, sl] = (qj * cc + pltpu.roll(qj, dh // 2, 1) * ss).astype(BF16)
            qkvg_ref[:, d_ret + j * dh:d_ret + (j + 1) * dh] = (
                (kj * cc + pltpu.roll(kj, dh // 2, 1) * ss) * k_scale).astype(BF16)
        sl = slice(2 * d_ret, 3 * d_ret)
        qkvg_ref[:, sl] = jnp.dot(hb, w_ref[:, sl], preferred_element_type=F32).astype(BF16)
        sl = slice(3 * d_ret, 4 * d_ret)
        qkvg_ref[:, sl] = _silu(jnp.dot(hb, w_ref[:, sl], preferred_element_type=F32)).astype(BF16)

    @pl.when(t == nt)
    def _():
        zero = jnp.zeros((1, d_hy), F32)
        outs = []
        for j in range(HYENA_PROJ):
            sl = slice(j * d_hy, (j + 1) * d_hy)
            outs.append(conv(hy_s[:, sl], tail_s[SUBLANES - 1:SUBLANES, sl], zero, sl))
        x0_ref[...] = outs[0]
        u_ref[...] = outs[1] * outs[2]


def _inproj_call(x, mod, norm1, w_in, cc, ss, short_w, short_b, heads, dh, d_hy):
    b, l, d = x.shape
    tm = IN_ROW_TILE
    nt = l // tm
    d_ret = heads * dh
    cur = lambda i, t: (i, jnp.minimum(t, nt - 1), 0)
    lag = lambda i, t: (i, jnp.maximum(t - 1, 0), 0)
    return pl.pallas_call(
        functools.partial(_inproj_kernel, heads=heads, dh=dh, d_hy=d_hy, nt=nt),
        grid=(b, nt + 1),
        in_specs=[pl.BlockSpec((None, tm, d), cur),
                  pl.BlockSpec((None, N_MOD, d), lambda i, t: (i, 0, 0)),
                  pl.BlockSpec((1, d), lambda i, t: (0, 0)),
                  _const_spec(w_in.shape),
                  pl.BlockSpec((tm, dh), lambda i, t: (jnp.minimum(t, nt - 1), 0)),
                  pl.BlockSpec((tm, dh), lambda i, t: (jnp.minimum(t, nt - 1), 0)),
                  pl.BlockSpec(short_w.shape, lambda i, t: (0, 0)),
                  pl.BlockSpec(short_b.shape, lambda i, t: (0, 0))],
        out_specs=[pl.BlockSpec((None, tm, 4 * d_ret), cur),
                   pl.BlockSpec((None, tm, d_hy), lag),
                   pl.BlockSpec((None, tm, d_hy), lag)],
        out_shape=[jax.ShapeDtypeStruct((b, l, 4 * d_ret), BF16),
                   jax.ShapeDtypeStruct((b, l, d_hy), F32),
                   jax.ShapeDtypeStruct((b, l, d_hy), F32)],
        scratch_shapes=[pltpu.VMEM((tm, HYENA_PROJ * d_hy), F32),
                        pltpu.VMEM((SUBLANES, HYENA_PROJ * d_hy), F32)],
        compiler_params=_cparams(("arbitrary", "arbitrary")),
        name="in_proj",
    )(x, mod, norm1, w_in, cc, ss, short_w, short_b)


def _ret_kernel(q_ref, k_ref, v_ref, g_ref, sf0_ref, sb0_ref, dec_ref, gain_ref,
                o_ref, kvf_s, kvb_s, st_s, *, heads, chunk):
    l, dh = q_ref.shape
    nc = l // chunk
    h = pl.program_id(1)
    lgf = jnp.log1p(-jnp.exp(dec_ref[pl.ds(h, 1), :]))
    lgb = jnp.log1p(-jnp.exp(dec_ref[pl.ds(h + heads, 1), :]))
    il = lax.broadcasted_iota(jnp.int32, (chunk, dh), 0).astype(F32)
    kw_f = jnp.exp(lgf * (chunk - 1.0 - il))
    kw_b = jnp.exp(lgb * il)
    qw_f = jnp.exp(lgf * (il + 1.0))
    qw_b = jnp.exp(lgb * (chunk - il))
    cd_f = jnp.exp(lgf * float(chunk))
    cd_b = jnp.exp(lgb * float(chunk))
    reps = chunk // dh
    lgf_c = jnp.concatenate([lgf] * reps, axis=1)
    lgb_c = jnp.concatenate([lgb] * reps, axis=1)
    ii = lax.broadcasted_iota(jnp.int32, (chunk, chunk), 0)
    jj = lax.broadcasted_iota(jnp.int32, (chunk, chunk), 1)
    diff = (ii - jj).astype(F32)
    dmask = (jnp.where(diff >= 0, jnp.exp(lgf_c * jnp.maximum(diff, 0.0)), 0.0)
             + jnp.where(diff <= 0, jnp.exp(lgb_c * jnp.maximum(-diff, 0.0)), 0.0))
    tdims = (((0,), (0,)), ((), ()))
    ntdims = (((1,), (1,)), ((), ()))

    def chunk_states(n, carry):
        r0 = pl.multiple_of(n * chunk, chunk)
        kr = k_ref[pl.ds(r0, chunk), :].astype(F32)
        vv = v_ref[pl.ds(r0, chunk), :]
        kvf_s[n] = lax.dot_general((kr * kw_f).astype(BF16), vv, tdims, preferred_element_type=F32)
        kvb_s[n] = lax.dot_general((kr * kw_b).astype(BF16), vv, tdims, preferred_element_type=F32)
        return carry

    lax.fori_loop(0, nc, chunk_states, 0, unroll=True)

    def scan_f(n, s):
        st_s[n, :dh, :] = s.astype(BF16)
        return s * cd_f + kvf_s[n]

    lax.fori_loop(0, nc, scan_f, sf0_ref[...])

    def scan_b(m, s):
        n = nc - 1 - m
        st_s[n, dh:, :] = s.astype(BF16)
        return s * cd_b + kvb_s[n]

    lax.fori_loop(0, nc, scan_b, sb0_ref[...])

    gain = gain_ref[...]

    def chunk_out(n, carry):
        r0 = pl.multiple_of(n * chunk, chunk)
        qb = q_ref[pl.ds(r0, chunk), :]
        qr = qb.astype(F32)
        sc = lax.dot_general(qb, k_ref[pl.ds(r0, chunk), :], ntdims, preferred_element_type=F32)
        lhs = jnp.concatenate([(qr * qw_f).astype(BF16), (qr * qw_b).astype(BF16),
                               (sc * dmask).astype(BF16)], axis=1)
        rhs = jnp.concatenate([st_s[n], v_ref[pl.ds(r0, chunk), :]], axis=0)
        o = jnp.dot(lhs, rhs, preferred_element_type=F32)
        mu = jnp.mean(o, axis=-1, keepdims=True)
        d = o - mu
        var = jnp.mean(d * d, axis=-1, keepdims=True)
        gg = g_ref[pl.ds(r0, chunk), :].astype(F32)
        o_ref[pl.ds(r0, chunk), :] = (d * lax.rsqrt(var + EPS) * gain * gg).astype(o_ref.dtype)
        return carry

    lax.fori_loop(0, nc, chunk_out, 0, unroll=True)


def _ret_call(qkvg, s0, dec, gn_gain, heads, dh):
    b, l, _ = qkvg.shape
    chunk = RET_CHUNK
    nc = l // chunk
    seq = lambda off: pl.BlockSpec((None, l, dh), lambda i, h: (i, 0, off + h))
    return pl.pallas_call(
        functools.partial(_ret_kernel, heads=heads, chunk=chunk),
        grid=(b, heads),
        in_specs=[seq(0), seq(heads), seq(2 * heads), seq(3 * heads),
                  pl.BlockSpec((None, None, dh, dh), lambda i, h: (i, h, 0, 0)),
                  pl.BlockSpec((None, None, dh, dh), lambda i, h: (i, heads + h, 0, 0)),
                  pl.BlockSpec(dec.shape, lambda i, h: (0, 0)),
                  pl.BlockSpec((1, dh), lambda i, h: (0, h))],
        out_specs=pl.BlockSpec((None, l, dh), lambda i, h: (i, 0, h)),
        out_shape=jax.ShapeDtypeStruct((b, l, heads * dh), BF16),
        scratch_shapes=[pltpu.VMEM((nc, dh, dh), F32), pltpu.VMEM((nc, dh, dh), F32),
                        pltpu.VMEM((nc, 2 * dh, dh), BF16)],
        compiler_params=_cparams(("arbitrary", "arbitrary")),
        name="retention",
    )(qkvg, qkvg, qkvg, qkvg, s0, s0, dec, gn_gain)


def _filt_time_kernel(emb_ref, w1_ref, b1_ref, w2_ref, b2_ref, w3_ref, b3_ref, w4f_ref, w4b_ref,
                      fr_ref, dl_ref, kern_ref, s_ref, *, seq_len):
    i = pl.program_id(0)
    rows, c = kern_ref.shape[1], kern_ref.shape[2]
    fr = fr_ref[...]
    hdot = functools.partial(jnp.dot, precision=HIGHEST, preferred_element_type=F32)
    z = jnp.sin(fr * (hdot(emb_ref[...], w1_ref[...]) + b1_ref[...]))
    z = jnp.sin(fr * (hdot(z, w2_ref[...]) + b2_ref[...]))
    z = jnp.sin(fr * (hdot(z, w3_ref[...]) + b3_ref[...]))
    pos = (i * rows + lax.broadcasted_iota(jnp.int32, (rows, c), 0)).astype(F32)
    inv = 1.0 / (seq_len - 1.0)
    adl = dl_ref[...]
    hf = hdot(z, w4f_ref[...]) * jnp.exp(-(pos * inv) * adl)
    hr = hdot(z, w4b_ref[...]) * jnp.exp(-((seq_len - pos) * inv) * adl)
    hr = jnp.where(pos == 0.0, 0.0, hr)
    kern_ref[0] = hf
    kern_ref[1] = hr
    part = jnp.sum(jnp.abs(hf) + jnp.abs(hr), axis=0, keepdims=True)

    @pl.when(i == 0)
    def _():
        s_ref[...] = part

    @pl.when(i != 0)
    def _():
        s_ref[...] += part


def _filt_time_call(emb2, w1, b1, w2, b2, w3, b3, w4f, w4b, freq, absdelta, seq_len):
    c = absdelta.shape[1]
    rows = FILT_ROWS
    small = lambda a: pl.BlockSpec(a.shape, lambda i: (0,) * a.ndim)
    return pl.pallas_call(
        functools.partial(_filt_time_kernel, seq_len=seq_len),
        grid=(seq_len // rows,),
        in_specs=[pl.BlockSpec((rows, emb2.shape[1]), lambda i: (i, 0)),
                  small(w1), small(b1), small(w2), small(b2), small(w3), small(b3), small(w4f), small(w4b),
                  small(freq), small(absdelta)],
        out_specs=[pl.BlockSpec((2, rows, c), lambda i: (0, i, 0)),
                   pl.BlockSpec((1, c), lambda i: (0, 0))],
        out_shape=[jax.ShapeDtypeStruct((2, seq_len, c), F32),
                   jax.ShapeDtypeStruct((1, c), F32)],
        compiler_params=_cparams(("arbitrary",)),
        name="filt_time",
    )(emb2, w1, b1, w2, b2, w3, b3, w4f, w4b, freq, absdelta)


@functools.lru_cache(maxsize=None)
def _fft_tables(seq_len):
    n = 2 * seq_len
    n1s = FFT_N1
    n2s = n // n1s
    half = n1s // 2
    k1n = half + 1
    n1 = np.arange(half)
    k1 = np.arange(k1n)
    th = 2.0 * np.pi * (np.outer(k1, n1) % n1s) / n1s
    herm = np.where((k1 == 0) | (k1 == half), 1.0, 2.0)
    fa = np.concatenate([np.cos(th), -np.sin(th)[1:half]], axis=0)
    sgn = np.concatenate([(-1.0) ** k1, (-1.0) ** k1[1:half]])
    fai = np.concatenate([np.cos(th) * herm[:, None], (-np.sin(th) * herm[:, None])[1:half]], axis=0).T / n
    eye = np.eye(SUBLANES)
    fk = np.kron(fa, eye)
    fks = np.kron(fa * sgn[:, None], eye)
    fki = np.kron(fai, eye)
    k2 = np.arange(n2s)
    n2 = np.arange(n2s)
    m = np.zeros((k1n, 2 * n2s, 2 * n2s))
    for a in range(k1n):
        ang = 2.0 * np.pi * (np.outer(a + n1s * k2, n2) % n) / n
        gr, gi = np.cos(ang), -np.sin(ang)
        m[a] = np.block([[gr, -gi], [gi, gr]])
    mt = np.transpose(m, (0, 2, 1))
    return dict(fk=fk, fkf=np.concatenate([fk, fks], axis=1), fki=fki, m=m, mt=mt,
                k1n=k1n, n2s=n2s, half=half)


def _stage_n1(src_refs, mat, dst_ref, n2s):
    rows = dst_ref.shape[0]
    for j in range(n2s // SUBLANES):
        sl = slice(j * SUBLANES, (j + 1) * SUBLANES)
        parts = [r[:, sl, :] for r in src_refs]
        xg = jnp.concatenate([p.reshape(p.shape[0] * SUBLANES, p.shape[2]) for p in parts], axis=0)
        a = jnp.dot(mat, xg.astype(BF16), preferred_element_type=F32)
        dst_ref[:, sl, :] = a.reshape(rows, SUBLANES, a.shape[1])


def _n2_input(a_s, k1, half, n2s):
    if 0 < k1 < half:
        return jnp.concatenate([a_s[k1], a_s[half + k1]], axis=0)
    return a_s[k1]


def _filt_spec_kernel(kern_ref, s_ref, fkf_ref, m_ref, kf_ref, a_s, *, k1n, n2s):
    half = k1n - 1
    _stage_n1([kern_ref.at[0], kern_ref.at[1]], fkf_ref[...], a_s, n2s)
    inv = 1.0 / (s_ref[...] + EPS)
    for k1 in range(k1n):
        a = _n2_input(a_s, k1, half, n2s).astype(BF16)
        kf_ref[k1] = jnp.dot(m_ref[k1, :, :a.shape[0]], a, preferred_element_type=F32) * inv


def _filt_spec_call(kern4, s, tabs):
    _, half, n2s, c = kern4.shape
    ct = HY_CT
    k1n = tabs["k1n"]
    return pl.pallas_call(
        functools.partial(_filt_spec_kernel, k1n=k1n, n2s=n2s),
        grid=(c // ct,),
        in_specs=[pl.BlockSpec((2, half, n2s, ct), lambda j: (0, 0, 0, j)),
                  pl.BlockSpec((1, ct), lambda j: (0, j)),
                  _const_spec(tabs["fkf"].shape), _const_spec(tabs["m"].shape)],
        out_specs=pl.BlockSpec((k1n, 2 * n2s, ct), lambda j: (0, 0, j)),
        out_shape=jax.ShapeDtypeStruct((k1n, 2 * n2s, c), F32),
        scratch_shapes=[pltpu.VMEM((2 * half, n2s, ct), F32)],
        compiler_params=_cparams(("arbitrary",)),
        name="filt_spec",
    )(kern4, s, tabs["fkf"], tabs["m"])


def _hyena_kernel(u_ref, x0_ref, bias_ref, kf_ref, fk_ref, fki_ref, m_ref, mt_ref, o_ref, a_s,
                  *, k1n, n2s):
    half = k1n - 1
    ct = u_ref.shape[2]
    _stage_n1([u_ref], fk_ref[...], a_s, n2s)

    for k1 in range(k1n):
        a = _n2_input(a_s, k1, half, n2s).astype(BF16)
        x = jnp.dot(m_ref[k1, :, :a.shape[0]], a, preferred_element_type=F32)
        xr, xi = x[:n2s], x[n2s:]
        kr, ki = kf_ref[k1, :n2s, :], kf_ref[k1, n2s:, :]
        y = jnp.concatenate([xr * kr - xi * ki, xr * ki + xi * kr], axis=0).astype(BF16)
        if 0 < k1 < half:
            z = jnp.dot(mt_ref[k1], y, preferred_element_type=F32)
            a_s[k1] = z[:n2s]
            a_s[half + k1] = z[n2s:]
        else:
            a_s[k1] = jnp.dot(mt_ref[k1, :n2s, :], y, preferred_element_type=F32)

    fki = fki_ref[...]
    bias = bias_ref[...]
    pair = 2 * SUBLANES
    for j in range(n2s // pair):
        ys = []
        for jj in range(2):
            sl = slice(j * pair + jj * SUBLANES, j * pair + (jj + 1) * SUBLANES)
            zg = a_s[:, sl, :].reshape(2 * half * SUBLANES, ct)
            ys.append(jnp.dot(fki, zg.astype(BF16), preferred_element_type=F32).reshape(half, SUBLANES, ct))
        y = jnp.concatenate(ys, axis=1)
        sl = slice(j * pair, (j + 1) * pair)
        o_ref[:, sl, :] = (x0_ref[:, sl, :] * (y + u_ref[:, sl, :] * bias)).astype(o_ref.dtype)


def _hyena_call(u4, x04, bias, kf, tabs):
    b, half, n2s, c = u4.shape
    ct = HY_CT
    k1n = tabs["k1n"]
    seq = pl.BlockSpec((None, half, n2s, ct), lambda j, i: (i, 0, 0, j))
    return pl.pallas_call(
        functools.partial(_hyena_kernel, k1n=k1n, n2s=n2s),
        grid=(c // ct, b),
        in_specs=[seq, seq,
                  pl.BlockSpec((1, ct), lambda j, i: (0, j)),
                  pl.BlockSpec((k1n, 2 * n2s, ct), lambda j, i: (0, 0, j), pipeline_mode=pl.Buffered(1)),
                  _const_spec(tabs["fk"].shape), _const_spec(tabs["fki"].shape),
                  _const_spec(tabs["m"].shape), _const_spec(tabs["mt"].shape)],
        out_specs=pl.BlockSpec((None, half, n2s, ct), lambda j, i: (i, 0, 0, j)),
        out_shape=jax.ShapeDtypeStruct((b, half, n2s, c), BF16),
        scratch_shapes=[pltpu.VMEM((2 * half, n2s, ct), F32)],
        compiler_params=_cparams(("arbitrary", "arbitrary")),
        name="hyena",
    )(u4, x04, bias, kf, tabs["fk"], tabs["fki"], tabs["m"], tabs["mt"])


def _out_ffn_kernel(ret_ref, hy_ref, x_ref, mod_ref, hg_ref, n2_ref, fn_ref, wo_ref, wgu_ref, wd_ref,
                    o_ref, *, d_ret, d_ff, ff_bounds, row_parts):
    rows = x_ref.shape[0] // row_parts
    for r in range(row_parts):
        rs = slice(r * rows, (r + 1) * rows)
        hy = hy_ref[rs, :].astype(F32)
        hms = jnp.mean(hy * hy, axis=-1, keepdims=True)
        hyn = (hy * lax.rsqrt(hms + EPS) * hg_ref[...]).astype(BF16)
        mix = jnp.dot(ret_ref[rs, :], wo_ref[:d_ret, :], preferred_element_type=F32)
        mix = mix + jnp.dot(hyn, wo_ref[d_ret:, :], preferred_element_type=F32)
        x1 = x_ref[rs, :] + mod_ref[2:3, :] * mix
        ms = jnp.mean(x1 * x1, axis=-1, keepdims=True)
        h2 = ((x1 * lax.rsqrt(ms + EPS) * n2_ref[...]) * (1.0 + mod_ref[4:5, :])
              + mod_ref[3:4, :]).astype(BF16)
        acc = jnp.zeros(x1.shape, F32)
        for lo, hi in zip(ff_bounds[:-1], ff_bounds[1:]):
            g = jnp.dot(h2, wgu_ref[:, lo:hi], preferred_element_type=F32)
            u = jnp.dot(h2, wgu_ref[:, d_ff + lo:d_ff + hi], preferred_element_type=F32)
            a = (_silu(g) * u).astype(BF16)
            acc = acc + jnp.dot(a, wd_ref[lo:hi, :], preferred_element_type=F32)
        x2 = x1 + mod_ref[5:6, :] * acc
        ms2 = jnp.mean(x2 * x2, axis=-1, keepdims=True)
        o_ref[rs, :] = x2 * lax.rsqrt(ms2 + EPS) * fn_ref[...]


def _ff_bounds(d_ff, parts=2):
    tiles = d_ff // MXU_WIDTH
    assert tiles * MXU_WIDTH == d_ff
    cuts = [-(-tiles * p // parts) for p in range(parts + 1)]
    return tuple(c * MXU_WIDTH for c in cuts)


def _out_ffn_call(ret, hy, x, mod, hy_gain, norm2, final_norm, w_out, w_gu, w_down):
    b, l, d = x.shape
    d_ret = ret.shape[2]
    d_hy = hy.shape[2]
    d_ff = w_down.shape[0]
    tm = ROW_TILE
    return pl.pallas_call(
        functools.partial(_out_ffn_kernel, d_ret=d_ret, d_ff=d_ff, ff_bounds=_ff_bounds(d_ff),
                          row_parts=1),
        grid=(b, l // tm),
        in_specs=[pl.BlockSpec((None, tm, d_ret), lambda i, t: (i, t, 0)),
                  pl.BlockSpec((None, tm, d_hy), lambda i, t: (i, t, 0)),
                  pl.BlockSpec((None, tm, d), lambda i, t: (i, t, 0)),
                  pl.BlockSpec((None, N_MOD, d), lambda i, t: (i, 0, 0)),
                  pl.BlockSpec((1, d_hy), lambda i, t: (0, 0)),
                  pl.BlockSpec((1, d), lambda i, t: (0, 0)),
                  pl.BlockSpec((1, d), lambda i, t: (0, 0)),
                  _const_spec(w_out.shape), _const_spec(w_gu.shape), _const_spec(w_down.shape)],
        out_specs=pl.BlockSpec((None, tm, d), lambda i, t: (i, t, 0)),
        out_shape=jax.ShapeDtypeStruct((b, l, d), x.dtype),
        compiler_params=_cparams(("arbitrary", "arbitrary")),
        name="out_ffn",
    )(ret, hy, x, mod, hy_gain, norm2, final_norm, w_out, w_gu, w_down)


@functools.lru_cache(maxsize=None)
def _rope_tables(seq_len, dh):
    n = dh // 4
    t = np.arange(seq_len)
    inv = ROPE_BASE ** (-np.arange(n, dtype=np.float64) / n)
    ang = np.concatenate([(t // GRID_W)[:, None] * inv, (t % GRID_W)[:, None] * inv], axis=-1)
    cc = np.concatenate([np.cos(ang), np.cos(ang)], axis=-1)
    ss = np.concatenate([-np.sin(ang), np.sin(ang)], axis=-1)
    return cc.astype(np.float32), ss.astype(np.float32)


@functools.lru_cache(maxsize=None)
def _filter_tables(seq_len, emb_dim, emb_pad, channels):
    t = np.linspace(0.0, 1.0, seq_len)[:, None]
    bands = (emb_dim - 1) // 2
    f = np.linspace(1e-4, bands - 1, bands)[None, :]
    wpos = 2.0 * np.pi * np.arange(seq_len)[:, None] / seq_len
    emb = np.concatenate([t, np.cos(f * wpos), -np.sin(f * wpos)], axis=-1)
    emb = np.pad(emb, ((0, 0), (0, emb_pad - emb_dim)))
    emb_rev = np.concatenate([emb[:1], emb[:0:-1]], axis=0)
    emb2 = np.concatenate([emb, emb_rev], axis=1)
    max_decay = math.log(FILTER_DECAY_TARGET) / FILTER_DECAY_FAST
    min_decay = math.log(FILTER_DECAY_TARGET) / FILTER_DECAY_SLOW
    absdelta = np.abs(np.linspace(min_decay, max_decay, channels))[None, :]
    return emb2.astype(np.float32), absdelta.astype(np.float32)


def _block_diag2(w):
    z = jnp.zeros_like(w)
    return jnp.concatenate([jnp.concatenate([w, z], axis=1), jnp.concatenate([z, w], axis=1)], axis=0)


def kernel(x, c, ctx, c_ctx, w_mod, b_mod, norm1, norm2, w_in, ret_decay, ret_gn_gain, hy_short_w,
           hy_short_b, hy_w1, hy_b1, hy_w2, hy_b2, hy_w3, hy_b3, hy_w4, hy_freq, hy_bias, hy_out_norm,
           w_out, w_gate_up, w_down, final_norm):
    b, seq_len, d = x.shape
    assert w_mod.shape[0] == 1, "single-layer block"
    heads = RET_HEADS
    d_ret = ret_gn_gain.shape[1]
    dh = d_ret // heads
    d_hy = hy_bias.shape[1]
    assert dh == LANES and d_hy % HY_CT == 0 and seq_len % (FFT_N1 // 2) == 0

    rows = -(-(b + 1) // SUBLANES) * SUBLANES
    c_rows = jnp.zeros((rows, d), F32).at[:b].set(c).at[b].set(c_ctx)
    mod_all = _mod_call(c_rows, w_mod[0], b_mod[0][None, :])
    mod = mod_all[:b].reshape(b, N_MOD, d)
    mod_c = mod_all[b].reshape(N_MOD, d)

    w_in_b = w_in[0].astype(BF16)
    dec = jnp.broadcast_to(ret_decay[0].reshape(2 * heads, 1), (2 * heads, LANES))
    n1g = norm1[0][None, :]

    s0 = _ctx_call(ctx, n1g, mod_c[0:1], mod_c[1:2], w_in_b[:, d_ret:3 * d_ret], dec, heads, dh)
    cc, ss = (jnp.asarray(t) for t in _rope_tables(seq_len, dh))
    qkvg, x0c, u = _inproj_call(x, mod, n1g, w_in_b, cc, ss, hy_short_w[0], hy_short_b, heads, dh, d_hy)
    ret = _ret_call(qkvg, s0, dec, ret_gn_gain, heads, dh)

    tabs = dict(_fft_tables(seq_len))
    for name in ("fk", "fkf", "fki", "m", "mt"):
        tabs[name] = jnp.asarray(tabs[name], dtype=F32).astype(BF16)
    half, n2s = tabs["half"], tabs["n2s"]
    emb_dim = hy_w1.shape[1]
    emb_pad = -(-emb_dim // SUBLANES) * SUBLANES
    emb2, absdelta = (jnp.asarray(t) for t in _filter_tables(seq_len, emb_dim, emb_pad, d_hy))
    w1p = jnp.pad(hy_w1[0], ((0, emb_pad - emb_dim), (0, 0)))
    two = lambda a: jnp.concatenate([a, a], axis=1)
    w4 = hy_w4[0]
    zero4 = jnp.zeros((w4.shape[0], d_hy), F32)
    kern, ksum = _filt_time_call(
        emb2, _block_diag2(w1p), two(hy_b1), _block_diag2(hy_w2[0]), two(hy_b2), _block_diag2(hy_w3[0]),
        two(hy_b3), jnp.concatenate([w4[:, :d_hy], zero4], axis=0), jnp.concatenate([zero4, w4[:, d_hy:]], axis=0),
        two(hy_freq), absdelta, seq_len)
    kf = _filt_spec_call(kern.reshape(2, half, n2s, d_hy), ksum, tabs)
    hy = _hyena_call(u.reshape(b, half, n2s, d_hy), x0c.reshape(b, half, n2s, d_hy), hy_bias, kf, tabs)
    hy = hy.reshape(b, seq_len, d_hy)

    return _out_ffn_call(ret, hy, x, mod, hy_out_norm, norm2[0][None, :], final_norm[None, :],
                         w_out[0].astype(BF16), w_gate_up[0].astype(BF16), w_down[0].astype(BF16))
```

```python
import functools
import math

import jax
import jax.numpy as jnp
import numpy as np
from jax import lax
from jax.experimental import pallas as pl
from jax.experimental.pallas import tpu as pltpu

F32 = jnp.float32
BF16 = jnp.bfloat16
HIGHEST = lax.Precision.HIGHEST

RET_HEADS = 4
GRID_W = 64
ROPE_BASE = 10000.0
N_MOD = 6
HYENA_PROJ = 3
FILTER_DECAY_FAST = 0.3
FILTER_DECAY_SLOW = 1.5
FILTER_DECAY_TARGET = 1e-2
EPS = 1e-6

LANES = 128
SUBLANES = 8
MXU_WIDTH = 256
VMEM_LIMIT_BYTES = 56 * 1024 * 1024

RET_CHUNK = 256
ROW_TILE = 512
IN_ROW_TILE = 512
HY_CT = 256
FILT_ROWS = 1024
FFT_N1 = 64


def _silu(v):
    return v / (1.0 + jnp.exp(-v))


def _dot3(a, b):
    a_hi = a.astype(BF16)
    b_hi = b.astype(BF16)
    a_lo = (a - a_hi.astype(F32)).astype(BF16)
    b_lo = (b - b_hi.astype(F32)).astype(BF16)
    dot = functools.partial(jnp.dot, preferred_element_type=F32)
    return dot(a_hi, b_hi) + (dot(a_hi, b_lo) + dot(a_lo, b_hi))


def _cparams(sem, vmem=VMEM_LIMIT_BYTES):
    return pltpu.CompilerParams(dimension_semantics=sem, vmem_limit_bytes=vmem)


def _const_spec(shape):
    nd = len(shape)
    return pl.BlockSpec(shape, lambda *_: (0,) * nd, pipeline_mode=pl.Buffered(1))


def _mod_kernel(c_ref, w_ref, b_ref, o_ref):
    s = _silu(c_ref[...])
    o_ref[...] = jnp.dot(s, w_ref[...], precision=HIGHEST, preferred_element_type=F32) + b_ref[...]


def _mod_call(c_rows, w_mod, b_mod):
    rows, d = c_rows.shape
    n = w_mod.shape[1]
    tn = 1536
    return pl.pallas_call(
        _mod_kernel,
        grid=(n // tn,),
        in_specs=[pl.BlockSpec((rows, d), lambda j: (0, 0)),
                  pl.BlockSpec((d, tn), lambda j: (0, j)),
                  pl.BlockSpec((1, tn), lambda j: (0, j))],
        out_specs=pl.BlockSpec((rows, tn), lambda j: (0, j)),
        out_shape=jax.ShapeDtypeStruct((rows, n), F32),
        compiler_params=_cparams(("arbitrary",)),
        name="mod",
    )(c_rows, w_mod, b_mod)


def _ctx_kernel(ctx_ref, n1_ref, sh_ref, sc_ref, w_ref, dec_ref, s_ref, *, heads, dh):
    xc = ctx_ref[...]
    n_ctx = xc.shape[0]
    ms = jnp.mean(xc * xc, axis=-1, keepdims=True)
    hc = (xc * lax.rsqrt(ms + EPS) * n1_ref[...]) * (1.0 + sc_ref[...]) + sh_ref[...]
    kv = jnp.dot(hc.astype(BF16), w_ref[...], preferred_element_type=F32)
    lg = jnp.log1p(-jnp.exp(dec_ref[...]))
    pos = lax.broadcasted_iota(jnp.int32, (n_ctx, dh), 0).astype(F32)
    k_scale = dh ** -0.5
    d_ret = heads * dh
    tdims = (((0,), (0,)), ((), ()))
    for h in range(heads):
        kh = kv[:, h * dh:(h + 1) * dh] * k_scale
        vh = kv[:, d_ret + h * dh:d_ret + (h + 1) * dh].astype(BF16)
        wf = jnp.exp(lg[h:h + 1, :] * (n_ctx - 1.0 - pos))
        wb = jnp.exp(lg[heads + h:heads + h + 1, :] * pos)
        s_ref[h] = lax.dot_general((kh * wf).astype(BF16), vh, tdims, preferred_element_type=F32)
        s_ref[heads + h] = lax.dot_general((kh * wb).astype(BF16), vh, tdims, preferred_element_type=F32)


def _ctx_call(ctx, norm1, shift_c, scale_c, w_kv, dec, heads, dh):
    b, n_ctx, d = ctx.shape
    return pl.pallas_call(
        functools.partial(_ctx_kernel, heads=heads, dh=dh),
        grid=(b,),
        in_specs=[pl.BlockSpec((None, n_ctx, d), lambda i: (i, 0, 0)),
                  pl.BlockSpec((1, d), lambda i: (0, 0)),
                  pl.BlockSpec((1, d), lambda i: (0, 0)),
                  pl.BlockSpec((1, d), lambda i: (0, 0)),
                  pl.BlockSpec(w_kv.shape, lambda i: (0, 0)),
                  pl.BlockSpec(dec.shape, lambda i: (0, 0))],
        out_specs=pl.BlockSpec((None, 2 * heads, dh, dh), lambda i: (i, 0, 0, 0)),
        out_shape=jax.ShapeDtypeStruct((b, 2 * heads, dh, dh), F32),
        compiler_params=_cparams(("arbitrary",)),
        name="ctx_state",
    )(ctx, norm1, shift_c, scale_c, w_kv, dec)


def _inproj_kernel(x_ref, mod_ref, n1_ref, w_ref, cc_ref, ss_ref, sw_ref, sb_ref,
                   qkvg_ref, x0_ref, u_ref, hy_s, tail_s, *, heads, dh, d_hy, nt):
    t = pl.program_id(1)
    tm = x_ref.shape[0]
    d_ret = heads * dh

    @pl.when((pl.program_id(0) == 0) & (t == 0))
    def _():
        hy_s[...] = jnp.zeros(hy_s.shape, F32)
        tail_s[...] = jnp.zeros(tail_s.shape, F32)

    r8 = lax.broadcasted_iota(jnp.int32, (SUBLANES, d_hy), 0)

    def conv(cur, prev_row, next_row, sl):
        w0, w1, w2, bb = sw_ref[0:1, sl], sw_ref[1:2, sl], sw_ref[2:3, sl], sb_ref[:, sl]
        up = pltpu.roll(cur, 1, 0)
        dn = pltpu.roll(cur, tm - 1, 0)
        body = w0 * up + w1 * cur + w2 * dn + bb
        e = tm - SUBLANES
        head = (w0 * jnp.where(r8 == 0, prev_row, up[:SUBLANES]) + w1 * cur[:SUBLANES]
                + w2 * dn[:SUBLANES] + bb)
        tail = (w0 * up[e:] + w1 * cur[e:]
                + w2 * jnp.where(r8 == SUBLANES - 1, next_row, dn[e:]) + bb)
        return jnp.concatenate([head, body[SUBLANES:e], tail], axis=0)

    @pl.when(t < nt)
    def _():
        x = x_ref[...]
        ms = jnp.mean(x * x, axis=-1, keepdims=True)
        gain = n1_ref[...] * (1.0 + mod_ref[1:2, :])
        hb = (x * lax.rsqrt(ms + EPS) * gain + mod_ref[0:1, :]).astype(BF16)
        cc = cc_ref[...]
        ss = ss_ref[...]
        k_scale = dh ** -0.5

        def hyena_cols(j):
            sl = slice(j * d_hy, (j + 1) * d_hy)
            cur = hy_s[:, sl]
            new = jnp.dot(hb, w_ref[:, 4 * d_ret + j * d_hy:4 * d_ret + (j + 1) * d_hy],
                          preferred_element_type=F32)
            out = conv(cur, tail_s[SUBLANES - 1:SUBLANES, sl], new[0:1], sl)
            tail_s[:, sl] = jnp.where(t == 0, 0.0, cur[tm - SUBLANES:])
            hy_s[:, sl] = new
            return out

        def roped_cols(base, scale):
            p = jnp.dot(hb, w_ref[:, base:base + d_ret], preferred_element_type=F32)
            for j in range(heads):
                pj = p[:, j * dh:(j + 1) * dh]
                r = pj * cc + pltpu.roll(pj, dh // 2, 1) * ss
                if scale != 1.0:
                    r = r * scale
                qkvg_ref[:, base + j * dh:base + (j + 1) * dh] = r.astype(BF16)

        roped_cols(0, 1.0)
        x0_ref[...] = hyena_cols(0).astype(x0_ref.dtype)
        roped_cols(d_ret, k_scale)
        x1c = hyena_cols(1)
        sl = slice(3 * d_ret, 4 * d_ret)
        qkvg_ref[:, sl] = _silu(jnp.dot(hb, w_ref[:, sl], preferred_element_type=F32)).astype(BF16)
        u_ref[...] = (x1c * hyena_cols(2)).astype(u_ref.dtype)
        sl = slice(2 * d_ret, 3 * d_ret)
        qkvg_ref[:, sl] = jnp.dot(hb, w_ref[:, sl], preferred_element_type=F32).astype(BF16)

    @pl.when(t == nt)
    def _():
        zero = jnp.zeros((1, d_hy), F32)
        outs = []
        for j in range(HYENA_PROJ):
            sl = slice(j * d_hy, (j + 1) * d_hy)
            outs.append(conv(hy_s[:, sl], tail_s[SUBLANES - 1:SUBLANES, sl], zero, sl))
        x0_ref[...] = outs[0].astype(x0_ref.dtype)
        u_ref[...] = (outs[1] * outs[2]).astype(u_ref.dtype)


def _inproj_call(x, mod, norm1, w_in, cc, ss, short_w, short_b, heads, dh, d_hy):
    b, l, d = x.shape
    tm = IN_ROW_TILE
    nt = l // tm
    d_ret = heads * dh
    cur = lambda i, t: (i, jnp.minimum(t, nt - 1), 0)
    lag = lambda i, t: (i, jnp.maximum(t - 1, 0), 0)
    return pl.pallas_call(
        functools.partial(_inproj_kernel, heads=heads, dh=dh, d_hy=d_hy, nt=nt),
        grid=(b, nt + 1),
        in_specs=[pl.BlockSpec((None, tm, d), cur),
                  pl.BlockSpec((None, N_MOD, d), lambda i, t: (i, 0, 0)),
                  pl.BlockSpec((1, d), lambda i, t: (0, 0)),
                  _const_spec(w_in.shape),
                  pl.BlockSpec((tm, dh), lambda i, t: (jnp.minimum(t, nt - 1), 0)),
                  pl.BlockSpec((tm, dh), lambda i, t: (jnp.minimum(t, nt - 1), 0)),
                  pl.BlockSpec(short_w.shape, lambda i, t: (0, 0)),
                  pl.BlockSpec(short_b.shape, lambda i, t: (0, 0))],
        out_specs=[pl.BlockSpec((None, tm, 4 * d_ret), cur),
                   pl.BlockSpec((None, tm, d_hy), lag),
                   pl.BlockSpec((None, tm, d_hy), lag)],
        out_shape=[jax.ShapeDtypeStruct((b, l, 4 * d_ret), BF16),
                   jax.ShapeDtypeStruct((b, l, d_hy), BF16),
                   jax.ShapeDtypeStruct((b, l, d_hy), BF16)],
        scratch_shapes=[pltpu.VMEM((tm, HYENA_PROJ * d_hy), F32),
                        pltpu.VMEM((SUBLANES, HYENA_PROJ * d_hy), F32)],
        compiler_params=_cparams(("arbitrary", "arbitrary")),
        name="in_proj",
    )(x, mod, norm1, w_in, cc, ss, short_w, short_b)


def _ret_kernel(q_ref, k_ref, v_ref, g_ref, sf0_ref, sb0_ref, dec_ref, gain_ref,
                o_ref, kvf_s, kvb_s, st_s, *, heads, chunk):
    l, dh = q_ref.shape
    nc = l // chunk
    h = pl.program_id(1)
    lgf = jnp.log1p(-jnp.exp(dec_ref[pl.ds(h, 1), :]))
    lgb = jnp.log1p(-jnp.exp(dec_ref[pl.ds(h + heads, 1), :]))
    il = lax.broadcasted_iota(jnp.int32, (chunk, dh), 0).astype(F32)
    kw_f = jnp.exp(lgf * (chunk - 1.0 - il))
    kw_b = jnp.exp(lgb * il)
    qw_f = jnp.exp(lgf * (il + 1.0))
    qw_b = jnp.exp(lgb * (chunk - il))
    cd_f = jnp.exp(lgf * float(chunk))
    cd_b = jnp.exp(lgb * float(chunk))
    reps = chunk // dh
    lgf_c = jnp.concatenate([lgf] * reps, axis=1)
    lgb_c = jnp.concatenate([lgb] * reps, axis=1)
    ii = lax.broadcasted_iota(jnp.int32, (chunk, chunk), 0)
    jj = lax.broadcasted_iota(jnp.int32, (chunk, chunk), 1)
    diff = (ii - jj).astype(F32)
    dmask = (jnp.where(diff >= 0, jnp.exp(lgf_c * jnp.maximum(diff, 0.0)), 0.0)
             + jnp.where(diff <= 0, jnp.exp(lgb_c * jnp.maximum(-diff, 0.0)), 0.0))
    tdims = (((0,), (0,)), ((), ()))
    ntdims = (((1,), (1,)), ((), ()))

    def chunk_states(n, carry):
        r0 = pl.multiple_of(n * chunk, chunk)
        kr = k_ref[pl.ds(r0, chunk), :].astype(F32)
        vv = v_ref[pl.ds(r0, chunk), :]
        kvf_s[n] = lax.dot_general((kr * kw_f).astype(BF16), vv, tdims, preferred_element_type=F32)
        kvb_s[n] = lax.dot_general((kr * kw_b).astype(BF16), vv, tdims, preferred_element_type=F32)
        return carry

    lax.fori_loop(0, nc, chunk_states, 0, unroll=True)

    def scan_f(n, s):
        st_s[n, :dh, :] = s.astype(BF16)
        return s * cd_f + kvf_s[n]

    lax.fori_loop(0, nc, scan_f, sf0_ref[...])

    def scan_b(m, s):
        n = nc - 1 - m
        st_s[n, dh:, :] = s.astype(BF16)
        return s * cd_b + kvb_s[n]

    lax.fori_loop(0, nc, scan_b, sb0_ref[...])

    gain = gain_ref[...]

    def chunk_out(n, carry):
        r0 = pl.multiple_of(n * chunk, chunk)
        qb = q_ref[pl.ds(r0, chunk), :]
        qr = qb.astype(F32)
        sc = lax.dot_general(qb, k_ref[pl.ds(r0, chunk), :], ntdims, preferred_element_type=F32)
        lhs = jnp.concatenate([(qr * qw_f).astype(BF16), (qr * qw_b).astype(BF16),
                               (sc * dmask).astype(BF16)], axis=1)
        rhs = jnp.concatenate([st_s[n], v_ref[pl.ds(r0, chunk), :]], axis=0)
        o = jnp.dot(lhs, rhs, preferred_element_type=F32)
        mu = jnp.mean(o, axis=-1, keepdims=True)
        d = o - mu
        var = jnp.mean(d * d, axis=-1, keepdims=True)
        gg = g_ref[pl.ds(r0, chunk), :].astype(F32)
        o_ref[pl.ds(r0, chunk), :] = (d * lax.rsqrt(var + EPS) * gain * gg).astype(o_ref.dtype)
        return carry

    lax.fori_loop(0, nc, chunk_out, 0, unroll=True)


def _ret_call(qkvg, s0, dec, gn_gain, heads, dh):
    b, l, _ = qkvg.shape
    chunk = RET_CHUNK
    nc = l // chunk
    seq = lambda off: pl.BlockSpec((None, l, dh), lambda i, h: (i, 0, off + h))
    return pl.pallas_call(
        functools.partial(_ret_kernel, heads=heads, chunk=chunk),
        grid=(b, heads),
        in_specs=[seq(0), seq(heads), seq(2 * heads), seq(3 * heads),
                  pl.BlockSpec((None, None, dh, dh), lambda i, h: (i, h, 0, 0)),
                  pl.BlockSpec((None, None, dh, dh), lambda i, h: (i, heads + h, 0, 0)),
                  pl.BlockSpec(dec.shape, lambda i, h: (0, 0)),
                  pl.BlockSpec((1, dh), lambda i, h: (0, h))],
        out_specs=pl.BlockSpec((None, l, dh), lambda i, h: (i, 0, h)),
        out_shape=jax.ShapeDtypeStruct((b, l, heads * dh), BF16),
        scratch_shapes=[pltpu.VMEM((nc, dh, dh), F32), pltpu.VMEM((nc, dh, dh), F32),
                        pltpu.VMEM((nc, 2 * dh, dh), BF16)],
        compiler_params=_cparams(("arbitrary", "arbitrary")),
        name="retention",
    )(qkvg, qkvg, qkvg, qkvg, s0, s0, dec, gn_gain)


def _filt_time_kernel(emb_ref, w1_ref, b1_ref, w2_ref, b2_ref, w3_ref, b3_ref, w4f_ref, w4b_ref,
                      fr_ref, dl_ref, kern_ref, s_ref, *, seq_len):
    i = pl.program_id(0)
    rows, c = kern_ref.shape[1], kern_ref.shape[2]
    fr = fr_ref[...]
    hdot = _dot3
    z = jnp.sin(fr * (hdot(emb_ref[...], w1_ref[...]) + b1_ref[...]))
    z = jnp.sin(fr * (hdot(z, w2_ref[...]) + b2_ref[...]))
    z = jnp.sin(fr * (hdot(z, w3_ref[...]) + b3_ref[...]))
    pos = (i * rows + lax.broadcasted_iota(jnp.int32, (rows, c), 0)).astype(F32)
    inv = 1.0 / (seq_len - 1.0)
    adl = dl_ref[...]
    hf = hdot(z, w4f_ref[...]) * jnp.exp(-(pos * inv) * adl)
    hr = hdot(z, w4b_ref[...]) * jnp.exp(-((seq_len - pos) * inv) * adl)
    hr = jnp.where(pos == 0.0, 0.0, hr)
    kern_ref[0] = hf
    kern_ref[1] = hr
    part = jnp.sum(jnp.abs(hf) + jnp.abs(hr), axis=0, keepdims=True)

    @pl.when(i == 0)
    def _():
        s_ref[...] = part

    @pl.when(i != 0)
    def _():
        s_ref[...] += part


def _filt_time_call(emb2, w1, b1, w2, b2, w3, b3, w4f, w4b, freq, absdelta, seq_len):
    c = absdelta.shape[1]
    rows = FILT_ROWS
    small = lambda a: pl.BlockSpec(a.shape, lambda i: (0,) * a.ndim)
    return pl.pallas_call(
        functools.partial(_filt_time_kernel, seq_len=seq_len),
        grid=(seq_len // rows,),
        in_specs=[pl.BlockSpec((rows, emb2.shape[1]), lambda i: (i, 0)),
                  small(w1), small(b1), small(w2), small(b2), small(w3), small(b3), small(w4f), small(w4b),
                  small(freq), small(absdelta)],
        out_specs=[pl.BlockSpec((2, rows, c), lambda i: (0, i, 0)),
                   pl.BlockSpec((1, c), lambda i: (0, 0))],
        out_shape=[jax.ShapeDtypeStruct((2, seq_len, c), F32),
                   jax.ShapeDtypeStruct((1, c), F32)],
        compiler_params=_cparams(("arbitrary",)),
        name="filt_time",
    )(emb2, w1, b1, w2, b2, w3, b3, w4f, w4b, freq, absdelta)


@functools.lru_cache(maxsize=None)
def _fft_tables(seq_len):
    n = 2 * seq_len
    n1s = FFT_N1
    n2s = n // n1s
    half = n1s // 2
    k1n = half + 1
    n1 = np.arange(half)
    k1 = np.arange(k1n)
    th = 2.0 * np.pi * (np.outer(k1, n1) % n1s) / n1s
    herm = np.where((k1 == 0) | (k1 == half), 1.0, 2.0)
    fa = np.concatenate([np.cos(th), -np.sin(th)[1:half]], axis=0)
    sgn = np.concatenate([(-1.0) ** k1, (-1.0) ** k1[1:half]])
    fai = np.concatenate([np.cos(th) * herm[:, None], (-np.sin(th) * herm[:, None])[1:half]], axis=0).T / n
    eye = np.eye(SUBLANES)
    fk = np.kron(fa, eye)
    fks = np.kron(fa * sgn[:, None], eye)
    fki = np.kron(fai, eye)
    k2 = np.arange(n2s)
    n2 = np.arange(n2s)
    m = np.zeros((k1n, 2 * n2s, 2 * n2s))
    for a in range(k1n):
        ang = 2.0 * np.pi * (np.outer(a + n1s * k2, n2) % n) / n
        gr, gi = np.cos(ang), -np.sin(ang)
        m[a] = np.block([[gr, -gi], [gi, gr]])
    mt = np.transpose(m, (0, 2, 1))
    return dict(fk=fk, fkf=np.concatenate([fk, fks], axis=1), fki=fki, m=m, mt=mt,
                k1n=k1n, n2s=n2s, half=half)


def _stage_n1(src_refs, mat, dst_ref, n2s):
    rows = dst_ref.shape[0]
    for j in range(n2s // SUBLANES):
        sl = slice(j * SUBLANES, (j + 1) * SUBLANES)
        parts = [r[:, sl, :] for r in src_refs]
        xg = jnp.concatenate([p.reshape(p.shape[0] * SUBLANES, p.shape[2]) for p in parts], axis=0)
        a = jnp.dot(mat, xg.astype(BF16), preferred_element_type=F32)
        dst_ref[:, sl, :] = a.reshape(rows, SUBLANES, a.shape[1])


def _n2_input(a_s, k1, half, n2s):
    if 0 < k1 < half:
        return jnp.concatenate([a_s[k1], a_s[half + k1]], axis=0)
    return a_s[k1]


def _filt_spec_kernel(kern_ref, s_ref, fkf_ref, m_ref, kf_ref, a_s, *, k1n, n2s):
    half = k1n - 1
    _stage_n1([kern_ref.at[0], kern_ref.at[1]], fkf_ref[...], a_s, n2s)
    inv = 1.0 / (s_ref[...] + EPS)
    for k1 in range(k1n):
        a = _n2_input(a_s, k1, half, n2s).astype(BF16)
        kf_ref[k1] = jnp.dot(m_ref[k1, :, :a.shape[0]], a, preferred_element_type=F32) * inv


def _filt_spec_call(kern4, s, tabs):
    _, half, n2s, c = kern4.shape
    ct = HY_CT
    k1n = tabs["k1n"]
    return pl.pallas_call(
        functools.partial(_filt_spec_kernel, k1n=k1n, n2s=n2s),
        grid=(c // ct,),
        in_specs=[pl.BlockSpec((2, half, n2s, ct), lambda j: (0, 0, 0, j)),
                  pl.BlockSpec((1, ct), lambda j: (0, j)),
                  _const_spec(tabs["fkf"].shape), _const_spec(tabs["m"].shape)],
        out_specs=pl.BlockSpec((k1n, 2 * n2s, ct), lambda j: (0, 0, j)),
        out_shape=jax.ShapeDtypeStruct((k1n, 2 * n2s, c), F32),
        scratch_shapes=[pltpu.VMEM((2 * half, n2s, ct), F32)],
        compiler_params=_cparams(("arbitrary",)),
        name="filt_spec",
    )(kern4, s, tabs["fkf"], tabs["m"])


def _hyena_kernel(u_ref, x0_ref, bias_ref, kf_ref, fk_ref, fki_ref, m_ref, mt_ref, o_ref, u_s, a_s,
                  *, k1n, n2s):
    half = k1n - 1
    ct = u_ref.shape[2]
    u_s[...] = u_ref[...].astype(F32)
    _stage_n1([u_s], fk_ref[...], a_s, n2s)

    for k1 in range(k1n):
        a = _n2_input(a_s, k1, half, n2s).astype(BF16)
        x = jnp.dot(m_ref[k1, :, :a.shape[0]], a, preferred_element_type=F32)
        xr, xi = x[:n2s], x[n2s:]
        kr, ki = kf_ref[k1, :n2s, :], kf_ref[k1, n2s:, :]
        y = jnp.concatenate([xr * kr - xi * ki, xr * ki + xi * kr], axis=0).astype(BF16)
        if 0 < k1 < half:
            z = jnp.dot(mt_ref[k1], y, preferred_element_type=F32)
            a_s[k1] = z[:n2s]
            a_s[half + k1] = z[n2s:]
        else:
            a_s[k1] = jnp.dot(mt_ref[k1, :n2s, :], y, preferred_element_type=F32)

    fki = fki_ref[...]
    bias = bias_ref[...]
    pair = 2 * SUBLANES
    for j in range(n2s // pair):
        ys = []
        for jj in range(2):
            sl = slice(j * pair + jj * SUBLANES, j * pair + (jj + 1) * SUBLANES)
            zg = a_s[:, sl, :].reshape(2 * half * SUBLANES, ct)
            ys.append(jnp.dot(fki, zg.astype(BF16), preferred_element_type=F32).reshape(half, SUBLANES, ct))
        y = jnp.concatenate(ys, axis=1)
        sl = slice(j * pair, (j + 1) * pair)
        o_ref[:, sl, :] = (x0_ref[:, sl, :].astype(F32) * (y + u_s[:, sl, :] * bias)).astype(o_ref.dtype)


def _hyena_call(u4, x04, bias, kf, tabs):
    b, half, n2s, c = u4.shape
    ct = HY_CT
    k1n = tabs["k1n"]
    seq = pl.BlockSpec((None, half, n2s, ct), lambda j, i: (i, 0, 0, j))
    return pl.pallas_call(
        functools.partial(_hyena_kernel, k1n=k1n, n2s=n2s),
        grid=(c // ct, b),
        in_specs=[seq, seq,
                  pl.BlockSpec((1, ct), lambda j, i: (0, j)),
                  pl.BlockSpec((k1n, 2 * n2s, ct), lambda j, i: (0, 0, j), pipeline_mode=pl.Buffered(1)),
                  _const_spec(tabs["fk"].shape), _const_spec(tabs["fki"].shape),
                  _const_spec(tabs["m"].shape), _const_spec(tabs["mt"].shape)],
        out_specs=pl.BlockSpec((None, half, n2s, ct), lambda j, i: (i, 0, 0, j)),
        out_shape=jax.ShapeDtypeStruct((b, half, n2s, c), BF16),
        scratch_shapes=[pltpu.VMEM((half, n2s, ct), F32), pltpu.VMEM((2 * half, n2s, ct), F32)],
        compiler_params=_cparams(("arbitrary", "arbitrary")),
        name="hyena",
    )(u4, x04, bias, kf, tabs["fk"], tabs["fki"], tabs["m"], tabs["mt"])


def _out_ffn_kernel(ret_ref, hy_ref, x_ref, mod_ref, hg_ref, n2_ref, fn_ref, wo_ref, wgu_ref, wd_ref,
                    o_ref, *, d_ret, d_ff, ff_bounds, row_parts):
    rows = x_ref.shape[0] // row_parts
    for r in range(row_parts):
        rs = slice(r * rows, (r + 1) * rows)
        hy = hy_ref[rs, :].astype(F32)
        hms = jnp.mean(hy * hy, axis=-1, keepdims=True)
        hyn = (hy * lax.rsqrt(hms + EPS) * hg_ref[...]).astype(BF16)
        mix = jnp.dot(ret_ref[rs, :], wo_ref[:d_ret, :], preferred_element_type=F32)
        mix = mix + jnp.dot(hyn, wo_ref[d_ret:, :], preferred_element_type=F32)
        x1 = x_ref[rs, :] + mod_ref[2:3, :] * mix
        ms = jnp.mean(x1 * x1, axis=-1, keepdims=True)
        gain2 = n2_ref[...] * (1.0 + mod_ref[4:5, :])
        h2 = (x1 * lax.rsqrt(ms + EPS) * gain2 + mod_ref[3:4, :]).astype(BF16)
        acc = jnp.zeros(x1.shape, F32)
        for lo, hi in zip(ff_bounds[:-1], ff_bounds[1:]):
            g = jnp.dot(h2, wgu_ref[:, lo:hi], preferred_element_type=F32)
            u = jnp.dot(h2, wgu_ref[:, d_ff + lo:d_ff + hi], preferred_element_type=F32)
            a = (_silu(g) * u).astype(BF16)
            acc = acc + jnp.dot(a, wd_ref[lo:hi, :], preferred_element_type=F32)
        x2 = x1 + mod_ref[5:6, :] * acc
        ms2 = jnp.mean(x2 * x2, axis=-1, keepdims=True)
        o_ref[rs, :] = x2 * lax.rsqrt(ms2 + EPS) * fn_ref[...]


def _ff_bounds(d_ff, parts=2):
    tiles = d_ff // MXU_WIDTH
    assert tiles * MXU_WIDTH == d_ff
    cuts = [-(-tiles * p // parts) for p in range(parts + 1)]
    return tuple(c * MXU_WIDTH for c in cuts)


def _out_ffn_call(ret, hy, x, mod, hy_gain, norm2, final_norm, w_out, w_gu, w_down):
    b, l, d = x.shape
    d_ret = ret.shape[2]
    d_hy = hy.shape[2]
    d_ff = w_down.shape[0]
    tm = ROW_TILE
    return pl.pallas_call(
        functools.partial(_out_ffn_kernel, d_ret=d_ret, d_ff=d_ff, ff_bounds=_ff_bounds(d_ff),
                          row_parts=1),
        grid=(b, l // tm),
        in_specs=[pl.BlockSpec((None, tm, d_ret), lambda i, t: (i, t, 0)),
                  pl.BlockSpec((None, tm, d_hy), lambda i, t: (i, t, 0)),
                  pl.BlockSpec((None, tm, d), lambda i, t: (i, t, 0)),
                  pl.BlockSpec((None, N_MOD, d), lambda i, t: (i, 0, 0)),
                  pl.BlockSpec((1, d_hy), lambda i, t: (0, 0)),
                  pl.BlockSpec((1, d), lambda i, t: (0, 0)),
                  pl.BlockSpec((1, d), lambda i, t: (0, 0)),
                  _const_spec(w_out.shape), _const_spec(w_gu.shape), _const_spec(w_down.shape)],
        out_specs=pl.BlockSpec((None, tm, d), lambda i, t: (i, t, 0)),
        out_shape=jax.ShapeDtypeStruct((b, l, d), x.dtype),
        compiler_params=_cparams(("arbitrary", "arbitrary")),
        name="out_ffn",
    )(ret, hy, x, mod, hy_gain, norm2, final_norm, w_out, w_gu, w_down)


@functools.lru_cache(maxsize=None)
def _rope_tables(seq_len, dh):
    n = dh // 4
    t = np.arange(seq_len)
    inv = ROPE_BASE ** (-np.arange(n, dtype=np.float64) / n)
    ang = np.concatenate([(t // GRID_W)[:, None] * inv, (t % GRID_W)[:, None] * inv], axis=-1)
    cc = np.concatenate([np.cos(ang), np.cos(ang)], axis=-1)
    ss = np.concatenate([-np.sin(ang), np.sin(ang)], axis=-1)
    return cc.astype(np.float32), ss.astype(np.float32)


@functools.lru_cache(maxsize=None)
def _filter_tables(seq_len, emb_dim, emb_pad, channels):
    t = np.linspace(0.0, 1.0, seq_len)[:, None]
    bands = (emb_dim - 1) // 2
    f = np.linspace(1e-4, bands - 1, bands)[None, :]
    wpos = 2.0 * np.pi * np.arange(seq_len)[:, None] / seq_len
    emb = np.concatenate([t, np.cos(f * wpos), -np.sin(f * wpos)], axis=-1)
    emb = np.pad(emb, ((0, 0), (0, emb_pad - emb_dim)))
    emb_rev = np.concatenate([emb[:1], emb[:0:-1]], axis=0)
    emb2 = np.concatenate([emb, emb_rev], axis=1)
    max_decay = math.log(FILTER_DECAY_TARGET) / FILTER_DECAY_FAST
    min_decay = math.log(FILTER_DECAY_TARGET) / FILTER_DECAY_SLOW
    absdelta = np.abs(np.linspace(min_decay, max_decay, channels))[None, :]
    return emb2.astype(np.float32), absdelta.astype(np.float32)


def _block_diag2(w):
    z = jnp.zeros_like(w)
    return jnp.concatenate([jnp.concatenate([w, z], axis=1), jnp.concatenate([z, w], axis=1)], axis=0)


def kernel(x, c, ctx, c_ctx, w_mod, b_mod, norm1, norm2, w_in, ret_decay, ret_gn_gain, hy_short_w,
           hy_short_b, hy_w1, hy_b1, hy_w2, hy_b2, hy_w3, hy_b3, hy_w4, hy_freq, hy_bias, hy_out_norm,
           w_out, w_gate_up, w_down, final_norm):
    b, seq_len, d = x.shape
    assert w_mod.shape[0] == 1, "single-layer block"
    heads = RET_HEADS
    d_ret = ret_gn_gain.shape[1]
    dh = d_ret // heads
    d_hy = hy_bias.shape[1]
    assert dh == LANES and d_hy % HY_CT == 0 and seq_len % (FFT_N1 // 2) == 0

    rows = -(-(b + 1) // SUBLANES) * SUBLANES
    c_rows = jnp.zeros((rows, d), F32).at[:b].set(c).at[b].set(c_ctx)
    mod_all = _mod_call(c_rows, w_mod[0], b_mod[0][None, :])
    mod = mod_all[:b].reshape(b, N_MOD, d)
    mod_c = mod_all[b].reshape(N_MOD, d)

    w_in_b = w_in[0].astype(BF16)
    dec = jnp.broadcast_to(ret_decay[0].reshape(2 * heads, 1), (2 * heads, LANES))
    n1g = norm1[0][None, :]

    s0 = _ctx_call(ctx, n1g, mod_c[0:1], mod_c[1:2], w_in_b[:, d_ret:3 * d_ret], dec, heads, dh)
    cc, ss = (jnp.asarray(t) for t in _rope_tables(seq_len, dh))
    qkvg, x0c, u = _inproj_call(x, mod, n1g, w_in_b, cc, ss, hy_short_w[0], hy_short_b, heads, dh, d_hy)
    ret = _ret_call(qkvg, s0, dec, ret_gn_gain, heads, dh)

    tabs = dict(_fft_tables(seq_len))
    for name in ("fk", "fkf", "fki", "m", "mt"):
        tabs[name] = jnp.asarray(tabs[name], dtype=F32).astype(BF16)
    half, n2s = tabs["half"], tabs["n2s"]
    emb_dim = hy_w1.shape[1]
    emb_pad = -(-emb_dim // SUBLANES) * SUBLANES
    emb2, absdelta = (jnp.asarray(t) for t in _filter_tables(seq_len, emb_dim, emb_pad, d_hy))
    w1p = jnp.pad(hy_w1[0], ((0, emb_pad - emb_dim), (0, 0)))
    two = lambda a: jnp.concatenate([a, a], axis=1)
    w4 = hy_w4[0]
    zero4 = jnp.zeros((w4.shape[0], d_hy), F32)
    kern, ksum = _filt_time_call(
        emb2, _block_diag2(w1p), two(hy_b1), _block_diag2(hy_w2[0]), two(hy_b2), _block_diag2(hy_w3[0]),
        two(hy_b3), jnp.concatenate([w4[:, :d_hy], zero4], axis=0), jnp.concatenate([zero4, w4[:, d_hy:]], axis=0),
        two(hy_freq), absdelta, seq_len)
    kf = _filt_spec_call(kern.reshape(2, half, n2s, d_hy), ksum, tabs)
    hy = _hyena_call(u.reshape(b, half, n2s, d_hy), x0c.reshape(b, half, n2s, d_hy), hy_bias, kf, tabs)
    hy = hy.reshape(b, seq_len, d_hy)

    return _out_ffn_call(ret, hy, x, mod, hy_out_norm, norm2[0][None, :], final_norm[None, :],
                         w_out[0].astype(BF16), w_gate_up[0].astype(BF16), w_down[0].astype(BF16))
```

```python
import functools
import math

import jax
import jax.numpy as jnp
import numpy as np
from jax import lax
from jax.experimental import pallas as pl
from jax.experimental.pallas import tpu as pltpu

F32 = jnp.float32
BF16 = jnp.bfloat16
HIGHEST = lax.Precision.HIGHEST

RET_HEADS = 4
GRID_W = 64
ROPE_BASE = 10000.0
N_MOD = 6
HYENA_PROJ = 3
FILTER_DECAY_FAST = 0.3
FILTER_DECAY_SLOW = 1.5
FILTER_DECAY_TARGET = 1e-2
EPS = 1e-6

LANES = 128
SUBLANES = 8
MXU_WIDTH = 256
VMEM_LIMIT_BYTES = 56 * 1024 * 1024

RET_CHUNK = 256
ROW_TILE = 512
IN_ROW_TILE = 512
HY_CT = 256
FILT_ROWS = 1024
FFT_N1 = 64


def _silu(v):
    return v / (1.0 + jnp.exp(-v))


def _dot3(a, b):
    a_hi = a.astype(BF16)
    b_hi = b.astype(BF16)
    a_lo = (a - a_hi.astype(F32)).astype(BF16)
    b_lo = (b - b_hi.astype(F32)).astype(BF16)
    dot = functools.partial(jnp.dot, preferred_element_type=F32)
    return dot(a_hi, b_hi) + (dot(a_hi, b_lo) + dot(a_lo, b_hi))


def _cparams(sem, vmem=VMEM_LIMIT_BYTES):
    return pltpu.CompilerParams(dimension_semantics=sem, vmem_limit_bytes=vmem)


def _const_spec(shape):
    nd = len(shape)
    return pl.BlockSpec(shape, lambda *_: (0,) * nd, pipeline_mode=pl.Buffered(1))


def _mod_kernel(c_ref, w_ref, b_ref, o_ref):
    s = _silu(c_ref[...])
    o_ref[...] = jnp.dot(s, w_ref[...], precision=HIGHEST, preferred_element_type=F32) + b_ref[...]


def _mod_call(c_rows, w_mod, b_mod):
    rows, d = c_rows.shape
    n = w_mod.shape[1]
    tn = 1536
    return pl.pallas_call(
        _mod_kernel,
        grid=(n // tn,),
        in_specs=[pl.BlockSpec((rows, d), lambda j: (0, 0)),
                  pl.BlockSpec((d, tn), lambda j: (0, j)),
                  pl.BlockSpec((1, tn), lambda j: (0, j))],
        out_specs=pl.BlockSpec((rows, tn), lambda j: (0, j)),
        out_shape=jax.ShapeDtypeStruct((rows, n), F32),
        compiler_params=_cparams(("arbitrary",)),
        name="mod",
    )(c_rows, w_mod, b_mod)


def _ctx_kernel(ctx_ref, n1_ref, sh_ref, sc_ref, w_ref, dec_ref, s_ref, *, heads, dh):
    xc = ctx_ref[...]
    n_ctx = xc.shape[0]
    ms = jnp.mean(xc * xc, axis=-1, keepdims=True)
    hc = (xc * lax.rsqrt(ms + EPS) * n1_ref[...]) * (1.0 + sc_ref[...]) + sh_ref[...]
    kv = jnp.dot(hc.astype(BF16), w_ref[...], preferred_element_type=F32)
    lg = jnp.log1p(-jnp.exp(dec_ref[...]))
    pos = lax.broadcasted_iota(jnp.int32, (n_ctx, dh), 0).astype(F32)
    k_scale = dh ** -0.5
    d_ret = heads * dh
    tdims = (((0,), (0,)), ((), ()))
    for h in range(heads):
        kh = kv[:, h * dh:(h + 1) * dh] * k_scale
        vh = kv[:, d_ret + h * dh:d_ret + (h + 1) * dh].astype(BF16)
        wf = jnp.exp(lg[h:h + 1, :] * (n_ctx - 1.0 - pos))
        wb = jnp.exp(lg[heads + h:heads + h + 1, :] * pos)
        s_ref[h] = lax.dot_general((kh * wf).astype(BF16), vh, tdims, preferred_element_type=F32)
        s_ref[heads + h] = lax.dot_general((kh * wb).astype(BF16), vh, tdims, preferred_element_type=F32)


def _ctx_call(ctx, norm1, shift_c, scale_c, w_kv, dec, heads, dh):
    b, n_ctx, d = ctx.shape
    return pl.pallas_call(
        functools.partial(_ctx_kernel, heads=heads, dh=dh),
        grid=(b,),
        in_specs=[pl.BlockSpec((None, n_ctx, d), lambda i: (i, 0, 0)),
                  pl.BlockSpec((1, d), lambda i: (0, 0)),
                  pl.BlockSpec((1, d), lambda i: (0, 0)),
                  pl.BlockSpec((1, d), lambda i: (0, 0)),
                  pl.BlockSpec(w_kv.shape, lambda i: (0, 0)),
                  pl.BlockSpec(dec.shape, lambda i: (0, 0))],
        out_specs=pl.BlockSpec((None, 2 * heads, dh, dh), lambda i: (i, 0, 0, 0)),
        out_shape=jax.ShapeDtypeStruct((b, 2 * heads, dh, dh), F32),
        compiler_params=_cparams(("arbitrary",)),
        name="ctx_state",
    )(ctx, norm1, shift_c, scale_c, w_kv, dec)


def _inproj_kernel(x_ref, mod_ref, n1_ref, w_ref, cc_ref, ss_ref, sw_ref, sb_ref,
                   qkvg_ref, x0_ref, u_ref, hy_s, tail_s, *, heads, dh, d_hy, nt):
    t = pl.program_id(1)
    tm = x_ref.shape[0]
    d_ret = heads * dh

    @pl.when((pl.program_id(0) == 0) & (t == 0))
    def _():
        hy_s[...] = jnp.zeros(hy_s.shape, F32)
        tail_s[...] = jnp.zeros(tail_s.shape, F32)

    r8 = lax.broadcasted_iota(jnp.int32, (SUBLANES, d_hy), 0)

    def conv(cur, prev_row, next_row, sl):
        w0, w1, w2, bb = sw_ref[0:1, sl], sw_ref[1:2, sl], sw_ref[2:3, sl], sb_ref[:, sl]
        up = pltpu.roll(cur, 1, 0)
        dn = pltpu.roll(cur, tm - 1, 0)
        body = w0 * up + w1 * cur + w2 * dn + bb
        e = tm - SUBLANES
        head = (w0 * jnp.where(r8 == 0, prev_row, up[:SUBLANES]) + w1 * cur[:SUBLANES]
                + w2 * dn[:SUBLANES] + bb)
        tail = (w0 * up[e:] + w1 * cur[e:]
                + w2 * jnp.where(r8 == SUBLANES - 1, next_row, dn[e:]) + bb)
        return jnp.concatenate([head, body[SUBLANES:e], tail], axis=0)

    @pl.when(t < nt)
    def _():
        groups = [slice(r * (tm // 2), (r + 1) * (tm // 2)) for r in range(2)]
        gain = n1_ref[...] * (1.0 + mod_ref[1:2, :])
        hbs = []
        for rs in groups:
            x = x_ref[rs, :]
            ms = jnp.mean(x * x, axis=-1, keepdims=True)
            hbs.append((x * lax.rsqrt(ms + EPS) * gain + mod_ref[0:1, :]).astype(BF16))
        k_scale = dh ** -0.5

        def proj(base, width):
            return [jnp.dot(hb, w_ref[:, base:base + width], preferred_element_type=F32) for hb in hbs]

        def hyena_cols(j):
            sl = slice(j * d_hy, (j + 1) * d_hy)
            cur = hy_s[:, sl]
            news = proj(4 * d_ret + j * d_hy, d_hy)
            out = conv(cur, tail_s[SUBLANES - 1:SUBLANES, sl], news[0][0:1], sl)
            tail_s[:, sl] = jnp.where(t == 0, 0.0, cur[tm - SUBLANES:])
            for rs, new in zip(groups, news):
                hy_s[rs, sl] = new
            return out

        def roped_cols(base, scale):
            for rs, p in zip(groups, proj(base, d_ret)):
                cc = cc_ref[rs, :]
                ss = ss_ref[rs, :]
                for j in range(heads):
                    pj = p[:, j * dh:(j + 1) * dh]
                    r = pj * cc + pltpu.roll(pj, dh // 2, 1) * ss
                    if scale != 1.0:
                        r = r * scale
                    qkvg_ref[rs, base + j * dh:base + (j + 1) * dh] = r.astype(BF16)

        roped_cols(0, 1.0)
        x0_ref[...] = hyena_cols(0).astype(x0_ref.dtype)
        roped_cols(d_ret, k_scale)
        x1c = hyena_cols(1)
        for rs, g in zip(groups, proj(3 * d_ret, d_ret)):
            qkvg_ref[rs, 3 * d_ret:4 * d_ret] = _silu(g).astype(BF16)
        u_ref[...] = (x1c * hyena_cols(2)).astype(u_ref.dtype)
        for rs, v in zip(groups, proj(2 * d_ret, d_ret)):
            qkvg_ref[rs, 2 * d_ret:3 * d_ret] = v.astype(BF16)

    @pl.when(t == nt)
    def _():
        zero = jnp.zeros((1, d_hy), F32)
        outs = []
        for j in range(HYENA_PROJ):
            sl = slice(j * d_hy, (j + 1) * d_hy)
            outs.append(conv(hy_s[:, sl], tail_s[SUBLANES - 1:SUBLANES, sl], zero, sl))
        x0_ref[...] = outs[0].astype(x0_ref.dtype)
        u_ref[...] = (outs[1] * outs[2]).astype(u_ref.dtype)


def _inproj_call(x, mod, norm1, w_in, cc, ss, short_w, short_b, heads, dh, d_hy):
    b, l, d = x.shape
    tm = IN_ROW_TILE
    nt = l // tm
    d_ret = heads * dh
    cur = lambda i, t: (i, jnp.minimum(t, nt - 1), 0)
    lag = lambda i, t: (i, jnp.maximum(t - 1, 0), 0)
    return pl.pallas_call(
        functools.partial(_inproj_kernel, heads=heads, dh=dh, d_hy=d_hy, nt=nt),
        grid=(b, nt + 1),
        in_specs=[pl.BlockSpec((None, tm, d), cur),
                  pl.BlockSpec((None, N_MOD, d), lambda i, t: (i, 0, 0)),
                  pl.BlockSpec((1, d), lambda i, t: (0, 0)),
                  _const_spec(w_in.shape),
                  pl.BlockSpec((tm, dh), lambda i, t: (jnp.minimum(t, nt - 1), 0)),
                  pl.BlockSpec((tm, dh), lambda i, t: (jnp.minimum(t, nt - 1), 0)),
                  pl.BlockSpec(short_w.shape, lambda i, t: (0, 0)),
                  pl.BlockSpec(short_b.shape, lambda i, t: (0, 0))],
        out_specs=[pl.BlockSpec((None, tm, 4 * d_ret), cur),
                   pl.BlockSpec((None, tm, d_hy), lag),
                   pl.BlockSpec((None, tm, d_hy), lag)],
        out_shape=[jax.ShapeDtypeStruct((b, l, 4 * d_ret), BF16),
                   jax.ShapeDtypeStruct((b, l, d_hy), BF16),
                   jax.ShapeDtypeStruct((b, l, d_hy), BF16)],
        scratch_shapes=[pltpu.VMEM((tm, HYENA_PROJ * d_hy), F32),
                        pltpu.VMEM((SUBLANES, HYENA_PROJ * d_hy), F32)],
        compiler_params=_cparams(("arbitrary", "arbitrary")),
        name="in_proj",
    )(x, mod, norm1, w_in, cc, ss, short_w, short_b)


def _ret_kernel(q_ref, k_ref, v_ref, g_ref, sf0_ref, sb0_ref, dec_ref, gain_ref,
                o_ref, kvf_s, kvb_s, st_s, *, heads, chunk):
    l, dh = q_ref.shape
    nc = l // chunk
    h = pl.program_id(1)
    lgf = jnp.log1p(-jnp.exp(dec_ref[pl.ds(h, 1), :]))
    lgb = jnp.log1p(-jnp.exp(dec_ref[pl.ds(h + heads, 1), :]))
    il = lax.broadcasted_iota(jnp.int32, (chunk, dh), 0).astype(F32)
    kw_f = jnp.exp(lgf * (chunk - 1.0 - il))
    kw_b = jnp.exp(lgb * il)
    qw_f = jnp.exp(lgf * (il + 1.0))
    qw_b = jnp.exp(lgb * (chunk - il))
    cd_f = jnp.exp(lgf * float(chunk))
    cd_b = jnp.exp(lgb * float(chunk))
    reps = chunk // dh
    lgf_c = jnp.concatenate([lgf] * reps, axis=1)
    lgb_c = jnp.concatenate([lgb] * reps, axis=1)
    ii = lax.broadcasted_iota(jnp.int32, (chunk, chunk), 0)
    jj = lax.broadcasted_iota(jnp.int32, (chunk, chunk), 1)
    diff = (ii - jj).astype(F32)
    dmask = (jnp.where(diff >= 0, jnp.exp(lgf_c * jnp.maximum(diff, 0.0)), 0.0)
             + jnp.where(diff <= 0, jnp.exp(lgb_c * jnp.maximum(-diff, 0.0)), 0.0))
    tdims = (((0,), (0,)), ((), ()))
    ntdims = (((1,), (1,)), ((), ()))

    def chunk_states(n, carry):
        r0 = pl.multiple_of(n * chunk, chunk)
        kr = k_ref[pl.ds(r0, chunk), :].astype(F32)
        vv = v_ref[pl.ds(r0, chunk), :]
        kvf_s[n] = lax.dot_general((kr * kw_f).astype(BF16), vv, tdims, preferred_element_type=F32)
        kvb_s[n] = lax.dot_general((kr * kw_b).astype(BF16), vv, tdims, preferred_element_type=F32)
        return carry

    lax.fori_loop(0, nc, chunk_states, 0, unroll=True)

    def scan_f(n, s):
        st_s[n, :dh, :] = s.astype(BF16)
        return s * cd_f + kvf_s[n]

    lax.fori_loop(0, nc, scan_f, sf0_ref[...])

    def scan_b(m, s):
        n = nc - 1 - m
        st_s[n, dh:, :] = s.astype(BF16)
        return s * cd_b + kvb_s[n]

    lax.fori_loop(0, nc, scan_b, sb0_ref[...])

    gain = gain_ref[...]

    def chunk_out(n, carry):
        r0 = pl.multiple_of(n * chunk, chunk)
        qb = q_ref[pl.ds(r0, chunk), :]
        qr = qb.astype(F32)
        sc = lax.dot_general(qb, k_ref[pl.ds(r0, chunk), :], ntdims, preferred_element_type=F32)
        lhs = jnp.concatenate([(qr * qw_f).astype(BF16), (qr * qw_b).astype(BF16),
                               (sc * dmask).astype(BF16)], axis=1)
        rhs = jnp.concatenate([st_s[n], v_ref[pl.ds(r0, chunk), :]], axis=0)
        o = jnp.dot(lhs, rhs, preferred_element_type=F32)
        mu = jnp.mean(o, axis=-1, keepdims=True)
        d = o - mu
        var = jnp.mean(d * d, axis=-1, keepdims=True)
        gg = g_ref[pl.ds(r0, chunk), :].astype(F32)
        o_ref[pl.ds(r0, chunk), :] = (d * lax.rsqrt(var + EPS) * gain * gg).astype(o_ref.dtype)
        return carry

    lax.fori_loop(0, nc, chunk_out, 0, unroll=True)


def _ret_call(qkvg, s0, dec, gn_gain, heads, dh):
    b, l, _ = qkvg.shape
    chunk = RET_CHUNK
    nc = l // chunk
    seq = lambda off: pl.BlockSpec((None, l, dh), lambda i, h: (i, 0, off + h))
    return pl.pallas_call(
        functools.partial(_ret_kernel, heads=heads, chunk=chunk),
        grid=(b, heads),
        in_specs=[seq(0), seq(heads), seq(2 * heads), seq(3 * heads),
                  pl.BlockSpec((None, None, dh, dh), lambda i, h: (i, h, 0, 0)),
                  pl.BlockSpec((None, None, dh, dh), lambda i, h: (i, heads + h, 0, 0)),
                  pl.BlockSpec(dec.shape, lambda i, h: (0, 0)),
                  pl.BlockSpec((1, dh), lambda i, h: (0, h))],
        out_specs=pl.BlockSpec((None, l, dh), lambda i, h: (i, 0, h)),
        out_shape=jax.ShapeDtypeStruct((b, l, heads * dh), BF16),
        scratch_shapes=[pltpu.VMEM((nc, dh, dh), F32), pltpu.VMEM((nc, dh, dh), F32),
                        pltpu.VMEM((nc, 2 * dh, dh), BF16)],
        compiler_params=_cparams(("arbitrary", "arbitrary")),
        name="retention",
    )(qkvg, qkvg, qkvg, qkvg, s0, s0, dec, gn_gain)


def _filt_time_kernel(emb_ref, w1_ref, b1_ref, w2_ref, b2_ref, w3_ref, b3_ref, w4f_ref, w4b_ref,
                      fr_ref, dl_ref, kern_ref, s_ref, *, seq_len):
    i = pl.program_id(0)
    rows, c = kern_ref.shape[1], kern_ref.shape[2]
    fr = fr_ref[...]
    hdot = _dot3
    z = jnp.sin(fr * (hdot(emb_ref[...], w1_ref[...]) + b1_ref[...]))
    z = jnp.sin(fr * (hdot(z, w2_ref[...]) + b2_ref[...]))
    z = jnp.sin(fr * (hdot(z, w3_ref[...]) + b3_ref[...]))
    pos = (i * rows + lax.broadcasted_iota(jnp.int32, (rows, c), 0)).astype(F32)
    inv = 1.0 / (seq_len - 1.0)
    adl = dl_ref[...]
    hf = hdot(z, w4f_ref[...]) * jnp.exp(-(pos * inv) * adl)
    hr = hdot(z, w4b_ref[...]) * jnp.exp(-((seq_len - pos) * inv) * adl)
    hr = jnp.where(pos == 0.0, 0.0, hr)
    kern_ref[0] = hf
    kern_ref[1] = hr
    part = jnp.sum(jnp.abs(hf) + jnp.abs(hr), axis=0, keepdims=True)

    @pl.when(i == 0)
    def _():
        s_ref[...] = part

    @pl.when(i != 0)
    def _():
        s_ref[...] += part


def _filt_time_call(emb2, w1, b1, w2, b2, w3, b3, w4f, w4b, freq, absdelta, seq_len):
    c = absdelta.shape[1]
    rows = FILT_ROWS
    small = lambda a: pl.BlockSpec(a.shape, lambda i: (0,) * a.ndim)
    return pl.pallas_call(
        functools.partial(_filt_time_kernel, seq_len=seq_len),
        grid=(seq_len // rows,),
        in_specs=[pl.BlockSpec((rows, emb2.shape[1]), lambda i: (i, 0)),
                  small(w1), small(b1), small(w2), small(b2), small(w3), small(b3), small(w4f), small(w4b),
                  small(freq), small(absdelta)],
        out_specs=[pl.BlockSpec((2, rows, c), lambda i: (0, i, 0)),
                   pl.BlockSpec((1, c), lambda i: (0, 0))],
        out_shape=[jax.ShapeDtypeStruct((2, seq_len, c), F32),
                   jax.ShapeDtypeStruct((1, c), F32)],
        compiler_params=_cparams(("arbitrary",)),
        name="filt_time",
    )(emb2, w1, b1, w2, b2, w3, b3, w4f, w4b, freq, absdelta)


@functools.lru_cache(maxsize=None)
def _fft_tables(seq_len):
    n = 2 * seq_len
    n1s = FFT_N1
    n2s = n // n1s
    half = n1s // 2
    k1n = half + 1
    n1 = np.arange(half)
    k1 = np.arange(k1n)
    th = 2.0 * np.pi * (np.outer(k1, n1) % n1s) / n1s
    herm = np.where((k1 == 0) | (k1 == half), 1.0, 2.0)
    fa = np.concatenate([np.cos(th), -np.sin(th)[1:half]], axis=0)
    sgn = np.concatenate([(-1.0) ** k1, (-1.0) ** k1[1:half]])
    fai = np.concatenate([np.cos(th) * herm[:, None], (-np.sin(th) * herm[:, None])[1:half]], axis=0).T / n
    eye = np.eye(SUBLANES)
    fk = np.kron(fa, eye)
    fks = np.kron(fa * sgn[:, None], eye)
    fki = np.kron(fai, eye)
    k2 = np.arange(n2s)
    n2 = np.arange(n2s)
    m = np.zeros((k1n, 2 * n2s, 2 * n2s))
    for a in range(k1n):
        ang = 2.0 * np.pi * (np.outer(a + n1s * k2, n2) % n) / n
        gr, gi = np.cos(ang), -np.sin(ang)
        m[a] = np.block([[gr, -gi], [gi, gr]])
    mt = np.transpose(m, (0, 2, 1))
    return dict(fk=fk, fkf=np.concatenate([fk, fks], axis=1), fki=fki, m=m, mt=mt,
                k1n=k1n, n2s=n2s, half=half)


def _stage_n1(src_refs, mat, dst_ref, n2s):
    rows = dst_ref.shape[0]
    for j in range(n2s // SUBLANES):
        sl = slice(j * SUBLANES, (j + 1) * SUBLANES)
        parts = [r[:, sl, :] for r in src_refs]
        xg = jnp.concatenate([p.reshape(p.shape[0] * SUBLANES, p.shape[2]) for p in parts], axis=0)
        a = jnp.dot(mat, xg.astype(BF16), preferred_element_type=F32)
        dst_ref[:, sl, :] = a.reshape(rows, SUBLANES, a.shape[1])


def _n2_input(a_s, k1, half, n2s):
    if 0 < k1 < half:
        return jnp.concatenate([a_s[k1], a_s[half + k1]], axis=0)
    return a_s[k1]


def _filt_spec_kernel(kern_ref, s_ref, fkf_ref, m_ref, kf_ref, a_s, *, k1n, n2s):
    half = k1n - 1
    _stage_n1([kern_ref.at[0], kern_ref.at[1]], fkf_ref[...], a_s, n2s)
    inv = 1.0 / (s_ref[...] + EPS)
    for k1 in range(k1n):
        a = _n2_input(a_s, k1, half, n2s).astype(BF16)
        kf_ref[k1] = jnp.dot(m_ref[k1, :, :a.shape[0]], a, preferred_element_type=F32) * inv


def _filt_spec_call(kern4, s, tabs):
    _, half, n2s, c = kern4.shape
    ct = HY_CT
    k1n = tabs["k1n"]
    return pl.pallas_call(
        functools.partial(_filt_spec_kernel, k1n=k1n, n2s=n2s),
        grid=(c // ct,),
        in_specs=[pl.BlockSpec((2, half, n2s, ct), lambda j: (0, 0, 0, j)),
                  pl.BlockSpec((1, ct), lambda j: (0, j)),
                  _const_spec(tabs["fkf"].shape), _const_spec(tabs["m"].shape)],
        out_specs=pl.BlockSpec((k1n, 2 * n2s, ct), lambda j: (0, 0, j)),
        out_shape=jax.ShapeDtypeStruct((k1n, 2 * n2s, c), F32),
        scratch_shapes=[pltpu.VMEM((2 * half, n2s, ct), F32)],
        compiler_params=_cparams(("arbitrary",)),
        name="filt_spec",
    )(kern4, s, tabs["fkf"], tabs["m"])


def _hyena_kernel(u_ref, x0_ref, bias_ref, kf_ref, fk_ref, fki_ref, m_ref, mt_ref, o_ref, u_s, a_s,
                  *, k1n, n2s):
    half = k1n - 1
    ct = u_ref.shape[2]
    u_s[...] = u_ref[...].astype(F32)
    _stage_n1([u_s], fk_ref[...], a_s, n2s)

    for k1 in range(k1n):
        a = _n2_input(a_s, k1, half, n2s).astype(BF16)
        x = jnp.dot(m_ref[k1, :, :a.shape[0]], a, preferred_element_type=F32)
        xr, xi = x[:n2s], x[n2s:]
        kr, ki = kf_ref[k1, :n2s, :], kf_ref[k1, n2s:, :]
        y = jnp.concatenate([xr * kr - xi * ki, xr * ki + xi * kr], axis=0).astype(BF16)
        if 0 < k1 < half:
            z = jnp.dot(mt_ref[k1], y, preferred_element_type=F32)
            a_s[k1] = z[:n2s]
            a_s[half + k1] = z[n2s:]
        else:
            a_s[k1] = jnp.dot(mt_ref[k1, :n2s, :], y, preferred_element_type=F32)

    fki = fki_ref[...]
    bias = bias_ref[...]
    pair = 2 * SUBLANES
    for j in range(n2s // pair):
        ys = []
        for jj in range(2):
            sl = slice(j * pair + jj * SUBLANES, j * pair + (jj + 1) * SUBLANES)
            zg = a_s[:, sl, :].reshape(2 * half * SUBLANES, ct)
            ys.append(jnp.dot(fki, zg.astype(BF16), preferred_element_type=F32).reshape(half, SUBLANES, ct))
        y = jnp.concatenate(ys, axis=1)
        sl = slice(j * pair, (j + 1) * pair)
        o_ref[:, sl, :] = (x0_ref[:, sl, :].astype(F32) * (y + u_s[:, sl, :] * bias)).astype(o_ref.dtype)


def _hyena_call(u4, x04, bias, kf, tabs):
    b, half, n2s, c = u4.shape
    ct = HY_CT
    k1n = tabs["k1n"]
    seq = pl.BlockSpec((None, half, n2s, ct), lambda j, i: (i, 0, 0, j))
    return pl.pallas_call(
        functools.partial(_hyena_kernel, k1n=k1n, n2s=n2s),
        grid=(c // ct, b),
        in_specs=[seq, seq,
                  pl.BlockSpec((1, ct), lambda j, i: (0, j)),
                  pl.BlockSpec((k1n, 2 * n2s, ct), lambda j, i: (0, 0, j), pipeline_mode=pl.Buffered(1)),
                  _const_spec(tabs["fk"].shape), _const_spec(tabs["fki"].shape),
                  _const_spec(tabs["m"].shape), _const_spec(tabs["mt"].shape)],
        out_specs=pl.BlockSpec((None, half, n2s, ct), lambda j, i: (i, 0, 0, j)),
        out_shape=jax.ShapeDtypeStruct((b, half, n2s, c), BF16),
        scratch_shapes=[pltpu.VMEM((half, n2s, ct), F32), pltpu.VMEM((2 * half, n2s, ct), F32)],
        compiler_params=_cparams(("arbitrary", "arbitrary")),
        name="hyena",
    )(u4, x04, bias, kf, tabs["fk"], tabs["fki"], tabs["m"], tabs["mt"])


def _out_ffn_kernel(ret_ref, hy_ref, x_ref, mod_ref, hg_ref, n2_ref, fn_ref, wo_ref, wgu_ref, wd_ref,
                    o_ref, *, d_ret, d_ff, ff_bounds, row_parts):
    rows = x_ref.shape[0] // row_parts
    groups = [slice(r * rows, (r + 1) * rows) for r in range(row_parts)]
    gain2 = n2_ref[...] * (1.0 + mod_ref[4:5, :])
    x1s, h2s = [], []
    for rs in groups:
        hy = hy_ref[rs, :].astype(F32)
        hms = jnp.mean(hy * hy, axis=-1, keepdims=True)
        hyn = (hy * lax.rsqrt(hms + EPS) * hg_ref[...]).astype(BF16)
        mix = jnp.dot(ret_ref[rs, :], wo_ref[:d_ret, :], preferred_element_type=F32)
        mix = mix + jnp.dot(hyn, wo_ref[d_ret:, :], preferred_element_type=F32)
        x1s.append(x_ref[rs, :] + mod_ref[2:3, :] * mix)
    for x1 in x1s:
        ms = jnp.mean(x1 * x1, axis=-1, keepdims=True)
        h2s.append((x1 * lax.rsqrt(ms + EPS) * gain2 + mod_ref[3:4, :]).astype(BF16))
    accs = [None] * row_parts
    for lo, hi in zip(ff_bounds[:-1], ff_bounds[1:]):
        acts = []
        for h2 in h2s:
            g = jnp.dot(h2, wgu_ref[:, lo:hi], preferred_element_type=F32)
            u = jnp.dot(h2, wgu_ref[:, d_ff + lo:d_ff + hi], preferred_element_type=F32)
            acts.append((_silu(g) * u).astype(BF16))
        for r, a in enumerate(acts):
            d = jnp.dot(a, wd_ref[lo:hi, :], preferred_element_type=F32)
            accs[r] = d if accs[r] is None else accs[r] + d
    for rs, x1, acc in zip(groups, x1s, accs):
        x2 = x1 + mod_ref[5:6, :] * acc
        ms2 = jnp.mean(x2 * x2, axis=-1, keepdims=True)
        o_ref[rs, :] = x2 * lax.rsqrt(ms2 + EPS) * fn_ref[...]


def _ff_bounds(d_ff, parts=2):
    tiles = d_ff // MXU_WIDTH
    assert tiles * MXU_WIDTH == d_ff
    cuts = [-(-tiles * p // parts) for p in range(parts + 1)]
    return tuple(c * MXU_WIDTH for c in cuts)


def _out_ffn_call(ret, hy, x, mod, hy_gain, norm2, final_norm, w_out, w_gu, w_down):
    b, l, d = x.shape
    d_ret = ret.shape[2]
    d_hy = hy.shape[2]
    d_ff = w_down.shape[0]
    tm = ROW_TILE
    return pl.pallas_call(
        functools.partial(_out_ffn_kernel, d_ret=d_ret, d_ff=d_ff, ff_bounds=_ff_bounds(d_ff),
                          row_parts=2),
        grid=(b, l // tm),
        in_specs=[pl.BlockSpec((None, tm, d_ret), lambda i, t: (i, t, 0)),
                  pl.BlockSpec((None, tm, d_hy), lambda i, t: (i, t, 0)),
                  pl.BlockSpec((None, tm, d), lambda i, t: (i, t, 0)),
                  pl.BlockSpec((None, N_MOD, d), lambda i, t: (i, 0, 0)),
                  pl.BlockSpec((1, d_hy), lambda i, t: (0, 0)),
                  pl.BlockSpec((1, d), lambda i, t: (0, 0)),
                  pl.BlockSpec((1, d), lambda i, t: (0, 0)),
                  _const_spec(w_out.shape), _const_spec(w_gu.shape), _const_spec(w_down.shape)],
        out_specs=pl.BlockSpec((None, tm, d), lambda i, t: (i, t, 0)),
        out_shape=jax.ShapeDtypeStruct((b, l, d), x.dtype),
        compiler_params=_cparams(("arbitrary", "arbitrary")),
        name="out_ffn",
    )(ret, hy, x, mod, hy_gain, norm2, final_norm, w_out, w_gu, w_down)


@functools.lru_cache(maxsize=None)
def _rope_tables(seq_len, dh):
    n = dh // 4
    t = np.arange(seq_len)
    inv = ROPE_BASE ** (-np.arange(n, dtype=np.float64) / n)
    ang = np.concatenate([(t // GRID_W)[:, None] * inv, (t % GRID_W)[:, None] * inv], axis=-1)
    cc = np.concatenate([np.cos(ang), np.cos(ang)], axis=-1)
    ss = np.concatenate([-np.sin(ang), np.sin(ang)], axis=-1)
    return cc.astype(np.float32), ss.astype(np.float32)


@functools.lru_cache(maxsize=None)
def _filter_tables(seq_len, emb_dim, emb_pad, channels):
    t = np.linspace(0.0, 1.0, seq_len)[:, None]
    bands = (emb_dim - 1) // 2
    f = np.linspace(1e-4, bands - 1, bands)[None, :]
    wpos = 2.0 * np.pi * np.arange(seq_len)[:, None] / seq_len
    emb = np.concatenate([t, np.cos(f * wpos), -np.sin(f * wpos)], axis=-1)
    emb = np.pad(emb, ((0, 0), (0, emb_pad - emb_dim)))
    emb_rev = np.concatenate([emb[:1], emb[:0:-1]], axis=0)
    emb2 = np.concatenate([emb, emb_rev], axis=1)
    max_decay = math.log(FILTER_DECAY_TARGET) / FILTER_DECAY_FAST
    min_decay = math.log(FILTER_DECAY_TARGET) / FILTER_DECAY_SLOW
    absdelta = np.abs(np.linspace(min_decay, max_decay, channels))[None, :]
    return emb2.astype(np.float32), absdelta.astype(np.float32)


def _block_diag2(w):
    z = jnp.zeros_like(w)
    return jnp.concatenate([jnp.concatenate([w, z], axis=1), jnp.concatenate([z, w], axis=1)], axis=0)


def kernel(x, c, ctx, c_ctx, w_mod, b_mod, norm1, norm2, w_in, ret_decay, ret_gn_gain, hy_short_w,
           hy_short_b, hy_w1, hy_b1, hy_w2, hy_b2, hy_w3, hy_b3, hy_w4, hy_freq, hy_bias, hy_out_norm,
           w_out, w_gate_up, w_down, final_norm):
    b, seq_len, d = x.shape
    assert w_mod.shape[0] == 1, "single-layer block"
    heads = RET_HEADS
    d_ret = ret_gn_gain.shape[1]
    dh = d_ret // heads
    d_hy = hy_bias.shape[1]
    assert dh == LANES and d_hy % HY_CT == 0 and seq_len % (FFT_N1 // 2) == 0

    rows = -(-(b + 1) // SUBLANES) * SUBLANES
    c_rows = jnp.zeros((rows, d), F32).at[:b].set(c).at[b].set(c_ctx)
    mod_all = _mod_call(c_rows, w_mod[0], b_mod[0][None, :])
    mod = mod_all[:b].reshape(b, N_MOD, d)
    mod_c = mod_all[b].reshape(N_MOD, d)

    w_in_b = w_in[0].astype(BF16)
    dec = jnp.broadcast_to(ret_decay[0].reshape(2 * heads, 1), (2 * heads, LANES))
    n1g = norm1[0][None, :]

    s0 = _ctx_call(ctx, n1g, mod_c[0:1], mod_c[1:2], w_in_b[:, d_ret:3 * d_ret], dec, heads, dh)
    cc, ss = (jnp.asarray(t) for t in _rope_tables(seq_len, dh))
    qkvg, x0c, u = _inproj_call(x, mod, n1g, w_in_b, cc, ss, hy_short_w[0], hy_short_b, heads, dh, d_hy)
    ret = _ret_call(qkvg, s0, dec, ret_gn_gain, heads, dh)

    tabs = dict(_fft_tables(seq_len))
    for name in ("fk", "fkf", "fki", "m", "mt"):
        tabs[name] = jnp.asarray(tabs[name], dtype=F32).astype(BF16)
    half, n2s = tabs["half"], tabs["n2s"]
    emb_dim = hy_w1.shape[1]
    emb_pad = -(-emb_dim // SUBLANES) * SUBLANES
    emb2, absdelta = (jnp.asarray(t) for t in _filter_tables(seq_len, emb_dim, emb_pad, d_hy))
    w1p = jnp.pad(hy_w1[0], ((0, emb_pad - emb_dim), (0, 0)))
    two = lambda a: jnp.concatenate([a, a], axis=1)
    w4 = hy_w4[0]
    zero4 = jnp.zeros((w4.shape[0], d_hy), F32)
    kern, ksum = _filt_time_call(
        emb2, _block_diag2(w1p), two(hy_b1), _block_diag2(hy_w2[0]), two(hy_b2), _block_diag2(hy_w3[0]),
        two(hy_b3), jnp.concatenate([w4[:, :d_hy], zero4], axis=0), jnp.concatenate([zero4, w4[:, d_hy:]], axis=0),
        two(hy_freq), absdelta, seq_len)
    kf = _filt_spec_call(kern.reshape(2, half, n2s, d_hy), ksum, tabs)
    hy = _hyena_call(u.reshape(b, half, n2s, d_hy), x0c.reshape(b, half, n2s, d_hy), hy_bias, kf, tabs)
    hy = hy.reshape(b, seq_len, d_hy)

    return _out_ffn_call(ret, hy, x, mod, hy_out_norm, norm2[0][None, :], final_norm[None, :],
                         w_out[0].astype(BF16), w_gate_up[0].astype(BF16), w_down[0].astype(BF16))
```

```python
import functools
import math

import jax
import jax.numpy as jnp
import numpy as np
from jax import lax
from jax.experimental import pallas as pl
from jax.experimental.pallas import tpu as pltpu

F32 = jnp.float32
BF16 = jnp.bfloat16
HIGHEST = lax.Precision.HIGHEST

RET_HEADS = 4
GRID_W = 64
ROPE_BASE = 10000.0
N_MOD = 6
HYENA_PROJ = 3
FILTER_DECAY_FAST = 0.3
FILTER_DECAY_SLOW = 1.5
FILTER_DECAY_TARGET = 1e-2
EPS = 1e-6

LANES = 128
SUBLANES = 8
MXU_WIDTH = 256
VMEM_LIMIT_BYTES = 56 * 1024 * 1024

RET_CHUNK = 256
ROW_TILE = 1024
IN_ROW_TILE = 1024
GROUP_ROWS = 256
HY_CT = 256
FILT_ROWS = 1024
FFT_N1 = 64


def _silu(v):
    return v / (1.0 + jnp.exp(-v))


def _dot3(a, b):
    a_hi = a.astype(BF16)
    b_hi = b.astype(BF16)
    a_lo = (a - a_hi.astype(F32)).astype(BF16)
    b_lo = (b - b_hi.astype(F32)).astype(BF16)
    dot = functools.partial(jnp.dot, preferred_element_type=F32)
    return dot(a_hi, b_hi) + (dot(a_hi, b_lo) + dot(a_lo, b_hi))


def _cparams(sem, vmem=VMEM_LIMIT_BYTES):
    return pltpu.CompilerParams(dimension_semantics=sem, vmem_limit_bytes=vmem)


def _const_spec(shape):
    nd = len(shape)
    return pl.BlockSpec(shape, lambda *_: (0,) * nd, pipeline_mode=pl.Buffered(1))


def _mod_kernel(c_ref, w_ref, b_ref, o_ref):
    s = _silu(c_ref[...])
    o_ref[...] = jnp.dot(s, w_ref[...], precision=HIGHEST, preferred_element_type=F32) + b_ref[...]


def _mod_call(c_rows, w_mod, b_mod):
    rows, d = c_rows.shape
    n = w_mod.shape[1]
    tn = 1536
    return pl.pallas_call(
        _mod_kernel,
        grid=(n // tn,),
        in_specs=[pl.BlockSpec((rows, d), lambda j: (0, 0)),
                  pl.BlockSpec((d, tn), lambda j: (0, j)),
                  pl.BlockSpec((1, tn), lambda j: (0, j))],
        out_specs=pl.BlockSpec((rows, tn), lambda j: (0, j)),
        out_shape=jax.ShapeDtypeStruct((rows, n), F32),
        compiler_params=_cparams(("arbitrary",)),
        name="mod",
    )(c_rows, w_mod, b_mod)


def _ctx_kernel(ctx_ref, n1_ref, sh_ref, sc_ref, w_ref, dec_ref, s_ref, *, heads, dh):
    xc = ctx_ref[...]
    n_ctx = xc.shape[0]
    ms = jnp.mean(xc * xc, axis=-1, keepdims=True)
    hc = (xc * lax.rsqrt(ms + EPS) * n1_ref[...]) * (1.0 + sc_ref[...]) + sh_ref[...]
    kv = jnp.dot(hc.astype(BF16), w_ref[...], preferred_element_type=F32)
    lg = jnp.log1p(-jnp.exp(dec_ref[...]))
    pos = lax.broadcasted_iota(jnp.int32, (n_ctx, dh), 0).astype(F32)
    k_scale = dh ** -0.5
    d_ret = heads * dh
    tdims = (((0,), (0,)), ((), ()))
    for h in range(heads):
        kh = kv[:, h * dh:(h + 1) * dh] * k_scale
        vh = kv[:, d_ret + h * dh:d_ret + (h + 1) * dh].astype(BF16)
        wf = jnp.exp(lg[h:h + 1, :] * (n_ctx - 1.0 - pos))
        wb = jnp.exp(lg[heads + h:heads + h + 1, :] * pos)
        s_ref[h] = lax.dot_general((kh * wf).astype(BF16), vh, tdims, preferred_element_type=F32)
        s_ref[heads + h] = lax.dot_general((kh * wb).astype(BF16), vh, tdims, preferred_element_type=F32)


def _ctx_call(ctx, norm1, shift_c, scale_c, w_kv, dec, heads, dh):
    b, n_ctx, d = ctx.shape
    return pl.pallas_call(
        functools.partial(_ctx_kernel, heads=heads, dh=dh),
        grid=(b,),
        in_specs=[pl.BlockSpec((None, n_ctx, d), lambda i: (i, 0, 0)),
                  pl.BlockSpec((1, d), lambda i: (0, 0)),
                  pl.BlockSpec((1, d), lambda i: (0, 0)),
                  pl.BlockSpec((1, d), lambda i: (0, 0)),
                  pl.BlockSpec(w_kv.shape, lambda i: (0, 0)),
                  pl.BlockSpec(dec.shape, lambda i: (0, 0))],
        out_specs=pl.BlockSpec((None, 2 * heads, dh, dh), lambda i: (i, 0, 0, 0)),
        out_shape=jax.ShapeDtypeStruct((b, 2 * heads, dh, dh), F32),
        compiler_params=_cparams(("arbitrary",)),
        name="ctx_state",
    )(ctx, norm1, shift_c, scale_c, w_kv, dec)


def _inproj_kernel(x_ref, mod_ref, n1_ref, w_ref, cc_ref, ss_ref, sw_ref, sb_ref,
                   qkvg_ref, x0_ref, u_ref, hy_s, tail_s, *, heads, dh, d_hy, nt):
    t = pl.program_id(1)
    tm = x_ref.shape[0]
    d_ret = heads * dh

    @pl.when((pl.program_id(0) == 0) & (t == 0))
    def _():
        hy_s[...] = jnp.zeros(hy_s.shape, F32)
        tail_s[...] = jnp.zeros(tail_s.shape, F32)

    r8 = lax.broadcasted_iota(jnp.int32, (SUBLANES, d_hy), 0)

    def conv(cur, prev_row, next_row, sl):
        w0, w1, w2, bb = sw_ref[0:1, sl], sw_ref[1:2, sl], sw_ref[2:3, sl], sb_ref[:, sl]
        up = pltpu.roll(cur, 1, 0)
        dn = pltpu.roll(cur, tm - 1, 0)
        body = w0 * up + w1 * cur + w2 * dn + bb
        e = tm - SUBLANES
        head = (w0 * jnp.where(r8 == 0, prev_row, up[:SUBLANES]) + w1 * cur[:SUBLANES]
                + w2 * dn[:SUBLANES] + bb)
        tail = (w0 * up[e:] + w1 * cur[e:]
                + w2 * jnp.where(r8 == SUBLANES - 1, next_row, dn[e:]) + bb)
        return jnp.concatenate([head, body[SUBLANES:e], tail], axis=0)

    @pl.when(t < nt)
    def _():
        groups = [slice(r, r + GROUP_ROWS) for r in range(0, tm, GROUP_ROWS)]
        gain = n1_ref[...] * (1.0 + mod_ref[1:2, :])
        hbs = []
        for rs in groups:
            x = x_ref[rs, :]
            ms = jnp.mean(x * x, axis=-1, keepdims=True)
            hbs.append((x * lax.rsqrt(ms + EPS) * gain + mod_ref[0:1, :]).astype(BF16))
        k_scale = dh ** -0.5

        def proj(base, width):
            return [jnp.dot(hb, w_ref[:, base:base + width], preferred_element_type=F32) for hb in hbs]

        def hyena_cols(j):
            sl = slice(j * d_hy, (j + 1) * d_hy)
            cur = hy_s[:, sl]
            news = proj(4 * d_ret + j * d_hy, d_hy)
            out = conv(cur, tail_s[SUBLANES - 1:SUBLANES, sl], news[0][0:1], sl)
            tail_s[:, sl] = jnp.where(t == 0, 0.0, cur[tm - SUBLANES:])
            for rs, new in zip(groups, news):
                hy_s[rs, sl] = new
            return out

        def roped_cols(base, scale):
            for rs, p in zip(groups, proj(base, d_ret)):
                cc = cc_ref[rs, :]
                ss = ss_ref[rs, :]
                for j in range(heads):
                    pj = p[:, j * dh:(j + 1) * dh]
                    r = pj * cc + pltpu.roll(pj, dh // 2, 1) * ss
                    if scale != 1.0:
                        r = r * scale
                    qkvg_ref[rs, base + j * dh:base + (j + 1) * dh] = r.astype(BF16)

        roped_cols(0, 1.0)
        x0_ref[...] = hyena_cols(0).astype(x0_ref.dtype)
        roped_cols(d_ret, k_scale)
        x1c = hyena_cols(1)
        for rs, g in zip(groups, proj(3 * d_ret, d_ret)):
            qkvg_ref[rs, 3 * d_ret:4 * d_ret] = _silu(g).astype(BF16)
        u_ref[...] = (x1c * hyena_cols(2)).astype(u_ref.dtype)
        for rs, v in zip(groups, proj(2 * d_ret, d_ret)):
            qkvg_ref[rs, 2 * d_ret:3 * d_ret] = v.astype(BF16)

    @pl.when(t == nt)
    def _():
        zero = jnp.zeros((1, d_hy), F32)
        outs = []
        for j in range(HYENA_PROJ):
            sl = slice(j * d_hy, (j + 1) * d_hy)
            outs.append(conv(hy_s[:, sl], tail_s[SUBLANES - 1:SUBLANES, sl], zero, sl))
        x0_ref[...] = outs[0].astype(x0_ref.dtype)
        u_ref[...] = (outs[1] * outs[2]).astype(u_ref.dtype)


def _inproj_call(x, mod, norm1, w_in, cc, ss, short_w, short_b, heads, dh, d_hy):
    b, l, d = x.shape
    tm = IN_ROW_TILE
    nt = l // tm
    d_ret = heads * dh
    cur = lambda i, t: (i, jnp.minimum(t, nt - 1), 0)
    lag = lambda i, t: (i, jnp.maximum(t - 1, 0), 0)
    return pl.pallas_call(
        functools.partial(_inproj_kernel, heads=heads, dh=dh, d_hy=d_hy, nt=nt),
        grid=(b, nt + 1),
        in_specs=[pl.BlockSpec((None, tm, d), cur),
                  pl.BlockSpec((None, N_MOD, d), lambda i, t: (i, 0, 0)),
                  pl.BlockSpec((1, d), lambda i, t: (0, 0)),
                  _const_spec(w_in.shape),
                  pl.BlockSpec((tm, dh), lambda i, t: (jnp.minimum(t, nt - 1), 0)),
                  pl.BlockSpec((tm, dh), lambda i, t: (jnp.minimum(t, nt - 1), 0)),
                  pl.BlockSpec(short_w.shape, lambda i, t: (0, 0)),
                  pl.BlockSpec(short_b.shape, lambda i, t: (0, 0))],
        out_specs=[pl.BlockSpec((None, tm, 4 * d_ret), cur),
                   pl.BlockSpec((None, tm, d_hy), lag),
                   pl.BlockSpec((None, tm, d_hy), lag)],
        out_shape=[jax.ShapeDtypeStruct((b, l, 4 * d_ret), BF16),
                   jax.ShapeDtypeStruct((b, l, d_hy), BF16),
                   jax.ShapeDtypeStruct((b, l, d_hy), BF16)],
        scratch_shapes=[pltpu.VMEM((tm, HYENA_PROJ * d_hy), F32),
                        pltpu.VMEM((SUBLANES, HYENA_PROJ * d_hy), F32)],
        compiler_params=_cparams(("arbitrary", "arbitrary")),
        name="in_proj",
    )(x, mod, norm1, w_in, cc, ss, short_w, short_b)


def _ret_kernel(q_ref, k_ref, v_ref, g_ref, sf0_ref, sb0_ref, dec_ref, gain_ref,
                o_ref, kvf_s, kvb_s, st_s, *, heads, chunk):
    l, dh = q_ref.shape
    nc = l // chunk
    h = pl.program_id(1)
    lgf = jnp.log1p(-jnp.exp(dec_ref[pl.ds(h, 1), :]))
    lgb = jnp.log1p(-jnp.exp(dec_ref[pl.ds(h + heads, 1), :]))
    il = lax.broadcasted_iota(jnp.int32, (chunk, dh), 0).astype(F32)
    kw_f = jnp.exp(lgf * (chunk - 1.0 - il))
    kw_b = jnp.exp(lgb * il)
    qw_f = jnp.exp(lgf * (il + 1.0))
    qw_b = jnp.exp(lgb * (chunk - il))
    cd_f = jnp.exp(lgf * float(chunk))
    cd_b = jnp.exp(lgb * float(chunk))
    reps = chunk // dh
    lgf_c = jnp.concatenate([lgf] * reps, axis=1)
    lgb_c = jnp.concatenate([lgb] * reps, axis=1)
    ii = lax.broadcasted_iota(jnp.int32, (chunk, chunk), 0)
    jj = lax.broadcasted_iota(jnp.int32, (chunk, chunk), 1)
    diff = (ii - jj).astype(F32)
    dmask = (jnp.where(diff >= 0, jnp.exp(lgf_c * jnp.maximum(diff, 0.0)), 0.0)
             + jnp.where(diff <= 0, jnp.exp(lgb_c * jnp.maximum(-diff, 0.0)), 0.0))
    tdims = (((0,), (0,)), ((), ()))
    ntdims = (((1,), (1,)), ((), ()))

    def chunk_states(n, carry):
        r0 = pl.multiple_of(n * chunk, chunk)
        kr = k_ref[pl.ds(r0, chunk), :].astype(F32)
        vv = v_ref[pl.ds(r0, chunk), :]
        kvf_s[n] = lax.dot_general((kr * kw_f).astype(BF16), vv, tdims, preferred_element_type=F32)
        kvb_s[n] = lax.dot_general((kr * kw_b).astype(BF16), vv, tdims, preferred_element_type=F32)
        return carry

    lax.fori_loop(0, nc, chunk_states, 0, unroll=True)

    def scan_f(n, s):
        st_s[n, :dh, :] = s.astype(BF16)
        return s * cd_f + kvf_s[n]

    lax.fori_loop(0, nc, scan_f, sf0_ref[...])

    def scan_b(m, s):
        n = nc - 1 - m
        st_s[n, dh:, :] = s.astype(BF16)
        return s * cd_b + kvb_s[n]

    lax.fori_loop(0, nc, scan_b, sb0_ref[...])

    gain = gain_ref[...]

    def chunk_out(n, carry):
        r0 = pl.multiple_of(n * chunk, chunk)
        qb = q_ref[pl.ds(r0, chunk), :]
        qr = qb.astype(F32)
        sc = lax.dot_general(qb, k_ref[pl.ds(r0, chunk), :], ntdims, preferred_element_type=F32)
        lhs = jnp.concatenate([(qr * qw_f).astype(BF16), (qr * qw_b).astype(BF16),
                               (sc * dmask).astype(BF16)], axis=1)
        rhs = jnp.concatenate([st_s[n], v_ref[pl.ds(r0, chunk), :]], axis=0)
        o = jnp.dot(lhs, rhs, preferred_element_type=F32)
        mu = jnp.mean(o, axis=-1, keepdims=True)
        d = o - mu
        var = jnp.mean(d * d, axis=-1, keepdims=True)
        gg = g_ref[pl.ds(r0, chunk), :].astype(F32)
        o_ref[pl.ds(r0, chunk), :] = (d * lax.rsqrt(var + EPS) * gain * gg).astype(o_ref.dtype)
        return carry

    lax.fori_loop(0, nc, chunk_out, 0, unroll=True)


def _ret_call(qkvg, s0, dec, gn_gain, heads, dh):
    b, l, _ = qkvg.shape
    chunk = RET_CHUNK
    nc = l // chunk
    seq = lambda off: pl.BlockSpec((None, l, dh), lambda i, h: (i, 0, off + h))
    return pl.pallas_call(
        functools.partial(_ret_kernel, heads=heads, chunk=chunk),
        grid=(b, heads),
        in_specs=[seq(0), seq(heads), seq(2 * heads), seq(3 * heads),
                  pl.BlockSpec((None, None, dh, dh), lambda i, h: (i, h, 0, 0)),
                  pl.BlockSpec((None, None, dh, dh), lambda i, h: (i, heads + h, 0, 0)),
                  pl.BlockSpec(dec.shape, lambda i, h: (0, 0)),
                  pl.BlockSpec((1, dh), lambda i, h: (0, h))],
        out_specs=pl.BlockSpec((None, l, dh), lambda i, h: (i, 0, h)),
        out_shape=jax.ShapeDtypeStruct((b, l, heads * dh), BF16),
        scratch_shapes=[pltpu.VMEM((nc, dh, dh), F32), pltpu.VMEM((nc, dh, dh), F32),
                        pltpu.VMEM((nc, 2 * dh, dh), BF16)],
        compiler_params=_cparams(("arbitrary", "arbitrary")),
        name="retention",
    )(qkvg, qkvg, qkvg, qkvg, s0, s0, dec, gn_gain)


def _filt_time_kernel(emb_ref, w1_ref, b1_ref, w2_ref, b2_ref, w3_ref, b3_ref, w4f_ref, w4b_ref,
                      fr_ref, dl_ref, kern_ref, s_ref, *, seq_len):
    i = pl.program_id(0)
    rows, c = kern_ref.shape[1], kern_ref.shape[2]
    fr = fr_ref[...]
    hdot = _dot3
    z = jnp.sin(fr * (hdot(emb_ref[...], w1_ref[...]) + b1_ref[...]))
    z = jnp.sin(fr * (hdot(z, w2_ref[...]) + b2_ref[...]))
    z = jnp.sin(fr * (hdot(z, w3_ref[...]) + b3_ref[...]))
    pos = (i * rows + lax.broadcasted_iota(jnp.int32, (rows, c), 0)).astype(F32)
    inv = 1.0 / (seq_len - 1.0)
    adl = dl_ref[...]
    hf = hdot(z, w4f_ref[...]) * jnp.exp(-(pos * inv) * adl)
    hr = hdot(z, w4b_ref[...]) * jnp.exp(-((seq_len - pos) * inv) * adl)
    hr = jnp.where(pos == 0.0, 0.0, hr)
    kern_ref[0] = hf
    kern_ref[1] = hr
    part = jnp.sum(jnp.abs(hf) + jnp.abs(hr), axis=0, keepdims=True)

    @pl.when(i == 0)
    def _():
        s_ref[...] = part

    @pl.when(i != 0)
    def _():
        s_ref[...] += part


def _filt_time_call(emb2, w1, b1, w2, b2, w3, b3, w4f, w4b, freq, absdelta, seq_len):
    c = absdelta.shape[1]
    rows = FILT_ROWS
    small = lambda a: pl.BlockSpec(a.shape, lambda i: (0,) * a.ndim)
    return pl.pallas_call(
        functools.partial(_filt_time_kernel, seq_len=seq_len),
        grid=(seq_len // rows,),
        in_specs=[pl.BlockSpec((rows, emb2.shape[1]), lambda i: (i, 0)),
                  small(w1), small(b1), small(w2), small(b2), small(w3), small(b3), small(w4f), small(w4b),
                  small(freq), small(absdelta)],
        out_specs=[pl.BlockSpec((2, rows, c), lambda i: (0, i, 0)),
                   pl.BlockSpec((1, c), lambda i: (0, 0))],
        out_shape=[jax.ShapeDtypeStruct((2, seq_len, c), F32),
                   jax.ShapeDtypeStruct((1, c), F32)],
        compiler_params=_cparams(("arbitrary",)),
        name="filt_time",
    )(emb2, w1, b1, w2, b2, w3, b3, w4f, w4b, freq, absdelta)


@functools.lru_cache(maxsize=None)
def _fft_tables(seq_len):
    n = 2 * seq_len
    n1s = FFT_N1
    n2s = n // n1s
    half = n1s // 2
    k1n = half + 1
    n1 = np.arange(half)
    k1 = np.arange(k1n)
    th = 2.0 * np.pi * (np.outer(k1, n1) % n1s) / n1s
    herm = np.where((k1 == 0) | (k1 == half), 1.0, 2.0)
    fa = np.concatenate([np.cos(th), -np.sin(th)[1:half]], axis=0)
    sgn = np.concatenate([(-1.0) ** k1, (-1.0) ** k1[1:half]])
    fai = np.concatenate([np.cos(th) * herm[:, None], (-np.sin(th) * herm[:, None])[1:half]], axis=0).T / n
    eye = np.eye(SUBLANES)
    fk = np.kron(fa, eye)
    fks = np.kron(fa * sgn[:, None], eye)
    fki = np.kron(fai, eye)
    k2 = np.arange(n2s)
    n2 = np.arange(n2s)
    m = np.zeros((k1n, 2 * n2s, 2 * n2s))
    for a in range(k1n):
        ang = 2.0 * np.pi * (np.outer(a + n1s * k2, n2) % n) / n
        gr, gi = np.cos(ang), -np.sin(ang)
        m[a] = np.block([[gr, -gi], [gi, gr]])
    mt = np.transpose(m, (0, 2, 1))
    return dict(fk=fk, fkf=np.concatenate([fk, fks], axis=1), fki=fki, m=m, mt=mt,
                k1n=k1n, n2s=n2s, half=half)


def _stage_n1(src_refs, mat, dst_ref, n2s):
    rows = dst_ref.shape[0]
    for j in range(n2s // SUBLANES):
        sl = slice(j * SUBLANES, (j + 1) * SUBLANES)
        parts = [r[:, sl, :] for r in src_refs]
        xg = jnp.concatenate([p.reshape(p.shape[0] * SUBLANES, p.shape[2]) for p in parts], axis=0)
        a = jnp.dot(mat, xg.astype(BF16), preferred_element_type=F32)
        dst_ref[:, sl, :] = a.reshape(rows, SUBLANES, a.shape[1])


def _n2_input(a_s, k1, half, n2s):
    if 0 < k1 < half:
        return jnp.concatenate([a_s[k1], a_s[half + k1]], axis=0)
    return a_s[k1]


def _filt_spec_kernel(kern_ref, s_ref, fkf_ref, m_ref, kf_ref, a_s, *, k1n, n2s):
    half = k1n - 1
    _stage_n1([kern_ref.at[0], kern_ref.at[1]], fkf_ref[...], a_s, n2s)
    inv = 1.0 / (s_ref[...] + EPS)
    for k1 in range(k1n):
        a = _n2_input(a_s, k1, half, n2s).astype(BF16)
        kf_ref[k1] = jnp.dot(m_ref[k1, :, :a.shape[0]], a, preferred_element_type=F32) * inv


def _filt_spec_call(kern4, s, tabs):
    _, half, n2s, c = kern4.shape
    ct = HY_CT
    k1n = tabs["k1n"]
    return pl.pallas_call(
        functools.partial(_filt_spec_kernel, k1n=k1n, n2s=n2s),
        grid=(c // ct,),
        in_specs=[pl.BlockSpec((2, half, n2s, ct), lambda j: (0, 0, 0, j)),
                  pl.BlockSpec((1, ct), lambda j: (0, j)),
                  _const_spec(tabs["fkf"].shape), _const_spec(tabs["m"].shape)],
        out_specs=pl.BlockSpec((k1n, 2 * n2s, ct), lambda j: (0, 0, j)),
        out_shape=jax.ShapeDtypeStruct((k1n, 2 * n2s, c), F32),
        scratch_shapes=[pltpu.VMEM((2 * half, n2s, ct), F32)],
        compiler_params=_cparams(("arbitrary",)),
        name="filt_spec",
    )(kern4, s, tabs["fkf"], tabs["m"])


def _hyena_kernel(u_ref, x0_ref, bias_ref, kf_ref, fk_ref, fki_ref, m_ref, mt_ref, o_ref, u_s, a_s,
                  *, k1n, n2s):
    half = k1n - 1
    ct = u_ref.shape[2]
    u_s[...] = u_ref[...].astype(F32)
    _stage_n1([u_s], fk_ref[...], a_s, n2s)

    for k1 in range(k1n):
        a = _n2_input(a_s, k1, half, n2s).astype(BF16)
        x = jnp.dot(m_ref[k1, :, :a.shape[0]], a, preferred_element_type=F32)
        xr, xi = x[:n2s], x[n2s:]
        kr, ki = kf_ref[k1, :n2s, :], kf_ref[k1, n2s:, :]
        y = jnp.concatenate([xr * kr - xi * ki, xr * ki + xi * kr], axis=0).astype(BF16)
        if 0 < k1 < half:
            z = jnp.dot(mt_ref[k1], y, preferred_element_type=F32)
            a_s[k1] = z[:n2s]
            a_s[half + k1] = z[n2s:]
        else:
            a_s[k1] = jnp.dot(mt_ref[k1, :n2s, :], y, preferred_element_type=F32)

    fki = fki_ref[...]
    bias = bias_ref[...]
    pair = 2 * SUBLANES
    for j in range(n2s // pair):
        ys = []
        for jj in range(2):
            sl = slice(j * pair + jj * SUBLANES, j * pair + (jj + 1) * SUBLANES)
            zg = a_s[:, sl, :].reshape(2 * half * SUBLANES, ct)
            ys.append(jnp.dot(fki, zg.astype(BF16), preferred_element_type=F32).reshape(half, SUBLANES, ct))
        y = jnp.concatenate(ys, axis=1)
        sl = slice(j * pair, (j + 1) * pair)
        o_ref[:, sl, :] = (x0_ref[:, sl, :].astype(F32) * (y + u_s[:, sl, :] * bias)).astype(o_ref.dtype)


def _hyena_call(u4, x04, bias, kf, tabs):
    b, half, n2s, c = u4.shape
    ct = HY_CT
    k1n = tabs["k1n"]
    seq = pl.BlockSpec((None, half, n2s, ct), lambda j, i: (i, 0, 0, j))
    return pl.pallas_call(
        functools.partial(_hyena_kernel, k1n=k1n, n2s=n2s),
        grid=(c // ct, b),
        in_specs=[seq, seq,
                  pl.BlockSpec((1, ct), lambda j, i: (0, j)),
                  pl.BlockSpec((k1n, 2 * n2s, ct), lambda j, i: (0, 0, j), pipeline_mode=pl.Buffered(1)),
                  _const_spec(tabs["fk"].shape), _const_spec(tabs["fki"].shape),
                  _const_spec(tabs["m"].shape), _const_spec(tabs["mt"].shape)],
        out_specs=pl.BlockSpec((None, half, n2s, ct), lambda j, i: (i, 0, 0, j)),
        out_shape=jax.ShapeDtypeStruct((b, half, n2s, c), BF16),
        scratch_shapes=[pltpu.VMEM((half, n2s, ct), F32), pltpu.VMEM((2 * half, n2s, ct), F32)],
        compiler_params=_cparams(("arbitrary", "arbitrary")),
        name="hyena",
    )(u4, x04, bias, kf, tabs["fk"], tabs["fki"], tabs["m"], tabs["mt"])


def _out_ffn_kernel(ret_ref, hy_ref, x_ref, mod_ref, hg_ref, n2_ref, fn_ref, wo_ref, wgu_ref, wd_ref,
                    o_ref, *, d_ret, d_ff, ff_bounds, row_parts):
    rows = x_ref.shape[0] // row_parts
    groups = [slice(r * rows, (r + 1) * rows) for r in range(row_parts)]
    gain2 = n2_ref[...] * (1.0 + mod_ref[4:5, :])
    x1s, h2s = [], []
    for rs in groups:
        hy = hy_ref[rs, :].astype(F32)
        hms = jnp.mean(hy * hy, axis=-1, keepdims=True)
        hyn = (hy * lax.rsqrt(hms + EPS) * hg_ref[...]).astype(BF16)
        mix = jnp.dot(ret_ref[rs, :], wo_ref[:d_ret, :], preferred_element_type=F32)
        mix = mix + jnp.dot(hyn, wo_ref[d_ret:, :], preferred_element_type=F32)
        x1s.append(x_ref[rs, :] + mod_ref[2:3, :] * mix)
    for x1 in x1s:
        ms = jnp.mean(x1 * x1, axis=-1, keepdims=True)
        h2s.append((x1 * lax.rsqrt(ms + EPS) * gain2 + mod_ref[3:4, :]).astype(BF16))
    accs = [None] * row_parts
    for lo, hi in zip(ff_bounds[:-1], ff_bounds[1:]):
        acts = []
        for h2 in h2s:
            g = jnp.dot(h2, wgu_ref[:, lo:hi], preferred_element_type=F32)
            u = jnp.dot(h2, wgu_ref[:, d_ff + lo:d_ff + hi], preferred_element_type=F32)
            acts.append((_silu(g) * u).astype(BF16))
        for r, a in enumerate(acts):
            d = jnp.dot(a, wd_ref[lo:hi, :], preferred_element_type=F32)
            accs[r] = d if accs[r] is None else accs[r] + d
    for rs, x1, acc in zip(groups, x1s, accs):
        x2 = x1 + mod_ref[5:6, :] * acc
        ms2 = jnp.mean(x2 * x2, axis=-1, keepdims=True)
        o_ref[rs, :] = x2 * lax.rsqrt(ms2 + EPS) * fn_ref[...]


def _ff_bounds(d_ff, parts=2):
    tiles = d_ff // MXU_WIDTH
    assert tiles * MXU_WIDTH == d_ff
    cuts = [-(-tiles * p // parts) for p in range(parts + 1)]
    return tuple(c * MXU_WIDTH for c in cuts)


def _out_ffn_call(ret, hy, x, mod, hy_gain, norm2, final_norm, w_out, w_gu, w_down):
    b, l, d = x.shape
    d_ret = ret.shape[2]
    d_hy = hy.shape[2]
    d_ff = w_down.shape[0]
    tm = ROW_TILE
    return pl.pallas_call(
        functools.partial(_out_ffn_kernel, d_ret=d_ret, d_ff=d_ff, ff_bounds=_ff_bounds(d_ff),
                          row_parts=tm // GROUP_ROWS),
        grid=(b, l // tm),
        in_specs=[pl.BlockSpec((None, tm, d_ret), lambda i, t: (i, t, 0)),
                  pl.BlockSpec((None, tm, d_hy), lambda i, t: (i, t, 0)),
                  pl.BlockSpec((None, tm, d), lambda i, t: (i, t, 0)),
                  pl.BlockSpec((None, N_MOD, d), lambda i, t: (i, 0, 0)),
                  pl.BlockSpec((1, d_hy), lambda i, t: (0, 0)),
                  pl.BlockSpec((1, d), lambda i, t: (0, 0)),
                  pl.BlockSpec((1, d), lambda i, t: (0, 0)),
                  _const_spec(w_out.shape), _const_spec(w_gu.shape), _const_spec(w_down.shape)],
        out_specs=pl.BlockSpec((None, tm, d), lambda i, t: (i, t, 0)),
        out_shape=jax.ShapeDtypeStruct((b, l, d), x.dtype),
        compiler_params=_cparams(("arbitrary", "arbitrary")),
        name="out_ffn",
    )(ret, hy, x, mod, hy_gain, norm2, final_norm, w_out, w_gu, w_down)


@functools.lru_cache(maxsize=None)
def _rope_tables(seq_len, dh):
    n = dh // 4
    t = np.arange(seq_len)
    inv = ROPE_BASE ** (-np.arange(n, dtype=np.float64) / n)
    ang = np.concatenate([(t // GRID_W)[:, None] * inv, (t % GRID_W)[:, None] * inv], axis=-1)
    cc = np.concatenate([np.cos(ang), np.cos(ang)], axis=-1)
    ss = np.concatenate([-np.sin(ang), np.sin(ang)], axis=-1)
    return cc.astype(np.float32), ss.astype(np.float32)


@functools.lru_cache(maxsize=None)
def _filter_tables(seq_len, emb_dim, emb_pad, channels):
    t = np.linspace(0.0, 1.0, seq_len)[:, None]
    bands = (emb_dim - 1) // 2
    f = np.linspace(1e-4, bands - 1, bands)[None, :]
    wpos = 2.0 * np.pi * np.arange(seq_len)[:, None] / seq_len
    emb = np.concatenate([t, np.cos(f * wpos), -np.sin(f * wpos)], axis=-1)
    emb = np.pad(emb, ((0, 0), (0, emb_pad - emb_dim)))
    emb_rev = np.concatenate([emb[:1], emb[:0:-1]], axis=0)
    emb2 = np.concatenate([emb, emb_rev], axis=1)
    max_decay = math.log(FILTER_DECAY_TARGET) / FILTER_DECAY_FAST
    min_decay = math.log(FILTER_DECAY_TARGET) / FILTER_DECAY_SLOW
    absdelta = np.abs(np.linspace(min_decay, max_decay, channels))[None, :]
    return emb2.astype(np.float32), absdelta.astype(np.float32)


def _block_diag2(w):
    z = jnp.zeros_like(w)
    return jnp.concatenate([jnp.concatenate([w, z], axis=1), jnp.concatenate([z, w], axis=1)], axis=0)


def kernel(x, c, ctx, c_ctx, w_mod, b_mod, norm1, norm2, w_in, ret_decay, ret_gn_gain, hy_short_w,
           hy_short_b, hy_w1, hy_b1, hy_w2, hy_b2, hy_w3, hy_b3, hy_w4, hy_freq, hy_bias, hy_out_norm,
           w_out, w_gate_up, w_down, final_norm):
    b, seq_len, d = x.shape
    assert w_mod.shape[0] == 1, "single-layer block"
    heads = RET_HEADS
    d_ret = ret_gn_gain.shape[1]
    dh = d_ret // heads
    d_hy = hy_bias.shape[1]
    assert dh == LANES and d_hy % HY_CT == 0 and seq_len % (FFT_N1 // 2) == 0

    rows = -(-(b + 1) // SUBLANES) * SUBLANES
    c_rows = jnp.zeros((rows, d), F32).at[:b].set(c).at[b].set(c_ctx)
    mod_all = _mod_call(c_rows, w_mod[0], b_mod[0][None, :])
    mod = mod_all[:b].reshape(b, N_MOD, d)
    mod_c = mod_all[b].reshape(N_MOD, d)

    w_in_b = w_in[0].astype(BF16)
    dec = jnp.broadcast_to(ret_decay[0].reshape(2 * heads, 1), (2 * heads, LANES))
    n1g = norm1[0][None, :]

    s0 = _ctx_call(ctx, n1g, mod_c[0:1], mod_c[1:2], w_in_b[:, d_ret:3 * d_ret], dec, heads, dh)
    cc, ss = (jnp.asarray(t) for t in _rope_tables(seq_len, dh))
    qkvg, x0c, u = _inproj_call(x, mod, n1g, w_in_b, cc, ss, hy_short_w[0], hy_short_b, heads, dh, d_hy)
    ret = _ret_call(qkvg, s0, dec, ret_gn_gain, heads, dh)

    tabs = dict(_fft_tables(seq_len))
    for name in ("fk", "fkf", "fki", "m", "mt"):
        tabs[name] = jnp.asarray(tabs[name], dtype=F32).astype(BF16)
    half, n2s = tabs["half"], tabs["n2s"]
    emb_dim = hy_w1.shape[1]
    emb_pad = -(-emb_dim // SUBLANES) * SUBLANES
    emb2, absdelta = (jnp.asarray(t) for t in _filter_tables(seq_len, emb_dim, emb_pad, d_hy))
    w1p = jnp.pad(hy_w1[0], ((0, emb_pad - emb_dim), (0, 0)))
    two = lambda a: jnp.concatenate([a, a], axis=1)
    w4 = hy_w4[0]
    zero4 = jnp.zeros((w4.shape[0], d_hy), F32)
    kern, ksum = _filt_time_call(
        emb2, _block_diag2(w1p), two(hy_b1), _block_diag2(hy_w2[0]), two(hy_b2), _block_diag2(hy_w3[0]),
        two(hy_b3), jnp.concatenate([w4[:, :d_hy], zero4], axis=0), jnp.concatenate([zero4, w4[:, d_hy:]], axis=0),
        two(hy_freq), absdelta, seq_len)
    kf = _filt_spec_call(kern.reshape(2, half, n2s, d_hy), ksum, tabs)
    hy = _hyena_call(u.reshape(b, half, n2s, d_hy), x0c.reshape(b, half, n2s, d_hy), hy_bias, kf, tabs)
    hy = hy.reshape(b, seq_len, d_hy)

    return _out_ffn_call(ret, hy, x, mod, hy_out_norm, norm2[0][None, :], final_norm[None, :],
                         w_out[0].astype(BF16), w_gate_up[0].astype(BF16), w_down[0].astype(BF16))
```

```python
import functools
import math

import jax
import jax.numpy as jnp
import numpy as np
from jax import lax
from jax.experimental import pallas as pl
from jax.experimental.pallas import tpu as pltpu

F32 = jnp.float32
BF16 = jnp.bfloat16
HIGHEST = lax.Precision.HIGHEST

RET_HEADS = 4
GRID_W = 64
ROPE_BASE = 10000.0
N_MOD = 6
HYENA_PROJ = 3
FILTER_DECAY_FAST = 0.3
FILTER_DECAY_SLOW = 1.5
FILTER_DECAY_TARGET = 1e-2
EPS = 1e-6

LANES = 128
SUBLANES = 8
MXU_WIDTH = 256
VMEM_LIMIT_BYTES = 56 * 1024 * 1024

RET_CHUNK = 256
ROW_TILE = 1024
IN_ROW_TILE = 1024
GROUP_ROWS = 256
HALO = 16
HY_CT = 256
FILT_ROWS = 1024
FFT_N1 = 64


def _silu(v):
    return v / (1.0 + jnp.exp(-v))


def _dot3(a, b):
    a_hi = a.astype(BF16)
    b_hi = b.astype(BF16)
    a_lo = (a - a_hi.astype(F32)).astype(BF16)
    b_lo = (b - b_hi.astype(F32)).astype(BF16)
    dot = functools.partial(jnp.dot, preferred_element_type=F32)
    return dot(a_hi, b_hi) + (dot(a_hi, b_lo) + dot(a_lo, b_hi))


def _cparams(sem, vmem=VMEM_LIMIT_BYTES):
    return pltpu.CompilerParams(dimension_semantics=sem, vmem_limit_bytes=vmem)


def _const_spec(shape):
    nd = len(shape)
    return pl.BlockSpec(shape, lambda *_: (0,) * nd, pipeline_mode=pl.Buffered(1))


def _mod_kernel(c_ref, w_ref, b_ref, o_ref):
    s = _silu(c_ref[...])
    o_ref[...] = jnp.dot(s, w_ref[...], precision=HIGHEST, preferred_element_type=F32) + b_ref[...]


def _mod_call(c_rows, w_mod, b_mod):
    rows, d = c_rows.shape
    n = w_mod.shape[1]
    tn = 1536
    return pl.pallas_call(
        _mod_kernel,
        grid=(n // tn,),
        in_specs=[pl.BlockSpec((rows, d), lambda j: (0, 0)),
                  pl.BlockSpec((d, tn), lambda j: (0, j)),
                  pl.BlockSpec((1, tn), lambda j: (0, j))],
        out_specs=pl.BlockSpec((rows, tn), lambda j: (0, j)),
        out_shape=jax.ShapeDtypeStruct((rows, n), F32),
        compiler_params=_cparams(("arbitrary",)),
        name="mod",
    )(c_rows, w_mod, b_mod)


def _ctx_kernel(ctx_ref, n1_ref, sh_ref, sc_ref, w_ref, dec_ref, s_ref, *, heads, dh):
    xc = ctx_ref[...]
    n_ctx = xc.shape[0]
    ms = jnp.mean(xc * xc, axis=-1, keepdims=True)
    hc = (xc * lax.rsqrt(ms + EPS) * n1_ref[...]) * (1.0 + sc_ref[...]) + sh_ref[...]
    kv = jnp.dot(hc.astype(BF16), w_ref[...], preferred_element_type=F32)
    lg = jnp.log1p(-jnp.exp(dec_ref[...]))
    pos = lax.broadcasted_iota(jnp.int32, (n_ctx, dh), 0).astype(F32)
    k_scale = dh ** -0.5
    d_ret = heads * dh
    tdims = (((0,), (0,)), ((), ()))
    for h in range(heads):
        kh = kv[:, h * dh:(h + 1) * dh] * k_scale
        vh = kv[:, d_ret + h * dh:d_ret + (h + 1) * dh].astype(BF16)
        wf = jnp.exp(lg[h:h + 1, :] * (n_ctx - 1.0 - pos))
        wb = jnp.exp(lg[heads + h:heads + h + 1, :] * pos)
        s_ref[h] = lax.dot_general((kh * wf).astype(BF16), vh, tdims, preferred_element_type=F32)
        s_ref[heads + h] = lax.dot_general((kh * wb).astype(BF16), vh, tdims, preferred_element_type=F32)


def _ctx_call(ctx, norm1, shift_c, scale_c, w_kv, dec, heads, dh):
    b, n_ctx, d = ctx.shape
    return pl.pallas_call(
        functools.partial(_ctx_kernel, heads=heads, dh=dh),
        grid=(b,),
        in_specs=[pl.BlockSpec((None, n_ctx, d), lambda i: (i, 0, 0)),
                  pl.BlockSpec((1, d), lambda i: (0, 0)),
                  pl.BlockSpec((1, d), lambda i: (0, 0)),
                  pl.BlockSpec((1, d), lambda i: (0, 0)),
                  pl.BlockSpec(w_kv.shape, lambda i: (0, 0)),
                  pl.BlockSpec(dec.shape, lambda i: (0, 0))],
        out_specs=pl.BlockSpec((None, 2 * heads, dh, dh), lambda i: (i, 0, 0, 0)),
        out_shape=jax.ShapeDtypeStruct((b, 2 * heads, dh, dh), F32),
        compiler_params=_cparams(("arbitrary",)),
        name="ctx_state",
    )(ctx, norm1, shift_c, scale_c, w_kv, dec)


def _inproj_kernel(x_ref, xp_ref, xn_ref, mod_ref, n1_ref, w_ref, cc_ref, ss_ref, sw_ref, sb_ref,
                   qkvg_ref, x0_ref, u_ref, *, heads, dh, d_hy, nt):
    t = pl.program_id(1)
    tm = x_ref.shape[0]
    d_ret = heads * dh
    gain = n1_ref[...] * (1.0 + mod_ref[1:2, :])

    def normed(x):
        ms = jnp.mean(x * x, axis=-1, keepdims=True)
        return x * lax.rsqrt(ms + EPS) * gain + mod_ref[0:1, :]

    groups = [slice(r, r + GROUP_ROWS) for r in range(0, tm, GROUP_ROWS)]
    hs = [normed(x_ref[rs, :]) for rs in groups]
    hbs = [h.astype(BF16) for h in hs]
    hy_lhs = list(hbs)
    hy_lhs[0] = jnp.concatenate([normed(xp_ref[...]), hs[0]], axis=0).astype(BF16)
    hy_lhs[-1] = jnp.concatenate([hs[-1], normed(xn_ref[...])], axis=0).astype(BF16)
    k_scale = dh ** -0.5

    def proj(lhs, base, width):
        return [jnp.dot(hb, w_ref[:, base:base + width], preferred_element_type=F32) for hb in lhs]

    def hyena_cols(j):
        sl = slice(j * d_hy, (j + 1) * d_hy)
        parts = proj(hy_lhs, 4 * d_ret + j * d_hy, d_hy)
        head, tail = parts[0], parts[-1]
        parts[0] = jnp.concatenate([jnp.where(t == 0, 0.0, head[:HALO]), head[HALO:]], axis=0)
        parts[-1] = jnp.concatenate([tail[:GROUP_ROWS], jnp.where(t == nt - 1, 0.0, tail[GROUP_ROWS:])],
                                    axis=0)
        p = jnp.concatenate(parts, axis=0)
        n = p.shape[0]
        y = (sw_ref[0:1, sl] * pltpu.roll(p, 1, 0) + sw_ref[1:2, sl] * p
             + sw_ref[2:3, sl] * pltpu.roll(p, n - 1, 0) + sb_ref[:, sl])
        return y[HALO:HALO + tm]

    def roped_cols(base, scale):
        for rs, p in zip(groups, proj(hbs, base, d_ret)):
            cc = cc_ref[rs, :]
            ss = ss_ref[rs, :]
            for j in range(heads):
                pj = p[:, j * dh:(j + 1) * dh]
                r = pj * cc + pltpu.roll(pj, dh // 2, 1) * ss
                if scale != 1.0:
                    r = r * scale
                qkvg_ref[rs, base + j * dh:base + (j + 1) * dh] = r.astype(BF16)

    roped_cols(0, 1.0)
    x0_ref[...] = hyena_cols(0).astype(x0_ref.dtype)
    roped_cols(d_ret, k_scale)
    x1c = hyena_cols(1)
    for rs, g in zip(groups, proj(hbs, 3 * d_ret, d_ret)):
        qkvg_ref[rs, 3 * d_ret:4 * d_ret] = _silu(g).astype(BF16)
    u_ref[...] = (x1c * hyena_cols(2)).astype(u_ref.dtype)
    for rs, v in zip(groups, proj(hbs, 2 * d_ret, d_ret)):
        qkvg_ref[rs, 2 * d_ret:3 * d_ret] = v.astype(BF16)


def _inproj_call(x, mod, norm1, w_in, cc, ss, short_w, short_b, heads, dh, d_hy):
    b, l, d = x.shape
    tm = IN_ROW_TILE
    nt = l // tm
    d_ret = heads * dh
    per_tile = tm // HALO
    tile = lambda i, t: (i, t, 0)
    return pl.pallas_call(
        functools.partial(_inproj_kernel, heads=heads, dh=dh, d_hy=d_hy, nt=nt),
        grid=(b, nt),
        in_specs=[pl.BlockSpec((None, tm, d), tile),
                  pl.BlockSpec((None, HALO, d), lambda i, t: (i, jnp.maximum(t * per_tile - 1, 0), 0)),
                  pl.BlockSpec((None, HALO, d),
                               lambda i, t: (i, jnp.minimum((t + 1) * per_tile, nt * per_tile - 1), 0)),
                  pl.BlockSpec((None, N_MOD, d), lambda i, t: (i, 0, 0)),
                  pl.BlockSpec((1, d), lambda i, t: (0, 0)),
                  _const_spec(w_in.shape),
                  pl.BlockSpec((tm, dh), lambda i, t: (t, 0)),
                  pl.BlockSpec((tm, dh), lambda i, t: (t, 0)),
                  pl.BlockSpec(short_w.shape, lambda i, t: (0, 0)),
                  pl.BlockSpec(short_b.shape, lambda i, t: (0, 0))],
        out_specs=[pl.BlockSpec((None, tm, 4 * d_ret), tile),
                   pl.BlockSpec((None, tm, d_hy), tile),
                   pl.BlockSpec((None, tm, d_hy), tile)],
        out_shape=[jax.ShapeDtypeStruct((b, l, 4 * d_ret), BF16),
                   jax.ShapeDtypeStruct((b, l, d_hy), BF16),
                   jax.ShapeDtypeStruct((b, l, d_hy), BF16)],
        compiler_params=_cparams(("arbitrary", "arbitrary")),
        name="in_proj",
    )(x, x, x, mod, norm1, w_in, cc, ss, short_w, short_b)


def _ret_kernel(q_ref, k_ref, v_ref, g_ref, sf0_ref, sb0_ref, dec_ref, gain_ref,
                o_ref, kvf_s, kvb_s, st_s, *, heads, chunk):
    l, dh = q_ref.shape
    nc = l // chunk
    h = pl.program_id(1)
    lgf = jnp.log1p(-jnp.exp(dec_ref[pl.ds(h, 1), :]))
    lgb = jnp.log1p(-jnp.exp(dec_ref[pl.ds(h + heads, 1), :]))
    il = lax.broadcasted_iota(jnp.int32, (chunk, dh), 0).astype(F32)
    kw_f = jnp.exp(lgf * (chunk - 1.0 - il))
    kw_b = jnp.exp(lgb * il)
    qw_f = jnp.exp(lgf * (il + 1.0))
    qw_b = jnp.exp(lgb * (chunk - il))
    cd_f = jnp.exp(lgf * float(chunk))
    cd_b = jnp.exp(lgb * float(chunk))
    reps = chunk // dh
    lgf_c = jnp.concatenate([lgf] * reps, axis=1)
    lgb_c = jnp.concatenate([lgb] * reps, axis=1)
    ii = lax.broadcasted_iota(jnp.int32, (chunk, chunk), 0)
    jj = lax.broadcasted_iota(jnp.int32, (chunk, chunk), 1)
    diff = (ii - jj).astype(F32)
    dmask = (jnp.where(diff >= 0, jnp.exp(lgf_c * jnp.maximum(diff, 0.0)), 0.0)
             + jnp.where(diff <= 0, jnp.exp(lgb_c * jnp.maximum(-diff, 0.0)), 0.0))
    tdims = (((0,), (0,)), ((), ()))
    ntdims = (((1,), (1,)), ((), ()))

    def chunk_states(n, carry):
        r0 = pl.multiple_of(n * chunk, chunk)
        kr = k_ref[pl.ds(r0, chunk), :].astype(F32)
        vv = v_ref[pl.ds(r0, chunk), :]
        kvf_s[n] = lax.dot_general((kr * kw_f).astype(BF16), vv, tdims, preferred_element_type=F32)
        kvb_s[n] = lax.dot_general((kr * kw_b).astype(BF16), vv, tdims, preferred_element_type=F32)
        return carry

    lax.fori_loop(0, nc, chunk_states, 0, unroll=True)

    def scan_f(n, s):
        st_s[n, :dh, :] = s.astype(BF16)
        return s * cd_f + kvf_s[n]

    lax.fori_loop(0, nc, scan_f, sf0_ref[...])

    def scan_b(m, s):
        n = nc - 1 - m
        st_s[n, dh:, :] = s.astype(BF16)
        return s * cd_b + kvb_s[n]

    lax.fori_loop(0, nc, scan_b, sb0_ref[...])

    gain = gain_ref[...]

    def chunk_out(n, carry):
        r0 = pl.multiple_of(n * chunk, chunk)
        qb = q_ref[pl.ds(r0, chunk), :]
        qr = qb.astype(F32)
        sc = lax.dot_general(qb, k_ref[pl.ds(r0, chunk), :], ntdims, preferred_element_type=F32)
        lhs = jnp.concatenate([(qr * qw_f).astype(BF16), (qr * qw_b).astype(BF16),
                               (sc * dmask).astype(BF16)], axis=1)
        rhs = jnp.concatenate([st_s[n], v_ref[pl.ds(r0, chunk), :]], axis=0)
        o = jnp.dot(lhs, rhs, preferred_element_type=F32)
        mu = jnp.mean(o, axis=-1, keepdims=True)
        d = o - mu
        var = jnp.mean(d * d, axis=-1, keepdims=True)
        gg = g_ref[pl.ds(r0, chunk), :].astype(F32)
        o_ref[pl.ds(r0, chunk), :] = (d * lax.rsqrt(var + EPS) * gain * gg).astype(o_ref.dtype)
        return carry

    lax.fori_loop(0, nc, chunk_out, 0, unroll=True)


def _ret_call(qkvg, s0, dec, gn_gain, heads, dh):
    b, l, _ = qkvg.shape
    chunk = RET_CHUNK
    nc = l // chunk
    seq = lambda off: pl.BlockSpec((None, l, dh), lambda i, h: (i, 0, off + h))
    return pl.pallas_call(
        functools.partial(_ret_kernel, heads=heads, chunk=chunk),
        grid=(b, heads),
        in_specs=[seq(0), seq(heads), seq(2 * heads), seq(3 * heads),
                  pl.BlockSpec((None, None, dh, dh), lambda i, h: (i, h, 0, 0)),
                  pl.BlockSpec((None, None, dh, dh), lambda i, h: (i, heads + h, 0, 0)),
                  pl.BlockSpec(dec.shape, lambda i, h: (0, 0)),
                  pl.BlockSpec((1, dh), lambda i, h: (0, h))],
        out_specs=pl.BlockSpec((None, l, dh), lambda i, h: (i, 0, h)),
        out_shape=jax.ShapeDtypeStruct((b, l, heads * dh), BF16),
        scratch_shapes=[pltpu.VMEM((nc, dh, dh), F32), pltpu.VMEM((nc, dh, dh), F32),
                        pltpu.VMEM((nc, 2 * dh, dh), BF16)],
        compiler_params=_cparams(("arbitrary", "arbitrary")),
        name="retention",
    )(qkvg, qkvg, qkvg, qkvg, s0, s0, dec, gn_gain)


def _filt_time_kernel(emb_ref, w1_ref, b1_ref, w2_ref, b2_ref, w3_ref, b3_ref, w4f_ref, w4b_ref,
                      fr_ref, dl_ref, kern_ref, s_ref, *, seq_len):
    i = pl.program_id(0)
    rows, c = kern_ref.shape[1], kern_ref.shape[2]
    fr = fr_ref[...]
    hdot = _dot3
    z = jnp.sin(fr * (hdot(emb_ref[...], w1_ref[...]) + b1_ref[...]))
    z = jnp.sin(fr * (hdot(z, w2_ref[...]) + b2_ref[...]))
    z = jnp.sin(fr * (hdot(z, w3_ref[...]) + b3_ref[...]))
    pos = (i * rows + lax.broadcasted_iota(jnp.int32, (rows, c), 0)).astype(F32)
    inv = 1.0 / (seq_len - 1.0)
    adl = dl_ref[...]
    hf = hdot(z, w4f_ref[...]) * jnp.exp(-(pos * inv) * adl)
    hr = hdot(z, w4b_ref[...]) * jnp.exp(-((seq_len - pos) * inv) * adl)
    hr = jnp.where(pos == 0.0, 0.0, hr)
    kern_ref[0] = hf
    kern_ref[1] = hr
    part = jnp.sum(jnp.abs(hf) + jnp.abs(hr), axis=0, keepdims=True)

    @pl.when(i == 0)
    def _():
        s_ref[...] = part

    @pl.when(i != 0)
    def _():
        s_ref[...] += part


def _filt_time_call(emb2, w1, b1, w2, b2, w3, b3, w4f, w4b, freq, absdelta, seq_len):
    c = absdelta.shape[1]
    rows = FILT_ROWS
    small = lambda a: pl.BlockSpec(a.shape, lambda i: (0,) * a.ndim)
    return pl.pallas_call(
        functools.partial(_filt_time_kernel, seq_len=seq_len),
        grid=(seq_len // rows,),
        in_specs=[pl.BlockSpec((rows, emb2.shape[1]), lambda i: (i, 0)),
                  small(w1), small(b1), small(w2), small(b2), small(w3), small(b3), small(w4f), small(w4b),
                  small(freq), small(absdelta)],
        out_specs=[pl.BlockSpec((2, rows, c), lambda i: (0, i, 0)),
                   pl.BlockSpec((1, c), lambda i: (0, 0))],
        out_shape=[jax.ShapeDtypeStruct((2, seq_len, c), F32),
                   jax.ShapeDtypeStruct((1, c), F32)],
        compiler_params=_cparams(("arbitrary",)),
        name="filt_time",
    )(emb2, w1, b1, w2, b2, w3, b3, w4f, w4b, freq, absdelta)


@functools.lru_cache(maxsize=None)
def _fft_tables(seq_len):
    n = 2 * seq_len
    n1s = FFT_N1
    n2s = n // n1s
    half = n1s // 2
    k1n = half + 1
    n1 = np.arange(half)
    k1 = np.arange(k1n)
    th = 2.0 * np.pi * (np.outer(k1, n1) % n1s) / n1s
    herm = np.where((k1 == 0) | (k1 == half), 1.0, 2.0)
    fa = np.concatenate([np.cos(th), -np.sin(th)[1:half]], axis=0)
    sgn = np.concatenate([(-1.0) ** k1, (-1.0) ** k1[1:half]])
    fai = np.concatenate([np.cos(th) * herm[:, None], (-np.sin(th) * herm[:, None])[1:half]], axis=0).T / n
    eye = np.eye(SUBLANES)
    fk = np.kron(fa, eye)
    fks = np.kron(fa * sgn[:, None], eye)
    fki = np.kron(fai, eye)
    k2 = np.arange(n2s)
    n2 = np.arange(n2s)
    m = np.zeros((k1n, 2 * n2s, 2 * n2s))
    for a in range(k1n):
        ang = 2.0 * np.pi * (np.outer(a + n1s * k2, n2) % n) / n
        gr, gi = np.cos(ang), -np.sin(ang)
        m[a] = np.block([[gr, -gi], [gi, gr]])
    mt = np.transpose(m, (0, 2, 1))
    return dict(fk=fk, fkf=np.concatenate([fk, fks], axis=1), fki=fki, m=m, mt=mt,
                k1n=k1n, n2s=n2s, half=half)


def _stage_n1(src_refs, mat, dst_ref, n2s):
    rows = dst_ref.shape[0]
    for j in range(n2s // SUBLANES):
        sl = slice(j * SUBLANES, (j + 1) * SUBLANES)
        parts = [r[:, sl, :] for r in src_refs]
        xg = jnp.concatenate([p.reshape(p.shape[0] * SUBLANES, p.shape[2]) for p in parts], axis=0)
        a = jnp.dot(mat, xg.astype(BF16), preferred_element_type=F32)
        dst_ref[:, sl, :] = a.reshape(rows, SUBLANES, a.shape[1])


def _n2_input(a_s, k1, half, n2s):
    if 0 < k1 < half:
        return jnp.concatenate([a_s[k1], a_s[half + k1]], axis=0)
    return a_s[k1]


def _filt_spec_kernel(kern_ref, s_ref, fkf_ref, m_ref, kf_ref, a_s, *, k1n, n2s):
    half = k1n - 1
    _stage_n1([kern_ref.at[0], kern_ref.at[1]], fkf_ref[...], a_s, n2s)
    inv = 1.0 / (s_ref[...] + EPS)
    for k1 in range(k1n):
        a = _n2_input(a_s, k1, half, n2s).astype(BF16)
        kf_ref[k1] = jnp.dot(m_ref[k1, :, :a.shape[0]], a, preferred_element_type=F32) * inv


def _filt_spec_call(kern4, s, tabs):
    _, half, n2s, c = kern4.shape
    ct = HY_CT
    k1n = tabs["k1n"]
    return pl.pallas_call(
        functools.partial(_filt_spec_kernel, k1n=k1n, n2s=n2s),
        grid=(c // ct,),
        in_specs=[pl.BlockSpec((2, half, n2s, ct), lambda j: (0, 0, 0, j)),
                  pl.BlockSpec((1, ct), lambda j: (0, j)),
                  _const_spec(tabs["fkf"].shape), _const_spec(tabs["m"].shape)],
        out_specs=pl.BlockSpec((k1n, 2 * n2s, ct), lambda j: (0, 0, j)),
        out_shape=jax.ShapeDtypeStruct((k1n, 2 * n2s, c), F32),
        scratch_shapes=[pltpu.VMEM((2 * half, n2s, ct), F32)],
        compiler_params=_cparams(("arbitrary",)),
        name="filt_spec",
    )(kern4, s, tabs["fkf"], tabs["m"])


def _hyena_kernel(u_ref, x0_ref, bias_ref, kf_ref, fk_ref, fki_ref, m_ref, mt_ref, o_ref, u_s, a_s,
                  *, k1n, n2s):
    half = k1n - 1
    ct = u_ref.shape[2]
    u_s[...] = u_ref[...].astype(F32)
    _stage_n1([u_s], fk_ref[...], a_s, n2s)

    for k1 in range(k1n):
        a = _n2_input(a_s, k1, half, n2s).astype(BF16)
        x = jnp.dot(m_ref[k1, :, :a.shape[0]], a, preferred_element_type=F32)
        xr, xi = x[:n2s], x[n2s:]
        kr, ki = kf_ref[k1, :n2s, :], kf_ref[k1, n2s:, :]
        y = jnp.concatenate([xr * kr - xi * ki, xr * ki + xi * kr], axis=0).astype(BF16)
        if 0 < k1 < half:
            z = jnp.dot(mt_ref[k1], y, preferred_element_type=F32)
            a_s[k1] = z[:n2s]
            a_s[half + k1] = z[n2s:]
        else:
            a_s[k1] = jnp.dot(mt_ref[k1, :n2s, :], y, preferred_element_type=F32)

    fki = fki_ref[...]
    bias = bias_ref[...]
    pair = 2 * SUBLANES
    for j in range(n2s // pair):
        ys = []
        for jj in range(2):
            sl = slice(j * pair + jj * SUBLANES, j * pair + (jj + 1) * SUBLANES)
            zg = a_s[:, sl, :].reshape(2 * half * SUBLANES, ct)
            ys.append(jnp.dot(fki, zg.astype(BF16), preferred_element_type=F32).reshape(half, SUBLANES, ct))
        y = jnp.concatenate(ys, axis=1)
        sl = slice(j * pair, (j + 1) * pair)
        o_ref[:, sl, :] = (x0_ref[:, sl, :].astype(F32) * (y + u_s[:, sl, :] * bias)).astype(o_ref.dtype)


def _hyena_call(u4, x04, bias, kf, tabs):
    b, half, n2s, c = u4.shape
    ct = HY_CT
    k1n = tabs["k1n"]
    seq = pl.BlockSpec((None, half, n2s, ct), lambda j, i: (i, 0, 0, j))
    return pl.pallas_call(
        functools.partial(_hyena_kernel, k1n=k1n, n2s=n2s),
        grid=(c // ct, b),
        in_specs=[seq, seq,
                  pl.BlockSpec((1, ct), lambda j, i: (0, j)),
                  pl.BlockSpec((k1n, 2 * n2s, ct), lambda j, i: (0, 0, j), pipeline_mode=pl.Buffered(1)),
                  _const_spec(tabs["fk"].shape), _const_spec(tabs["fki"].shape),
                  _const_spec(tabs["m"].shape), _const_spec(tabs["mt"].shape)],
        out_specs=pl.BlockSpec((None, half, n2s, ct), lambda j, i: (i, 0, 0, j)),
        out_shape=jax.ShapeDtypeStruct((b, half, n2s, c), BF16),
        scratch_shapes=[pltpu.VMEM((half, n2s, ct), F32), pltpu.VMEM((2 * half, n2s, ct), F32)],
        compiler_params=_cparams(("arbitrary", "arbitrary")),
        name="hyena",
    )(u4, x04, bias, kf, tabs["fk"], tabs["fki"], tabs["m"], tabs["mt"])


def _out_ffn_kernel(ret_ref, hy_ref, x_ref, mod_ref, hg_ref, n2_ref, fn_ref, wo_ref, wgu_ref, wd_ref,
                    o_ref, *, d_ret, d_ff, ff_bounds, row_parts):
    rows = x_ref.shape[0] // row_parts
    groups = [slice(r * rows, (r + 1) * rows) for r in range(row_parts)]
    gain2 = n2_ref[...] * (1.0 + mod_ref[4:5, :])
    x1s, h2s = [], []
    for rs in groups:
        hy = hy_ref[rs, :].astype(F32)
        hms = jnp.mean(hy * hy, axis=-1, keepdims=True)
        hyn = (hy * lax.rsqrt(hms + EPS) * hg_ref[...]).astype(BF16)
        mix = jnp.dot(ret_ref[rs, :], wo_ref[:d_ret, :], preferred_element_type=F32)
        mix = mix + jnp.dot(hyn, wo_ref[d_ret:, :], preferred_element_type=F32)
        x1s.append(x_ref[rs, :] + mod_ref[2:3, :] * mix)
    for x1 in x1s:
        ms = jnp.mean(x1 * x1, axis=-1, keepdims=True)
        h2s.append((x1 * lax.rsqrt(ms + EPS) * gain2 + mod_ref[3:4, :]).astype(BF16))
    accs = [None] * row_parts
    for lo, hi in zip(ff_bounds[:-1], ff_bounds[1:]):
        acts = []
        for h2 in h2s:
            g = jnp.dot(h2, wgu_ref[:, lo:hi], preferred_element_type=F32)
            u = jnp.dot(h2, wgu_ref[:, d_ff + lo:d_ff + hi], preferred_element_type=F32)
            acts.append((_silu(g) * u).astype(BF16))
        for r, a in enumerate(acts):
            d = jnp.dot(a, wd_ref[lo:hi, :], preferred_element_type=F32)
            accs[r] = d if accs[r] is None else accs[r] + d
    for rs, x1, acc in zip(groups, x1s, accs):
        x2 = x1 + mod_ref[5:6, :] * acc
        ms2 = jnp.mean(x2 * x2, axis=-1, keepdims=True)
        o_ref[rs, :] = x2 * lax.rsqrt(ms2 + EPS) * fn_ref[...]


def _ff_bounds(d_ff, parts=2):
    tiles = d_ff // MXU_WIDTH
    assert tiles * MXU_WIDTH == d_ff
    cuts = [-(-tiles * p // parts) for p in range(parts + 1)]
    return tuple(c * MXU_WIDTH for c in cuts)


def _out_ffn_call(ret, hy, x, mod, hy_gain, norm2, final_norm, w_out, w_gu, w_down):
    b, l, d = x.shape
    d_ret = ret.shape[2]
    d_hy = hy.shape[2]
    d_ff = w_down.shape[0]
    tm = ROW_TILE
    return pl.pallas_call(
        functools.partial(_out_ffn_kernel, d_ret=d_ret, d_ff=d_ff, ff_bounds=_ff_bounds(d_ff),
                          row_parts=tm // GROUP_ROWS),
        grid=(b, l // tm),
        in_specs=[pl.BlockSpec((None, tm, d_ret), lambda i, t: (i, t, 0)),
                  pl.BlockSpec((None, tm, d_hy), lambda i, t: (i, t, 0)),
                  pl.BlockSpec((None, tm, d), lambda i, t: (i, t, 0)),
                  pl.BlockSpec((None, N_MOD, d), lambda i, t: (i, 0, 0)),
                  pl.BlockSpec((1, d_hy), lambda i, t: (0, 0)),
                  pl.BlockSpec((1, d), lambda i, t: (0, 0)),
                  pl.BlockSpec((1, d), lambda i, t: (0, 0)),
                  _const_spec(w_out.shape), _const_spec(w_gu.shape), _const_spec(w_down.shape)],
        out_specs=pl.BlockSpec((None, tm, d), lambda i, t: (i, t, 0)),
        out_shape=jax.ShapeDtypeStruct((b, l, d), x.dtype),
        compiler_params=_cparams(("arbitrary", "arbitrary")),
        name="out_ffn",
    )(ret, hy, x, mod, hy_gain, norm2, final_norm, w_out, w_gu, w_down)


@functools.lru_cache(maxsize=None)
def _rope_tables(seq_len, dh):
    n = dh // 4
    t = np.arange(seq_len)
    inv = ROPE_BASE ** (-np.arange(n, dtype=np.float64) / n)
    ang = np.concatenate([(t // GRID_W)[:, None] * inv, (t % GRID_W)[:, None] * inv], axis=-1)
    cc = np.concatenate([np.cos(ang), np.cos(ang)], axis=-1)
    ss = np.concatenate([-np.sin(ang), np.sin(ang)], axis=-1)
    return cc.astype(np.float32), ss.astype(np.float32)


@functools.lru_cache(maxsize=None)
def _filter_tables(seq_len, emb_dim, emb_pad, channels):
    t = np.linspace(0.0, 1.0, seq_len)[:, None]
    bands = (emb_dim - 1) // 2
    f = np.linspace(1e-4, bands - 1, bands)[None, :]
    wpos = 2.0 * np.pi * np.arange(seq_len)[:, None] / seq_len
    emb = np.concatenate([t, np.cos(f * wpos), -np.sin(f * wpos)], axis=-1)
    emb = np.pad(emb, ((0, 0), (0, emb_pad - emb_dim)))
    emb_rev = np.concatenate([emb[:1], emb[:0:-1]], axis=0)
    emb2 = np.concatenate([emb, emb_rev], axis=1)
    max_decay = math.log(FILTER_DECAY_TARGET) / FILTER_DECAY_FAST
    min_decay = math.log(FILTER_DECAY_TARGET) / FILTER_DECAY_SLOW
    absdelta = np.abs(np.linspace(min_decay, max_decay, channels))[None, :]
    return emb2.astype(np.float32), absdelta.astype(np.float32)


def _block_diag2(w):
    z = jnp.zeros_like(w)
    return jnp.concatenate([jnp.concatenate([w, z], axis=1), jnp.concatenate([z, w], axis=1)], axis=0)


def kernel(x, c, ctx, c_ctx, w_mod, b_mod, norm1, norm2, w_in, ret_decay, ret_gn_gain, hy_short_w,
           hy_short_b, hy_w1, hy_b1, hy_w2, hy_b2, hy_w3, hy_b3, hy_w4, hy_freq, hy_bias, hy_out_norm,
           w_out, w_gate_up, w_down, final_norm):
    b, seq_len, d = x.shape
    assert w_mod.shape[0] == 1, "single-layer block"
    heads = RET_HEADS
    d_ret = ret_gn_gain.shape[1]
    dh = d_ret // heads
    d_hy = hy_bias.shape[1]
    assert dh == LANES and d_hy % HY_CT == 0 and seq_len % (FFT_N1 // 2) == 0

    rows = -(-(b + 1) // SUBLANES) * SUBLANES
    c_rows = jnp.zeros((rows, d), F32).at[:b].set(c).at[b].set(c_ctx)
    mod_all = _mod_call(c_rows, w_mod[0], b_mod[0][None, :])
    mod = mod_all[:b].reshape(b, N_MOD, d)
    mod_c = mod_all[b].reshape(N_MOD, d)

    w_in_b = w_in[0].astype(BF16)
    dec = jnp.broadcast_to(ret_decay[0].reshape(2 * heads, 1), (2 * heads, LANES))
    n1g = norm1[0][None, :]

    s0 = _ctx_call(ctx, n1g, mod_c[0:1], mod_c[1:2], w_in_b[:, d_ret:3 * d_ret], dec, heads, dh)
    cc, ss = (jnp.asarray(t) for t in _rope_tables(seq_len, dh))
    qkvg, x0c, u = _inproj_call(x, mod, n1g, w_in_b, cc, ss, hy_short_w[0], hy_short_b, heads, dh, d_hy)
    ret = _ret_call(qkvg, s0, dec, ret_gn_gain, heads, dh)

    tabs = dict(_fft_tables(seq_len))
    for name in ("fk", "fkf", "fki", "m", "mt"):
        tabs[name] = jnp.asarray(tabs[name], dtype=F32).astype(BF16)
    half, n2s = tabs["half"], tabs["n2s"]
    emb_dim = hy_w1.shape[1]
    emb_pad = -(-emb_dim // SUBLANES) * SUBLANES
    emb2, absdelta = (jnp.asarray(t) for t in _filter_tables(seq_len, emb_dim, emb_pad, d_hy))
    w1p = jnp.pad(hy_w1[0], ((0, emb_pad - emb_dim), (0, 0)))
    two = lambda a: jnp.concatenate([a, a], axis=1)
    w4 = hy_w4[0]
    zero4 = jnp.zeros((w4.shape[0], d_hy), F32)
    kern, ksum = _filt_time_call(
        emb2, _block_diag2(w1p), two(hy_b1), _block_diag2(hy_w2[0]), two(hy_b2), _block_diag2(hy_w3[0]),
        two(hy_b3), jnp.concatenate([w4[:, :d_hy], zero4], axis=0), jnp.concatenate([zero4, w4[:, d_hy:]], axis=0),
        two(hy_freq), absdelta, seq_len)
    kf = _filt_spec_call(kern.reshape(2, half, n2s, d_hy), ksum, tabs)
    hy = _hyena_call(u.reshape(b, half, n2s, d_hy), x0c.reshape(b, half, n2s, d_hy), hy_bias, kf, tabs)
    hy = hy.reshape(b, seq_len, d_hy)

    return _out_ffn_call(ret, hy, x, mod, hy_out_norm, norm2[0][None, :], final_norm[None, :],
                         w_out[0].astype(BF16), w_gate_up[0].astype(BF16), w_down[0].astype(BF16))
```

```python
import functools
import math

import jax
import jax.numpy as jnp
import numpy as np
from jax import lax
from jax.experimental import pallas as pl
from jax.experimental.pallas import tpu as pltpu

F32 = jnp.float32
BF16 = jnp.bfloat16

RET_HEADS = 4
GRID_W = 64
ROPE_BASE = 10000.0
N_MOD = 6
HYENA_PROJ = 3
FILTER_DECAY_FAST = 0.3
FILTER_DECAY_SLOW = 1.5
FILTER_DECAY_TARGET = 1e-2
EPS = 1e-6

LANES = 128
SUBLANES = 8
MXU_WIDTH = 256
VMEM_LIMIT_BYTES = 56 * 1024 * 1024

RET_CHUNK = 256
ROW_TILE = 1024
IN_ROW_TILE = 1024
GROUP_ROWS = 256
HALO = 16
HY_CT = 256
FILT_ROWS = 1024
FFT_N1 = 64


def _silu(v):
    return v / (1.0 + jnp.exp(-v))


def _dot3(a, b):
    a_hi = a.astype(BF16)
    b_hi = b.astype(BF16)
    a_lo = (a - a_hi.astype(F32)).astype(BF16)
    b_lo = (b - b_hi.astype(F32)).astype(BF16)
    dot = functools.partial(jnp.dot, preferred_element_type=F32)
    return dot(a_hi, b_hi) + (dot(a_hi, b_lo) + dot(a_lo, b_hi))


def _cparams(sem, vmem=VMEM_LIMIT_BYTES):
    return pltpu.CompilerParams(dimension_semantics=sem, vmem_limit_bytes=vmem)


def _const_spec(shape):
    nd = len(shape)
    return pl.BlockSpec(shape, lambda *_: (0,) * nd, pipeline_mode=pl.Buffered(1))


def _mod_kernel(c_ref, w_ref, b_ref, o_ref):
    s = _silu(c_ref[...])
    o_ref[...] = _dot3(s, w_ref[...]) + b_ref[...]


def _mod_call(c_rows, w_mod, b_mod):
    rows, d = c_rows.shape
    n = w_mod.shape[1]
    tn = 1536
    return pl.pallas_call(
        _mod_kernel,
        grid=(n // tn,),
        in_specs=[pl.BlockSpec((rows, d), lambda j: (0, 0)),
                  pl.BlockSpec((d, tn), lambda j: (0, j)),
                  pl.BlockSpec((1, tn), lambda j: (0, j))],
        out_specs=pl.BlockSpec((rows, tn), lambda j: (0, j)),
        out_shape=jax.ShapeDtypeStruct((rows, n), F32),
        compiler_params=_cparams(("arbitrary",)),
        name="mod",
    )(c_rows, w_mod, b_mod)


def _ctx_kernel(ctx_ref, n1_ref, sh_ref, sc_ref, wk_ref, wv_ref, dec_ref, s_ref, *, heads, dh):
    xc = ctx_ref[...]
    n_ctx = xc.shape[0]
    ms = jnp.mean(xc * xc, axis=-1, keepdims=True)
    hc = ((xc * lax.rsqrt(ms + EPS) * n1_ref[...]) * (1.0 + sc_ref[...]) + sh_ref[...]).astype(BF16)
    k = jnp.dot(hc, wk_ref[...], preferred_element_type=F32)
    v = jnp.dot(hc, wv_ref[...], preferred_element_type=F32)
    lg = jnp.log1p(-jnp.exp(dec_ref[...]))
    pos = lax.broadcasted_iota(jnp.int32, (n_ctx, dh), 0).astype(F32)
    k_scale = dh ** -0.5
    tdims = (((0,), (0,)), ((), ()))
    for h in range(heads):
        kh = k[:, h * dh:(h + 1) * dh] * k_scale
        vh = v[:, h * dh:(h + 1) * dh].astype(BF16)
        wf = jnp.exp(lg[h:h + 1, :] * (n_ctx - 1.0 - pos))
        wb = jnp.exp(lg[heads + h:heads + h + 1, :] * pos)
        s_ref[h] = lax.dot_general((kh * wf).astype(BF16), vh, tdims, preferred_element_type=F32)
        s_ref[heads + h] = lax.dot_general((kh * wb).astype(BF16), vh, tdims, preferred_element_type=F32)


def _ctx_call(ctx, norm1, shift_c, scale_c, w_in, dec, heads, dh):
    b, n_ctx, d = ctx.shape
    d_ret = heads * dh
    return pl.pallas_call(
        functools.partial(_ctx_kernel, heads=heads, dh=dh),
        grid=(b,),
        in_specs=[pl.BlockSpec((None, n_ctx, d), lambda i: (i, 0, 0)),
                  pl.BlockSpec((1, d), lambda i: (0, 0)),
                  pl.BlockSpec((1, d), lambda i: (0, 0)),
                  pl.BlockSpec((1, d), lambda i: (0, 0)),
                  pl.BlockSpec((d, d_ret), lambda i: (0, 1)),
                  pl.BlockSpec((d, d_ret), lambda i: (0, 2)),
                  pl.BlockSpec(dec.shape, lambda i: (0, 0))],
        out_specs=pl.BlockSpec((None, 2 * heads, dh, dh), lambda i: (i, 0, 0, 0)),
        out_shape=jax.ShapeDtypeStruct((b, 2 * heads, dh, dh), F32),
        compiler_params=_cparams(("arbitrary",)),
        name="ctx_state",
    )(ctx, norm1, shift_c, scale_c, w_in, w_in, dec)


def _inproj_kernel(x_ref, xp_ref, xn_ref, mod_ref, n1_ref, w_ref, cq_ref, sq_ref, ck_ref, sk_ref,
                   sw_ref, sb_ref, qkvg_ref, x0_ref, u_ref, *, heads, dh, d_hy, nt):
    t = pl.program_id(1)
    tm = x_ref.shape[0]
    d_ret = heads * dh
    gain = n1_ref[...] * (1.0 + mod_ref[1:2, :])

    def normed(x):
        ms = jnp.mean(x * x, axis=-1, keepdims=True)
        return x * lax.rsqrt(ms + EPS) * gain + mod_ref[0:1, :]

    groups = [slice(r, r + GROUP_ROWS) for r in range(0, tm, GROUP_ROWS)]
    hs = [normed(x_ref[rs, :]) for rs in groups]
    hbs = [h.astype(BF16) for h in hs]
    hy_lhs = list(hbs)
    hy_lhs[0] = jnp.concatenate([normed(xp_ref[...]), hs[0]], axis=0).astype(BF16)
    hy_lhs[-1] = jnp.concatenate([hs[-1], normed(xn_ref[...])], axis=0).astype(BF16)

    def proj(lhs, base, width):
        return [jnp.dot(hb, w_ref[:, base:base + width], preferred_element_type=F32) for hb in lhs]

    def hyena_cols(j):
        sl = slice(j * d_hy, (j + 1) * d_hy)
        parts = proj(hy_lhs, 4 * d_ret + j * d_hy, d_hy)
        head, tail = parts[0], parts[-1]
        parts[0] = jnp.concatenate([jnp.where(t == 0, 0.0, head[:HALO]), head[HALO:]], axis=0)
        parts[-1] = jnp.concatenate([tail[:GROUP_ROWS], jnp.where(t == nt - 1, 0.0, tail[GROUP_ROWS:])],
                                    axis=0)
        p = jnp.concatenate(parts, axis=0)
        n = p.shape[0]
        y = (sw_ref[0:1, sl] * pltpu.roll(p, 1, 0) + sw_ref[1:2, sl] * p
             + sw_ref[2:3, sl] * pltpu.roll(p, n - 1, 0) + sb_ref[:, sl])
        return y[HALO:HALO + tm]

    def roped_cols(base, cos_ref, sin_ref):
        for rs, p in zip(groups, proj(hbs, base, d_ret)):
            cc = cos_ref[rs, :]
            ss = sin_ref[rs, :]
            for j in range(heads):
                pj = p[:, j * dh:(j + 1) * dh]
                qkvg_ref[rs, base + j * dh:base + (j + 1) * dh] = (
                    pj * cc + pltpu.roll(pj, dh // 2, 1) * ss).astype(BF16)

    roped_cols(0, cq_ref, sq_ref)
    x0_ref[...] = hyena_cols(0).astype(x0_ref.dtype)
    roped_cols(d_ret, ck_ref, sk_ref)
    x1c = hyena_cols(1)
    for rs, g in zip(groups, proj(hbs, 3 * d_ret, d_ret)):
        qkvg_ref[rs, 3 * d_ret:4 * d_ret] = _silu(g).astype(BF16)
    u_ref[...] = (x1c * hyena_cols(2)).astype(u_ref.dtype)
    for rs, v in zip(groups, proj(hbs, 2 * d_ret, d_ret)):
        qkvg_ref[rs, 2 * d_ret:3 * d_ret] = v.astype(BF16)


def _inproj_call(x, mod, norm1, w_in, rope, short_w, short_b, heads, dh, d_hy):
    b, l, d = x.shape
    tm = IN_ROW_TILE
    nt = l // tm
    d_ret = heads * dh
    per_tile = tm // HALO
    tile = lambda i, t: (i, t, 0)
    return pl.pallas_call(
        functools.partial(_inproj_kernel, heads=heads, dh=dh, d_hy=d_hy, nt=nt),
        grid=(b, nt),
        in_specs=[pl.BlockSpec((None, tm, d), tile),
                  pl.BlockSpec((None, HALO, d), lambda i, t: (i, jnp.maximum(t * per_tile - 1, 0), 0)),
                  pl.BlockSpec((None, HALO, d),
                               lambda i, t: (i, jnp.minimum((t + 1) * per_tile, nt * per_tile - 1), 0)),
                  pl.BlockSpec((None, N_MOD, d), lambda i, t: (i, 0, 0)),
                  pl.BlockSpec((1, d), lambda i, t: (0, 0)),
                  _const_spec(w_in.shape),
                  *[pl.BlockSpec((tm, dh), lambda i, t: (t, 0)) for _ in rope],
                  pl.BlockSpec(short_w.shape, lambda i, t: (0, 0)),
                  pl.BlockSpec(short_b.shape, lambda i, t: (0, 0))],
        out_specs=[pl.BlockSpec((None, tm, 4 * d_ret), tile),
                   pl.BlockSpec((None, tm, d_hy), tile),
                   pl.BlockSpec((None, tm, d_hy), tile)],
        out_shape=[jax.ShapeDtypeStruct((b, l, 4 * d_ret), BF16),
                   jax.ShapeDtypeStruct((b, l, d_hy), BF16),
                   jax.ShapeDtypeStruct((b, l, d_hy), BF16)],
        compiler_params=_cparams(("arbitrary", "arbitrary")),
        name="in_proj",
    )(x, x, x, mod, norm1, w_in, *rope, short_w, short_b)


def _ret_kernel(q_ref, k_ref, v_ref, g_ref, sf0_ref, sb0_ref, dec_ref, gain_ref,
                o_ref, kvf_s, kvb_s, st_s, *, heads, chunk):
    l, dh = q_ref.shape
    nc = l // chunk
    h = pl.program_id(1)
    lgf = jnp.log1p(-jnp.exp(dec_ref[pl.ds(h, 1), :]))
    lgb = jnp.log1p(-jnp.exp(dec_ref[pl.ds(h + heads, 1), :]))
    il = lax.broadcasted_iota(jnp.int32, (chunk, dh), 0).astype(F32)
    kw_f = jnp.exp(lgf * (chunk - 1.0 - il))
    kw_b = jnp.exp(lgb * il)
    qw_f = jnp.exp(lgf * (il + 1.0))
    qw_b = jnp.exp(lgb * (chunk - il))
    cd_f = jnp.exp(lgf * float(chunk))
    cd_b = jnp.exp(lgb * float(chunk))
    reps = chunk // dh
    lgf_c = jnp.concatenate([lgf] * reps, axis=1)
    lgb_c = jnp.concatenate([lgb] * reps, axis=1)
    ii = lax.broadcasted_iota(jnp.int32, (chunk, chunk), 0)
    jj = lax.broadcasted_iota(jnp.int32, (chunk, chunk), 1)
    diff = (ii - jj).astype(F32)
    dmask = (jnp.where(diff >= 0, jnp.exp(lgf_c * jnp.maximum(diff, 0.0)), 0.0)
             + jnp.where(diff <= 0, jnp.exp(lgb_c * jnp.maximum(-diff, 0.0)), 0.0))
    tdims = (((0,), (0,)), ((), ()))
    ntdims = (((1,), (1,)), ((), ()))

    def chunk_states(n, carry):
        r0 = pl.multiple_of(n * chunk, chunk)
        kr = k_ref[pl.ds(r0, chunk), :].astype(F32)
        vv = v_ref[pl.ds(r0, chunk), :]
        kvf_s[n] = lax.dot_general((kr * kw_f).astype(BF16), vv, tdims, preferred_element_type=F32)
        kvb_s[n] = lax.dot_general((kr * kw_b).astype(BF16), vv, tdims, preferred_element_type=F32)
        return carry

    lax.fori_loop(0, nc, chunk_states, 0, unroll=True)

    def scan_f(n, s):
        st_s[n, :dh, :] = s.astype(BF16)
        return s * cd_f + kvf_s[n]

    lax.fori_loop(0, nc, scan_f, sf0_ref[...])

    def scan_b(m, s):
        n = nc - 1 - m
        st_s[n, dh:, :] = s.astype(BF16)
        return s * cd_b + kvb_s[n]

    lax.fori_loop(0, nc, scan_b, sb0_ref[...])

    gain = gain_ref[...]

    def chunk_out(n, carry):
        r0 = pl.multiple_of(n * chunk, chunk)
        qb = q_ref[pl.ds(r0, chunk), :]
        qr = qb.astype(F32)
        sc = lax.dot_general(qb, k_ref[pl.ds(r0, chunk), :], ntdims, preferred_element_type=F32)
        lhs = jnp.concatenate([(qr * qw_f).astype(BF16), (qr * qw_b).astype(BF16),
                               (sc * dmask).astype(BF16)], axis=1)
        rhs = jnp.concatenate([st_s[n], v_ref[pl.ds(r0, chunk), :]], axis=0)
        o = jnp.dot(lhs, rhs, preferred_element_type=F32)
        mu = jnp.mean(o, axis=-1, keepdims=True)
        d = o - mu
        var = jnp.mean(d * d, axis=-1, keepdims=True)
        gg = g_ref[pl.ds(r0, chunk), :].astype(F32)
        o_ref[pl.ds(r0, chunk), :] = (d * lax.rsqrt(var + EPS) * gain * gg).astype(o_ref.dtype)
        return carry

    lax.fori_loop(0, nc, chunk_out, 0, unroll=True)


def _ret_call(qkvg, s0, dec, gn_gain, heads, dh):
    b, l, _ = qkvg.shape
    chunk = RET_CHUNK
    nc = l // chunk
    seq = lambda off: pl.BlockSpec((None, l, dh), lambda i, h: (i, 0, off + h))
    return pl.pallas_call(
        functools.partial(_ret_kernel, heads=heads, chunk=chunk),
        grid=(b, heads),
        in_specs=[seq(0), seq(heads), seq(2 * heads), seq(3 * heads),
                  pl.BlockSpec((None, None, dh, dh), lambda i, h: (i, h, 0, 0)),
                  pl.BlockSpec((None, None, dh, dh), lambda i, h: (i, heads + h, 0, 0)),
                  pl.BlockSpec(dec.shape, lambda i, h: (0, 0)),
                  pl.BlockSpec((1, dh), lambda i, h: (0, h))],
        out_specs=pl.BlockSpec((None, l, dh), lambda i, h: (i, 0, h)),
        out_shape=jax.ShapeDtypeStruct((b, l, heads * dh), BF16),
        scratch_shapes=[pltpu.VMEM((nc, dh, dh), F32), pltpu.VMEM((nc, dh, dh), F32),
                        pltpu.VMEM((nc, 2 * dh, dh), BF16)],
        compiler_params=_cparams(("arbitrary", "arbitrary")),
        name="retention",
    )(qkvg, qkvg, qkvg, qkvg, s0, s0, dec, gn_gain)


def _filt_time_kernel(emb_ref, w1_ref, b1_ref, w2_ref, b2_ref, w3_ref, b3_ref, w4f_ref, w4b_ref,
                      fr_ref, dl_ref, kern_ref, s_ref, *, seq_len):
    i = pl.program_id(0)
    rows, c = kern_ref.shape[1], kern_ref.shape[2]
    fr = fr_ref[...]
    hdot = _dot3
    z = jnp.sin(fr * (hdot(emb_ref[...], w1_ref[...]) + b1_ref[...]))
    z = jnp.sin(fr * (hdot(z, w2_ref[...]) + b2_ref[...]))
    z = jnp.sin(fr * (hdot(z, w3_ref[...]) + b3_ref[...]))
    pos = (i * rows + lax.broadcasted_iota(jnp.int32, (rows, c), 0)).astype(F32)
    inv = 1.0 / (seq_len - 1.0)
    adl = dl_ref[...]
    hf = hdot(z, w4f_ref[...]) * jnp.exp(-(pos * inv) * adl)
    hr = hdot(z, w4b_ref[...]) * jnp.exp(-((seq_len - pos) * inv) * adl)
    hr = jnp.where(pos == 0.0, 0.0, hr)
    kern_ref[0] = hf
    kern_ref[1] = hr
    part = jnp.sum(jnp.abs(hf) + jnp.abs(hr), axis=0, keepdims=True)

    @pl.when(i == 0)
    def _():
        s_ref[...] = part

    @pl.when(i != 0)
    def _():
        s_ref[...] += part


def _filt_time_call(emb2, w1, b1, w2, b2, w3, b3, w4f, w4b, freq, absdelta, seq_len):
    c = absdelta.shape[1]
    rows = FILT_ROWS
    small = lambda a: pl.BlockSpec(a.shape, lambda i: (0,) * a.ndim)
    return pl.pallas_call(
        functools.partial(_filt_time_kernel, seq_len=seq_len),
        grid=(seq_len // rows,),
        in_specs=[pl.BlockSpec((rows, emb2.shape[1]), lambda i: (i, 0)),
                  small(w1), small(b1), small(w2), small(b2), small(w3), small(b3), small(w4f), small(w4b),
                  small(freq), small(absdelta)],
        out_specs=[pl.BlockSpec((2, rows, c), lambda i: (0, i, 0)),
                   pl.BlockSpec((1, c), lambda i: (0, 0))],
        out_shape=[jax.ShapeDtypeStruct((2, seq_len, c), F32),
                   jax.ShapeDtypeStruct((1, c), F32)],
        compiler_params=_cparams(("arbitrary",)),
        name="filt_time",
    )(emb2, w1, b1, w2, b2, w3, b3, w4f, w4b, freq, absdelta)


@functools.lru_cache(maxsize=None)
def _fft_tables(seq_len):
    n = 2 * seq_len
    n1s = FFT_N1
    n2s = n // n1s
    half = n1s // 2
    k1n = half + 1
    n1 = np.arange(half)
    k1 = np.arange(k1n)
    th = 2.0 * np.pi * (np.outer(k1, n1) % n1s) / n1s
    herm = np.where((k1 == 0) | (k1 == half), 1.0, 2.0)
    fa = np.concatenate([np.cos(th), -np.sin(th)[1:half]], axis=0)
    sgn = np.concatenate([(-1.0) ** k1, (-1.0) ** k1[1:half]])
    fai = np.concatenate([np.cos(th) * herm[:, None], (-np.sin(th) * herm[:, None])[1:half]], axis=0).T / n
    eye = np.eye(SUBLANES)
    fk = np.kron(fa, eye)
    fks = np.kron(fa * sgn[:, None], eye)
    fki = np.kron(fai, eye)
    k2 = np.arange(n2s)
    n2 = np.arange(n2s)
    m = np.zeros((k1n, 2 * n2s, 2 * n2s))
    for a in range(k1n):
        ang = 2.0 * np.pi * (np.outer(a + n1s * k2, n2) % n) / n
        gr, gi = np.cos(ang), -np.sin(ang)
        m[a] = np.block([[gr, -gi], [gi, gr]])
    mt = np.transpose(m, (0, 2, 1))
    return dict(fk=fk, fkf=np.concatenate([fk, fks], axis=1), fki=fki, m=m, mt=mt,
                k1n=k1n, n2s=n2s, half=half)


def _stage_n1(src_refs, mat, dst_ref, n2s):
    rows = dst_ref.shape[0]
    for j in range(n2s // SUBLANES):
        sl = slice(j * SUBLANES, (j + 1) * SUBLANES)
        parts = [r[:, sl, :] for r in src_refs]
        xg = jnp.concatenate([p.reshape(p.shape[0] * SUBLANES, p.shape[2]) for p in parts], axis=0)
        a = jnp.dot(mat, xg.astype(BF16), preferred_element_type=F32)
        dst_ref[:, sl, :] = a.reshape(rows, SUBLANES, a.shape[1])


def _n2_input(a_s, k1, half, n2s):
    if 0 < k1 < half:
        return jnp.concatenate([a_s[k1], a_s[half + k1]], axis=0)
    return a_s[k1]


def _filt_spec_kernel(kern_ref, s_ref, fkf_ref, m_ref, kf_ref, a_s, *, k1n, n2s):
    half = k1n - 1
    _stage_n1([kern_ref.at[0], kern_ref.at[1]], fkf_ref[...], a_s, n2s)
    inv = 1.0 / (s_ref[...] + EPS)
    for k1 in range(k1n):
        a = _n2_input(a_s, k1, half, n2s).astype(BF16)
        kf_ref[k1] = jnp.dot(m_ref[k1, :, :a.shape[0]], a, preferred_element_type=F32) * inv


def _filt_spec_call(kern4, s, tabs):
    _, half, n2s, c = kern4.shape
    ct = HY_CT
    k1n = tabs["k1n"]
    return pl.pallas_call(
        functools.partial(_filt_spec_kernel, k1n=k1n, n2s=n2s),
        grid=(c // ct,),
        in_specs=[pl.BlockSpec((2, half, n2s, ct), lambda j: (0, 0, 0, j)),
                  pl.BlockSpec((1, ct), lambda j: (0, j)),
                  _const_spec(tabs["fkf"].shape), _const_spec(tabs["m"].shape)],
        out_specs=pl.BlockSpec((k1n, 2 * n2s, ct), lambda j: (0, 0, j)),
        out_shape=jax.ShapeDtypeStruct((k1n, 2 * n2s, c), F32),
        scratch_shapes=[pltpu.VMEM((2 * half, n2s, ct), F32)],
        compiler_params=_cparams(("arbitrary",)),
        name="filt_spec",
    )(kern4, s, tabs["fkf"], tabs["m"])


def _hyena_kernel(u_ref, x0_ref, bias_ref, kf_ref, fk_ref, fki_ref, m_ref, mt_ref, *rest, k1n, n2s, n_cast):
    cast_in, (o_ref, *cast_out), (u_s, a_s) = rest[:n_cast], rest[n_cast:2 * n_cast + 1], rest[2 * n_cast + 1:]
    for src, dst in zip(cast_in, cast_out):
        dst[...] = src[...].astype(dst.dtype)
    half = k1n - 1
    ct = u_ref.shape[2]
    u_s[...] = u_ref[...].astype(F32)
    _stage_n1([u_s], fk_ref[...], a_s, n2s)

    for k1 in range(k1n):
        a = _n2_input(a_s, k1, half, n2s).astype(BF16)
        x = jnp.dot(m_ref[k1, :, :a.shape[0]], a, preferred_element_type=F32)
        xr, xi = x[:n2s], x[n2s:]
        kr, ki = kf_ref[k1, :n2s, :], kf_ref[k1, n2s:, :]
        y = jnp.concatenate([xr * kr - xi * ki, xr * ki + xi * kr], axis=0).astype(BF16)
        if 0 < k1 < half:
            z = jnp.dot(mt_ref[k1], y, preferred_element_type=F32)
            a_s[k1] = z[:n2s]
            a_s[half + k1] = z[n2s:]
        else:
            a_s[k1] = jnp.dot(mt_ref[k1, :n2s, :], y, preferred_element_type=F32)

    fki = fki_ref[...]
    bias = bias_ref[...]
    pair = 2 * SUBLANES
    for j in range(n2s // pair):
        ys = []
        for jj in range(2):
            sl = slice(j * pair + jj * SUBLANES, j * pair + (jj + 1) * SUBLANES)
            zg = a_s[:, sl, :].reshape(2 * half * SUBLANES, ct)
            ys.append(jnp.dot(fki, zg.astype(BF16), preferred_element_type=F32).reshape(half, SUBLANES, ct))
        y = jnp.concatenate(ys, axis=1)
        sl = slice(j * pair, (j + 1) * pair)
        o_ref[:, sl, :] = (x0_ref[:, sl, :].astype(F32) * (y + u_s[:, sl, :] * bias)).astype(o_ref.dtype)


def _hyena_call(u4, x04, bias, kf, tabs, cast_weights):
    b, half, n2s, c = u4.shape
    ct = HY_CT
    k1n = tabs["k1n"]
    nct = c // ct
    steps = nct * b
    seq = pl.BlockSpec((None, half, n2s, ct), lambda j, i: (i, 0, 0, j))
    slabs = [pl.BlockSpec((w.shape[0] // steps, w.shape[1]), lambda j, i: (j * b + i, 0)) for w in cast_weights]
    assert all(w.shape[0] % (steps * 2 * SUBLANES) == 0 for w in cast_weights)
    outs = pl.pallas_call(
        functools.partial(_hyena_kernel, k1n=k1n, n2s=n2s, n_cast=len(cast_weights)),
        grid=(nct, b),
        in_specs=[seq, seq,
                  pl.BlockSpec((1, ct), lambda j, i: (0, j)),
                  pl.BlockSpec((k1n, 2 * n2s, ct), lambda j, i: (0, 0, j), pipeline_mode=pl.Buffered(1)),
                  _const_spec(tabs["fk"].shape), _const_spec(tabs["fki"].shape),
                  _const_spec(tabs["m"].shape), _const_spec(tabs["mt"].shape), *slabs],
        out_specs=[seq, *slabs],
        out_shape=[jax.ShapeDtypeStruct((b, half, n2s, c), BF16),
                   *[jax.ShapeDtypeStruct(w.shape, BF16) for w in cast_weights]],
        scratch_shapes=[pltpu.VMEM((half, n2s, ct), F32), pltpu.VMEM((2 * half, n2s, ct), F32)],
        compiler_params=_cparams(("arbitrary", "arbitrary")),
        name="hyena",
    )(u4, x04, bias, kf, tabs["fk"], tabs["fki"], tabs["m"], tabs["mt"], *cast_weights)
    return outs[0], outs[1:]


def _out_ffn_kernel(ret_ref, hy_ref, x_ref, mod_ref, hg_ref, n2_ref, fn_ref, wo_ref, wgu_ref, wd_ref,
                    o_ref, *, d_ret, d_ff, ff_bounds, row_parts):
    rows = x_ref.shape[0] // row_parts
    groups = [slice(r * rows, (r + 1) * rows) for r in range(row_parts)]
    gain2 = n2_ref[...] * (1.0 + mod_ref[4:5, :])
    x1s, h2s = [], []
    for rs in groups:
        hy = hy_ref[rs, :].astype(F32)
        hms = jnp.mean(hy * hy, axis=-1, keepdims=True)
        hyn = (hy * lax.rsqrt(hms + EPS) * hg_ref[...]).astype(BF16)
        mix = jnp.dot(ret_ref[rs, :], wo_ref[:d_ret, :], preferred_element_type=F32)
        mix = mix + jnp.dot(hyn, wo_ref[d_ret:, :], preferred_element_type=F32)
        x1s.append(x_ref[rs, :] + mod_ref[2:3, :] * mix)
    for x1 in x1s:
        ms = jnp.mean(x1 * x1, axis=-1, keepdims=True)
        h2s.append((x1 * lax.rsqrt(ms + EPS) * gain2 + mod_ref[3:4, :]).astype(BF16))
    accs = [None] * row_parts
    for lo, hi in zip(ff_bounds[:-1], ff_bounds[1:]):
        acts = []
        for h2 in h2s:
            g = jnp.dot(h2, wgu_ref[:, lo:hi], preferred_element_type=F32)
            u = jnp.dot(h2, wgu_ref[:, d_ff + lo:d_ff + hi], preferred_element_type=F32)
            acts.append((_silu(g) * u).astype(BF16))
        for r, a in enumerate(acts):
            d = jnp.dot(a, wd_ref[lo:hi, :], preferred_element_type=F32)
            accs[r] = d if accs[r] is None else accs[r] + d
    for rs, x1, acc in zip(groups, x1s, accs):
        x2 = x1 + mod_ref[5:6, :] * acc
        ms2 = jnp.mean(x2 * x2, axis=-1, keepdims=True)
        o_ref[rs, :] = x2 * lax.rsqrt(ms2 + EPS) * fn_ref[...]


def _ff_bounds(d_ff, parts=2):
    tiles = d_ff // MXU_WIDTH
    assert tiles * MXU_WIDTH == d_ff
    cuts = [-(-tiles * p // parts) for p in range(parts + 1)]
    return tuple(c * MXU_WIDTH for c in cuts)


def _out_ffn_call(ret, hy, x, mod, hy_gain, norm2, final_norm, w_out, w_gu, w_down):
    b, l, d = x.shape
    d_ret = ret.shape[2]
    d_hy = hy.shape[2]
    d_ff = w_down.shape[0]
    tm = ROW_TILE
    return pl.pallas_call(
        functools.partial(_out_ffn_kernel, d_ret=d_ret, d_ff=d_ff, ff_bounds=_ff_bounds(d_ff),
                          row_parts=tm // GROUP_ROWS),
        grid=(b, l // tm),
        in_specs=[pl.BlockSpec((None, tm, d_ret), lambda i, t: (i, t, 0)),
                  pl.BlockSpec((None, tm, d_hy), lambda i, t: (i, t, 0)),
                  pl.BlockSpec((None, tm, d), lambda i, t: (i, t, 0)),
                  pl.BlockSpec((None, N_MOD, d), lambda i, t: (i, 0, 0)),
                  pl.BlockSpec((1, d_hy), lambda i, t: (0, 0)),
                  pl.BlockSpec((1, d), lambda i, t: (0, 0)),
                  pl.BlockSpec((1, d), lambda i, t: (0, 0)),
                  _const_spec(w_out.shape), _const_spec(w_gu.shape), _const_spec(w_down.shape)],
        out_specs=pl.BlockSpec((None, tm, d), lambda i, t: (i, t, 0)),
        out_shape=jax.ShapeDtypeStruct((b, l, d), x.dtype),
        compiler_params=_cparams(("arbitrary", "arbitrary")),
        name="out_ffn",
    )(ret, hy, x, mod, hy_gain, norm2, final_norm, w_out, w_gu, w_down)


@functools.lru_cache(maxsize=None)
def _rope_tables(seq_len, dh):
    n = dh // 4
    t = np.arange(seq_len)
    inv = ROPE_BASE ** (-np.arange(n, dtype=np.float64) / n)
    ang = np.concatenate([(t // GRID_W)[:, None] * inv, (t % GRID_W)[:, None] * inv], axis=-1)
    cc = np.concatenate([np.cos(ang), np.cos(ang)], axis=-1)
    ss = np.concatenate([-np.sin(ang), np.sin(ang)], axis=-1)
    k_scale = dh ** -0.5
    return tuple(t.astype(np.float32) for t in (cc, ss, cc * k_scale, ss * k_scale))


@functools.lru_cache(maxsize=None)
def _filter_tables(seq_len, emb_dim, emb_pad, channels):
    t = np.linspace(0.0, 1.0, seq_len)[:, None]
    bands = (emb_dim - 1) // 2
    f = np.linspace(1e-4, bands - 1, bands)[None, :]
    wpos = 2.0 * np.pi * np.arange(seq_len)[:, None] / seq_len
    emb = np.concatenate([t, np.cos(f * wpos), -np.sin(f * wpos)], axis=-1)
    emb = np.pad(emb, ((0, 0), (0, emb_pad - emb_dim)))
    emb_rev = np.concatenate([emb[:1], emb[:0:-1]], axis=0)
    emb2 = np.concatenate([emb, emb_rev], axis=1)
    max_decay = math.log(FILTER_DECAY_TARGET) / FILTER_DECAY_FAST
    min_decay = math.log(FILTER_DECAY_TARGET) / FILTER_DECAY_SLOW
    absdelta = np.abs(np.linspace(min_decay, max_decay, channels))[None, :]
    return emb2.astype(np.float32), absdelta.astype(np.float32)


def _block_diag2(w):
    z = jnp.zeros_like(w)
    return jnp.concatenate([jnp.concatenate([w, z], axis=1), jnp.concatenate([z, w], axis=1)], axis=0)


def kernel(x, c, ctx, c_ctx, w_mod, b_mod, norm1, norm2, w_in, ret_decay, ret_gn_gain, hy_short_w,
           hy_short_b, hy_w1, hy_b1, hy_w2, hy_b2, hy_w3, hy_b3, hy_w4, hy_freq, hy_bias, hy_out_norm,
           w_out, w_gate_up, w_down, final_norm):
    b, seq_len, d = x.shape
    assert w_mod.shape[0] == 1, "single-layer block"
    heads = RET_HEADS
    d_ret = ret_gn_gain.shape[1]
    dh = d_ret // heads
    d_hy = hy_bias.shape[1]
    assert dh == LANES and d_hy % HY_CT == 0 and seq_len % (FFT_N1 // 2) == 0

    rows = -(-(b + 1) // SUBLANES) * SUBLANES
    c_rows = jnp.zeros((rows, d), F32).at[:b].set(c).at[b].set(c_ctx)
    mod_all = _mod_call(c_rows, w_mod[0], b_mod[0][None, :])
    mod = mod_all[:b].reshape(b, N_MOD, d)
    mod_c = mod_all[b].reshape(N_MOD, d)

    w_in_b = w_in[0].astype(BF16)
    dec = jnp.broadcast_to(ret_decay[0].reshape(2 * heads, 1), (2 * heads, LANES))
    n1g = norm1[0][None, :]

    s0 = _ctx_call(ctx, n1g, mod_c[0:1], mod_c[1:2], w_in_b, dec, heads, dh)
    rope = [jnp.asarray(t) for t in _rope_tables(seq_len, dh)]
    qkvg, x0c, u = _inproj_call(x, mod, n1g, w_in_b, rope, hy_short_w[0], hy_short_b, heads, dh, d_hy)
    ret = _ret_call(qkvg, s0, dec, ret_gn_gain, heads, dh)

    tabs = dict(_fft_tables(seq_len))
    for name in ("fk", "fkf", "fki", "m", "mt"):
        tabs[name] = jnp.asarray(tabs[name], dtype=F32).astype(BF16)
    half, n2s = tabs["half"], tabs["n2s"]
    emb_dim = hy_w1.shape[1]
    emb_pad = -(-emb_dim // SUBLANES) * SUBLANES
    emb2, absdelta = (jnp.asarray(t) for t in _filter_tables(seq_len, emb_dim, emb_pad, d_hy))
    w1p = jnp.pad(hy_w1[0], ((0, emb_pad - emb_dim), (0, 0)))
    two = lambda a: jnp.concatenate([a, a], axis=1)
    w4 = hy_w4[0]
    zero4 = jnp.zeros((w4.shape[0], d_hy), F32)
    kern, ksum = _filt_time_call(
        emb2, _block_diag2(w1p), two(hy_b1), _block_diag2(hy_w2[0]), two(hy_b2), _block_diag2(hy_w3[0]),
        two(hy_b3), jnp.concatenate([w4[:, :d_hy], zero4], axis=0), jnp.concatenate([zero4, w4[:, d_hy:]], axis=0),
        two(hy_freq), absdelta, seq_len)
    kf = _filt_spec_call(kern.reshape(2, half, n2s, d_hy), ksum, tabs)
    hy, (w_out_b, w_gu_b, w_down_b) = _hyena_call(
        u.reshape(b, half, n2s, d_hy), x0c.reshape(b, half, n2s, d_hy), hy_bias, kf, tabs,
        [w_out[0], w_gate_up[0], w_down[0]])
    hy = hy.reshape(b, seq_len, d_hy)

    return _out_ffn_call(ret, hy, x, mod, hy_out_norm, norm2[0][None, :], final_norm[None, :],
                         w_out_b, w_gu_b, w_down_b)
```

```python
import functools
import math

import jax
import jax.numpy as jnp
import numpy as np
from jax import lax
from jax.experimental import pallas as pl
from jax.experimental.pallas import tpu as pltpu

F32 = jnp.float32
BF16 = jnp.bfloat16

RET_HEADS = 4
GRID_W = 64
ROPE_BASE = 10000.0
N_MOD = 6
HYENA_PROJ = 3
FILTER_DECAY_FAST = 0.3
FILTER_DECAY_SLOW = 1.5
FILTER_DECAY_TARGET = 1e-2
EPS = 1e-6

LANES = 128
SUBLANES = 8
MXU_WIDTH = 256
VMEM_LIMIT_BYTES = 56 * 1024 * 1024

RET_CHUNK = 256
ROW_TILE = 1024
IN_ROW_TILE = 1024
GROUP_ROWS = 256
CTX_BATCHES = 4
HALO = 16
HY_CT = 256
FILT_ROWS = 1024
FFT_N1 = 64


def _silu(v):
    return v / (1.0 + jnp.exp(-v))


def _dot3(a, b):
    a_hi = a.astype(BF16)
    b_hi = b.astype(BF16)
    a_lo = (a - a_hi.astype(F32)).astype(BF16)
    b_lo = (b - b_hi.astype(F32)).astype(BF16)
    dot = functools.partial(jnp.dot, preferred_element_type=F32)
    return dot(a_hi, b_hi) + (dot(a_hi, b_lo) + dot(a_lo, b_hi))


def _cparams(sem, vmem=VMEM_LIMIT_BYTES):
    return pltpu.CompilerParams(dimension_semantics=sem, vmem_limit_bytes=vmem)


def _const_spec(shape):
    nd = len(shape)
    return pl.BlockSpec(shape, lambda *_: (0,) * nd, pipeline_mode=pl.Buffered(1))


def _mod_kernel(c_ref, w_ref, b_ref, o_ref):
    s = _silu(c_ref[...])
    o_ref[...] = _dot3(s, w_ref[...]) + b_ref[...]


def _mod_call(c_rows, w_mod, b_mod):
    rows, d = c_rows.shape
    n = w_mod.shape[1]
    tn = 1536
    return pl.pallas_call(
        _mod_kernel,
        grid=(n // tn,),
        in_specs=[pl.BlockSpec((rows, d), lambda j: (0, 0)),
                  pl.BlockSpec((d, tn), lambda j: (0, j)),
                  pl.BlockSpec((1, tn), lambda j: (0, j))],
        out_specs=pl.BlockSpec((rows, tn), lambda j: (0, j)),
        out_shape=jax.ShapeDtypeStruct((rows, n), F32),
        compiler_params=_cparams(("arbitrary",)),
        name="mod",
    )(c_rows, w_mod, b_mod)


def _ctx_kernel(ctx_ref, n1_ref, sh_ref, sc_ref, wk_ref, wv_ref, dec_ref, s_ref, *, heads, dh):
    nb, n_ctx, d = ctx_ref.shape
    xc = ctx_ref[...].reshape(nb * n_ctx, d)
    ms = jnp.mean(xc * xc, axis=-1, keepdims=True)
    hc = ((xc * lax.rsqrt(ms + EPS) * n1_ref[...]) * (1.0 + sc_ref[...]) + sh_ref[...]).astype(BF16)
    k = jnp.dot(hc, wk_ref[...], preferred_element_type=F32)
    v = jnp.dot(hc, wv_ref[...], preferred_element_type=F32)
    lg = jnp.log1p(-jnp.exp(dec_ref[...]))
    pos = lax.broadcasted_iota(jnp.int32, (n_ctx, dh), 0).astype(F32)
    k_scale = dh ** -0.5
    tdims = (((0,), (0,)), ((), ()))
    for h in range(heads):
        wf = jnp.exp(lg[h:h + 1, :] * (n_ctx - 1.0 - pos)) * k_scale
        wb = jnp.exp(lg[heads + h:heads + h + 1, :] * pos) * k_scale
        for i in range(nb):
            rows = slice(i * n_ctx, (i + 1) * n_ctx)
            kh = k[rows, h * dh:(h + 1) * dh]
            vh = v[rows, h * dh:(h + 1) * dh].astype(BF16)
            s_ref[i, h] = lax.dot_general((kh * wf).astype(BF16), vh, tdims, preferred_element_type=F32)
            s_ref[i, heads + h] = lax.dot_general((kh * wb).astype(BF16), vh, tdims,
                                                  preferred_element_type=F32)


def _ctx_call(ctx, norm1, shift_c, scale_c, w_in, dec, heads, dh):
    b, n_ctx, d = ctx.shape
    d_ret = heads * dh
    nb = CTX_BATCHES
    assert b % nb == 0
    return pl.pallas_call(
        functools.partial(_ctx_kernel, heads=heads, dh=dh),
        grid=(b // nb,),
        in_specs=[pl.BlockSpec((nb, n_ctx, d), lambda i: (i, 0, 0)),
                  pl.BlockSpec((1, d), lambda i: (0, 0)),
                  pl.BlockSpec((1, d), lambda i: (0, 0)),
                  pl.BlockSpec((1, d), lambda i: (0, 0)),
                  pl.BlockSpec((d, d_ret), lambda i: (0, 1)),
                  pl.BlockSpec((d, d_ret), lambda i: (0, 2)),
                  pl.BlockSpec(dec.shape, lambda i: (0, 0))],
        out_specs=pl.BlockSpec((nb, 2 * heads, dh, dh), lambda i: (i, 0, 0, 0)),
        out_shape=jax.ShapeDtypeStruct((b, 2 * heads, dh, dh), F32),
        compiler_params=_cparams(("arbitrary",)),
        name="ctx_state",
    )(ctx, norm1, shift_c, scale_c, w_in, w_in, dec)


def _inproj_kernel(x_ref, xp_ref, xn_ref, mod_ref, n1_ref, w_ref, cq_ref, sq_ref, ck_ref, sk_ref,
                   sw_ref, sb_ref, qkvg_ref, x0_ref, u_ref, *, heads, dh, d_hy, nt):
    t = pl.program_id(1)
    tm = x_ref.shape[0]
    d_ret = heads * dh
    gain = n1_ref[...] * (1.0 + mod_ref[1:2, :])

    def normed(x):
        ms = jnp.mean(x * x, axis=-1, keepdims=True)
        return x * lax.rsqrt(ms + EPS) * gain + mod_ref[0:1, :]

    groups = [slice(r, r + GROUP_ROWS) for r in range(0, tm, GROUP_ROWS)]
    hs = [normed(x_ref[rs, :]) for rs in groups]
    hbs = [h.astype(BF16) for h in hs]
    hy_lhs = list(hbs)
    hy_lhs[0] = jnp.concatenate([normed(xp_ref[...]), hs[0]], axis=0).astype(BF16)
    hy_lhs[-1] = jnp.concatenate([hs[-1], normed(xn_ref[...])], axis=0).astype(BF16)

    def proj(lhs, base, width):
        return [jnp.dot(hb, w_ref[:, base:base + width], preferred_element_type=F32) for hb in lhs]

    def hyena_cols(j):
        sl = slice(j * d_hy, (j + 1) * d_hy)
        parts = proj(hy_lhs, 4 * d_ret + j * d_hy, d_hy)
        head, tail = parts[0], parts[-1]
        parts[0] = jnp.concatenate([jnp.where(t == 0, 0.0, head[:HALO]), head[HALO:]], axis=0)
        parts[-1] = jnp.concatenate([tail[:GROUP_ROWS], jnp.where(t == nt - 1, 0.0, tail[GROUP_ROWS:])],
                                    axis=0)
        p = jnp.concatenate(parts, axis=0)
        n = p.shape[0]
        y = (sw_ref[0:1, sl] * pltpu.roll(p, 1, 0) + sw_ref[1:2, sl] * p
             + sw_ref[2:3, sl] * pltpu.roll(p, n - 1, 0) + sb_ref[:, sl])
        return y[HALO:HALO + tm]

    def roped_cols(base, cos_ref, sin_ref):
        for rs, p in zip(groups, proj(hbs, base, d_ret)):
            cc = cos_ref[rs, :]
            ss = sin_ref[rs, :]
            for j in range(heads):
                pj = p[:, j * dh:(j + 1) * dh]
                qkvg_ref[rs, base + j * dh:base + (j + 1) * dh] = (
                    pj * cc + pltpu.roll(pj, dh // 2, 1) * ss).astype(BF16)

    roped_cols(0, cq_ref, sq_ref)
    x0_ref[...] = hyena_cols(0).astype(x0_ref.dtype)
    roped_cols(d_ret, ck_ref, sk_ref)
    x1c = hyena_cols(1)
    for rs, g in zip(groups, proj(hbs, 3 * d_ret, d_ret)):
        qkvg_ref[rs, 3 * d_ret:4 * d_ret] = _silu(g).astype(BF16)
    u_ref[...] = (x1c * hyena_cols(2)).astype(u_ref.dtype)
    for rs, v in zip(groups, proj(hbs, 2 * d_ret, d_ret)):
        qkvg_ref[rs, 2 * d_ret:3 * d_ret] = v.astype(BF16)


def _inproj_call(x, mod, norm1, w_in, rope, short_w, short_b, heads, dh, d_hy):
    b, l, d = x.shape
    tm = IN_ROW_TILE
    nt = l // tm
    d_ret = heads * dh
    per_tile = tm // HALO
    tile = lambda i, t: (i, t, 0)
    return pl.pallas_call(
        functools.partial(_inproj_kernel, heads=heads, dh=dh, d_hy=d_hy, nt=nt),
        grid=(b, nt),
        in_specs=[pl.BlockSpec((None, tm, d), tile),
                  pl.BlockSpec((None, HALO, d), lambda i, t: (i, jnp.maximum(t * per_tile - 1, 0), 0)),
                  pl.BlockSpec((None, HALO, d),
                               lambda i, t: (i, jnp.minimum((t + 1) * per_tile, nt * per_tile - 1), 0)),
                  pl.BlockSpec((None, N_MOD, d), lambda i, t: (i, 0, 0)),
                  pl.BlockSpec((1, d), lambda i, t: (0, 0)),
                  _const_spec(w_in.shape),
                  *[pl.BlockSpec((tm, dh), lambda i, t: (t, 0)) for _ in rope],
                  pl.BlockSpec(short_w.shape, lambda i, t: (0, 0)),
                  pl.BlockSpec(short_b.shape, lambda i, t: (0, 0))],
        out_specs=[pl.BlockSpec((None, tm, 4 * d_ret), tile),
                   pl.BlockSpec((None, tm, d_hy), tile),
                   pl.BlockSpec((None, tm, d_hy), tile)],
        out_shape=[jax.ShapeDtypeStruct((b, l, 4 * d_ret), BF16),
                   jax.ShapeDtypeStruct((b, l, d_hy), BF16),
                   jax.ShapeDtypeStruct((b, l, d_hy), BF16)],
        compiler_params=_cparams(("arbitrary", "arbitrary")),
        name="in_proj",
    )(x, x, x, mod, norm1, w_in, *rope, short_w, short_b)


def _ret_kernel(q_ref, k_ref, v_ref, g_ref, sf0_ref, sb0_ref, dec_ref, gain_ref,
                o_ref, kvf_s, kvb_s, st_s, *, heads, chunk):
    l, dh = q_ref.shape
    nc = l // chunk
    h = pl.program_id(1)
    lgf = jnp.log1p(-jnp.exp(dec_ref[pl.ds(h, 1), :]))
    lgb = jnp.log1p(-jnp.exp(dec_ref[pl.ds(h + heads, 1), :]))
    il = lax.broadcasted_iota(jnp.int32, (chunk, dh), 0).astype(F32)
    jl = lax.broadcasted_iota(jnp.int32, (dh, chunk), 1).astype(F32)
    qw_f = jnp.exp(lgf * (il + 1.0))
    qw_b = jnp.exp(lgb * (chunk - il))
    cd_f = jnp.exp(lgf * float(chunk))
    cd_b = jnp.exp(lgb * float(chunk))
    reps = chunk // dh
    lgf_c = jnp.concatenate([lgf] * reps, axis=1)
    lgb_c = jnp.concatenate([lgb] * reps, axis=1)
    kwt_f = jnp.exp(lgf_c * (chunk - 1.0 - jl))
    kwt_b = jnp.exp(lgb_c * jl)
    ii = lax.broadcasted_iota(jnp.int32, (chunk, chunk), 0)
    jj = lax.broadcasted_iota(jnp.int32, (chunk, chunk), 1)
    diff = (ii - jj).astype(F32)
    dmask = (jnp.where(diff >= 0, jnp.exp(lgf_c * jnp.maximum(diff, 0.0)), 0.0)
             + jnp.where(diff <= 0, jnp.exp(lgb_c * jnp.maximum(-diff, 0.0)), 0.0))
    ntdims = (((1,), (1,)), ((), ()))

    def chunk_states(n, carry):
        r0 = pl.multiple_of(n * chunk, chunk)
        kt = k_ref[pl.ds(r0, chunk), :].astype(F32).T
        vv = v_ref[pl.ds(r0, chunk), :]
        kvf_s[n] = jnp.dot((kt * kwt_f).astype(BF16), vv, preferred_element_type=F32)
        kvb_s[n] = jnp.dot((kt * kwt_b).astype(BF16), vv, preferred_element_type=F32)
        return carry

    lax.fori_loop(0, nc, chunk_states, 0, unroll=True)

    def scan_f(n, s):
        st_s[n, :dh, :] = s.astype(BF16)
        return s * cd_f + kvf_s[n]

    lax.fori_loop(0, nc, scan_f, sf0_ref[...])

    def scan_b(m, s):
        n = nc - 1 - m
        st_s[n, dh:, :] = s.astype(BF16)
        return s * cd_b + kvb_s[n]

    lax.fori_loop(0, nc, scan_b, sb0_ref[...])

    gain = gain_ref[...]

    def chunk_out(n, carry):
        r0 = pl.multiple_of(n * chunk, chunk)
        qb = q_ref[pl.ds(r0, chunk), :]
        qr = qb.astype(F32)
        sc = lax.dot_general(qb, k_ref[pl.ds(r0, chunk), :], ntdims, preferred_element_type=F32)
        lhs = jnp.concatenate([(qr * qw_f).astype(BF16), (qr * qw_b).astype(BF16),
                               (sc * dmask).astype(BF16)], axis=1)
        rhs = jnp.concatenate([st_s[n], v_ref[pl.ds(r0, chunk), :]], axis=0)
        o = jnp.dot(lhs, rhs, preferred_element_type=F32)
        mu = jnp.mean(o, axis=-1, keepdims=True)
        d = o - mu
        var = jnp.mean(d * d, axis=-1, keepdims=True)
        gg = g_ref[pl.ds(r0, chunk), :].astype(F32)
        o_ref[pl.ds(r0, chunk), :] = (d * lax.rsqrt(var + EPS) * gain * gg).astype(o_ref.dtype)
        return carry

    lax.fori_loop(0, nc, chunk_out, 0, unroll=True)


def _ret_call(qkvg, s0, dec, gn_gain, heads, dh):
    b, l, _ = qkvg.shape
    chunk = RET_CHUNK
    nc = l // chunk
    seq = lambda off: pl.BlockSpec((None, l, dh), lambda i, h: (i, 0, off + h))
    return pl.pallas_call(
        functools.partial(_ret_kernel, heads=heads, chunk=chunk),
        grid=(b, heads),
        in_specs=[seq(0), seq(heads), seq(2 * heads), seq(3 * heads),
                  pl.BlockSpec((None, None, dh, dh), lambda i, h: (i, h, 0, 0)),
                  pl.BlockSpec((None, None, dh, dh), lambda i, h: (i, heads + h, 0, 0)),
                  pl.BlockSpec(dec.shape, lambda i, h: (0, 0)),
                  pl.BlockSpec((1, dh), lambda i, h: (0, h))],
        out_specs=pl.BlockSpec((None, l, dh), lambda i, h: (i, 0, h)),
        out_shape=jax.ShapeDtypeStruct((b, l, heads * dh), BF16),
        scratch_shapes=[pltpu.VMEM((nc, dh, dh), F32), pltpu.VMEM((nc, dh, dh), F32),
                        pltpu.VMEM((nc, 2 * dh, dh), BF16)],
        compiler_params=_cparams(("arbitrary", "arbitrary")),
        name="retention",
    )(qkvg, qkvg, qkvg, qkvg, s0, s0, dec, gn_gain)


def _filt_time_kernel(emb_ref, w1_ref, b1_ref, w2_ref, b2_ref, w3_ref, b3_ref, w4f_ref, w4b_ref,
                      fr_ref, dl_ref, cast_ref, kern_ref, s_ref, cast_out_ref, *, seq_len):
    cast_out_ref[...] = cast_ref[...].astype(cast_out_ref.dtype)
    i = pl.program_id(0)
    rows, c = kern_ref.shape[1], kern_ref.shape[2]
    fr = fr_ref[...]
    hdot = _dot3
    z = jnp.sin(fr * (hdot(emb_ref[...], w1_ref[...]) + b1_ref[...]))
    z = jnp.sin(fr * (hdot(z, w2_ref[...]) + b2_ref[...]))
    z = jnp.sin(fr * (hdot(z, w3_ref[...]) + b3_ref[...]))
    pos = (i * rows + lax.broadcasted_iota(jnp.int32, (rows, c), 0)).astype(F32)
    inv = 1.0 / (seq_len - 1.0)
    adl = dl_ref[...]
    hf = hdot(z, w4f_ref[...]) * jnp.exp(-(pos * inv) * adl)
    hr = hdot(z, w4b_ref[...]) * jnp.exp(-((seq_len - pos) * inv) * adl)
    hr = jnp.where(pos == 0.0, 0.0, hr)
    kern_ref[0] = hf
    kern_ref[1] = hr
    part = jnp.sum(jnp.abs(hf) + jnp.abs(hr), axis=0, keepdims=True)

    @pl.when(i == 0)
    def _():
        s_ref[...] = part

    @pl.when(i != 0)
    def _():
        s_ref[...] += part


def _filt_time_call(emb2, w1, b1, w2, b2, w3, b3, w4f, w4b, freq, absdelta, seq_len, cast_w):
    c = absdelta.shape[1]
    rows = FILT_ROWS
    steps = seq_len // rows
    assert cast_w.shape[0] % (steps * 2 * SUBLANES) == 0
    slab = pl.BlockSpec((cast_w.shape[0] // steps, cast_w.shape[1]), lambda i: (i, 0))
    small = lambda a: pl.BlockSpec(a.shape, lambda i: (0,) * a.ndim)
    return pl.pallas_call(
        functools.partial(_filt_time_kernel, seq_len=seq_len),
        grid=(steps,),
        in_specs=[pl.BlockSpec((rows, emb2.shape[1]), lambda i: (i, 0)),
                  small(w1), small(b1), small(w2), small(b2), small(w3), small(b3), small(w4f), small(w4b),
                  small(freq), small(absdelta), slab],
        out_specs=[pl.BlockSpec((2, rows, c), lambda i: (0, i, 0)),
                   pl.BlockSpec((1, c), lambda i: (0, 0)), slab],
        out_shape=[jax.ShapeDtypeStruct((2, seq_len, c), F32),
                   jax.ShapeDtypeStruct((1, c), F32),
                   jax.ShapeDtypeStruct(cast_w.shape, BF16)],
        compiler_params=_cparams(("arbitrary",)),
        name="filt_time",
    )(emb2, w1, b1, w2, b2, w3, b3, w4f, w4b, freq, absdelta, cast_w)


@functools.lru_cache(maxsize=None)
def _fft_tables(seq_len):
    n = 2 * seq_len
    n1s = FFT_N1
    n2s = n // n1s
    half = n1s // 2
    k1n = half + 1
    n1 = np.arange(half)
    k1 = np.arange(k1n)
    th = 2.0 * np.pi * (np.outer(k1, n1) % n1s) / n1s
    herm = np.where((k1 == 0) | (k1 == half), 1.0, 2.0)
    fa = np.concatenate([np.cos(th), -np.sin(th)[1:half]], axis=0)
    sgn = np.concatenate([(-1.0) ** k1, (-1.0) ** k1[1:half]])
    fai = np.concatenate([np.cos(th) * herm[:, None], (-np.sin(th) * herm[:, None])[1:half]], axis=0).T / n
    eye = np.eye(SUBLANES)
    fk = np.kron(fa, eye)
    fks = np.kron(fa * sgn[:, None], eye)
    fki = np.kron(fai, eye)
    k2 = np.arange(n2s)
    n2 = np.arange(n2s)
    m = np.zeros((k1n, 2 * n2s, 2 * n2s))
    for a in range(k1n):
        ang = 2.0 * np.pi * (np.outer(a + n1s * k2, n2) % n) / n
        gr, gi = np.cos(ang), -np.sin(ang)
        m[a] = np.block([[gr, -gi], [gi, gr]])
    mt = np.transpose(m, (0, 2, 1))
    return dict(fk=fk, fkf=np.concatenate([fk, fks], axis=1), fki=fki, m=m, mt=mt,
                k1n=k1n, n2s=n2s, half=half)


def _stage_n1(src_refs, mat, dst_ref, n2s):
    rows = dst_ref.shape[0]
    for j in range(n2s // SUBLANES):
        sl = slice(j * SUBLANES, (j + 1) * SUBLANES)
        parts = [r[:, sl, :] for r in src_refs]
        xg = jnp.concatenate([p.reshape(p.shape[0] * SUBLANES, p.shape[2]) for p in parts], axis=0)
        a = jnp.dot(mat, xg.astype(BF16), preferred_element_type=F32)
        dst_ref[:, sl, :] = a.reshape(rows, SUBLANES, a.shape[1])


def _n2_input(a_s, k1, half, n2s):
    if 0 < k1 < half:
        return jnp.concatenate([a_s[k1], a_s[half + k1]], axis=0)
    return a_s[k1]


def _filt_spec_kernel(kern_ref, s_ref, fkf_ref, m_ref, kf_ref, a_s, *, k1n, n2s):
    half = k1n - 1
    _stage_n1([kern_ref.at[0], kern_ref.at[1]], fkf_ref[...], a_s, n2s)
    inv = 1.0 / (s_ref[...] + EPS)
    for k1 in range(k1n):
        a = _n2_input(a_s, k1, half, n2s).astype(BF16)
        kf_ref[k1] = jnp.dot(m_ref[k1, :, :a.shape[0]], a, preferred_element_type=F32) * inv


def _filt_spec_call(kern4, s, tabs):
    _, half, n2s, c = kern4.shape
    ct = HY_CT
    k1n = tabs["k1n"]
    return pl.pallas_call(
        functools.partial(_filt_spec_kernel, k1n=k1n, n2s=n2s),
        grid=(c // ct,),
        in_specs=[pl.BlockSpec((2, half, n2s, ct), lambda j: (0, 0, 0, j)),
                  pl.BlockSpec((1, ct), lambda j: (0, j)),
                  _const_spec(tabs["fkf"].shape), _const_spec(tabs["m"].shape)],
        out_specs=pl.BlockSpec((k1n, 2 * n2s, ct), lambda j: (0, 0, j)),
        out_shape=jax.ShapeDtypeStruct((k1n, 2 * n2s, c), F32),
        scratch_shapes=[pltpu.VMEM((2 * half, n2s, ct), F32)],
        compiler_params=_cparams(("arbitrary",)),
        name="filt_spec",
    )(kern4, s, tabs["fkf"], tabs["m"])


def _hyena_kernel(u_ref, x0_ref, bias_ref, kf_ref, fk_ref, fki_ref, m_ref, mt_ref, *rest, k1n, n2s, n_cast):
    cast_in, (o_ref, *cast_out), (u_s, a_s) = rest[:n_cast], rest[n_cast:2 * n_cast + 1], rest[2 * n_cast + 1:]
    for src, dst in zip(cast_in, cast_out):
        dst[...] = src[...].astype(dst.dtype)
    half = k1n - 1
    ct = u_ref.shape[2]
    u_s[...] = u_ref[...].astype(F32)
    _stage_n1([u_s], fk_ref[...], a_s, n2s)

    for k1 in range(k1n):
        a = _n2_input(a_s, k1, half, n2s).astype(BF16)
        x = jnp.dot(m_ref[k1, :, :a.shape[0]], a, preferred_element_type=F32)
        xr, xi = x[:n2s], x[n2s:]
        kr, ki = kf_ref[k1, :n2s, :], kf_ref[k1, n2s:, :]
        y = jnp.concatenate([xr * kr - xi * ki, xr * ki + xi * kr], axis=0).astype(BF16)
        if 0 < k1 < half:
            z = jnp.dot(mt_ref[k1], y, preferred_element_type=F32)
            a_s[k1] = z[:n2s]
            a_s[half + k1] = z[n2s:]
        else:
            a_s[k1] = jnp.dot(mt_ref[k1, :n2s, :], y, preferred_element_type=F32)

    fki = fki_ref[...]
    bias = bias_ref[...]
    pair = 2 * SUBLANES
    for j in range(n2s // pair):
        ys = []
        for jj in range(2):
            sl = slice(j * pair + jj * SUBLANES, j * pair + (jj + 1) * SUBLANES)
            zg = a_s[:, sl, :].reshape(2 * half * SUBLANES, ct)
            ys.append(jnp.dot(fki, zg.astype(BF16), preferred_element_type=F32).reshape(half, SUBLANES, ct))
        y = jnp.concatenate(ys, axis=1)
        sl = slice(j * pair, (j + 1) * pair)
        o_ref[:, sl, :] = (x0_ref[:, sl, :].astype(F32) * (y + u_s[:, sl, :] * bias)).astype(o_ref.dtype)


def _hyena_call(u4, x04, bias, kf, tabs, cast_weights):
    b, half, n2s, c = u4.shape
    ct = HY_CT
    k1n = tabs["k1n"]
    nct = c // ct
    steps = nct * b
    seq = pl.BlockSpec((None, half, n2s, ct), lambda j, i: (i, 0, 0, j))
    slabs = [pl.BlockSpec((w.shape[0] // steps, w.shape[1]), lambda j, i: (j * b + i, 0)) for w in cast_weights]
    assert all(w.shape[0] % (steps * 2 * SUBLANES) == 0 for w in cast_weights)
    outs = pl.pallas_call(
        functools.partial(_hyena_kernel, k1n=k1n, n2s=n2s, n_cast=len(cast_weights)),
        grid=(nct, b),
        in_specs=[seq, seq,
                  pl.BlockSpec((1, ct), lambda j, i: (0, j)),
                  pl.BlockSpec((k1n, 2 * n2s, ct), lambda j, i: (0, 0, j), pipeline_mode=pl.Buffered(1)),
                  _const_spec(tabs["fk"].shape), _const_spec(tabs["fki"].shape),
                  _const_spec(tabs["m"].shape), _const_spec(tabs["mt"].shape), *slabs],
        out_specs=[seq, *slabs],
        out_shape=[jax.ShapeDtypeStruct((b, half, n2s, c), BF16),
                   *[jax.ShapeDtypeStruct(w.shape, BF16) for w in cast_weights]],
        scratch_shapes=[pltpu.VMEM((half, n2s, ct), F32), pltpu.VMEM((2 * half, n2s, ct), F32)],
        compiler_params=_cparams(("arbitrary", "arbitrary")),
        name="hyena",
    )(u4, x04, bias, kf, tabs["fk"], tabs["fki"], tabs["m"], tabs["mt"], *cast_weights)
    return outs[0], outs[1:]


def _out_ffn_kernel(ret_ref, hy_ref, x_ref, mod_ref, hg_ref, n2_ref, fn_ref, wo_ref, wgu_ref, wd_ref,
                    o_ref, *, d_ret, d_ff, ff_bounds, row_parts):
    rows = x_ref.shape[0] // row_parts
    groups = [slice(r * rows, (r + 1) * rows) for r in range(row_parts)]
    gain2 = n2_ref[...] * (1.0 + mod_ref[4:5, :])
    x1s, h2s = [], []
    for rs in groups:
        hy = hy_ref[rs, :].astype(F32)
        hms = jnp.mean(hy * hy, axis=-1, keepdims=True)
        hyn = (hy * lax.rsqrt(hms + EPS) * hg_ref[...]).astype(BF16)
        mix = jnp.dot(ret_ref[rs, :], wo_ref[:d_ret, :], preferred_element_type=F32)
        mix = mix + jnp.dot(hyn, wo_ref[d_ret:, :], preferred_element_type=F32)
        x1s.append(x_ref[rs, :] + mod_ref[2:3, :] * mix)
    for x1 in x1s:
        ms = jnp.mean(x1 * x1, axis=-1, keepdims=True)
        h2s.append((x1 * lax.rsqrt(ms + EPS) * gain2 + mod_ref[3:4, :]).astype(BF16))
    accs = [None] * row_parts
    for lo, hi in zip(ff_bounds[:-1], ff_bounds[1:]):
        acts = []
        for h2 in h2s:
            g = jnp.dot(h2, wgu_ref[:, lo:hi], preferred_element_type=F32)
            u = jnp.dot(h2, wgu_ref[:, d_ff + lo:d_ff + hi], preferred_element_type=F32)
            acts.append((_silu(g) * u).astype(BF16))
        for r, a in enumerate(acts):
            d = jnp.dot(a, wd_ref[lo:hi, :], preferred_element_type=F32)
            accs[r] = d if accs[r] is None else accs[r] + d
    for rs, x1, acc in zip(groups, x1s, accs):
        x2 = x1 + mod_ref[5:6, :] * acc
        ms2 = jnp.mean(x2 * x2, axis=-1, keepdims=True)
        o_ref[rs, :] = x2 * lax.rsqrt(ms2 + EPS) * fn_ref[...]


def _ff_bounds(d_ff, parts=2):
    tiles = d_ff // MXU_WIDTH
    assert tiles * MXU_WIDTH == d_ff
    cuts = [-(-tiles * p // parts) for p in range(parts + 1)]
    return tuple(c * MXU_WIDTH for c in cuts)


def _out_ffn_call(ret, hy, x, mod, hy_gain, norm2, final_norm, w_out, w_gu, w_down):
    b, l, d = x.shape
    d_ret = ret.shape[2]
    d_hy = hy.shape[2]
    d_ff = w_down.shape[0]
    tm = ROW_TILE
    return pl.pallas_call(
        functools.partial(_out_ffn_kernel, d_ret=d_ret, d_ff=d_ff, ff_bounds=_ff_bounds(d_ff),
                          row_parts=tm // GROUP_ROWS),
        grid=(b, l // tm),
        in_specs=[pl.BlockSpec((None, tm, d_ret), lambda i, t: (i, t, 0)),
                  pl.BlockSpec((None, tm, d_hy), lambda i, t: (i, t, 0)),
                  pl.BlockSpec((None, tm, d), lambda i, t: (i, t, 0)),
                  pl.BlockSpec((None, N_MOD, d), lambda i, t: (i, 0, 0)),
                  pl.BlockSpec((1, d_hy), lambda i, t: (0, 0)),
                  pl.BlockSpec((1, d), lambda i, t: (0, 0)),
                  pl.BlockSpec((1, d), lambda i, t: (0, 0)),
                  _const_spec(w_out.shape), _const_spec(w_gu.shape), _const_spec(w_down.shape)],
        out_specs=pl.BlockSpec((None, tm, d), lambda i, t: (i, t, 0)),
        out_shape=jax.ShapeDtypeStruct((b, l, d), x.dtype),
        compiler_params=_cparams(("arbitrary", "arbitrary")),
        name="out_ffn",
    )(ret, hy, x, mod, hy_gain, norm2, final_norm, w_out, w_gu, w_down)


@functools.lru_cache(maxsize=None)
def _rope_tables(seq_len, dh):
    n = dh // 4
    t = np.arange(seq_len)
    inv = ROPE_BASE ** (-np.arange(n, dtype=np.float64) / n)
    ang = np.concatenate([(t // GRID_W)[:, None] * inv, (t % GRID_W)[:, None] * inv], axis=-1)
    cc = np.concatenate([np.cos(ang), np.cos(ang)], axis=-1)
    ss = np.concatenate([-np.sin(ang), np.sin(ang)], axis=-1)
    k_scale = dh ** -0.5
    return tuple(t.astype(np.float32) for t in (cc, ss, cc * k_scale, ss * k_scale))


@functools.lru_cache(maxsize=None)
def _filter_tables(seq_len, emb_dim, emb_pad, channels):
    t = np.linspace(0.0, 1.0, seq_len)[:, None]
    bands = (emb_dim - 1) // 2
    f = np.linspace(1e-4, bands - 1, bands)[None, :]
    wpos = 2.0 * np.pi * np.arange(seq_len)[:, None] / seq_len
    emb = np.concatenate([t, np.cos(f * wpos), -np.sin(f * wpos)], axis=-1)
    emb = np.pad(emb, ((0, 0), (0, emb_pad - emb_dim)))
    emb_rev = np.concatenate([emb[:1], emb[:0:-1]], axis=0)
    emb2 = np.concatenate([emb, emb_rev], axis=1)
    max_decay = math.log(FILTER_DECAY_TARGET) / FILTER_DECAY_FAST
    min_decay = math.log(FILTER_DECAY_TARGET) / FILTER_DECAY_SLOW
    absdelta = np.abs(np.linspace(min_decay, max_decay, channels))[None, :]
    return emb2.astype(np.float32), absdelta.astype(np.float32)


def _block_diag2(w):
    z = jnp.zeros_like(w)
    return jnp.concatenate([jnp.concatenate([w, z], axis=1), jnp.concatenate([z, w], axis=1)], axis=0)


def kernel(x, c, ctx, c_ctx, w_mod, b_mod, norm1, norm2, w_in, ret_decay, ret_gn_gain, hy_short_w,
           hy_short_b, hy_w1, hy_b1, hy_w2, hy_b2, hy_w3, hy_b3, hy_w4, hy_freq, hy_bias, hy_out_norm,
           w_out, w_gate_up, w_down, final_norm):
    b, seq_len, d = x.shape
    assert w_mod.shape[0] == 1, "single-layer block"
    heads = RET_HEADS
    d_ret = ret_gn_gain.shape[1]
    dh = d_ret // heads
    d_hy = hy_bias.shape[1]
    assert dh == LANES and d_hy % HY_CT == 0 and seq_len % (FFT_N1 // 2) == 0

    rows = -(-(b + 1) // SUBLANES) * SUBLANES
    c_rows = jnp.zeros((rows, d), F32).at[:b].set(c).at[b].set(c_ctx)
    mod_all = _mod_call(c_rows, w_mod[0], b_mod[0][None, :])
    mod = mod_all[:b].reshape(b, N_MOD, d)
    mod_c = mod_all[b].reshape(N_MOD, d)

    tabs = dict(_fft_tables(seq_len))
    for name in ("fk", "fkf", "fki", "m", "mt"):
        tabs[name] = jnp.asarray(tabs[name], dtype=F32).astype(BF16)
    half, n2s = tabs["half"], tabs["n2s"]
    emb_dim = hy_w1.shape[1]
    emb_pad = -(-emb_dim // SUBLANES) * SUBLANES
    emb2, absdelta = (jnp.asarray(t) for t in _filter_tables(seq_len, emb_dim, emb_pad, d_hy))
    w1p = jnp.pad(hy_w1[0], ((0, emb_pad - emb_dim), (0, 0)))
    two = lambda a: jnp.concatenate([a, a], axis=1)
    w4 = hy_w4[0]
    zero4 = jnp.zeros((w4.shape[0], d_hy), F32)
    kern, ksum, w_in_b = _filt_time_call(
        emb2, _block_diag2(w1p), two(hy_b1), _block_diag2(hy_w2[0]), two(hy_b2), _block_diag2(hy_w3[0]),
        two(hy_b3), jnp.concatenate([w4[:, :d_hy], zero4], axis=0), jnp.concatenate([zero4, w4[:, d_hy:]], axis=0),
        two(hy_freq), absdelta, seq_len, w_in[0])
    kf = _filt_spec_call(kern.reshape(2, half, n2s, d_hy), ksum, tabs)

    dec = jnp.broadcast_to(ret_decay[0].reshape(2 * heads, 1), (2 * heads, LANES))
    n1g = norm1[0][None, :]
    s0 = _ctx_call(ctx, n1g, mod_c[0:1], mod_c[1:2], w_in_b, dec, heads, dh)
    rope = [jnp.asarray(t) for t in _rope_tables(seq_len, dh)]
    qkvg, x0c, u = _inproj_call(x, mod, n1g, w_in_b, rope, hy_short_w[0], hy_short_b, heads, dh, d_hy)
    ret = _ret_call(qkvg, s0, dec, ret_gn_gain, heads, dh)

    hy, (w_out_b, w_gu_b, w_down_b) = _hyena_call(
        u.reshape(b, half, n2s, d_hy), x0c.reshape(b, half, n2s, d_hy), hy_bias, kf, tabs,
        [w_out[0], w_gate_up[0], w_down[0]])
    hy = hy.reshape(b, seq_len, d_hy)

    return _out_ffn_call(ret, hy, x, mod, hy_out_norm, norm2[0][None, :], final_norm[None, :],
                         w_out_b, w_gu_b, w_down_b)
```

```python
import functools
import math

import jax
import jax.numpy as jnp
import numpy as np
from jax import lax
from jax.experimental import pallas as pl
from jax.experimental.pallas import tpu as pltpu

F32 = jnp.float32
BF16 = jnp.bfloat16

RET_HEADS = 4
GRID_W = 64
ROPE_BASE = 10000.0
N_MOD = 6
HYENA_PROJ = 3
FILTER_DECAY_FAST = 0.3
FILTER_DECAY_SLOW = 1.5
FILTER_DECAY_TARGET = 1e-2
EPS = 1e-6

LANES = 128
SUBLANES = 8
MXU_WIDTH = 256
VMEM_LIMIT_BYTES = 56 * 1024 * 1024

RET_CHUNK = 256
ROW_TILE = 1024
IN_ROW_TILE = 1024
GROUP_ROWS = 256
CTX_BATCHES = 4
HALO = 16
HY_CT = 256
FILT_ROWS = 1024
FFT_N1 = 64


def _silu(v):
    return v / (1.0 + jnp.exp(-v))


def _dot3(a, b):
    a_hi = a.astype(BF16)
    b_hi = b.astype(BF16)
    a_lo = (a - a_hi.astype(F32)).astype(BF16)
    b_lo = (b - b_hi.astype(F32)).astype(BF16)
    dot = functools.partial(jnp.dot, preferred_element_type=F32)
    return dot(a_hi, b_hi) + (dot(a_hi, b_lo) + dot(a_lo, b_hi))


def _cparams(sem, vmem=VMEM_LIMIT_BYTES):
    return pltpu.CompilerParams(dimension_semantics=sem, vmem_limit_bytes=vmem)


def _const_spec(shape):
    nd = len(shape)
    return pl.BlockSpec(shape, lambda *_: (0,) * nd, pipeline_mode=pl.Buffered(1))


def _mod_kernel(c_ref, w_ref, b_ref, o_ref):
    s = _silu(c_ref[...])
    o_ref[...] = _dot3(s, w_ref[...]) + b_ref[...]


def _mod_call(c_rows, w_mod, b_mod):
    rows, d = c_rows.shape
    n = w_mod.shape[1]
    tn = 1536
    return pl.pallas_call(
        _mod_kernel,
        grid=(n // tn,),
        in_specs=[pl.BlockSpec((rows, d), lambda j: (0, 0)),
                  pl.BlockSpec((d, tn), lambda j: (0, j)),
                  pl.BlockSpec((1, tn), lambda j: (0, j))],
        out_specs=pl.BlockSpec((rows, tn), lambda j: (0, j)),
        out_shape=jax.ShapeDtypeStruct((rows, n), F32),
        compiler_params=_cparams(("arbitrary",)),
        name="mod",
    )(c_rows, w_mod, b_mod)


def _ctx_kernel(ctx_ref, n1_ref, sh_ref, sc_ref, wk_ref, wv_ref, dec_ref, s_ref, *, heads, dh):
    nb, n_ctx, d = ctx_ref.shape
    xc = ctx_ref[...].reshape(nb * n_ctx, d)
    ms = jnp.mean(xc * xc, axis=-1, keepdims=True)
    hc = ((xc * lax.rsqrt(ms + EPS) * n1_ref[...]) * (1.0 + sc_ref[...]) + sh_ref[...]).astype(BF16)
    k = jnp.dot(hc, wk_ref[...], preferred_element_type=F32)
    v = jnp.dot(hc, wv_ref[...], preferred_element_type=F32)
    lg = jnp.log1p(-jnp.exp(dec_ref[...]))
    pos = lax.broadcasted_iota(jnp.int32, (n_ctx, dh), 0).astype(F32)
    k_scale = dh ** -0.5
    tdims = (((0,), (0,)), ((), ()))
    for h in range(heads):
        wf = jnp.exp(lg[h:h + 1, :] * (n_ctx - 1.0 - pos)) * k_scale
        wb = jnp.exp(lg[heads + h:heads + h + 1, :] * pos) * k_scale
        for i in range(nb):
            rows = slice(i * n_ctx, (i + 1) * n_ctx)
            kh = k[rows, h * dh:(h + 1) * dh]
            vh = v[rows, h * dh:(h + 1) * dh].astype(BF16)
            s_ref[i, h] = lax.dot_general((kh * wf).astype(BF16), vh, tdims, preferred_element_type=F32)
            s_ref[i, heads + h] = lax.dot_general((kh * wb).astype(BF16), vh, tdims,
                                                  preferred_element_type=F32)


def _ctx_call(ctx, norm1, shift_c, scale_c, w_in, dec, heads, dh):
    b, n_ctx, d = ctx.shape
    d_ret = heads * dh
    nb = CTX_BATCHES
    assert b % nb == 0
    return pl.pallas_call(
        functools.partial(_ctx_kernel, heads=heads, dh=dh),
        grid=(b // nb,),
        in_specs=[pl.BlockSpec((nb, n_ctx, d), lambda i: (i, 0, 0)),
                  pl.BlockSpec((1, d), lambda i: (0, 0)),
                  pl.BlockSpec((1, d), lambda i: (0, 0)),
                  pl.BlockSpec((1, d), lambda i: (0, 0)),
                  pl.BlockSpec((d, d_ret), lambda i: (0, 1)),
                  pl.BlockSpec((d, d_ret), lambda i: (0, 2)),
                  pl.BlockSpec(dec.shape, lambda i: (0, 0))],
        out_specs=pl.BlockSpec((nb, 2 * heads, dh, dh), lambda i: (i, 0, 0, 0)),
        out_shape=jax.ShapeDtypeStruct((b, 2 * heads, dh, dh), F32),
        compiler_params=_cparams(("arbitrary",)),
        name="ctx_state",
    )(ctx, norm1, shift_c, scale_c, w_in, w_in, dec)


def _inproj_kernel(x_ref, xp_ref, xn_ref, mod_ref, n1_ref, w_ref, cq_ref, sq_ref, ck_ref, sk_ref,
                   sw_ref, sb_ref, qkvg_ref, x0_ref, u_ref, *, heads, dh, d_hy, nt):
    t = pl.program_id(1)
    tm = x_ref.shape[0]
    d_ret = heads * dh
    gain = n1_ref[...] * (1.0 + mod_ref[1:2, :])

    def normed(x):
        ms = jnp.mean(x * x, axis=-1, keepdims=True)
        return x * lax.rsqrt(ms + EPS) * gain + mod_ref[0:1, :]

    groups = [slice(r, r + GROUP_ROWS) for r in range(0, tm, GROUP_ROWS)]
    hs = [normed(x_ref[rs, :]) for rs in groups]
    hbs = [h.astype(BF16) for h in hs]
    hy_lhs = list(hbs)
    hy_lhs[0] = jnp.concatenate([normed(xp_ref[...]), hs[0]], axis=0).astype(BF16)
    hy_lhs[-1] = jnp.concatenate([hs[-1], normed(xn_ref[...])], axis=0).astype(BF16)

    def proj(lhs, base, width):
        return [jnp.dot(hb, w_ref[:, base:base + width], preferred_element_type=F32) for hb in lhs]

    def hyena_cols(j):
        sl = slice(j * d_hy, (j + 1) * d_hy)
        parts = proj(hy_lhs, 4 * d_ret + j * d_hy, d_hy)
        head, tail = parts[0], parts[-1]
        parts[0] = jnp.concatenate([jnp.where(t == 0, 0.0, head[:HALO]), head[HALO:]], axis=0)
        parts[-1] = jnp.concatenate([tail[:GROUP_ROWS], jnp.where(t == nt - 1, 0.0, tail[GROUP_ROWS:])],
                                    axis=0)
        p = jnp.concatenate(parts, axis=0)
        n = p.shape[0]
        y = (sw_ref[0:1, sl] * pltpu.roll(p, 1, 0) + sw_ref[1:2, sl] * p
             + sw_ref[2:3, sl] * pltpu.roll(p, n - 1, 0) + sb_ref[:, sl])
        return y[HALO:HALO + tm]

    def roped_cols(base, cos_ref, sin_ref):
        for rs, p in zip(groups, proj(hbs, base, d_ret)):
            cc = cos_ref[rs, :]
            ss = sin_ref[rs, :]
            for j in range(heads):
                pj = p[:, j * dh:(j + 1) * dh]
                qkvg_ref[rs, base + j * dh:base + (j + 1) * dh] = (
                    pj * cc + pltpu.roll(pj, dh // 2, 1) * ss).astype(BF16)

    roped_cols(0, cq_ref, sq_ref)
    x0_ref[...] = hyena_cols(0).astype(x0_ref.dtype)
    roped_cols(d_ret, ck_ref, sk_ref)
    x1c = hyena_cols(1)
    for rs, g in zip(groups, proj(hbs, 3 * d_ret, d_ret)):
        qkvg_ref[rs, 3 * d_ret:4 * d_ret] = _silu(g).astype(BF16)
    u_ref[...] = (x1c * hyena_cols(2)).astype(u_ref.dtype)
    for rs, v in zip(groups, proj(hbs, 2 * d_ret, d_ret)):
        qkvg_ref[rs, 2 * d_ret:3 * d_ret] = v.astype(BF16)


def _inproj_call(x, mod, norm1, w_in, rope, short_w, short_b, heads, dh, d_hy):
    b, l, d = x.shape
    tm = IN_ROW_TILE
    nt = l // tm
    d_ret = heads * dh
    per_tile = tm // HALO
    tile = lambda i, t: (i, t, 0)
    return pl.pallas_call(
        functools.partial(_inproj_kernel, heads=heads, dh=dh, d_hy=d_hy, nt=nt),
        grid=(b, nt),
        in_specs=[pl.BlockSpec((None, tm, d), tile),
                  pl.BlockSpec((None, HALO, d), lambda i, t: (i, jnp.maximum(t * per_tile - 1, 0), 0)),
                  pl.BlockSpec((None, HALO, d),
                               lambda i, t: (i, jnp.minimum((t + 1) * per_tile, nt * per_tile - 1), 0)),
                  pl.BlockSpec((None, N_MOD, d), lambda i, t: (i, 0, 0)),
                  pl.BlockSpec((1, d), lambda i, t: (0, 0)),
                  _const_spec(w_in.shape),
                  *[pl.BlockSpec((tm, dh), lambda i, t: (t, 0)) for _ in rope],
                  pl.BlockSpec(short_w.shape, lambda i, t: (0, 0)),
                  pl.BlockSpec(short_b.shape, lambda i, t: (0, 0))],
        out_specs=[pl.BlockSpec((None, tm, 4 * d_ret), tile),
                   pl.BlockSpec((None, tm, d_hy), tile),
                   pl.BlockSpec((None, tm, d_hy), tile)],
        out_shape=[jax.ShapeDtypeStruct((b, l, 4 * d_ret), BF16),
                   jax.ShapeDtypeStruct((b, l, d_hy), BF16),
                   jax.ShapeDtypeStruct((b, l, d_hy), BF16)],
        compiler_params=_cparams(("arbitrary", "arbitrary")),
        name="in_proj",
    )(x, x, x, mod, norm1, w_in, *rope, short_w, short_b)


def _ret_kernel(q_ref, k_ref, v_ref, g_ref, sf0_ref, sb0_ref, dec_ref, gain_ref,
                o_ref, kvf_s, kvb_s, st_s, *, heads, chunk):
    l, dh = q_ref.shape
    nc = l // chunk
    h = pl.program_id(1)
    lgf = jnp.log1p(-jnp.exp(dec_ref[pl.ds(h, 1), :]))
    lgb = jnp.log1p(-jnp.exp(dec_ref[pl.ds(h + heads, 1), :]))
    il = lax.broadcasted_iota(jnp.int32, (chunk, dh), 0).astype(F32)
    kw_f = jnp.exp(lgf * (chunk - 1.0 - il))
    kw_b = jnp.exp(lgb * il)
    qw_f = jnp.exp(lgf * (il + 1.0))
    qw_b = jnp.exp(lgb * (chunk - il))
    cd_f = jnp.exp(lgf * float(chunk))
    cd_b = jnp.exp(lgb * float(chunk))
    reps = chunk // dh
    lgf_c = jnp.concatenate([lgf] * reps, axis=1)
    lgb_c = jnp.concatenate([lgb] * reps, axis=1)
    ii = lax.broadcasted_iota(jnp.int32, (chunk, chunk), 0)
    jj = lax.broadcasted_iota(jnp.int32, (chunk, chunk), 1)
    diff = (ii - jj).astype(F32)
    dmask = (jnp.where(diff >= 0, jnp.exp(lgf_c * jnp.maximum(diff, 0.0)), 0.0)
             + jnp.where(diff <= 0, jnp.exp(lgb_c * jnp.maximum(-diff, 0.0)), 0.0))
    tdims = (((0,), (0,)), ((), ()))
    ntdims = (((1,), (1,)), ((), ()))

    def chunk_states(n, carry):
        r0 = pl.multiple_of(n * chunk, chunk)
        kr = k_ref[pl.ds(r0, chunk), :].astype(F32)
        vv = v_ref[pl.ds(r0, chunk), :]
        kvf_s[n] = lax.dot_general((kr * kw_f).astype(BF16), vv, tdims, preferred_element_type=F32)
        kvb_s[n] = lax.dot_general((kr * kw_b).astype(BF16), vv, tdims, preferred_element_type=F32)
        return carry

    lax.fori_loop(0, nc, chunk_states, 0, unroll=True)

    def scan_f(n, s):
        st_s[n, :dh, :] = s.astype(BF16)
        return s * cd_f + kvf_s[n]

    lax.fori_loop(0, nc, scan_f, sf0_ref[...])

    def scan_b(m, s):
        n = nc - 1 - m
        st_s[n, dh:, :] = s.astype(BF16)
        return s * cd_b + kvb_s[n]

    lax.fori_loop(0, nc, scan_b, sb0_ref[...])

    gain = gain_ref[...]

    def chunk_out(n, carry):
        r0 = pl.multiple_of(n * chunk, chunk)
        qb = q_ref[pl.ds(r0, chunk), :]
        qr = qb.astype(F32)
        sc = lax.dot_general(qb, k_ref[pl.ds(r0, chunk), :], ntdims, preferred_element_type=F32)
        lhs = jnp.concatenate([(qr * qw_f).astype(BF16), (qr * qw_b).astype(BF16),
                               (sc * dmask).astype(BF16)], axis=1)
        rhs = jnp.concatenate([st_s[n], v_ref[pl.ds(r0, chunk), :]], axis=0)
        o = jnp.dot(lhs, rhs, preferred_element_type=F32)
        mu = jnp.mean(o, axis=-1, keepdims=True)
        d = o - mu
        var = jnp.mean(d * d, axis=-1, keepdims=True)
        gg = g_ref[pl.ds(r0, chunk), :].astype(F32)
        o_ref[pl.ds(r0, chunk), :] = (d * lax.rsqrt(var + EPS) * gain * gg).astype(o_ref.dtype)
        return carry

    lax.fori_loop(0, nc, chunk_out, 0, unroll=True)


def _ret_call(qkvg, s0, dec, gn_gain, heads, dh):
    b, l, _ = qkvg.shape
    chunk = RET_CHUNK
    nc = l // chunk
    seq = lambda off: pl.BlockSpec((None, l, dh), lambda i, h: (i, 0, off + h))
    return pl.pallas_call(
        functools.partial(_ret_kernel, heads=heads, chunk=chunk),
        grid=(b, heads),
        in_specs=[seq(0), seq(heads), seq(2 * heads), seq(3 * heads),
                  pl.BlockSpec((None, None, dh, dh), lambda i, h: (i, h, 0, 0)),
                  pl.BlockSpec((None, None, dh, dh), lambda i, h: (i, heads + h, 0, 0)),
                  pl.BlockSpec(dec.shape, lambda i, h: (0, 0)),
                  pl.BlockSpec((1, dh), lambda i, h: (0, h))],
        out_specs=pl.BlockSpec((None, l, dh), lambda i, h: (i, 0, h)),
        out_shape=jax.ShapeDtypeStruct((b, l, heads * dh), BF16),
        scratch_shapes=[pltpu.VMEM((nc, dh, dh), F32), pltpu.VMEM((nc, dh, dh), F32),
                        pltpu.VMEM((nc, 2 * dh, dh), BF16)],
        compiler_params=_cparams(("arbitrary", "arbitrary")),
        name="retention",
    )(qkvg, qkvg, qkvg, qkvg, s0, s0, dec, gn_gain)


def _filt_time_kernel(emb_ref, w1_ref, b1_ref, w2_ref, b2_ref, w3_ref, b3_ref, w4f_ref, w4b_ref,
                      fr_ref, dl_ref, cast_ref, kern_ref, s_ref, cast_out_ref, *, seq_len):
    cast_out_ref[...] = cast_ref[...].astype(cast_out_ref.dtype)
    i = pl.program_id(0)
    rows, c = kern_ref.shape[1], kern_ref.shape[2]
    fr = fr_ref[...]
    hdot = _dot3
    z = jnp.sin(fr * (hdot(emb_ref[...], w1_ref[...]) + b1_ref[...]))
    z = jnp.sin(fr * (hdot(z, w2_ref[...]) + b2_ref[...]))
    z = jnp.sin(fr * (hdot(z, w3_ref[...]) + b3_ref[...]))
    pos = (i * rows + lax.broadcasted_iota(jnp.int32, (rows, c), 0)).astype(F32)
    inv = 1.0 / (seq_len - 1.0)
    adl = dl_ref[...]
    hf = hdot(z, w4f_ref[...]) * jnp.exp(-(pos * inv) * adl)
    hr = hdot(z, w4b_ref[...]) * jnp.exp(-((seq_len - pos) * inv) * adl)
    hr = jnp.where(pos == 0.0, 0.0, hr)
    kern_ref[0] = hf
    kern_ref[1] = hr
    part = jnp.sum(jnp.abs(hf) + jnp.abs(hr), axis=0, keepdims=True)

    @pl.when(i == 0)
    def _():
        s_ref[...] = part

    @pl.when(i != 0)
    def _():
        s_ref[...] += part


def _filt_time_call(emb2, w1, b1, w2, b2, w3, b3, w4f, w4b, freq, absdelta, seq_len, cast_w):
    c = absdelta.shape[1]
    rows = FILT_ROWS
    steps = seq_len // rows
    assert cast_w.shape[0] % (steps * 2 * SUBLANES) == 0
    slab = pl.BlockSpec((cast_w.shape[0] // steps, cast_w.shape[1]), lambda i: (i, 0))
    small = lambda a: pl.BlockSpec(a.shape, lambda i: (0,) * a.ndim)
    return pl.pallas_call(
        functools.partial(_filt_time_kernel, seq_len=seq_len),
        grid=(steps,),
        in_specs=[pl.BlockSpec((rows, emb2.shape[1]), lambda i: (i, 0)),
                  small(w1), small(b1), small(w2), small(b2), small(w3), small(b3), small(w4f), small(w4b),
                  small(freq), small(absdelta), slab],
        out_specs=[pl.BlockSpec((2, rows, c), lambda i: (0, i, 0)),
                   pl.BlockSpec((1, c), lambda i: (0, 0)), slab],
        out_shape=[jax.ShapeDtypeStruct((2, seq_len, c), F32),
                   jax.ShapeDtypeStruct((1, c), F32),
                   jax.ShapeDtypeStruct(cast_w.shape, BF16)],
        compiler_params=_cparams(("arbitrary",)),
        name="filt_time",
    )(emb2, w1, b1, w2, b2, w3, b3, w4f, w4b, freq, absdelta, cast_w)


@functools.lru_cache(maxsize=None)
def _fft_tables(seq_len):
    n = 2 * seq_len
    n1s = FFT_N1
    n2s = n // n1s
    half = n1s // 2
    k1n = half + 1
    n1 = np.arange(half)
    k1 = np.arange(k1n)
    th = 2.0 * np.pi * (np.outer(k1, n1) % n1s) / n1s
    herm = np.where((k1 == 0) | (k1 == half), 1.0, 2.0)
    fa = np.concatenate([np.cos(th), -np.sin(th)[1:half]], axis=0)
    sgn = np.concatenate([(-1.0) ** k1, (-1.0) ** k1[1:half]])
    fai = np.concatenate([np.cos(th) * herm[:, None], (-np.sin(th) * herm[:, None])[1:half]], axis=0).T / n
    eye = np.eye(SUBLANES)
    fk = np.kron(fa, eye)
    fks = np.kron(fa * sgn[:, None], eye)
    fki = np.kron(fai, eye)
    k2 = np.arange(n2s)
    n2 = np.arange(n2s)
    m = np.zeros((k1n, 2 * n2s, 2 * n2s))
    for a in range(k1n):
        ang = 2.0 * np.pi * (np.outer(a + n1s * k2, n2) % n) / n
        gr, gi = np.cos(ang), -np.sin(ang)
        m[a] = np.block([[gr, -gi], [gi, gr]])
    groups = np.arange(n2s).reshape(-1, SUBLANES)
    perm = np.concatenate([np.concatenate([g, n2s + g]) for g in groups])
    m = m[:, perm, :]
    mt = np.transpose(m, (0, 2, 1))
    return dict(fk=fk, fkf=np.concatenate([fk, fks], axis=1), fki=fki, m=m, mt=mt,
                k1n=k1n, n2s=n2s, half=half)


def _stage_n1(src_refs, mat, dst_ref, n2s):
    rows = dst_ref.shape[0]
    for j in range(n2s // SUBLANES):
        sl = slice(j * SUBLANES, (j + 1) * SUBLANES)
        parts = [r[:, sl, :] for r in src_refs]
        xg = jnp.concatenate([p.reshape(p.shape[0] * SUBLANES, p.shape[2]) for p in parts], axis=0)
        a = jnp.dot(mat, xg.astype(BF16), preferred_element_type=F32)
        dst_ref[:, sl, :] = a.reshape(rows, SUBLANES, a.shape[1])


def _n2_input(a_s, k1, half, n2s):
    if 0 < k1 < half:
        return jnp.concatenate([a_s[k1], a_s[half + k1]], axis=0)
    return a_s[k1]


def _filt_spec_kernel(kern_ref, s_ref, fkf_ref, m_ref, kf_ref, a_s, *, k1n, n2s):
    half = k1n - 1
    _stage_n1([kern_ref.at[0], kern_ref.at[1]], fkf_ref[...], a_s, n2s)
    inv = 1.0 / (s_ref[...] + EPS)
    for k1 in range(k1n):
        a = _n2_input(a_s, k1, half, n2s).astype(BF16)
        kf_ref[k1] = jnp.dot(m_ref[k1, :, :a.shape[0]], a, preferred_element_type=F32) * inv


def _filt_spec_call(kern4, s, tabs):
    _, half, n2s, c = kern4.shape
    ct = HY_CT
    k1n = tabs["k1n"]
    return pl.pallas_call(
        functools.partial(_filt_spec_kernel, k1n=k1n, n2s=n2s),
        grid=(c // ct,),
        in_specs=[pl.BlockSpec((2, half, n2s, ct), lambda j: (0, 0, 0, j)),
                  pl.BlockSpec((1, ct), lambda j: (0, j)),
                  _const_spec(tabs["fkf"].shape), _const_spec(tabs["m"].shape)],
        out_specs=pl.BlockSpec((k1n, 2 * n2s, ct), lambda j: (0, 0, j)),
        out_shape=jax.ShapeDtypeStruct((k1n, 2 * n2s, c), F32),
        scratch_shapes=[pltpu.VMEM((2 * half, n2s, ct), F32)],
        compiler_params=_cparams(("arbitrary",)),
        name="filt_spec",
    )(kern4, s, tabs["fkf"], tabs["m"])


def _hyena_kernel(u_ref, x0_ref, bias_ref, kf_ref, fk_ref, fki_ref, m_ref, mt_ref, *rest, k1n, n2s, n_cast):
    cast_in, (o_ref, *cast_out), (u_s, a_s) = rest[:n_cast], rest[n_cast:2 * n_cast + 1], rest[2 * n_cast + 1:]
    for src, dst in zip(cast_in, cast_out):
        dst[...] = src[...].astype(dst.dtype)
    half = k1n - 1
    ct = u_ref.shape[2]
    u_s[...] = u_ref[...].astype(F32)
    _stage_n1([u_s], fk_ref[...], a_s, n2s)

    for k1 in range(k1n):
        a = _n2_input(a_s, k1, half, n2s).astype(BF16)
        x = jnp.dot(m_ref[k1, :, :a.shape[0]], a, preferred_element_type=F32)
        x = x.reshape(n2s // SUBLANES, 2 * SUBLANES, ct)
        kf = kf_ref[k1].reshape(n2s // SUBLANES, 2 * SUBLANES, ct)
        xr, xi = x[:, :SUBLANES], x[:, SUBLANES:]
        kr, ki = kf[:, :SUBLANES], kf[:, SUBLANES:]
        y = jnp.concatenate([xr * kr - xi * ki, xr * ki + xi * kr], axis=1)
        y = y.reshape(2 * n2s, ct).astype(BF16)
        if 0 < k1 < half:
            z = jnp.dot(mt_ref[k1], y, preferred_element_type=F32)
            a_s[k1] = z[:n2s]
            a_s[half + k1] = z[n2s:]
        else:
            a_s[k1] = jnp.dot(mt_ref[k1, :n2s, :], y, preferred_element_type=F32)

    fki = fki_ref[...]
    bias = bias_ref[...]
    pair = 2 * SUBLANES
    for j in range(n2s // pair):
        ys = []
        for jj in range(2):
            sl = slice(j * pair + jj * SUBLANES, j * pair + (jj + 1) * SUBLANES)
            zg = a_s[:, sl, :].reshape(2 * half * SUBLANES, ct)
            ys.append(jnp.dot(fki, zg.astype(BF16), preferred_element_type=F32).reshape(half, SUBLANES, ct))
        y = jnp.concatenate(ys, axis=1)
        sl = slice(j * pair, (j + 1) * pair)
        o_ref[:, sl, :] = (x0_ref[:, sl, :].astype(F32) * (y + u_s[:, sl, :] * bias)).astype(o_ref.dtype)


def _hyena_call(u4, x04, bias, kf, tabs, cast_weights):
    b, half, n2s, c = u4.shape
    ct = HY_CT
    k1n = tabs["k1n"]
    nct = c // ct
    steps = nct * b
    seq = pl.BlockSpec((None, half, n2s, ct), lambda j, i: (i, 0, 0, j))
    slabs = [pl.BlockSpec((w.shape[0] // steps, w.shape[1]), lambda j, i: (j * b + i, 0)) for w in cast_weights]
    assert all(w.shape[0] % (steps * 2 * SUBLANES) == 0 for w in cast_weights)
    outs = pl.pallas_call(
        functools.partial(_hyena_kernel, k1n=k1n, n2s=n2s, n_cast=len(cast_weights)),
        grid=(nct, b),
        in_specs=[seq, seq,
                  pl.BlockSpec((1, ct), lambda j, i: (0, j)),
                  pl.BlockSpec((k1n, 2 * n2s, ct), lambda j, i: (0, 0, j), pipeline_mode=pl.Buffered(1)),
                  _const_spec(tabs["fk"].shape), _const_spec(tabs["fki"].shape),
                  _const_spec(tabs["m"].shape), _const_spec(tabs["mt"].shape), *slabs],
        out_specs=[seq, *slabs],
        out_shape=[jax.ShapeDtypeStruct((b, half, n2s, c), BF16),
                   *[jax.ShapeDtypeStruct(w.shape, BF16) for w in cast_weights]],
        scratch_shapes=[pltpu.VMEM((half, n2s, ct), F32), pltpu.VMEM((2 * half, n2s, ct), F32)],
        compiler_params=_cparams(("arbitrary", "arbitrary")),
        name="hyena",
    )(u4, x04, bias, kf, tabs["fk"], tabs["fki"], tabs["m"], tabs["mt"], *cast_weights)
    return outs[0], outs[1:]


def _out_ffn_kernel(ret_ref, hy_ref, x_ref, mod_ref, hg_ref, n2_ref, fn_ref, wo_ref, wgu_ref, wd_ref,
                    o_ref, *, d_ret, d_ff, ff_bounds, row_parts):
    rows = x_ref.shape[0] // row_parts
    groups = [slice(r * rows, (r + 1) * rows) for r in range(row_parts)]
    gain2 = n2_ref[...] * (1.0 + mod_ref[4:5, :])
    x1s, h2s = [], []
    for rs in groups:
        hy = hy_ref[rs, :].astype(F32)
        hms = jnp.mean(hy * hy, axis=-1, keepdims=True)
        hyn = (hy * lax.rsqrt(hms + EPS) * hg_ref[...]).astype(BF16)
        mix = jnp.dot(ret_ref[rs, :], wo_ref[:d_ret, :], preferred_element_type=F32)
        mix = mix + jnp.dot(hyn, wo_ref[d_ret:, :], preferred_element_type=F32)
        x1s.append(x_ref[rs, :] + mod_ref[2:3, :] * mix)
    for x1 in x1s:
        ms = jnp.mean(x1 * x1, axis=-1, keepdims=True)
        h2s.append((x1 * lax.rsqrt(ms + EPS) * gain2 + mod_ref[3:4, :]).astype(BF16))
    accs = [None] * row_parts
    for lo, hi in zip(ff_bounds[:-1], ff_bounds[1:]):
        acts = []
        for h2 in h2s:
            g = jnp.dot(h2, wgu_ref[:, lo:hi], preferred_element_type=F32)
            u = jnp.dot(h2, wgu_ref[:, d_ff + lo:d_ff + hi], preferred_element_type=F32)
            acts.append((_silu(g) * u).astype(BF16))
        for r, a in enumerate(acts):
            d = jnp.dot(a, wd_ref[lo:hi, :], preferred_element_type=F32)
            accs[r] = d if accs[r] is None else accs[r] + d
    for rs, x1, acc in zip(groups, x1s, accs):
        x2 = x1 + mod_ref[5:6, :] * acc
        ms2 = jnp.mean(x2 * x2, axis=-1, keepdims=True)
        o_ref[rs, :] = x2 * lax.rsqrt(ms2 + EPS) * fn_ref[...]


def _ff_bounds(d_ff, parts=2):
    tiles = d_ff // MXU_WIDTH
    assert tiles * MXU_WIDTH == d_ff
    cuts = [-(-tiles * p // parts) for p in range(parts + 1)]
    return tuple(c * MXU_WIDTH for c in cuts)


def _out_ffn_call(ret, hy, x, mod, hy_gain, norm2, final_norm, w_out, w_gu, w_down):
    b, l, d = x.shape
    d_ret = ret.shape[2]
    d_hy = hy.shape[2]
    d_ff = w_down.shape[0]
    tm = ROW_TILE
    return pl.pallas_call(
        functools.partial(_out_ffn_kernel, d_ret=d_ret, d_ff=d_ff, ff_bounds=_ff_bounds(d_ff),
                          row_parts=tm // GROUP_ROWS),
        grid=(b, l // tm),
        in_specs=[pl.BlockSpec((None, tm, d_ret), lambda i, t: (i, t, 0)),
                  pl.BlockSpec((None, tm, d_hy), lambda i, t: (i, t, 0)),
                  pl.BlockSpec((None, tm, d), lambda i, t: (i, t, 0)),
                  pl.BlockSpec((None, N_MOD, d), lambda i, t: (i, 0, 0)),
                  pl.BlockSpec((1, d_hy), lambda i, t: (0, 0)),
                  pl.BlockSpec((1, d), lambda i, t: (0, 0)),
                  pl.BlockSpec((1, d), lambda i, t: (0, 0)),
                  _const_spec(w_out.shape), _const_spec(w_gu.shape), _const_spec(w_down.shape)],
        out_specs=pl.BlockSpec((None, tm, d), lambda i, t: (i, t, 0)),
        out_shape=jax.ShapeDtypeStruct((b, l, d), x.dtype),
        compiler_params=_cparams(("arbitrary", "arbitrary")),
        name="out_ffn",
    )(ret, hy, x, mod, hy_gain, norm2, final_norm, w_out, w_gu, w_down)


@functools.lru_cache(maxsize=None)
def _rope_tables(seq_len, dh):
    n = dh // 4
    t = np.arange(seq_len)
    inv = ROPE_BASE ** (-np.arange(n, dtype=np.float64) / n)
    ang = np.concatenate([(t // GRID_W)[:, None] * inv, (t % GRID_W)[:, None] * inv], axis=-1)
    cc = np.concatenate([np.cos(ang), np.cos(ang)], axis=-1)
    ss = np.concatenate([-np.sin(ang), np.sin(ang)], axis=-1)
    k_scale = dh ** -0.5
    return tuple(t.astype(np.float32) for t in (cc, ss, cc * k_scale, ss * k_scale))


@functools.lru_cache(maxsize=None)
def _filter_tables(seq_len, emb_dim, emb_pad, channels):
    t = np.linspace(0.0, 1.0, seq_len)[:, None]
    bands = (emb_dim - 1) // 2
    f = np.linspace(1e-4, bands - 1, bands)[None, :]
    wpos = 2.0 * np.pi * np.arange(seq_len)[:, None] / seq_len
    emb = np.concatenate([t, np.cos(f * wpos), -np.sin(f * wpos)], axis=-1)
    emb = np.pad(emb, ((0, 0), (0, emb_pad - emb_dim)))
    emb_rev = np.concatenate([emb[:1], emb[:0:-1]], axis=0)
    emb2 = np.concatenate([emb, emb_rev], axis=1)
    max_decay = math.log(FILTER_DECAY_TARGET) / FILTER_DECAY_FAST
    min_decay = math.log(FILTER_DECAY_TARGET) / FILTER_DECAY_SLOW
    absdelta = np.abs(np.linspace(min_decay, max_decay, channels))[None, :]
    return emb2.astype(np.float32), absdelta.astype(np.float32)


def _block_diag2(w):
    z = jnp.zeros_like(w)
    return jnp.concatenate([jnp.concatenate([w, z], axis=1), jnp.concatenate([z, w], axis=1)], axis=0)


def kernel(x, c, ctx, c_ctx, w_mod, b_mod, norm1, norm2, w_in, ret_decay, ret_gn_gain, hy_short_w,
           hy_short_b, hy_w1, hy_b1, hy_w2, hy_b2, hy_w3, hy_b3, hy_w4, hy_freq, hy_bias, hy_out_norm,
           w_out, w_gate_up, w_down, final_norm):
    b, seq_len, d = x.shape
    assert w_mod.shape[0] == 1, "single-layer block"
    heads = RET_HEADS
    d_ret = ret_gn_gain.shape[1]
    dh = d_ret // heads
    d_hy = hy_bias.shape[1]
    assert dh == LANES and d_hy % HY_CT == 0 and seq_len % (FFT_N1 // 2) == 0

    rows = -(-(b + 1) // SUBLANES) * SUBLANES
    c_rows = jnp.zeros((rows, d), F32).at[:b].set(c).at[b].set(c_ctx)
    mod_all = _mod_call(c_rows, w_mod[0], b_mod[0][None, :])
    mod = mod_all[:b].reshape(b, N_MOD, d)
    mod_c = mod_all[b].reshape(N_MOD, d)

    tabs = dict(_fft_tables(seq_len))
    for name in ("fk", "fkf", "fki", "m", "mt"):
        tabs[name] = jnp.asarray(tabs[name], dtype=F32).astype(BF16)
    half, n2s = tabs["half"], tabs["n2s"]
    emb_dim = hy_w1.shape[1]
    emb_pad = -(-emb_dim // SUBLANES) * SUBLANES
    emb2, absdelta = (jnp.asarray(t) for t in _filter_tables(seq_len, emb_dim, emb_pad, d_hy))
    w1p = jnp.pad(hy_w1[0], ((0, emb_pad - emb_dim), (0, 0)))
    two = lambda a: jnp.concatenate([a, a], axis=1)
    w4 = hy_w4[0]
    zero4 = jnp.zeros((w4.shape[0], d_hy), F32)
    kern, ksum, w_in_b = _filt_time_call(
        emb2, _block_diag2(w1p), two(hy_b1), _block_diag2(hy_w2[0]), two(hy_b2), _block_diag2(hy_w3[0]),
        two(hy_b3), jnp.concatenate([w4[:, :d_hy], zero4], axis=0), jnp.concatenate([zero4, w4[:, d_hy:]], axis=0),
        two(hy_freq), absdelta, seq_len, w_in[0])
    kf = _filt_spec_call(kern.reshape(2, half, n2s, d_hy), ksum, tabs)

    dec = jnp.broadcast_to(ret_decay[0].reshape(2 * heads, 1), (2 * heads, LANES))
    n1g = norm1[0][None, :]
    s0 = _ctx_call(ctx, n1g, mod_c[0:1], mod_c[1:2], w_in_b, dec, heads, dh)
    rope = [jnp.asarray(t) for t in _rope_tables(seq_len, dh)]
    qkvg, x0c, u = _inproj_call(x, mod, n1g, w_in_b, rope, hy_short_w[0], hy_short_b, heads, dh, d_hy)
    ret = _ret_call(qkvg, s0, dec, ret_gn_gain, heads, dh)

    hy, (w_out_b, w_gu_b, w_down_b) = _hyena_call(
        u.reshape(b, half, n2s, d_hy), x0c.reshape(b, half, n2s, d_hy), hy_bias, kf, tabs,
        [w_out[0], w_gate_up[0], w_down[0]])
    hy = hy.reshape(b, seq_len, d_hy)

    return _out_ffn_call(ret, hy, x, mod, hy_out_norm, norm2[0][None, :], final_norm[None, :],
                         w_out_b, w_gu_b, w_down_b)
```

```python
import functools
import math

import jax
import jax.numpy as jnp
import numpy as np
from jax import lax
from jax.experimental import pallas as pl
from jax.experimental.pallas import tpu as pltpu

F32 = jnp.float32
BF16 = jnp.bfloat16

RET_HEADS = 4
GRID_W = 64
ROPE_BASE = 10000.0
N_MOD = 6
HYENA_PROJ = 3
FILTER_DECAY_FAST = 0.3
FILTER_DECAY_SLOW = 1.5
FILTER_DECAY_TARGET = 1e-2
EPS = 1e-6

LANES = 128
SUBLANES = 8
MXU_WIDTH = 256
VMEM_LIMIT_BYTES = 56 * 1024 * 1024

RET_CHUNK = 256
ROW_TILE = 1024
IN_ROW_TILE = 1024
GROUP_ROWS = 256
CTX_BATCHES = 4
HALO = 16
HY_CT = 256
FILT_ROWS = 1024
FFT_N1 = 64


def _silu(v):
    return v / (1.0 + jnp.exp(-v))


def _dot3(a, b):
    a_hi = a.astype(BF16)
    b_hi = b.astype(BF16)
    a_lo = (a - a_hi.astype(F32)).astype(BF16)
    b_lo = (b - b_hi.astype(F32)).astype(BF16)
    dot = functools.partial(jnp.dot, preferred_element_type=F32)
    return dot(a_hi, b_hi) + (dot(a_hi, b_lo) + dot(a_lo, b_hi))


def _cparams(sem, vmem=VMEM_LIMIT_BYTES):
    return pltpu.CompilerParams(dimension_semantics=sem, vmem_limit_bytes=vmem)


def _const_spec(shape):
    nd = len(shape)
    return pl.BlockSpec(shape, lambda *_: (0,) * nd, pipeline_mode=pl.Buffered(1))


def _mod_kernel(c_ref, w_ref, b_ref, o_ref):
    s = _silu(c_ref[...])
    o_ref[...] = _dot3(s, w_ref[...]) + b_ref[...]


def _mod_call(c_rows, w_mod, b_mod):
    rows, d = c_rows.shape
    n = w_mod.shape[1]
    tn = 1536
    return pl.pallas_call(
        _mod_kernel,
        grid=(n // tn,),
        in_specs=[pl.BlockSpec((rows, d), lambda j: (0, 0)),
                  pl.BlockSpec((d, tn), lambda j: (0, j)),
                  pl.BlockSpec((1, tn), lambda j: (0, j))],
        out_specs=pl.BlockSpec((rows, tn), lambda j: (0, j)),
        out_shape=jax.ShapeDtypeStruct((rows, n), F32),
        compiler_params=_cparams(("arbitrary",)),
        name="mod",
    )(c_rows, w_mod, b_mod)


def _ctx_kernel(ctx_ref, n1_ref, sh_ref, sc_ref, wk_ref, wv_ref, dec_ref, s_ref, *, heads, dh):
    nb, n_ctx, d = ctx_ref.shape
    xc = ctx_ref[...].reshape(nb * n_ctx, d)
    ms = jnp.mean(xc * xc, axis=-1, keepdims=True)
    hc = ((xc * lax.rsqrt(ms + EPS) * n1_ref[...]) * (1.0 + sc_ref[...]) + sh_ref[...]).astype(BF16)
    k = jnp.dot(hc, wk_ref[...], preferred_element_type=F32)
    v = jnp.dot(hc, wv_ref[...], preferred_element_type=F32)
    lg = jnp.log1p(-jnp.exp(dec_ref[...]))
    pos = lax.broadcasted_iota(jnp.int32, (n_ctx, dh), 0).astype(F32)
    k_scale = dh ** -0.5
    tdims = (((0,), (0,)), ((), ()))
    for h in range(heads):
        wf = jnp.exp(lg[h:h + 1, :] * (n_ctx - 1.0 - pos)) * k_scale
        wb = jnp.exp(lg[heads + h:heads + h + 1, :] * pos) * k_scale
        for i in range(nb):
            rows = slice(i * n_ctx, (i + 1) * n_ctx)
            kh = k[rows, h * dh:(h + 1) * dh]
            vh = v[rows, h * dh:(h + 1) * dh].astype(BF16)
            s_ref[i, h] = lax.dot_general((kh * wf).astype(BF16), vh, tdims, preferred_element_type=F32)
            s_ref[i, heads + h] = lax.dot_general((kh * wb).astype(BF16), vh, tdims,
                                                  preferred_element_type=F32)


def _ctx_call(ctx, norm1, shift_c, scale_c, w_in, dec, heads, dh):
    b, n_ctx, d = ctx.shape
    d_ret = heads * dh
    nb = CTX_BATCHES
    assert b % nb == 0
    return pl.pallas_call(
        functools.partial(_ctx_kernel, heads=heads, dh=dh),
        grid=(b // nb,),
        in_specs=[pl.BlockSpec((nb, n_ctx, d), lambda i: (i, 0, 0)),
                  pl.BlockSpec((1, d), lambda i: (0, 0)),
                  pl.BlockSpec((1, d), lambda i: (0, 0)),
                  pl.BlockSpec((1, d), lambda i: (0, 0)),
                  pl.BlockSpec((d, d_ret), lambda i: (0, 1)),
                  pl.BlockSpec((d, d_ret), lambda i: (0, 2)),
                  pl.BlockSpec(dec.shape, lambda i: (0, 0))],
        out_specs=pl.BlockSpec((nb, 2 * heads, dh, dh), lambda i: (i, 0, 0, 0)),
        out_shape=jax.ShapeDtypeStruct((b, 2 * heads, dh, dh), F32),
        compiler_params=_cparams(("arbitrary",)),
        name="ctx_state",
    )(ctx, norm1, shift_c, scale_c, w_in, w_in, dec)


def _inproj_kernel(x_ref, xp_ref, xn_ref, mod_ref, n1_ref, w_ref, cq_ref, sq_ref, ck_ref, sk_ref,
                   sw_ref, sb_ref, qkvg_ref, x0_ref, u_ref, *, heads, dh, d_hy, nt):
    t = pl.program_id(1)
    tm = x_ref.shape[0]
    d_ret = heads * dh
    gain = n1_ref[...] * (1.0 + mod_ref[1:2, :])

    def normed(x):
        ms = jnp.mean(x * x, axis=-1, keepdims=True)
        return x * lax.rsqrt(ms + EPS) * gain + mod_ref[0:1, :]

    groups = [slice(r, r + GROUP_ROWS) for r in range(0, tm, GROUP_ROWS)]
    hs = [normed(x_ref[rs, :]) for rs in groups]
    hbs = [h.astype(BF16) for h in hs]
    hy_lhs = list(hbs)
    hy_lhs[0] = jnp.concatenate([normed(xp_ref[...]), hs[0]], axis=0).astype(BF16)
    hy_lhs[-1] = jnp.concatenate([hs[-1], normed(xn_ref[...])], axis=0).astype(BF16)

    def proj(lhs, base, width):
        return [jnp.dot(hb, w_ref[:, base:base + width], preferred_element_type=F32) for hb in lhs]

    def hyena_cols(j):
        sl = slice(j * d_hy, (j + 1) * d_hy)
        parts = proj(hy_lhs, 4 * d_ret + j * d_hy, d_hy)
        head, tail = parts[0], parts[-1]
        parts[0] = jnp.concatenate([jnp.where(t == 0, 0.0, head[:HALO]), head[HALO:]], axis=0)
        parts[-1] = jnp.concatenate([tail[:GROUP_ROWS], jnp.where(t == nt - 1, 0.0, tail[GROUP_ROWS:])],
                                    axis=0)
        p = jnp.concatenate(parts, axis=0)
        n = p.shape[0]
        y = (sw_ref[0:1, sl] * pltpu.roll(p, 1, 0) + sw_ref[1:2, sl] * p
             + sw_ref[2:3, sl] * pltpu.roll(p, n - 1, 0) + sb_ref[:, sl])
        return y[HALO:HALO + tm]

    def roped_cols(base, cos_ref, sin_ref):
        for rs, p in zip(groups, proj(hbs, base, d_ret)):
            cc = cos_ref[rs, :]
            ss = sin_ref[rs, :]
            for j in range(heads):
                pj = p[:, j * dh:(j + 1) * dh]
                qkvg_ref[rs, base + j * dh:base + (j + 1) * dh] = (
                    pj * cc + pltpu.roll(pj, dh // 2, 1) * ss).astype(BF16)

    roped_cols(0, cq_ref, sq_ref)
    x0_ref[...] = hyena_cols(0).astype(x0_ref.dtype)
    roped_cols(d_ret, ck_ref, sk_ref)
    x1c = hyena_cols(1)
    for rs, g in zip(groups, proj(hbs, 3 * d_ret, d_ret)):
        qkvg_ref[rs, 3 * d_ret:4 * d_ret] = _silu(g).astype(BF16)
    u_ref[...] = (x1c * hyena_cols(2)).astype(u_ref.dtype)
    for rs, v in zip(groups, proj(hbs, 2 * d_ret, d_ret)):
        qkvg_ref[rs, 2 * d_ret:3 * d_ret] = v.astype(BF16)


def _inproj_call(x, mod, norm1, w_in, rope, short_w, short_b, heads, dh, d_hy):
    b, l, d = x.shape
    tm = IN_ROW_TILE
    nt = l // tm
    d_ret = heads * dh
    per_tile = tm // HALO
    tile = lambda i, t: (i, t, 0)
    return pl.pallas_call(
        functools.partial(_inproj_kernel, heads=heads, dh=dh, d_hy=d_hy, nt=nt),
        grid=(b, nt),
        in_specs=[pl.BlockSpec((None, tm, d), tile),
                  pl.BlockSpec((None, HALO, d), lambda i, t: (i, jnp.maximum(t * per_tile - 1, 0), 0)),
                  pl.BlockSpec((None, HALO, d),
                               lambda i, t: (i, jnp.minimum((t + 1) * per_tile, nt * per_tile - 1), 0)),
                  pl.BlockSpec((None, N_MOD, d), lambda i, t: (i, 0, 0)),
                  pl.BlockSpec((1, d), lambda i, t: (0, 0)),
                  _const_spec(w_in.shape),
                  *[pl.BlockSpec((tm, dh), lambda i, t: (t, 0)) for _ in rope],
                  pl.BlockSpec(short_w.shape, lambda i, t: (0, 0)),
                  pl.BlockSpec(short_b.shape, lambda i, t: (0, 0))],
        out_specs=[pl.BlockSpec((None, tm, 4 * d_ret), tile),
                   pl.BlockSpec((None, tm, d_hy), tile),
                   pl.BlockSpec((None, tm, d_hy), tile)],
        out_shape=[jax.ShapeDtypeStruct((b, l, 4 * d_ret), BF16),
                   jax.ShapeDtypeStruct((b, l, d_hy), BF16),
                   jax.ShapeDtypeStruct((b, l, d_hy), BF16)],
        compiler_params=_cparams(("arbitrary", "arbitrary")),
        name="in_proj",
    )(x, x, x, mod, norm1, w_in, *rope, short_w, short_b)


def _ret_kernel(q_ref, k_ref, v_ref, g_ref, sf0_ref, sb0_ref, dec_ref, gain_ref,
                o_ref, tab_s, dmask_s, *, heads, chunk):
    l, dh = q_ref.shape
    nc = l // chunk
    h = pl.program_id(0)

    @pl.when(pl.program_id(1) == 0)
    def _():
        lgf = jnp.log1p(-jnp.exp(dec_ref[pl.ds(h, 1), :]))
        lgb = jnp.log1p(-jnp.exp(dec_ref[pl.ds(h + heads, 1), :]))
        il = lax.broadcasted_iota(jnp.int32, (chunk, dh), 0).astype(F32)
        tab_s[0] = jnp.exp(lgf * (chunk - 1.0 - il))
        tab_s[1] = jnp.exp(lgb * il)
        tab_s[2] = jnp.exp(lgf * (il + 1.0))
        tab_s[3] = jnp.exp(lgb * (chunk - il))
        tab_s[4] = jnp.exp(jnp.broadcast_to(lgf, (chunk, dh)) * float(chunk))
        tab_s[5] = jnp.exp(jnp.broadcast_to(lgb, (chunk, dh)) * float(chunk))
        reps = chunk // dh
        lgf_c = jnp.concatenate([lgf] * reps, axis=1)
        lgb_c = jnp.concatenate([lgb] * reps, axis=1)
        ii = lax.broadcasted_iota(jnp.int32, (chunk, chunk), 0)
        jj = lax.broadcasted_iota(jnp.int32, (chunk, chunk), 1)
        diff = (ii - jj).astype(F32)
        dmask_s[...] = (jnp.where(diff >= 0, jnp.exp(lgf_c * jnp.maximum(diff, 0.0)), 0.0)
                        + jnp.where(diff <= 0, jnp.exp(lgb_c * jnp.maximum(-diff, 0.0)), 0.0))

    tdims = (((0,), (0,)), ((), ()))
    ntdims = (((1,), (1,)), ((), ()))
    rows = lambda n: pl.ds(n * chunk, chunk)

    kvf, kvb = [], []
    for n in range(nc):
        kr = k_ref[rows(n), :].astype(F32)
        vv = v_ref[rows(n), :]
        kvf.append(lax.dot_general((kr * tab_s[0]).astype(BF16), vv, tdims, preferred_element_type=F32))
        kvb.append(lax.dot_general((kr * tab_s[1]).astype(BF16), vv, tdims, preferred_element_type=F32))

    cd_f = tab_s[4, :dh, :]
    cd_b = tab_s[5, :dh, :]
    sf, sb = [None] * nc, [None] * nc
    s = sf0_ref[...]
    for n in range(nc):
        sf[n] = s.astype(BF16)
        s = s * cd_f + kvf[n]
    s = sb0_ref[...]
    for n in reversed(range(nc)):
        sb[n] = s.astype(BF16)
        s = s * cd_b + kvb[n]

    gain = gain_ref[...]
    for n in range(nc):
        qb = q_ref[rows(n), :]
        qr = qb.astype(F32)
        sc = lax.dot_general(qb, k_ref[rows(n), :], ntdims, preferred_element_type=F32)
        lhs = jnp.concatenate([(qr * tab_s[2]).astype(BF16), (qr * tab_s[3]).astype(BF16),
                               (sc * dmask_s[...]).astype(BF16)], axis=1)
        rhs = jnp.concatenate([sf[n], sb[n], v_ref[rows(n), :]], axis=0)
        o = jnp.dot(lhs, rhs, preferred_element_type=F32)
        mu = jnp.mean(o, axis=-1, keepdims=True)
        d = o - mu
        var = jnp.mean(d * d, axis=-1, keepdims=True)
        gg = g_ref[rows(n), :].astype(F32)
        o_ref[rows(n), :] = (d * lax.rsqrt(var + EPS) * gain * gg).astype(o_ref.dtype)


def _ret_call(qkvg, s0, dec, gn_gain, heads, dh):
    b, l, _ = qkvg.shape
    chunk = RET_CHUNK
    seq = lambda off: pl.BlockSpec((None, l, dh), lambda h, i: (i, 0, off + h))
    return pl.pallas_call(
        functools.partial(_ret_kernel, heads=heads, chunk=chunk),
        grid=(heads, b),
        in_specs=[seq(0), seq(heads), seq(2 * heads), seq(3 * heads),
                  pl.BlockSpec((None, None, dh, dh), lambda h, i: (i, h, 0, 0)),
                  pl.BlockSpec((None, None, dh, dh), lambda h, i: (i, heads + h, 0, 0)),
                  pl.BlockSpec(dec.shape, lambda h, i: (0, 0)),
                  pl.BlockSpec((1, dh), lambda h, i: (0, h))],
        out_specs=pl.BlockSpec((None, l, dh), lambda h, i: (i, 0, h)),
        out_shape=jax.ShapeDtypeStruct((b, l, heads * dh), BF16),
        scratch_shapes=[pltpu.VMEM((6, chunk, dh), F32), pltpu.VMEM((chunk, chunk), F32)],
        compiler_params=_cparams(("arbitrary", "arbitrary")),
        name="retention",
    )(qkvg, qkvg, qkvg, qkvg, s0, s0, dec, gn_gain)


def _filt_time_kernel(emb_ref, w1_ref, b1_ref, w2_ref, b2_ref, w3_ref, b3_ref, w4f_ref, w4b_ref,
                      fr_ref, dl_ref, cast_ref, kern_ref, s_ref, cast_out_ref, *, seq_len):
    cast_out_ref[...] = cast_ref[...].astype(cast_out_ref.dtype)
    i = pl.program_id(0)
    rows, c = kern_ref.shape[1], kern_ref.shape[2]
    fr = fr_ref[...]
    hdot = _dot3
    z = jnp.sin(fr * (hdot(emb_ref[...], w1_ref[...]) + b1_ref[...]))
    z = jnp.sin(fr * (hdot(z, w2_ref[...]) + b2_ref[...]))
    z = jnp.sin(fr * (hdot(z, w3_ref[...]) + b3_ref[...]))
    pos = (i * rows + lax.broadcasted_iota(jnp.int32, (rows, c), 0)).astype(F32)
    inv = 1.0 / (seq_len - 1.0)
    adl = dl_ref[...]
    hf = hdot(z, w4f_ref[...]) * jnp.exp(-(pos * inv) * adl)
    hr = hdot(z, w4b_ref[...]) * jnp.exp(-((seq_len - pos) * inv) * adl)
    hr = jnp.where(pos == 0.0, 0.0, hr)
    kern_ref[0] = hf
    kern_ref[1] = hr
    part = jnp.sum(jnp.abs(hf) + jnp.abs(hr), axis=0, keepdims=True)

    @pl.when(i == 0)
    def _():
        s_ref[...] = part

    @pl.when(i != 0)
    def _():
        s_ref[...] += part


def _filt_time_call(emb2, w1, b1, w2, b2, w3, b3, w4f, w4b, freq, absdelta, seq_len, cast_w):
    c = absdelta.shape[1]
    rows = FILT_ROWS
    steps = seq_len // rows
    assert cast_w.shape[0] % (steps * 2 * SUBLANES) == 0
    slab = pl.BlockSpec((cast_w.shape[0] // steps, cast_w.shape[1]), lambda i: (i, 0))
    small = lambda a: pl.BlockSpec(a.shape, lambda i: (0,) * a.ndim)
    return pl.pallas_call(
        functools.partial(_filt_time_kernel, seq_len=seq_len),
        grid=(steps,),
        in_specs=[pl.BlockSpec((rows, emb2.shape[1]), lambda i: (i, 0)),
                  small(w1), small(b1), small(w2), small(b2), small(w3), small(b3), small(w4f), small(w4b),
                  small(freq), small(absdelta), slab],
        out_specs=[pl.BlockSpec((2, rows, c), lambda i: (0, i, 0)),
                   pl.BlockSpec((1, c), lambda i: (0, 0)), slab],
        out_shape=[jax.ShapeDtypeStruct((2, seq_len, c), F32),
                   jax.ShapeDtypeStruct((1, c), F32),
                   jax.ShapeDtypeStruct(cast_w.shape, BF16)],
        compiler_params=_cparams(("arbitrary",)),
        name="filt_time",
    )(emb2, w1, b1, w2, b2, w3, b3, w4f, w4b, freq, absdelta, cast_w)


@functools.lru_cache(maxsize=None)
def _fft_tables(seq_len):
    n = 2 * seq_len
    n1s = FFT_N1
    n2s = n // n1s
    half = n1s // 2
    k1n = half + 1
    n1 = np.arange(half)
    k1 = np.arange(k1n)
    th = 2.0 * np.pi * (np.outer(k1, n1) % n1s) / n1s
    herm = np.where((k1 == 0) | (k1 == half), 1.0, 2.0)
    fa = np.concatenate([np.cos(th), -np.sin(th)[1:half]], axis=0)
    sgn = np.concatenate([(-1.0) ** k1, (-1.0) ** k1[1:half]])
    fai = np.concatenate([np.cos(th) * herm[:, None], (-np.sin(th) * herm[:, None])[1:half]], axis=0).T / n
    eye = np.eye(SUBLANES)
    fk = np.kron(fa, eye)
    fks = np.kron(fa * sgn[:, None], eye)
    fki = np.kron(fai, eye)
    k2 = np.arange(n2s)
    n2 = np.arange(n2s)
    m = np.zeros((k1n, 2 * n2s, 2 * n2s))
    for a in range(k1n):
        ang = 2.0 * np.pi * (np.outer(a + n1s * k2, n2) % n) / n
        gr, gi = np.cos(ang), -np.sin(ang)
        m[a] = np.block([[gr, -gi], [gi, gr]])
    groups = np.arange(n2s).reshape(-1, SUBLANES)
    perm = np.concatenate([np.concatenate([g, n2s + g]) for g in groups])
    m = m[:, perm, :]
    mt = np.transpose(m, (0, 2, 1))
    return dict(fk=fk, fkf=np.concatenate([fk, fks], axis=1), fki=fki, m=m, mt=mt,
                k1n=k1n, n2s=n2s, half=half)


def _stage_n1(src_refs, mat, dst_ref, n2s):
    rows = dst_ref.shape[0]
    for j in range(n2s // SUBLANES):
        sl = slice(j * SUBLANES, (j + 1) * SUBLANES)
        parts = [r[:, sl, :] for r in src_refs]
        xg = jnp.concatenate([p.reshape(p.shape[0] * SUBLANES, p.shape[2]) for p in parts], axis=0)
        a = jnp.dot(mat, xg.astype(BF16), preferred_element_type=F32)
        dst_ref[:, sl, :] = a.reshape(rows, SUBLANES, a.shape[1])


def _n2_input(a_s, k1, half, n2s):
    if 0 < k1 < half:
        return jnp.concatenate([a_s[k1], a_s[half + k1]], axis=0)
    return a_s[k1]


def _filt_spec_kernel(kern_ref, s_ref, fkf_ref, m_ref, kf_ref, a_s, *, k1n, n2s):
    half = k1n - 1
    _stage_n1([kern_ref.at[0], kern_ref.at[1]], fkf_ref[...], a_s, n2s)
    inv = 1.0 / (s_ref[...] + EPS)
    for k1 in range(k1n):
        a = _n2_input(a_s, k1, half, n2s).astype(BF16)
        kf_ref[k1] = jnp.dot(m_ref[k1, :, :a.shape[0]], a, preferred_element_type=F32) * inv


def _filt_spec_call(kern4, s, tabs):
    _, half, n2s, c = kern4.shape
    ct = HY_CT
    k1n = tabs["k1n"]
    return pl.pallas_call(
        functools.partial(_filt_spec_kernel, k1n=k1n, n2s=n2s),
        grid=(c // ct,),
        in_specs=[pl.BlockSpec((2, half, n2s, ct), lambda j: (0, 0, 0, j)),
                  pl.BlockSpec((1, ct), lambda j: (0, j)),
                  _const_spec(tabs["fkf"].shape), _const_spec(tabs["m"].shape)],
        out_specs=pl.BlockSpec((k1n, 2 * n2s, ct), lambda j: (0, 0, j)),
        out_shape=jax.ShapeDtypeStruct((k1n, 2 * n2s, c), F32),
        scratch_shapes=[pltpu.VMEM((2 * half, n2s, ct), F32)],
        compiler_params=_cparams(("arbitrary",)),
        name="filt_spec",
    )(kern4, s, tabs["fkf"], tabs["m"])


def _hyena_kernel(u_ref, x0_ref, bias_ref, kf_ref, fk_ref, fki_ref, m_ref, mt_ref, *rest, k1n, n2s, n_cast):
    cast_in, (o_ref, *cast_out), (u_s, a_s) = rest[:n_cast], rest[n_cast:2 * n_cast + 1], rest[2 * n_cast + 1:]
    for src, dst in zip(cast_in, cast_out):
        dst[...] = src[...].astype(dst.dtype)
    half = k1n - 1
    ct = u_ref.shape[2]
    u_s[...] = u_ref[...].astype(F32)
    _stage_n1([u_s], fk_ref[...], a_s, n2s)

    for k1 in range(k1n):
        a = _n2_input(a_s, k1, half, n2s).astype(BF16)
        x = jnp.dot(m_ref[k1, :, :a.shape[0]], a, preferred_element_type=F32)
        x = x.reshape(n2s // SUBLANES, 2 * SUBLANES, ct)
        kf = kf_ref[k1].reshape(n2s // SUBLANES, 2 * SUBLANES, ct)
        xr, xi = x[:, :SUBLANES], x[:, SUBLANES:]
        kr, ki = kf[:, :SUBLANES], kf[:, SUBLANES:]
        y = jnp.concatenate([xr * kr - xi * ki, xr * ki + xi * kr], axis=1)
        y = y.reshape(2 * n2s, ct).astype(BF16)
        if 0 < k1 < half:
            z = jnp.dot(mt_ref[k1], y, preferred_element_type=F32)
            a_s[k1] = z[:n2s]
            a_s[half + k1] = z[n2s:]
        else:
            a_s[k1] = jnp.dot(mt_ref[k1, :n2s, :], y, preferred_element_type=F32)

    fki = fki_ref[...]
    bias = bias_ref[...]
    pair = 2 * SUBLANES
    for j in range(n2s // pair):
        ys = []
        for jj in range(2):
            sl = slice(j * pair + jj * SUBLANES, j * pair + (jj + 1) * SUBLANES)
            zg = a_s[:, sl, :].reshape(2 * half * SUBLANES, ct)
            ys.append(jnp.dot(fki, zg.astype(BF16), preferred_element_type=F32).reshape(half, SUBLANES, ct))
        y = jnp.concatenate(ys, axis=1)
        sl = slice(j * pair, (j + 1) * pair)
        o_ref[:, sl, :] = (x0_ref[:, sl, :].astype(F32) * (y + u_s[:, sl, :] * bias)).astype(o_ref.dtype)


def _hyena_call(u4, x04, bias, kf, tabs, cast_weights):
    b, half, n2s, c = u4.shape
    ct = HY_CT
    k1n = tabs["k1n"]
    nct = c // ct
    steps = nct * b
    seq = pl.BlockSpec((None, half, n2s, ct), lambda j, i: (i, 0, 0, j))
    slabs = [pl.BlockSpec((w.shape[0] // steps, w.shape[1]), lambda j, i: (j * b + i, 0)) for w in cast_weights]
    assert all(w.shape[0] % (steps * 2 * SUBLANES) == 0 for w in cast_weights)
    outs = pl.pallas_call(
        functools.partial(_hyena_kernel, k1n=k1n, n2s=n2s, n_cast=len(cast_weights)),
        grid=(nct, b),
        in_specs=[seq, seq,
                  pl.BlockSpec((1, ct), lambda j, i: (0, j)),
                  pl.BlockSpec((k1n, 2 * n2s, ct), lambda j, i: (0, 0, j), pipeline_mode=pl.Buffered(1)),
                  _const_spec(tabs["fk"].shape), _const_spec(tabs["fki"].shape),
                  _const_spec(tabs["m"].shape), _const_spec(tabs["mt"].shape), *slabs],
        out_specs=[seq, *slabs],
        out_shape=[jax.ShapeDtypeStruct((b, half, n2s, c), BF16),
                   *[jax.ShapeDtypeStruct(w.shape, BF16) for w in cast_weights]],
        scratch_shapes=[pltpu.VMEM((half, n2s, ct), F32), pltpu.VMEM((2 * half, n2s, ct), F32)],
        compiler_params=_cparams(("arbitrary", "arbitrary")),
        name="hyena",
    )(u4, x04, bias, kf, tabs["fk"], tabs["fki"], tabs["m"], tabs["mt"], *cast_weights)
    return outs[0], outs[1:]


def _out_ffn_kernel(ret_ref, hy_ref, x_ref, mod_ref, hg_ref, n2_ref, fn_ref, wo_ref, wgu_ref, wd_ref,
                    o_ref, *, d_ret, d_ff, ff_bounds, row_parts):
    rows = x_ref.shape[0] // row_parts
    groups = [slice(r * rows, (r + 1) * rows) for r in range(row_parts)]
    gain2 = n2_ref[...] * (1.0 + mod_ref[4:5, :])
    x1s, h2s = [], []
    for rs in groups:
        hy = hy_ref[rs, :].astype(F32)
        hms = jnp.mean(hy * hy, axis=-1, keepdims=True)
        hyn = (hy * lax.rsqrt(hms + EPS) * hg_ref[...]).astype(BF16)
        mix = jnp.dot(ret_ref[rs, :], wo_ref[:d_ret, :], preferred_element_type=F32)
        mix = mix + jnp.dot(hyn, wo_ref[d_ret:, :], preferred_element_type=F32)
        x1s.append(x_ref[rs, :] + mod_ref[2:3, :] * mix)
    for x1 in x1s:
        ms = jnp.mean(x1 * x1, axis=-1, keepdims=True)
        h2s.append((x1 * lax.rsqrt(ms + EPS) * gain2 + mod_ref[3:4, :]).astype(BF16))
    accs = [None] * row_parts
    for lo, hi in zip(ff_bounds[:-1], ff_bounds[1:]):
        acts = []
        for h2 in h2s:
            g = jnp.dot(h2, wgu_ref[:, lo:hi], preferred_element_type=F32)
            u = jnp.dot(h2, wgu_ref[:, d_ff + lo:d_ff + hi], preferred_element_type=F32)
            acts.append((_silu(g) * u).astype(BF16))
        for r, a in enumerate(acts):
            d = jnp.dot(a, wd_ref[lo:hi, :], preferred_element_type=F32)
            accs[r] = d if accs[r] is None else accs[r] + d
    for rs, x1, acc in zip(groups, x1s, accs):
        x2 = x1 + mod_ref[5:6, :] * acc
        ms2 = jnp.mean(x2 * x2, axis=-1, keepdims=True)
        o_ref[rs, :] = x2 * lax.rsqrt(ms2 + EPS) * fn_ref[...]


def _ff_bounds(d_ff, parts=2):
    tiles = d_ff // MXU_WIDTH
    assert tiles * MXU_WIDTH == d_ff
    cuts = [-(-tiles * p // parts) for p in range(parts + 1)]
    return tuple(c * MXU_WIDTH for c in cuts)


def _out_ffn_call(ret, hy, x, mod, hy_gain, norm2, final_norm, w_out, w_gu, w_down):
    b, l, d = x.shape
    d_ret = ret.shape[2]
    d_hy = hy.shape[2]
    d_ff = w_down.shape[0]
    tm = ROW_TILE
    return pl.pallas_call(
        functools.partial(_out_ffn_kernel, d_ret=d_ret, d_ff=d_ff, ff_bounds=_ff_bounds(d_ff),
                          row_parts=tm // GROUP_ROWS),
        grid=(b, l // tm),
        in_specs=[pl.BlockSpec((None, tm, d_ret), lambda i, t: (i, t, 0)),
                  pl.BlockSpec((None, tm, d_hy), lambda i, t: (i, t, 0)),
                  pl.BlockSpec((None, tm, d), lambda i, t: (i, t, 0)),
                  pl.BlockSpec((None, N_MOD, d), lambda i, t: (i, 0, 0)),
                  pl.BlockSpec((1, d_hy), lambda i, t: (0, 0)),
                  pl.BlockSpec((1, d), lambda i, t: (0, 0)),
                  pl.BlockSpec((1, d), lambda i, t: (0, 0)),
                  _const_spec(w_out.shape), _const_spec(w_gu.shape), _const_spec(w_down.shape)],
        out_specs=pl.BlockSpec((None, tm, d), lambda i, t: (i, t, 0)),
        out_shape=jax.ShapeDtypeStruct((b, l, d), x.dtype),
        compiler_params=_cparams(("arbitrary", "arbitrary")),
        name="out_ffn",
    )(ret, hy, x, mod, hy_gain, norm2, final_norm, w_out, w_gu, w_down)


@functools.lru_cache(maxsize=None)
def _rope_tables(seq_len, dh):
    n = dh // 4
    t = np.arange(seq_len)
    inv = ROPE_BASE ** (-np.arange(n, dtype=np.float64) / n)
    ang = np.concatenate([(t // GRID_W)[:, None] * inv, (t % GRID_W)[:, None] * inv], axis=-1)
    cc = np.concatenate([np.cos(ang), np.cos(ang)], axis=-1)
    ss = np.concatenate([-np.sin(ang), np.sin(ang)], axis=-1)
    k_scale = dh ** -0.5
    return tuple(t.astype(np.float32) for t in (cc, ss, cc * k_scale, ss * k_scale))


@functools.lru_cache(maxsize=None)
def _filter_tables(seq_len, emb_dim, emb_pad, channels):
    t = np.linspace(0.0, 1.0, seq_len)[:, None]
    bands = (emb_dim - 1) // 2
    f = np.linspace(1e-4, bands - 1, bands)[None, :]
    wpos = 2.0 * np.pi * np.arange(seq_len)[:, None] / seq_len
    emb = np.concatenate([t, np.cos(f * wpos), -np.sin(f * wpos)], axis=-1)
    emb = np.pad(emb, ((0, 0), (0, emb_pad - emb_dim)))
    emb_rev = np.concatenate([emb[:1], emb[:0:-1]], axis=0)
    emb2 = np.concatenate([emb, emb_rev], axis=1)
    max_decay = math.log(FILTER_DECAY_TARGET) / FILTER_DECAY_FAST
    min_decay = math.log(FILTER_DECAY_TARGET) / FILTER_DECAY_SLOW
    absdelta = np.abs(np.linspace(min_decay, max_decay, channels))[None, :]
    return emb2.astype(np.float32), absdelta.astype(np.float32)


def _block_diag2(w):
    z = jnp.zeros_like(w)
    return jnp.concatenate([jnp.concatenate([w, z], axis=1), jnp.concatenate([z, w], axis=1)], axis=0)


def kernel(x, c, ctx, c_ctx, w_mod, b_mod, norm1, norm2, w_in, ret_decay, ret_gn_gain, hy_short_w,
           hy_short_b, hy_w1, hy_b1, hy_w2, hy_b2, hy_w3, hy_b3, hy_w4, hy_freq, hy_bias, hy_out_norm,
           w_out, w_gate_up, w_down, final_norm):
    b, seq_len, d = x.shape
    assert w_mod.shape[0] == 1, "single-layer block"
    heads = RET_HEADS
    d_ret = ret_gn_gain.shape[1]
    dh = d_ret // heads
    d_hy = hy_bias.shape[1]
    assert dh == LANES and d_hy % HY_CT == 0 and seq_len % (FFT_N1 // 2) == 0

    rows = -(-(b + 1) // SUBLANES) * SUBLANES
    c_rows = jnp.zeros((rows, d), F32).at[:b].set(c).at[b].set(c_ctx)
    mod_all = _mod_call(c_rows, w_mod[0], b_mod[0][None, :])
    mod = mod_all[:b].reshape(b, N_MOD, d)
    mod_c = mod_all[b].reshape(N_MOD, d)

    tabs = dict(_fft_tables(seq_len))
    for name in ("fk", "fkf", "fki", "m", "mt"):
        tabs[name] = jnp.asarray(tabs[name], dtype=F32).astype(BF16)
    half, n2s = tabs["half"], tabs["n2s"]
    emb_dim = hy_w1.shape[1]
    emb_pad = -(-emb_dim // SUBLANES) * SUBLANES
    emb2, absdelta = (jnp.asarray(t) for t in _filter_tables(seq_len, emb_dim, emb_pad, d_hy))
    w1p = jnp.pad(hy_w1[0], ((0, emb_pad - emb_dim), (0, 0)))
    two = lambda a: jnp.concatenate([a, a], axis=1)
    w4 = hy_w4[0]
    zero4 = jnp.zeros((w4.shape[0], d_hy), F32)
    kern, ksum, w_in_b = _filt_time_call(
        emb2, _block_diag2(w1p), two(hy_b1), _block_diag2(hy_w2[0]), two(hy_b2), _block_diag2(hy_w3[0]),
        two(hy_b3), jnp.concatenate([w4[:, :d_hy], zero4], axis=0), jnp.concatenate([zero4, w4[:, d_hy:]], axis=0),
        two(hy_freq), absdelta, seq_len, w_in[0])
    kf = _filt_spec_call(kern.reshape(2, half, n2s, d_hy), ksum, tabs)

    dec = jnp.broadcast_to(ret_decay[0].reshape(2 * heads, 1), (2 * heads, LANES))
    n1g = norm1[0][None, :]
    s0 = _ctx_call(ctx, n1g, mod_c[0:1], mod_c[1:2], w_in_b, dec, heads, dh)
    rope = [jnp.asarray(t) for t in _rope_tables(seq_len, dh)]
    qkvg, x0c, u = _inproj_call(x, mod, n1g, w_in_b, rope, hy_short_w[0], hy_short_b, heads, dh, d_hy)
    ret = _ret_call(qkvg, s0, dec, ret_gn_gain, heads, dh)

    hy, (w_out_b, w_gu_b, w_down_b) = _hyena_call(
        u.reshape(b, half, n2s, d_hy), x0c.reshape(b, half, n2s, d_hy), hy_bias, kf, tabs,
        [w_out[0], w_gate_up[0], w_down[0]])
    hy = hy.reshape(b, seq_len, d_hy)

    return _out_ffn_call(ret, hy, x, mod, hy_out_norm, norm2[0][None, :], final_norm[None, :],
                         w_out_b, w_gu_b, w_down_b)
```

```python
import functools
import math

import jax
import jax.numpy as jnp
import numpy as np
from jax import lax
from jax.experimental import pallas as pl
from jax.experimental.pallas import tpu as pltpu

F32 = jnp.float32
BF16 = jnp.bfloat16

RET_HEADS = 4
GRID_W = 64
ROPE_BASE = 10000.0
N_MOD = 6
HYENA_PROJ = 3
FILTER_DECAY_FAST = 0.3
FILTER_DECAY_SLOW = 1.5
FILTER_DECAY_TARGET = 1e-2
EPS = 1e-6

LANES = 128
SUBLANES = 8
MXU_WIDTH = 256
VMEM_LIMIT_BYTES = 56 * 1024 * 1024

RET_CHUNK = 256
ROW_TILE = 1024
IN_ROW_TILE = 1024
GROUP_ROWS = 256
CTX_BATCHES = 4
HALO = 16
HY_CT = 256
FILT_ROWS = 1024
FFT_N1 = 64


def _silu(v):
    return v / (1.0 + jnp.exp(-v))


def _dot3(a, b):
    a_hi = a.astype(BF16)
    b_hi = b.astype(BF16)
    a_lo = (a - a_hi.astype(F32)).astype(BF16)
    b_lo = (b - b_hi.astype(F32)).astype(BF16)
    dot = functools.partial(jnp.dot, preferred_element_type=F32)
    return dot(a_hi, b_hi) + (dot(a_hi, b_lo) + dot(a_lo, b_hi))


def _cparams(sem, vmem=VMEM_LIMIT_BYTES):
    return pltpu.CompilerParams(dimension_semantics=sem, vmem_limit_bytes=vmem)


def _const_spec(shape):
    nd = len(shape)
    return pl.BlockSpec(shape, lambda *_: (0,) * nd, pipeline_mode=pl.Buffered(1))


def _ctx_kernel(ctx_ref, n1_ref, sh_ref, sc_ref, wk_ref, wv_ref, dec_ref, s_ref, *, heads, dh):
    nb, n_ctx, d = ctx_ref.shape
    xc = ctx_ref[...].reshape(nb * n_ctx, d)
    ms = jnp.mean(xc * xc, axis=-1, keepdims=True)
    hc = ((xc * lax.rsqrt(ms + EPS) * n1_ref[...]) * (1.0 + sc_ref[...]) + sh_ref[...]).astype(BF16)
    k = jnp.dot(hc, wk_ref[...], preferred_element_type=F32)
    v = jnp.dot(hc, wv_ref[...], preferred_element_type=F32)
    lg = jnp.log1p(-jnp.exp(dec_ref[...]))
    pos = lax.broadcasted_iota(jnp.int32, (n_ctx, dh), 0).astype(F32)
    k_scale = dh ** -0.5
    tdims = (((0,), (0,)), ((), ()))
    for h in range(heads):
        wf = jnp.exp(lg[h:h + 1, :] * (n_ctx - 1.0 - pos)) * k_scale
        wb = jnp.exp(lg[heads + h:heads + h + 1, :] * pos) * k_scale
        for i in range(nb):
            rows = slice(i * n_ctx, (i + 1) * n_ctx)
            kh = k[rows, h * dh:(h + 1) * dh]
            vh = v[rows, h * dh:(h + 1) * dh].astype(BF16)
            s_ref[i, h] = lax.dot_general((kh * wf).astype(BF16), vh, tdims, preferred_element_type=F32)
            s_ref[i, heads + h] = lax.dot_general((kh * wb).astype(BF16), vh, tdims,
                                                  preferred_element_type=F32)


def _ctx_call(ctx, norm1, shift_c, scale_c, w_in, dec, heads, dh):
    b, n_ctx, d = ctx.shape
    d_ret = heads * dh
    nb = CTX_BATCHES
    assert b % nb == 0
    return pl.pallas_call(
        functools.partial(_ctx_kernel, heads=heads, dh=dh),
        grid=(b // nb,),
        in_specs=[pl.BlockSpec((nb, n_ctx, d), lambda i: (i, 0, 0)),
                  pl.BlockSpec((1, d), lambda i: (0, 0)),
                  pl.BlockSpec((1, d), lambda i: (0, 0)),
                  pl.BlockSpec((1, d), lambda i: (0, 0)),
                  pl.BlockSpec((d, d_ret), lambda i: (0, 1)),
                  pl.BlockSpec((d, d_ret), lambda i: (0, 2)),
                  pl.BlockSpec(dec.shape, lambda i: (0, 0))],
        out_specs=pl.BlockSpec((nb, 2 * heads, dh, dh), lambda i: (i, 0, 0, 0)),
        out_shape=jax.ShapeDtypeStruct((b, 2 * heads, dh, dh), F32),
        compiler_params=_cparams(("arbitrary",)),
        name="ctx_state",
    )(ctx, norm1, shift_c, scale_c, w_in, w_in, dec)


def _inproj_kernel(x_ref, xp_ref, xn_ref, mod_ref, n1_ref, w_ref, cq_ref, sq_ref, ck_ref, sk_ref,
                   sw_ref, sb_ref, qkvg_ref, x0_ref, u_ref, *, heads, dh, d_hy, nt):
    t = pl.program_id(1)
    tm = x_ref.shape[0]
    d_ret = heads * dh
    gain = n1_ref[...] * (1.0 + mod_ref[1:2, :])

    def normed(x):
        ms = jnp.mean(x * x, axis=-1, keepdims=True)
        return x * lax.rsqrt(ms + EPS) * gain + mod_ref[0:1, :]

    groups = [slice(r, r + GROUP_ROWS) for r in range(0, tm, GROUP_ROWS)]
    hs = [normed(x_ref[rs, :]) for rs in groups]
    hbs = [h.astype(BF16) for h in hs]
    hy_lhs = list(hbs)
    hy_lhs[0] = jnp.concatenate([normed(xp_ref[...]), hs[0]], axis=0).astype(BF16)
    hy_lhs[-1] = jnp.concatenate([hs[-1], normed(xn_ref[...])], axis=0).astype(BF16)

    def proj(lhs, base, width):
        return [jnp.dot(hb, w_ref[:, base:base + width], preferred_element_type=F32) for hb in lhs]

    def hyena_cols(j):
        sl = slice(j * d_hy, (j + 1) * d_hy)
        parts = proj(hy_lhs, 4 * d_ret + j * d_hy, d_hy)
        head, tail = parts[0], parts[-1]
        parts[0] = jnp.concatenate([jnp.where(t == 0, 0.0, head[:HALO]), head[HALO:]], axis=0)
        parts[-1] = jnp.concatenate([tail[:GROUP_ROWS], jnp.where(t == nt - 1, 0.0, tail[GROUP_ROWS:])],
                                    axis=0)
        p = jnp.concatenate(parts, axis=0)
        n = p.shape[0]
        y = (sw_ref[0:1, sl] * pltpu.roll(p, 1, 0) + sw_ref[1:2, sl] * p
             + sw_ref[2:3, sl] * pltpu.roll(p, n - 1, 0) + sb_ref[:, sl])
        return y[HALO:HALO + tm]

    def roped_cols(base, cos_ref, sin_ref):
        for rs, p in zip(groups, proj(hbs, base, d_ret)):
            cc = cos_ref[rs, :]
            ss = sin_ref[rs, :]
            for j in range(heads):
                pj = p[:, j * dh:(j + 1) * dh]
                qkvg_ref[rs, base + j * dh:base + (j + 1) * dh] = (
                    pj * cc + pltpu.roll(pj, dh // 2, 1) * ss).astype(BF16)

    roped_cols(0, cq_ref, sq_ref)
    x0_ref[...] = hyena_cols(0).astype(x0_ref.dtype)
    roped_cols(d_ret, ck_ref, sk_ref)
    x1c = hyena_cols(1)
    for rs, g in zip(groups, proj(hbs, 3 * d_ret, d_ret)):
        qkvg_ref[rs, 3 * d_ret:4 * d_ret] = _silu(g).astype(BF16)
    u_ref[...] = (x1c * hyena_cols(2)).astype(u_ref.dtype)
    for rs, v in zip(groups, proj(hbs, 2 * d_ret, d_ret)):
        qkvg_ref[rs, 2 * d_ret:3 * d_ret] = v.astype(BF16)


def _inproj_call(x, mod, norm1, w_in, rope, short_w, short_b, heads, dh, d_hy):
    b, l, d = x.shape
    tm = IN_ROW_TILE
    nt = l // tm
    d_ret = heads * dh
    per_tile = tm // HALO
    tile = lambda i, t: (i, t, 0)
    return pl.pallas_call(
        functools.partial(_inproj_kernel, heads=heads, dh=dh, d_hy=d_hy, nt=nt),
        grid=(b, nt),
        in_specs=[pl.BlockSpec((None, tm, d), tile),
                  pl.BlockSpec((None, HALO, d), lambda i, t: (i, jnp.maximum(t * per_tile - 1, 0), 0)),
                  pl.BlockSpec((None, HALO, d),
                               lambda i, t: (i, jnp.minimum((t + 1) * per_tile, nt * per_tile - 1), 0)),
                  pl.BlockSpec((None, N_MOD, d), lambda i, t: (i, 0, 0)),
                  pl.BlockSpec((1, d), lambda i, t: (0, 0)),
                  _const_spec(w_in.shape),
                  *[pl.BlockSpec((tm, dh), lambda i, t: (t, 0)) for _ in rope],
                  pl.BlockSpec(short_w.shape, lambda i, t: (0, 0)),
                  pl.BlockSpec(short_b.shape, lambda i, t: (0, 0))],
        out_specs=[pl.BlockSpec((None, tm, 4 * d_ret), tile),
                   pl.BlockSpec((None, tm, d_hy), tile),
                   pl.BlockSpec((None, tm, d_hy), tile)],
        out_shape=[jax.ShapeDtypeStruct((b, l, 4 * d_ret), BF16),
                   jax.ShapeDtypeStruct((b, l, d_hy), BF16),
                   jax.ShapeDtypeStruct((b, l, d_hy), BF16)],
        compiler_params=_cparams(("arbitrary", "arbitrary")),
        name="in_proj",
    )(x, x, x, mod, norm1, w_in, *rope, short_w, short_b)


def _ret_kernel(q_ref, k_ref, v_ref, g_ref, sf0_ref, sb0_ref, dec_ref, gain_ref,
                o_ref, tab_s, dmask_s, *, heads, chunk):
    l, dh = q_ref.shape
    nc = l // chunk
    h = pl.program_id(0)

    @pl.when(pl.program_id(1) == 0)
    def _():
        lgf = jnp.log1p(-jnp.exp(dec_ref[pl.ds(h, 1), :]))
        lgb = jnp.log1p(-jnp.exp(dec_ref[pl.ds(h + heads, 1), :]))
        il = lax.broadcasted_iota(jnp.int32, (chunk, dh), 0).astype(F32)
        tab_s[0] = jnp.exp(lgf * (chunk - 1.0 - il))
        tab_s[1] = jnp.exp(lgb * il)
        tab_s[2] = jnp.exp(lgf * (il + 1.0))
        tab_s[3] = jnp.exp(lgb * (chunk - il))
        tab_s[4] = jnp.exp(jnp.broadcast_to(lgf, (chunk, dh)) * float(chunk))
        tab_s[5] = jnp.exp(jnp.broadcast_to(lgb, (chunk, dh)) * float(chunk))
        reps = chunk // dh
        lgf_c = jnp.concatenate([lgf] * reps, axis=1)
        lgb_c = jnp.concatenate([lgb] * reps, axis=1)
        ii = lax.broadcasted_iota(jnp.int32, (chunk, chunk), 0)
        jj = lax.broadcasted_iota(jnp.int32, (chunk, chunk), 1)
        diff = (ii - jj).astype(F32)
        dmask_s[...] = (jnp.where(diff >= 0, jnp.exp(lgf_c * jnp.maximum(diff, 0.0)), 0.0)
                        + jnp.where(diff <= 0, jnp.exp(lgb_c * jnp.maximum(-diff, 0.0)), 0.0))

    tdims = (((0,), (0,)), ((), ()))
    ntdims = (((1,), (1,)), ((), ()))
    rows = lambda n: pl.ds(n * chunk, chunk)

    kvf, kvb = [], []
    for n in range(nc):
        kr = k_ref[rows(n), :].astype(F32)
        vv = v_ref[rows(n), :]
        kvf.append(lax.dot_general((kr * tab_s[0]).astype(BF16), vv, tdims, preferred_element_type=F32))
        kvb.append(lax.dot_general((kr * tab_s[1]).astype(BF16), vv, tdims, preferred_element_type=F32))

    cd_f = tab_s[4, :dh, :]
    cd_b = tab_s[5, :dh, :]
    sf, sb = [None] * nc, [None] * nc
    s = sf0_ref[...]
    for n in range(nc):
        sf[n] = s.astype(BF16)
        s = s * cd_f + kvf[n]
    s = sb0_ref[...]
    for n in reversed(range(nc)):
        sb[n] = s.astype(BF16)
        s = s * cd_b + kvb[n]

    gain = gain_ref[...]
    for n in range(nc):
        qb = q_ref[rows(n), :]
        qr = qb.astype(F32)
        sc = lax.dot_general(qb, k_ref[rows(n), :], ntdims, preferred_element_type=F32)
        lhs = jnp.concatenate([(qr * tab_s[2]).astype(BF16), (qr * tab_s[3]).astype(BF16),
                               (sc * dmask_s[...]).astype(BF16)], axis=1)
        rhs = jnp.concatenate([sf[n], sb[n], v_ref[rows(n), :]], axis=0)
        o = jnp.dot(lhs, rhs, preferred_element_type=F32)
        mu = jnp.mean(o, axis=-1, keepdims=True)
        d = o - mu
        var = jnp.mean(d * d, axis=-1, keepdims=True)
        gg = g_ref[rows(n), :].astype(F32)
        o_ref[rows(n), :] = (d * lax.rsqrt(var + EPS) * gain * gg).astype(o_ref.dtype)


def _ret_call(qkvg, s0, dec, gn_gain, heads, dh):
    b, l, _ = qkvg.shape
    chunk = RET_CHUNK
    seq = lambda off: pl.BlockSpec((None, l, dh), lambda h, i: (i, 0, off + h))
    return pl.pallas_call(
        functools.partial(_ret_kernel, heads=heads, chunk=chunk),
        grid=(heads, b),
        in_specs=[seq(0), seq(heads), seq(2 * heads), seq(3 * heads),
                  pl.BlockSpec((None, None, dh, dh), lambda h, i: (i, h, 0, 0)),
                  pl.BlockSpec((None, None, dh, dh), lambda h, i: (i, heads + h, 0, 0)),
                  pl.BlockSpec(dec.shape, lambda h, i: (0, 0)),
                  pl.BlockSpec((1, dh), lambda h, i: (0, h))],
        out_specs=pl.BlockSpec((None, l, dh), lambda h, i: (i, 0, h)),
        out_shape=jax.ShapeDtypeStruct((b, l, heads * dh), BF16),
        scratch_shapes=[pltpu.VMEM((6, chunk, dh), F32), pltpu.VMEM((chunk, chunk), F32)],
        compiler_params=_cparams(("arbitrary", "arbitrary")),
        name="retention",
    )(qkvg, qkvg, qkvg, qkvg, s0, s0, dec, gn_gain)


def _filt_time_kernel(emb_ref, w1_ref, b1_ref, w2_ref, b2_ref, w3_ref, b3_ref, w4f_ref, w4b_ref,
                      fr_ref, dl_ref, cast_ref, c_ref, wm_ref, bm_ref,
                      kern_ref, s_ref, cast_out_ref, mod_ref, *, seq_len):
    cast_out_ref[...] = cast_ref[...].astype(cast_out_ref.dtype)
    mod_ref[...] = _dot3(_silu(c_ref[...]), wm_ref[...]) + bm_ref[...]
    i = pl.program_id(0)
    rows, c = kern_ref.shape[1], kern_ref.shape[2]
    fr = fr_ref[...]
    hdot = _dot3
    z = jnp.sin(fr * (hdot(emb_ref[...], w1_ref[...]) + b1_ref[...]))
    z = jnp.sin(fr * (hdot(z, w2_ref[...]) + b2_ref[...]))
    z = jnp.sin(fr * (hdot(z, w3_ref[...]) + b3_ref[...]))
    pos = (i * rows + lax.broadcasted_iota(jnp.int32, (rows, c), 0)).astype(F32)
    inv = 1.0 / (seq_len - 1.0)
    adl = dl_ref[...]
    hf = hdot(z, w4f_ref[...]) * jnp.exp(-(pos * inv) * adl)
    hr = hdot(z, w4b_ref[...]) * jnp.exp(-((seq_len - pos) * inv) * adl)
    hr = jnp.where(pos == 0.0, 0.0, hr)
    kern_ref[0] = hf
    kern_ref[1] = hr
    part = jnp.sum(jnp.abs(hf) + jnp.abs(hr), axis=0, keepdims=True)

    @pl.when(i == 0)
    def _():
        s_ref[...] = part

    @pl.when(i != 0)
    def _():
        s_ref[...] += part


def _filt_time_call(emb2, w1, b1, w2, b2, w3, b3, w4f, w4b, freq, absdelta, seq_len, cast_w,
                    c_rows, w_mod, b_mod):
    c = absdelta.shape[1]
    rows = FILT_ROWS
    steps = seq_len // rows
    assert cast_w.shape[0] % (steps * 2 * SUBLANES) == 0
    slab = pl.BlockSpec((cast_w.shape[0] // steps, cast_w.shape[1]), lambda i: (i, 0))
    mrows, d = c_rows.shape
    n_mod = w_mod.shape[1]
    assert n_mod % (steps * LANES) == 0
    tn = n_mod // steps
    small = lambda a: pl.BlockSpec(a.shape, lambda i: (0,) * a.ndim)
    return pl.pallas_call(
        functools.partial(_filt_time_kernel, seq_len=seq_len),
        grid=(steps,),
        in_specs=[pl.BlockSpec((rows, emb2.shape[1]), lambda i: (i, 0)),
                  small(w1), small(b1), small(w2), small(b2), small(w3), small(b3), small(w4f), small(w4b),
                  small(freq), small(absdelta), slab,
                  small(c_rows),
                  pl.BlockSpec((d, tn), lambda i: (0, i)),
                  pl.BlockSpec((1, tn), lambda i: (0, i))],
        out_specs=[pl.BlockSpec((2, rows, c), lambda i: (0, i, 0)),
                   pl.BlockSpec((1, c), lambda i: (0, 0)), slab,
                   pl.BlockSpec((mrows, tn), lambda i: (0, i))],
        out_shape=[jax.ShapeDtypeStruct((2, seq_len, c), F32),
                   jax.ShapeDtypeStruct((1, c), F32),
                   jax.ShapeDtypeStruct(cast_w.shape, BF16),
                   jax.ShapeDtypeStruct((mrows, n_mod), F32)],
        compiler_params=_cparams(("arbitrary",)),
        name="filt_time",
    )(emb2, w1, b1, w2, b2, w3, b3, w4f, w4b, freq, absdelta, cast_w, c_rows, w_mod, b_mod)


@functools.lru_cache(maxsize=None)
def _fft_tables(seq_len):
    n = 2 * seq_len
    n1s = FFT_N1
    n2s = n // n1s
    half = n1s // 2
    k1n = half + 1
    n1 = np.arange(half)
    k1 = np.arange(k1n)
    th = 2.0 * np.pi * (np.outer(k1, n1) % n1s) / n1s
    herm = np.where((k1 == 0) | (k1 == half), 1.0, 2.0)
    fa = np.concatenate([np.cos(th), -np.sin(th)[1:half]], axis=0)
    sgn = np.concatenate([(-1.0) ** k1, (-1.0) ** k1[1:half]])
    fai = np.concatenate([np.cos(th) * herm[:, None], (-np.sin(th) * herm[:, None])[1:half]], axis=0).T / n
    eye = np.eye(SUBLANES)
    fk = np.kron(fa, eye)
    fks = np.kron(fa * sgn[:, None], eye)
    fki = np.kron(fai, eye)
    k2 = np.arange(n2s)
    n2 = np.arange(n2s)
    m = np.zeros((k1n, 2 * n2s, 2 * n2s))
    for a in range(k1n):
        ang = 2.0 * np.pi * (np.outer(a + n1s * k2, n2) % n) / n
        gr, gi = np.cos(ang), -np.sin(ang)
        m[a] = np.block([[gr, -gi], [gi, gr]])
    groups = np.arange(n2s).reshape(-1, SUBLANES)
    perm = np.concatenate([np.concatenate([g, n2s + g]) for g in groups])
    m = m[:, perm, :]
    mt = np.transpose(m, (0, 2, 1))
    return dict(fk=fk, fkf=np.concatenate([fk, fks], axis=1), fki=fki, m=m, mt=mt,
                k1n=k1n, n2s=n2s, half=half)


def _stage_n1(src_refs, mat, dst_ref, n2s):
    rows = dst_ref.shape[0]
    for j in range(n2s // SUBLANES):
        sl = slice(j * SUBLANES, (j + 1) * SUBLANES)
        parts = [r[:, sl, :] for r in src_refs]
        xg = jnp.concatenate([p.reshape(p.shape[0] * SUBLANES, p.shape[2]) for p in parts], axis=0)
        a = jnp.dot(mat, xg.astype(BF16), preferred_element_type=F32)
        dst_ref[:, sl, :] = a.reshape(rows, SUBLANES, a.shape[1])


def _n2_input(a_s, k1, half, n2s):
    if 0 < k1 < half:
        return jnp.concatenate([a_s[k1], a_s[half + k1]], axis=0)
    return a_s[k1]


def _filt_spec_kernel(kern_ref, s_ref, fkf_ref, m_ref, kf_ref, a_s, *, k1n, n2s):
    half = k1n - 1
    _stage_n1([kern_ref.at[0], kern_ref.at[1]], fkf_ref[...], a_s, n2s)
    inv = 1.0 / (s_ref[...] + EPS)
    for k1 in range(k1n):
        a = _n2_input(a_s, k1, half, n2s).astype(BF16)
        kf_ref[k1] = jnp.dot(m_ref[k1, :, :a.shape[0]], a, preferred_element_type=F32) * inv


def _filt_spec_call(kern4, s, tabs):
    _, half, n2s, c = kern4.shape
    ct = HY_CT
    k1n = tabs["k1n"]
    return pl.pallas_call(
        functools.partial(_filt_spec_kernel, k1n=k1n, n2s=n2s),
        grid=(c // ct,),
        in_specs=[pl.BlockSpec((2, half, n2s, ct), lambda j: (0, 0, 0, j)),
                  pl.BlockSpec((1, ct), lambda j: (0, j)),
                  _const_spec(tabs["fkf"].shape), _const_spec(tabs["m"].shape)],
        out_specs=pl.BlockSpec((k1n, 2 * n2s, ct), lambda j: (0, 0, j)),
        out_shape=jax.ShapeDtypeStruct((k1n, 2 * n2s, c), F32),
        scratch_shapes=[pltpu.VMEM((2 * half, n2s, ct), F32)],
        compiler_params=_cparams(("arbitrary",)),
        name="filt_spec",
    )(kern4, s, tabs["fkf"], tabs["m"])


def _hyena_kernel(u_ref, x0_ref, bias_ref, kf_ref, fk_ref, fki_ref, m_ref, mt_ref, *rest, k1n, n2s, n_cast):
    cast_in, (o_ref, *cast_out), (u_s, a_s) = rest[:n_cast], rest[n_cast:2 * n_cast + 1], rest[2 * n_cast + 1:]
    for src, dst in zip(cast_in, cast_out):
        dst[...] = src[...].astype(dst.dtype)
    half = k1n - 1
    ct = u_ref.shape[2]
    u_s[...] = u_ref[...].astype(F32)
    _stage_n1([u_s], fk_ref[...], a_s, n2s)

    for k1 in range(k1n):
        a = _n2_input(a_s, k1, half, n2s).astype(BF16)
        x = jnp.dot(m_ref[k1, :, :a.shape[0]], a, preferred_element_type=F32)
        x = x.reshape(n2s // SUBLANES, 2 * SUBLANES, ct)
        kf = kf_ref[k1].reshape(n2s // SUBLANES, 2 * SUBLANES, ct)
        xr, xi = x[:, :SUBLANES], x[:, SUBLANES:]
        kr, ki = kf[:, :SUBLANES], kf[:, SUBLANES:]
        y = jnp.concatenate([xr * kr - xi * ki, xr * ki + xi * kr], axis=1)
        y = y.reshape(2 * n2s, ct).astype(BF16)
        if 0 < k1 < half:
            z = jnp.dot(mt_ref[k1], y, preferred_element_type=F32)
            a_s[k1] = z[:n2s]
            a_s[half + k1] = z[n2s:]
        else:
            a_s[k1] = jnp.dot(mt_ref[k1, :n2s, :], y, preferred_element_type=F32)

    fki = fki_ref[...]
    bias = bias_ref[...]
    pair = 2 * SUBLANES
    for j in range(n2s // pair):
        ys = []
        for jj in range(2):
            sl = slice(j * pair + jj * SUBLANES, j * pair + (jj + 1) * SUBLANES)
            zg = a_s[:, sl, :].reshape(2 * half * SUBLANES, ct)
            ys.append(jnp.dot(fki, zg.astype(BF16), preferred_element_type=F32).reshape(half, SUBLANES, ct))
        y = jnp.concatenate(ys, axis=1)
        sl = slice(j * pair, (j + 1) * pair)
        o_ref[:, sl, :] = (x0_ref[:, sl, :].astype(F32) * (y + u_s[:, sl, :] * bias)).astype(o_ref.dtype)


def _hyena_call(u4, x04, bias, kf, tabs, cast_weights):
    b, half, n2s, c = u4.shape
    ct = HY_CT
    k1n = tabs["k1n"]
    nct = c // ct
    steps = nct * b
    seq = pl.BlockSpec((None, half, n2s, ct), lambda j, i: (i, 0, 0, j))
    slabs = [pl.BlockSpec((w.shape[0] // steps, w.shape[1]), lambda j, i: (j * b + i, 0)) for w in cast_weights]
    assert all(w.shape[0] % (steps * 2 * SUBLANES) == 0 for w in cast_weights)
    outs = pl.pallas_call(
        functools.partial(_hyena_kernel, k1n=k1n, n2s=n2s, n_cast=len(cast_weights)),
        grid=(nct, b),
        in_specs=[seq, seq,
                  pl.BlockSpec((1, ct), lambda j, i: (0, j)),
                  pl.BlockSpec((k1n, 2 * n2s, ct), lambda j, i: (0, 0, j)),
                  _const_spec(tabs["fk"].shape), _const_spec(tabs["fki"].shape),
                  _const_spec(tabs["m"].shape), _const_spec(tabs["mt"].shape), *slabs],
        out_specs=[seq, *slabs],
        out_shape=[jax.ShapeDtypeStruct((b, half, n2s, c), BF16),
                   *[jax.ShapeDtypeStruct(w.shape, BF16) for w in cast_weights]],
        scratch_shapes=[pltpu.VMEM((half, n2s, ct), F32), pltpu.VMEM((2 * half, n2s, ct), F32)],
        compiler_params=_cparams(("arbitrary", "arbitrary")),
        name="hyena",
    )(u4, x04, bias, kf, tabs["fk"], tabs["fki"], tabs["m"], tabs["mt"], *cast_weights)
    return outs[0], outs[1:]


def _out_ffn_kernel(ret_ref, hy_ref, x_ref, mod_ref, hg_ref, n2_ref, fn_ref, wo_ref, wgu_ref, wd_ref,
                    o_ref, *, d_ret, d_ff, ff_bounds, row_parts):
    rows = x_ref.shape[0] // row_parts
    groups = [slice(r * rows, (r + 1) * rows) for r in range(row_parts)]
    gain2 = n2_ref[...] * (1.0 + mod_ref[4:5, :])
    x1s, h2s = [], []
    for rs in groups:
        hy = hy_ref[rs, :].astype(F32)
        hms = jnp.mean(hy * hy, axis=-1, keepdims=True)
        hyn = (hy * lax.rsqrt(hms + EPS) * hg_ref[...]).astype(BF16)
        mix = jnp.dot(ret_ref[rs, :], wo_ref[:d_ret, :], preferred_element_type=F32)
        mix = mix + jnp.dot(hyn, wo_ref[d_ret:, :], preferred_element_type=F32)
        x1s.append(x_ref[rs, :] + mod_ref[2:3, :] * mix)
    for x1 in x1s:
        ms = jnp.mean(x1 * x1, axis=-1, keepdims=True)
        h2s.append((x1 * lax.rsqrt(ms + EPS) * gain2 + mod_ref[3:4, :]).astype(BF16))
    accs = [None] * row_parts
    for lo, hi in zip(ff_bounds[:-1], ff_bounds[1:]):
        acts = []
        for h2 in h2s:
            g = jnp.dot(h2, wgu_ref[:, lo:hi], preferred_element_type=F32)
            u = jnp.dot(h2, wgu_ref[:, d_ff + lo:d_ff + hi], preferred_element_type=F32)
            acts.append((_silu(g) * u).astype(BF16))
        for r, a in enumerate(acts):
            d = jnp.dot(a, wd_ref[lo:hi, :], preferred_element_type=F32)
            accs[r] = d if accs[r] is None else accs[r] + d
    for rs, x1, acc in zip(groups, x1s, accs):
        x2 = x1 + mod_ref[5:6, :] * acc
        ms2 = jnp.mean(x2 * x2, axis=-1, keepdims=True)
        o_ref[rs, :] = x2 * lax.rsqrt(ms2 + EPS) * fn_ref[...]


def _ff_bounds(d_ff, parts=2):
    tiles = d_ff // MXU_WIDTH
    assert tiles * MXU_WIDTH == d_ff
    cuts = [-(-tiles * p // parts) for p in range(parts + 1)]
    return tuple(c * MXU_WIDTH for c in cuts)


def _out_ffn_call(ret, hy, x, mod, hy_gain, norm2, final_norm, w_out, w_gu, w_down):
    b, l, d = x.shape
    d_ret = ret.shape[2]
    d_hy = hy.shape[2]
    d_ff = w_down.shape[0]
    tm = ROW_TILE
    return pl.pallas_call(
        functools.partial(_out_ffn_kernel, d_ret=d_ret, d_ff=d_ff, ff_bounds=_ff_bounds(d_ff),
                          row_parts=tm // GROUP_ROWS),
        grid=(b, l // tm),
        in_specs=[pl.BlockSpec((None, tm, d_ret), lambda i, t: (i, t, 0)),
                  pl.BlockSpec((None, tm, d_hy), lambda i, t: (i, t, 0)),
                  pl.BlockSpec((None, tm, d), lambda i, t: (i, t, 0)),
                  pl.BlockSpec((None, N_MOD, d), lambda i, t: (i, 0, 0)),
                  pl.BlockSpec((1, d_hy), lambda i, t: (0, 0)),
                  pl.BlockSpec((1, d), lambda i, t: (0, 0)),
                  pl.BlockSpec((1, d), lambda i, t: (0, 0)),
                  _const_spec(w_out.shape), _const_spec(w_gu.shape), _const_spec(w_down.shape)],
        out_specs=pl.BlockSpec((None, tm, d), lambda i, t: (i, t, 0)),
        out_shape=jax.ShapeDtypeStruct((b, l, d), x.dtype),
        compiler_params=_cparams(("arbitrary", "arbitrary")),
        name="out_ffn",
    )(ret, hy, x, mod, hy_gain, norm2, final_norm, w_out, w_gu, w_down)


@functools.lru_cache(maxsize=None)
def _rope_tables(seq_len, dh):
    n = dh // 4
    t = np.arange(seq_len)
    inv = ROPE_BASE ** (-np.arange(n, dtype=np.float64) / n)
    ang = np.concatenate([(t // GRID_W)[:, None] * inv, (t % GRID_W)[:, None] * inv], axis=-1)
    cc = np.concatenate([np.cos(ang), np.cos(ang)], axis=-1)
    ss = np.concatenate([-np.sin(ang), np.sin(ang)], axis=-1)
    k_scale = dh ** -0.5
    return tuple(t.astype(np.float32) for t in (cc, ss, cc * k_scale, ss * k_scale))


@functools.lru_cache(maxsize=None)
def _filter_tables(seq_len, emb_dim, emb_pad, channels):
    t = np.linspace(0.0, 1.0, seq_len)[:, None]
    bands = (emb_dim - 1) // 2
    f = np.linspace(1e-4, bands - 1, bands)[None, :]
    wpos = 2.0 * np.pi * np.arange(seq_len)[:, None] / seq_len
    emb = np.concatenate([t, np.cos(f * wpos), -np.sin(f * wpos)], axis=-1)
    emb = np.pad(emb, ((0, 0), (0, emb_pad - emb_dim)))
    emb_rev = np.concatenate([emb[:1], emb[:0:-1]], axis=0)
    emb2 = np.concatenate([emb, emb_rev], axis=1)
    max_decay = math.log(FILTER_DECAY_TARGET) / FILTER_DECAY_FAST
    min_decay = math.log(FILTER_DECAY_TARGET) / FILTER_DECAY_SLOW
    absdelta = np.abs(np.linspace(min_decay, max_decay, channels))[None, :]
    return emb2.astype(np.float32), absdelta.astype(np.float32)


def _block_diag2(w):
    z = jnp.zeros_like(w)
    return jnp.concatenate([jnp.concatenate([w, z], axis=1), jnp.concatenate([z, w], axis=1)], axis=0)


def kernel(x, c, ctx, c_ctx, w_mod, b_mod, norm1, norm2, w_in, ret_decay, ret_gn_gain, hy_short_w,
           hy_short_b, hy_w1, hy_b1, hy_w2, hy_b2, hy_w3, hy_b3, hy_w4, hy_freq, hy_bias, hy_out_norm,
           w_out, w_gate_up, w_down, final_norm):
    b, seq_len, d = x.shape
    assert w_mod.shape[0] == 1, "single-layer block"
    heads = RET_HEADS
    d_ret = ret_gn_gain.shape[1]
    dh = d_ret // heads
    d_hy = hy_bias.shape[1]
    assert dh == LANES and d_hy % HY_CT == 0 and seq_len % (FFT_N1 // 2) == 0

    rows = -(-(b + 1) // SUBLANES) * SUBLANES
    c_rows = jnp.zeros((rows, d), F32).at[:b].set(c).at[b].set(c_ctx)

    tabs = dict(_fft_tables(seq_len))
    for name in ("fk", "fkf", "fki", "m", "mt"):
        tabs[name] = jnp.asarray(tabs[name], dtype=F32).astype(BF16)
    half, n2s = tabs["half"], tabs["n2s"]
    emb_dim = hy_w1.shape[1]
    emb_pad = -(-emb_dim // SUBLANES) * SUBLANES
    emb2, absdelta = (jnp.asarray(t) for t in _filter_tables(seq_len, emb_dim, emb_pad, d_hy))
    w1p = jnp.pad(hy_w1[0], ((0, emb_pad - emb_dim), (0, 0)))
    two = lambda a: jnp.concatenate([a, a], axis=1)
    w4 = hy_w4[0]
    zero4 = jnp.zeros((w4.shape[0], d_hy), F32)
    kern, ksum, w_in_b, mod_all = _filt_time_call(
        emb2, _block_diag2(w1p), two(hy_b1), _block_diag2(hy_w2[0]), two(hy_b2), _block_diag2(hy_w3[0]),
        two(hy_b3), jnp.concatenate([w4[:, :d_hy], zero4], axis=0), jnp.concatenate([zero4, w4[:, d_hy:]], axis=0),
        two(hy_freq), absdelta, seq_len, w_in[0], c_rows, w_mod[0], b_mod[0][None, :])
    mod = mod_all[:b].reshape(b, N_MOD, d)
    mod_c = mod_all[b].reshape(N_MOD, d)
    kf = _filt_spec_call(kern.reshape(2, half, n2s, d_hy), ksum, tabs)

    dec = jnp.broadcast_to(ret_decay[0].reshape(2 * heads, 1), (2 * heads, LANES))
    n1g = norm1[0][None, :]
    s0 = _ctx_call(ctx, n1g, mod_c[0:1], mod_c[1:2], w_in_b, dec, heads, dh)
    rope = [jnp.asarray(t) for t in _rope_tables(seq_len, dh)]
    qkvg, x0c, u = _inproj_call(x, mod, n1g, w_in_b, rope, hy_short_w[0], hy_short_b, heads, dh, d_hy)
    ret = _ret_call(qkvg, s0, dec, ret_gn_gain, heads, dh)

    hy, (w_out_b, w_gu_b, w_down_b) = _hyena_call(
        u.reshape(b, half, n2s, d_hy), x0c.reshape(b, half, n2s, d_hy), hy_bias, kf, tabs,
        [w_out[0], w_gate_up[0], w_down[0]])
    hy = hy.reshape(b, seq_len, d_hy)

    return _out_ffn_call(ret, hy, x, mod, hy_out_norm, norm2[0][None, :], final_norm[None, :],
                         w_out_b, w_gu_b, w_down_b)
```

```python
import functools
import math

import jax
import jax.numpy as jnp
import numpy as np
from jax import lax
from jax.experimental import pallas as pl
from jax.experimental.pallas import tpu as pltpu

F32 = jnp.float32
BF16 = jnp.bfloat16

RET_HEADS = 4
GRID_W = 64
ROPE_BASE = 10000.0
N_MOD = 6
HYENA_PROJ = 3
FILTER_DECAY_FAST = 0.3
FILTER_DECAY_SLOW = 1.5
FILTER_DECAY_TARGET = 1e-2
EPS = 1e-6

LANES = 128
SUBLANES = 8
MXU_WIDTH = 256
VMEM_LIMIT_BYTES = 56 * 1024 * 1024

RET_CHUNK = 256
ROW_TILE = 1024
IN_ROW_TILE = 1024
GROUP_ROWS = 256
CTX_BATCHES = 4
HALO = 16
HY_CT = 256
FILT_ROWS = 1024
FFT_N1 = 64


def _silu(v):
    return v / (1.0 + jnp.exp(-v))


def _dot3(a, b):
    a_hi = a.astype(BF16)
    b_hi = b.astype(BF16)
    a_lo = (a - a_hi.astype(F32)).astype(BF16)
    b_lo = (b - b_hi.astype(F32)).astype(BF16)
    dot = functools.partial(jnp.dot, preferred_element_type=F32)
    return dot(a_hi, b_hi) + (dot(a_hi, b_lo) + dot(a_lo, b_hi))


def _cparams(sem, vmem=VMEM_LIMIT_BYTES):
    return pltpu.CompilerParams(dimension_semantics=sem, vmem_limit_bytes=vmem)


def _const_spec(shape):
    nd = len(shape)
    return pl.BlockSpec(shape, lambda *_: (0,) * nd, pipeline_mode=pl.Buffered(1))


def _ctx_kernel(ctx_ref, n1_ref, sh_ref, sc_ref, wk_ref, wv_ref, dec_ref, s_ref, *, heads, dh):
    nb, n_ctx, d = ctx_ref.shape
    xc = ctx_ref[...].reshape(nb * n_ctx, d)
    ms = jnp.mean(xc * xc, axis=-1, keepdims=True)
    hc = ((xc * lax.rsqrt(ms + EPS) * n1_ref[...]) * (1.0 + sc_ref[...]) + sh_ref[...]).astype(BF16)
    k = jnp.dot(hc, wk_ref[...], preferred_element_type=F32)
    v = jnp.dot(hc, wv_ref[...], preferred_element_type=F32)
    lg = jnp.log1p(-jnp.exp(dec_ref[...]))
    pos = lax.broadcasted_iota(jnp.int32, (n_ctx, dh), 0).astype(F32)
    k_scale = dh ** -0.5
    tdims = (((0,), (0,)), ((), ()))
    for h in range(heads):
        wf = jnp.exp(lg[h:h + 1, :] * (n_ctx - 1.0 - pos)) * k_scale
        wb = jnp.exp(lg[heads + h:heads + h + 1, :] * pos) * k_scale
        for i in range(nb):
            rows = slice(i * n_ctx, (i + 1) * n_ctx)
            kh = k[rows, h * dh:(h + 1) * dh]
            vh = v[rows, h * dh:(h + 1) * dh].astype(BF16)
            s_ref[i, h] = lax.dot_general((kh * wf).astype(BF16), vh, tdims, preferred_element_type=F32)
            s_ref[i, heads + h] = lax.dot_general((kh * wb).astype(BF16), vh, tdims,
                                                  preferred_element_type=F32)


def _ctx_call(ctx, norm1, shift_c, scale_c, w_in, dec, heads, dh):
    b, n_ctx, d = ctx.shape
    d_ret = heads * dh
    nb = CTX_BATCHES
    assert b % nb == 0
    return pl.pallas_call(
        functools.partial(_ctx_kernel, heads=heads, dh=dh),
        grid=(b // nb,),
        in_specs=[pl.BlockSpec((nb, n_ctx, d), lambda i: (i, 0, 0)),
                  pl.BlockSpec((1, d), lambda i: (0, 0)),
                  pl.BlockSpec((1, d), lambda i: (0, 0)),
                  pl.BlockSpec((1, d), lambda i: (0, 0)),
                  pl.BlockSpec((d, d_ret), lambda i: (0, 1)),
                  pl.BlockSpec((d, d_ret), lambda i: (0, 2)),
                  pl.BlockSpec(dec.shape, lambda i: (0, 0))],
        out_specs=pl.BlockSpec((nb, 2 * heads, dh, dh), lambda i: (i, 0, 0, 0)),
        out_shape=jax.ShapeDtypeStruct((b, 2 * heads, dh, dh), F32),
        compiler_params=_cparams(("arbitrary",)),
        name="ctx_state",
    )(ctx, norm1, shift_c, scale_c, w_in, w_in, dec)


def _inproj_kernel(x_ref, xp_ref, xn_ref, mod_ref, n1_ref, w_ref, cq_ref, sq_ref, ck_ref, sk_ref,
                   sw_ref, sb_ref, qkvg_ref, x0_ref, u_ref, *, heads, dh, d_hy, nt):
    t = pl.program_id(1)
    tm = x_ref.shape[0]
    d_ret = heads * dh
    gain = n1_ref[...] * (1.0 + mod_ref[1:2, :])

    def normed(x):
        ms = jnp.mean(x * x, axis=-1, keepdims=True)
        return x * lax.rsqrt(ms + EPS) * gain + mod_ref[0:1, :]

    groups = [slice(r, r + GROUP_ROWS) for r in range(0, tm, GROUP_ROWS)]
    hs = [normed(x_ref[rs, :]) for rs in groups]
    hbs = [h.astype(BF16) for h in hs]
    hy_lhs = list(hbs)
    hy_lhs[0] = jnp.concatenate([normed(xp_ref[...]), hs[0]], axis=0).astype(BF16)
    hy_lhs[-1] = jnp.concatenate([hs[-1], normed(xn_ref[...])], axis=0).astype(BF16)

    def proj(lhs, base, width):
        return [jnp.dot(hb, w_ref[:, base:base + width], preferred_element_type=F32) for hb in lhs]

    def hyena_cols(j, lo, hi):
        sl = slice(j * d_hy + lo, j * d_hy + hi)
        parts = proj(hy_lhs, 4 * d_ret + j * d_hy + lo, hi - lo)
        head, tail = parts[0], parts[-1]
        parts[0] = jnp.concatenate([jnp.where(t == 0, 0.0, head[:HALO]), head[HALO:]], axis=0)
        parts[-1] = jnp.concatenate([tail[:GROUP_ROWS], jnp.where(t == nt - 1, 0.0, tail[GROUP_ROWS:])],
                                    axis=0)
        p = jnp.concatenate(parts, axis=0)
        n = p.shape[0]
        y = (sw_ref[0:1, sl] * pltpu.roll(p, 1, 0) + sw_ref[1:2, sl] * p
             + sw_ref[2:3, sl] * pltpu.roll(p, n - 1, 0) + sb_ref[:, sl])
        return y[HALO:HALO + tm]

    def roped_cols(base, cos_ref, sin_ref):
        for rs, p in zip(groups, proj(hbs, base, d_ret)):
            cc = cos_ref[rs, :]
            ss = sin_ref[rs, :]
            for j in range(heads):
                pj = p[:, j * dh:(j + 1) * dh]
                qkvg_ref[rs, base + j * dh:base + (j + 1) * dh] = (
                    pj * cc + pltpu.roll(pj, dh // 2, 1) * ss).astype(BF16)

    half_hy = d_hy // 2
    roped_cols(0, cq_ref, sq_ref)
    x0_ref[:, :half_hy] = hyena_cols(0, 0, half_hy).astype(x0_ref.dtype)
    roped_cols(d_ret, ck_ref, sk_ref)
    x0_ref[:, half_hy:] = hyena_cols(0, half_hy, d_hy).astype(x0_ref.dtype)
    u_ref[:, :half_hy] = (hyena_cols(1, 0, half_hy) * hyena_cols(2, 0, half_hy)).astype(u_ref.dtype)
    for rs, g in zip(groups, proj(hbs, 3 * d_ret, d_ret)):
        qkvg_ref[rs, 3 * d_ret:4 * d_ret] = _silu(g).astype(BF16)
    u_ref[:, half_hy:] = (hyena_cols(1, half_hy, d_hy) * hyena_cols(2, half_hy, d_hy)).astype(u_ref.dtype)
    for rs, v in zip(groups, proj(hbs, 2 * d_ret, d_ret)):
        qkvg_ref[rs, 2 * d_ret:3 * d_ret] = v.astype(BF16)


def _inproj_call(x, mod, norm1, w_in, rope, short_w, short_b, heads, dh, d_hy):
    b, l, d = x.shape
    tm = IN_ROW_TILE
    nt = l // tm
    d_ret = heads * dh
    per_tile = tm // HALO
    tile = lambda i, t: (i, t, 0)
    return pl.pallas_call(
        functools.partial(_inproj_kernel, heads=heads, dh=dh, d_hy=d_hy, nt=nt),
        grid=(b, nt),
        in_specs=[pl.BlockSpec((None, tm, d), tile),
                  pl.BlockSpec((None, HALO, d), lambda i, t: (i, jnp.maximum(t * per_tile - 1, 0), 0)),
                  pl.BlockSpec((None, HALO, d),
                               lambda i, t: (i, jnp.minimum((t + 1) * per_tile, nt * per_tile - 1), 0)),
                  pl.BlockSpec((None, N_MOD, d), lambda i, t: (i, 0, 0)),
                  pl.BlockSpec((1, d), lambda i, t: (0, 0)),
                  _const_spec(w_in.shape),
                  *[pl.BlockSpec((tm, dh), lambda i, t: (t, 0)) for _ in rope],
                  pl.BlockSpec(short_w.shape, lambda i, t: (0, 0)),
                  pl.BlockSpec(short_b.shape, lambda i, t: (0, 0))],
        out_specs=[pl.BlockSpec((None, tm, 4 * d_ret), tile),
                   pl.BlockSpec((None, tm, d_hy), tile),
                   pl.BlockSpec((None, tm, d_hy), tile)],
        out_shape=[jax.ShapeDtypeStruct((b, l, 4 * d_ret), BF16),
                   jax.ShapeDtypeStruct((b, l, d_hy), BF16),
                   jax.ShapeDtypeStruct((b, l, d_hy), BF16)],
        compiler_params=_cparams(("arbitrary", "arbitrary")),
        name="in_proj",
    )(x, x, x, mod, norm1, w_in, *rope, short_w, short_b)


def _ret_kernel(q_ref, k_ref, v_ref, g_ref, sf0_ref, sb0_ref, dec_ref, gain_ref,
                o_ref, tab_s, dmask_s, *, heads, chunk):
    l, dh = q_ref.shape
    nc = l // chunk
    h = pl.program_id(0)

    @pl.when(pl.program_id(1) == 0)
    def _():
        lgf = jnp.log1p(-jnp.exp(dec_ref[pl.ds(h, 1), :]))
        lgb = jnp.log1p(-jnp.exp(dec_ref[pl.ds(h + heads, 1), :]))
        il = lax.broadcasted_iota(jnp.int32, (chunk, dh), 0).astype(F32)
        tab_s[0] = jnp.exp(lgf * (chunk - 1.0 - il))
        tab_s[1] = jnp.exp(lgb * il)
        tab_s[2] = jnp.exp(lgf * (il + 1.0))
        tab_s[3] = jnp.exp(lgb * (chunk - il))
        tab_s[4] = jnp.exp(jnp.broadcast_to(lgf, (chunk, dh)) * float(chunk))
        tab_s[5] = jnp.exp(jnp.broadcast_to(lgb, (chunk, dh)) * float(chunk))
        reps = chunk // dh
        lgf_c = jnp.concatenate([lgf] * reps, axis=1)
        lgb_c = jnp.concatenate([lgb] * reps, axis=1)
        ii = lax.broadcasted_iota(jnp.int32, (chunk, chunk), 0)
        jj = lax.broadcasted_iota(jnp.int32, (chunk, chunk), 1)
        diff = (ii - jj).astype(F32)
        dmask_s[...] = (jnp.where(diff >= 0, jnp.exp(lgf_c * jnp.maximum(diff, 0.0)), 0.0)
                        + jnp.where(diff <= 0, jnp.exp(lgb_c * jnp.maximum(-diff, 0.0)), 0.0))

    tdims = (((0,), (0,)), ((), ()))
    ntdims = (((1,), (1,)), ((), ()))
    rows = lambda n: pl.ds(n * chunk, chunk)

    kvf, kvb = [], []
    for n in range(nc):
        kr = k_ref[rows(n), :].astype(F32)
        vv = v_ref[rows(n), :]
        kvf.append(lax.dot_general((kr * tab_s[0]).astype(BF16), vv, tdims, preferred_element_type=F32))
        kvb.append(lax.dot_general((kr * tab_s[1]).astype(BF16), vv, tdims, preferred_element_type=F32))

    cd_f = tab_s[4, :dh, :]
    cd_b = tab_s[5, :dh, :]
    sf, sb = [None] * nc, [None] * nc
    s = sf0_ref[...]
    for n in range(nc):
        sf[n] = s.astype(BF16)
        s = s * cd_f + kvf[n]
    s = sb0_ref[...]
    for n in reversed(range(nc)):
        sb[n] = s.astype(BF16)
        s = s * cd_b + kvb[n]

    gain = gain_ref[...]
    for n in range(nc):
        qb = q_ref[rows(n), :]
        qr = qb.astype(F32)
        sc = lax.dot_general(qb, k_ref[rows(n), :], ntdims, preferred_element_type=F32)
        lhs = jnp.concatenate([(qr * tab_s[2]).astype(BF16), (qr * tab_s[3]).astype(BF16),
                               (sc * dmask_s[...]).astype(BF16)], axis=1)
        rhs = jnp.concatenate([sf[n], sb[n], v_ref[rows(n), :]], axis=0)
        o = jnp.dot(lhs, rhs, preferred_element_type=F32)
        mu = jnp.mean(o, axis=-1, keepdims=True)
        d = o - mu
        var = jnp.mean(d * d, axis=-1, keepdims=True)
        gg = g_ref[rows(n), :].astype(F32)
        o_ref[rows(n), :] = (d * lax.rsqrt(var + EPS) * gain * gg).astype(o_ref.dtype)


def _ret_call(qkvg, s0, dec, gn_gain, heads, dh):
    b, l, _ = qkvg.shape
    chunk = RET_CHUNK
    seq = lambda off: pl.BlockSpec((None, l, dh), lambda h, i: (i, 0, off + h))
    return pl.pallas_call(
        functools.partial(_ret_kernel, heads=heads, chunk=chunk),
        grid=(heads, b),
        in_specs=[seq(0), seq(heads), seq(2 * heads), seq(3 * heads),
                  pl.BlockSpec((None, None, dh, dh), lambda h, i: (i, h, 0, 0)),
                  pl.BlockSpec((None, None, dh, dh), lambda h, i: (i, heads + h, 0, 0)),
                  pl.BlockSpec(dec.shape, lambda h, i: (0, 0)),
                  pl.BlockSpec((1, dh), lambda h, i: (0, h))],
        out_specs=pl.BlockSpec((None, l, dh), lambda h, i: (i, 0, h)),
        out_shape=jax.ShapeDtypeStruct((b, l, heads * dh), BF16),
        scratch_shapes=[pltpu.VMEM((6, chunk, dh), F32), pltpu.VMEM((chunk, chunk), F32)],
        compiler_params=_cparams(("arbitrary", "arbitrary")),
        name="retention",
    )(qkvg, qkvg, qkvg, qkvg, s0, s0, dec, gn_gain)


def _filt_time_kernel(emb_ref, w1_ref, b1_ref, w2_ref, b2_ref, w3_ref, b3_ref, w4f_ref, w4b_ref,
                      fr_ref, dl_ref, cast_ref, c_ref, wm_ref, bm_ref,
                      kern_ref, s_ref, cast_out_ref, mod_ref, *, seq_len):
    cast_out_ref[...] = cast_ref[...].astype(cast_out_ref.dtype)
    mod_ref[...] = _dot3(_silu(c_ref[...]), wm_ref[...]) + bm_ref[...]
    i = pl.program_id(0)
    rows, c = kern_ref.shape[1], kern_ref.shape[2]
    fr = fr_ref[...]
    hdot = _dot3
    z = jnp.sin(fr * (hdot(emb_ref[...], w1_ref[...]) + b1_ref[...]))
    z = jnp.sin(fr * (hdot(z, w2_ref[...]) + b2_ref[...]))
    z = jnp.sin(fr * (hdot(z, w3_ref[...]) + b3_ref[...]))
    pos = (i * rows + lax.broadcasted_iota(jnp.int32, (rows, c), 0)).astype(F32)
    inv = 1.0 / (seq_len - 1.0)
    adl = dl_ref[...]
    hf = hdot(z, w4f_ref[...]) * jnp.exp(-(pos * inv) * adl)
    hr = hdot(z, w4b_ref[...]) * jnp.exp(-((seq_len - pos) * inv) * adl)
    hr = jnp.where(pos == 0.0, 0.0, hr)
    kern_ref[0] = hf
    kern_ref[1] = hr
    part = jnp.sum(jnp.abs(hf) + jnp.abs(hr), axis=0, keepdims=True)

    @pl.when(i == 0)
    def _():
        s_ref[...] = part

    @pl.when(i != 0)
    def _():
        s_ref[...] += part


def _filt_time_call(emb2, w1, b1, w2, b2, w3, b3, w4f, w4b, freq, absdelta, seq_len, cast_w,
                    c_rows, w_mod, b_mod):
    c = absdelta.shape[1]
    rows = FILT_ROWS
    steps = seq_len // rows
    assert cast_w.shape[0] % (steps * 2 * SUBLANES) == 0
    slab = pl.BlockSpec((cast_w.shape[0] // steps, cast_w.shape[1]), lambda i: (i, 0))
    mrows, d = c_rows.shape
    n_mod = w_mod.shape[1]
    assert n_mod % (steps * LANES) == 0
    tn = n_mod // steps
    small = lambda a: pl.BlockSpec(a.shape, lambda i: (0,) * a.ndim)
    return pl.pallas_call(
        functools.partial(_filt_time_kernel, seq_len=seq_len),
        grid=(steps,),
        in_specs=[pl.BlockSpec((rows, emb2.shape[1]), lambda i: (i, 0)),
                  small(w1), small(b1), small(w2), small(b2), small(w3), small(b3), small(w4f), small(w4b),
                  small(freq), small(absdelta), slab,
                  small(c_rows),
                  pl.BlockSpec((d, tn), lambda i: (0, i)),
                  pl.BlockSpec((1, tn), lambda i: (0, i))],
        out_specs=[pl.BlockSpec((2, rows, c), lambda i: (0, i, 0)),
                   pl.BlockSpec((1, c), lambda i: (0, 0)), slab,
                   pl.BlockSpec((mrows, tn), lambda i: (0, i))],
        out_shape=[jax.ShapeDtypeStruct((2, seq_len, c), F32),
                   jax.ShapeDtypeStruct((1, c), F32),
                   jax.ShapeDtypeStruct(cast_w.shape, BF16),
                   jax.ShapeDtypeStruct((mrows, n_mod), F32)],
        compiler_params=_cparams(("arbitrary",)),
        name="filt_time",
    )(emb2, w1, b1, w2, b2, w3, b3, w4f, w4b, freq, absdelta, cast_w, c_rows, w_mod, b_mod)


@functools.lru_cache(maxsize=None)
def _fft_tables(seq_len):
    n = 2 * seq_len
    n1s = FFT_N1
    n2s = n // n1s
    half = n1s // 2
    k1n = half + 1
    n1 = np.arange(half)
    k1 = np.arange(k1n)
    th = 2.0 * np.pi * (np.outer(k1, n1) % n1s) / n1s
    herm = np.where((k1 == 0) | (k1 == half), 1.0, 2.0)
    fa = np.concatenate([np.cos(th), -np.sin(th)[1:half]], axis=0)
    sgn = np.concatenate([(-1.0) ** k1, (-1.0) ** k1[1:half]])
    fai = np.concatenate([np.cos(th) * herm[:, None], (-np.sin(th) * herm[:, None])[1:half]], axis=0).T / n
    eye = np.eye(SUBLANES)
    fk = np.kron(fa, eye)
    fks = np.kron(fa * sgn[:, None], eye)
    fki = np.kron(fai, eye)
    k2 = np.arange(n2s)
    n2 = np.arange(n2s)
    m = np.zeros((k1n, 2 * n2s, 2 * n2s))
    for a in range(k1n):
        ang = 2.0 * np.pi * (np.outer(a + n1s * k2, n2) % n) / n
        gr, gi = np.cos(ang), -np.sin(ang)
        m[a] = np.block([[gr, -gi], [gi, gr]])
    groups = np.arange(n2s).reshape(-1, SUBLANES)
    perm = np.concatenate([np.concatenate([g, n2s + g]) for g in groups])
    m = m[:, perm, :]
    mt = np.transpose(m, (0, 2, 1))
    return dict(fk=fk, fkf=np.concatenate([fk, fks], axis=1), fki=fki, m=m, mt=mt,
                k1n=k1n, n2s=n2s, half=half)


def _stage_n1(src_refs, mat, dst_ref, n2s):
    rows = dst_ref.shape[0]
    for j in range(n2s // SUBLANES):
        sl = slice(j * SUBLANES, (j + 1) * SUBLANES)
        parts = [r[:, sl, :] for r in src_refs]
        xg = jnp.concatenate([p.reshape(p.shape[0] * SUBLANES, p.shape[2]) for p in parts], axis=0)
        a = jnp.dot(mat, xg.astype(BF16), preferred_element_type=F32)
        dst_ref[:, sl, :] = a.reshape(rows, SUBLANES, a.shape[1])


def _n2_input(a_s, k1, half, n2s):
    if 0 < k1 < half:
        return jnp.concatenate([a_s[k1], a_s[half + k1]], axis=0)
    return a_s[k1]


def _filt_spec_kernel(kern_ref, s_ref, fkf_ref, m_ref, kf_ref, a_s, *, k1n, n2s):
    half = k1n - 1
    _stage_n1([kern_ref.at[0], kern_ref.at[1]], fkf_ref[...], a_s, n2s)
    inv = 1.0 / (s_ref[...] + EPS)
    for k1 in range(k1n):
        a = _n2_input(a_s, k1, half, n2s).astype(BF16)
        kf_ref[k1] = jnp.dot(m_ref[k1, :, :a.shape[0]], a, preferred_element_type=F32) * inv


def _filt_spec_call(kern4, s, tabs):
    _, half, n2s, c = kern4.shape
    ct = HY_CT
    k1n = tabs["k1n"]
    return pl.pallas_call(
        functools.partial(_filt_spec_kernel, k1n=k1n, n2s=n2s),
        grid=(c // ct,),
        in_specs=[pl.BlockSpec((2, half, n2s, ct), lambda j: (0, 0, 0, j)),
                  pl.BlockSpec((1, ct), lambda j: (0, j)),
                  _const_spec(tabs["fkf"].shape), _const_spec(tabs["m"].shape)],
        out_specs=pl.BlockSpec((k1n, 2 * n2s, ct), lambda j: (0, 0, j)),
        out_shape=jax.ShapeDtypeStruct((k1n, 2 * n2s, c), F32),
        scratch_shapes=[pltpu.VMEM((2 * half, n2s, ct), F32)],
        compiler_params=_cparams(("arbitrary",)),
        name="filt_spec",
    )(kern4, s, tabs["fkf"], tabs["m"])


def _hyena_kernel(u_ref, x0_ref, bias_ref, kf_ref, fk_ref, fki_ref, m_ref, mt_ref, *rest, k1n, n2s, n_cast):
    cast_in, (o_ref, *cast_out), (u_s, a_s) = rest[:n_cast], rest[n_cast:2 * n_cast + 1], rest[2 * n_cast + 1:]
    for src, dst in zip(cast_in, cast_out):
        dst[...] = src[...].astype(dst.dtype)
    half = k1n - 1
    ct = u_ref.shape[2]
    u_s[...] = u_ref[...].astype(F32)
    _stage_n1([u_s], fk_ref[...], a_s, n2s)

    for k1 in range(k1n):
        a = _n2_input(a_s, k1, half, n2s).astype(BF16)
        x = jnp.dot(m_ref[k1, :, :a.shape[0]], a, preferred_element_type=F32)
        x = x.reshape(n2s // SUBLANES, 2 * SUBLANES, ct)
        kf = kf_ref[k1].reshape(n2s // SUBLANES, 2 * SUBLANES, ct)
        xr, xi = x[:, :SUBLANES], x[:, SUBLANES:]
        kr, ki = kf[:, :SUBLANES], kf[:, SUBLANES:]
        y = jnp.concatenate([xr * kr - xi * ki, xr * ki + xi * kr], axis=1)
        y = y.reshape(2 * n2s, ct).astype(BF16)
        if 0 < k1 < half:
            z = jnp.dot(mt_ref[k1], y, preferred_element_type=F32)
            a_s[k1] = z[:n2s]
            a_s[half + k1] = z[n2s:]
        else:
            a_s[k1] = jnp.dot(mt_ref[k1, :n2s, :], y, preferred_element_type=F32)

    fki = fki_ref[...]
    bias = bias_ref[...]
    pair = 2 * SUBLANES
    for j in range(n2s // pair):
        ys = []
        for jj in range(2):
            sl = slice(j * pair + jj * SUBLANES, j * pair + (jj + 1) * SUBLANES)
            zg = a_s[:, sl, :].reshape(2 * half * SUBLANES, ct)
            ys.append(jnp.dot(fki, zg.astype(BF16), preferred_element_type=F32).reshape(half, SUBLANES, ct))
        y = jnp.concatenate(ys, axis=1)
        sl = slice(j * pair, (j + 1) * pair)
        o_ref[:, sl, :] = (x0_ref[:, sl, :].astype(F32) * (y + u_s[:, sl, :] * bias)).astype(o_ref.dtype)


def _hyena_call(u4, x04, bias, kf, tabs, cast_weights):
    b, half, n2s, c = u4.shape
    ct = HY_CT
    k1n = tabs["k1n"]
    nct = c // ct
    steps = nct * b
    seq = pl.BlockSpec((None, half, n2s, ct), lambda j, i: (i, 0, 0, j))
    slabs = [pl.BlockSpec((w.shape[0] // steps, w.shape[1]), lambda j, i: (j * b + i, 0)) for w in cast_weights]
    assert all(w.shape[0] % (steps * 2 * SUBLANES) == 0 for w in cast_weights)
    outs = pl.pallas_call(
        functools.partial(_hyena_kernel, k1n=k1n, n2s=n2s, n_cast=len(cast_weights)),
        grid=(nct, b),
        in_specs=[seq, seq,
                  pl.BlockSpec((1, ct), lambda j, i: (0, j)),
                  pl.BlockSpec((k1n, 2 * n2s, ct), lambda j, i: (0, 0, j)),
                  _const_spec(tabs["fk"].shape), _const_spec(tabs["fki"].shape),
                  _const_spec(tabs["m"].shape), _const_spec(tabs["mt"].shape), *slabs],
        out_specs=[seq, *slabs],
        out_shape=[jax.ShapeDtypeStruct((b, half, n2s, c), BF16),
                   *[jax.ShapeDtypeStruct(w.shape, BF16) for w in cast_weights]],
        scratch_shapes=[pltpu.VMEM((half, n2s, ct), F32), pltpu.VMEM((2 * half, n2s, ct), F32)],
        compiler_params=_cparams(("arbitrary", "arbitrary")),
        name="hyena",
    )(u4, x04, bias, kf, tabs["fk"], tabs["fki"], tabs["m"], tabs["mt"], *cast_weights)
    return outs[0], outs[1:]


def _out_ffn_kernel(ret_ref, hy_ref, x_ref, mod_ref, hg_ref, n2_ref, fn_ref, wo_ref, wgu_ref, wd_ref,
                    o_ref, *, d_ret, d_ff, ff_bounds, row_parts):
    rows = x_ref.shape[0] // row_parts
    groups = [slice(r * rows, (r + 1) * rows) for r in range(row_parts)]
    gain2 = n2_ref[...] * (1.0 + mod_ref[4:5, :])
    x1s, h2s = [], []
    for rs in groups:
        hy = hy_ref[rs, :].astype(F32)
        hms = jnp.mean(hy * hy, axis=-1, keepdims=True)
        hyn = (hy * lax.rsqrt(hms + EPS) * hg_ref[...]).astype(BF16)
        mix = jnp.dot(ret_ref[rs, :], wo_ref[:d_ret, :], preferred_element_type=F32)
        mix = mix + jnp.dot(hyn, wo_ref[d_ret:, :], preferred_element_type=F32)
        x1s.append(x_ref[rs, :] + mod_ref[2:3, :] * mix)
    for x1 in x1s:
        ms = jnp.mean(x1 * x1, axis=-1, keepdims=True)
        h2s.append((x1 * lax.rsqrt(ms + EPS) * gain2 + mod_ref[3:4, :]).astype(BF16))
    accs = [None] * row_parts
    for lo, hi in zip(ff_bounds[:-1], ff_bounds[1:]):
        acts = []
        for h2 in h2s:
            g = jnp.dot(h2, wgu_ref[:, lo:hi], preferred_element_type=F32)
            u = jnp.dot(h2, wgu_ref[:, d_ff + lo:d_ff + hi], preferred_element_type=F32)
            acts.append((_silu(g) * u).astype(BF16))
        for r, a in enumerate(acts):
            d = jnp.dot(a, wd_ref[lo:hi, :], preferred_element_type=F32)
            accs[r] = d if accs[r] is None else accs[r] + d
    for rs, x1, acc in zip(groups, x1s, accs):
        x2 = x1 + mod_ref[5:6, :] * acc
        ms2 = jnp.mean(x2 * x2, axis=-1, keepdims=True)
        o_ref[rs, :] = x2 * lax.rsqrt(ms2 + EPS) * fn_ref[...]


def _ff_bounds(d_ff, parts=2):
    tiles = d_ff // MXU_WIDTH
    assert tiles * MXU_WIDTH == d_ff
    cuts = [-(-tiles * p // parts) for p in range(parts + 1)]
    return tuple(c * MXU_WIDTH for c in cuts)


def _out_ffn_call(ret, hy, x, mod, hy_gain, norm2, final_norm, w_out, w_gu, w_down):
    b, l, d = x.shape
    d_ret = ret.shape[2]
    d_hy = hy.shape[2]
    d_ff = w_down.shape[0]
    tm = ROW_TILE
    return pl.pallas_call(
        functools.partial(_out_ffn_kernel, d_ret=d_ret, d_ff=d_ff, ff_bounds=_ff_bounds(d_ff),
                          row_parts=tm // GROUP_ROWS),
        grid=(b, l // tm),
        in_specs=[pl.BlockSpec((None, tm, d_ret), lambda i, t: (i, t, 0)),
                  pl.BlockSpec((None, tm, d_hy), lambda i, t: (i, t, 0)),
                  pl.BlockSpec((None, tm, d), lambda i, t: (i, t, 0)),
                  pl.BlockSpec((None, N_MOD, d), lambda i, t: (i, 0, 0)),
                  pl.BlockSpec((1, d_hy), lambda i, t: (0, 0)),
                  pl.BlockSpec((1, d), lambda i, t: (0, 0)),
                  pl.BlockSpec((1, d), lambda i, t: (0, 0)),
                  _const_spec(w_out.shape), _const_spec(w_gu.shape), _const_spec(w_down.shape)],
        out_specs=pl.BlockSpec((None, tm, d), lambda i, t: (i, t, 0)),
        out_shape=jax.ShapeDtypeStruct((b, l, d), x.dtype),
        compiler_params=_cparams(("arbitrary", "arbitrary")),
        name="out_ffn",
    )(ret, hy, x, mod, hy_gain, norm2, final_norm, w_out, w_gu, w_down)


@functools.lru_cache(maxsize=None)
def _rope_tables(seq_len, dh):
    n = dh // 4
    t = np.arange(seq_len)
    inv = ROPE_BASE ** (-np.arange(n, dtype=np.float64) / n)
    ang = np.concatenate([(t // GRID_W)[:, None] * inv, (t % GRID_W)[:, None] * inv], axis=-1)
    cc = np.concatenate([np.cos(ang), np.cos(ang)], axis=-1)
    ss = np.concatenate([-np.sin(ang), np.sin(ang)], axis=-1)
    k_scale = dh ** -0.5
    return tuple(t.astype(np.float32) for t in (cc, ss, cc * k_scale, ss * k_scale))


@functools.lru_cache(maxsize=None)
def _filter_tables(seq_len, emb_dim, emb_pad, channels):
    t = np.linspace(0.0, 1.0, seq_len)[:, None]
    bands = (emb_dim - 1) // 2
    f = np.linspace(1e-4, bands - 1, bands)[None, :]
    wpos = 2.0 * np.pi * np.arange(seq_len)[:, None] / seq_len
    emb = np.concatenate([t, np.cos(f * wpos), -np.sin(f * wpos)], axis=-1)
    emb = np.pad(emb, ((0, 0), (0, emb_pad - emb_dim)))
    emb_rev = np.concatenate([emb[:1], emb[:0:-1]], axis=0)
    emb2 = np.concatenate([emb, emb_rev], axis=1)
    max_decay = math.log(FILTER_DECAY_TARGET) / FILTER_DECAY_FAST
    min_decay = math.log(FILTER_DECAY_TARGET) / FILTER_DECAY_SLOW
    absdelta = np.abs(np.linspace(min_decay, max_decay, channels))[None, :]
    return emb2.astype(np.float32), absdelta.astype(np.float32)


def _block_diag2(w):
    z = jnp.zeros_like(w)
    return jnp.concatenate([jnp.concatenate([w, z], axis=1), jnp.concatenate([z, w], axis=1)], axis=0)


def kernel(x, c, ctx, c_ctx, w_mod, b_mod, norm1, norm2, w_in, ret_decay, ret_gn_gain, hy_short_w,
           hy_short_b, hy_w1, hy_b1, hy_w2, hy_b2, hy_w3, hy_b3, hy_w4, hy_freq, hy_bias, hy_out_norm,
           w_out, w_gate_up, w_down, final_norm):
    b, seq_len, d = x.shape
    assert w_mod.shape[0] == 1, "single-layer block"
    heads = RET_HEADS
    d_ret = ret_gn_gain.shape[1]
    dh = d_ret // heads
    d_hy = hy_bias.shape[1]
    assert dh == LANES and d_hy % HY_CT == 0 and seq_len % (FFT_N1 // 2) == 0

    rows = -(-(b + 1) // SUBLANES) * SUBLANES
    c_rows = jnp.zeros((rows, d), F32).at[:b].set(c).at[b].set(c_ctx)

    tabs = dict(_fft_tables(seq_len))
    for name in ("fk", "fkf", "fki", "m", "mt"):
        tabs[name] = jnp.asarray(tabs[name], dtype=F32).astype(BF16)
    half, n2s = tabs["half"], tabs["n2s"]
    emb_dim = hy_w1.shape[1]
    emb_pad = -(-emb_dim // SUBLANES) * SUBLANES
    emb2, absdelta = (jnp.asarray(t) for t in _filter_tables(seq_len, emb_dim, emb_pad, d_hy))
    w1p = jnp.pad(hy_w1[0], ((0, emb_pad - emb_dim), (0, 0)))
    two = lambda a: jnp.concatenate([a, a], axis=1)
    w4 = hy_w4[0]
    zero4 = jnp.zeros((w4.shape[0], d_hy), F32)
    kern, ksum, w_in_b, mod_all = _filt_time_call(
        emb2, _block_diag2(w1p), two(hy_b1), _block_diag2(hy_w2[0]), two(hy_b2), _block_diag2(hy_w3[0]),
        two(hy_b3), jnp.concatenate([w4[:, :d_hy], zero4], axis=0), jnp.concatenate([zero4, w4[:, d_hy:]], axis=0),
        two(hy_freq), absdelta, seq_len, w_in[0], c_rows, w_mod[0], b_mod[0][None, :])
    mod = mod_all[:b].reshape(b, N_MOD, d)
    mod_c = mod_all[b].reshape(N_MOD, d)
    kf = _filt_spec_call(kern.reshape(2, half, n2s, d_hy), ksum, tabs)

    dec = jnp.broadcast_to(ret_decay[0].reshape(2 * heads, 1), (2 * heads, LANES))
    n1g = norm1[0][None, :]
    s0 = _ctx_call(ctx, n1g, mod_c[0:1], mod_c[1:2], w_in_b, dec, heads, dh)
    rope = [jnp.asarray(t) for t in _rope_tables(seq_len, dh)]
    qkvg, x0c, u = _inproj_call(x, mod, n1g, w_in_b, rope, hy_short_w[0], hy_short_b, heads, dh, d_hy)
    ret = _ret_call(qkvg, s0, dec, ret_gn_gain, heads, dh)

    hy, (w_out_b, w_gu_b, w_down_b) = _hyena_call(
        u.reshape(b, half, n2s, d_hy), x0c.reshape(b, half, n2s, d_hy), hy_bias, kf, tabs,
        [w_out[0], w_gate_up[0], w_down[0]])
    hy = hy.reshape(b, seq_len, d_hy)

    return _out_ffn_call(ret, hy, x, mod, hy_out_norm, norm2[0][None, :], final_norm[None, :],
                         w_out_b, w_gu_b, w_down_b)
```

```python
import functools
import math

import jax
import jax.numpy as jnp
import numpy as np
from jax import lax
from jax.experimental import pallas as pl
from jax.experimental.pallas import tpu as pltpu

F32 = jnp.float32
BF16 = jnp.bfloat16

RET_HEADS = 4
GRID_W = 64
ROPE_BASE = 10000.0
N_MOD = 6
HYENA_PROJ = 3
FILTER_DECAY_FAST = 0.3
FILTER_DECAY_SLOW = 1.5
FILTER_DECAY_TARGET = 1e-2
EPS = 1e-6

LANES = 128
SUBLANES = 8
MXU_WIDTH = 256
VMEM_LIMIT_BYTES = 56 * 1024 * 1024

RET_CHUNK = 256
ROW_TILE = 1024
IN_ROW_TILE = 1024
GROUP_ROWS = 256
CTX_BATCHES = 4
HALO = 16
HY_CT = 256
FILT_ROWS = 1024
FFT_N1 = 64


def _silu(v):
    return v / (1.0 + jnp.exp(-v))


def _dot3(a, b):
    a_hi = a.astype(BF16)
    b_hi = b.astype(BF16)
    a_lo = (a - a_hi.astype(F32)).astype(BF16)
    b_lo = (b - b_hi.astype(F32)).astype(BF16)
    dot = functools.partial(jnp.dot, preferred_element_type=F32)
    return dot(a_hi, b_hi) + (dot(a_hi, b_lo) + dot(a_lo, b_hi))


def _cparams(sem, vmem=VMEM_LIMIT_BYTES):
    return pltpu.CompilerParams(dimension_semantics=sem, vmem_limit_bytes=vmem)


def _const_spec(shape):
    nd = len(shape)
    return pl.BlockSpec(shape, lambda *_: (0,) * nd, pipeline_mode=pl.Buffered(1))


def _ctx_kernel(ctx_ref, n1_ref, sh_ref, sc_ref, wk_ref, wv_ref, dec_ref, s_ref, *, heads, dh):
    nb, n_ctx, d = ctx_ref.shape
    xc = ctx_ref[...].reshape(nb * n_ctx, d)
    ms = jnp.mean(xc * xc, axis=-1, keepdims=True)
    hc = ((xc * lax.rsqrt(ms + EPS) * n1_ref[...]) * (1.0 + sc_ref[...]) + sh_ref[...]).astype(BF16)
    k = jnp.dot(hc, wk_ref[...], preferred_element_type=F32)
    v = jnp.dot(hc, wv_ref[...], preferred_element_type=F32)
    lg = jnp.log1p(-jnp.exp(dec_ref[...]))
    pos = lax.broadcasted_iota(jnp.int32, (n_ctx, dh), 0).astype(F32)
    k_scale = dh ** -0.5
    tdims = (((0,), (0,)), ((), ()))
    for h in range(heads):
        wf = jnp.exp(lg[h:h + 1, :] * (n_ctx - 1.0 - pos)) * k_scale
        wb = jnp.exp(lg[heads + h:heads + h + 1, :] * pos) * k_scale
        for i in range(nb):
            rows = slice(i * n_ctx, (i + 1) * n_ctx)
            kh = k[rows, h * dh:(h + 1) * dh]
            vh = v[rows, h * dh:(h + 1) * dh].astype(BF16)
            s_ref[i, h] = lax.dot_general((kh * wf).astype(BF16), vh, tdims, preferred_element_type=F32)
            s_ref[i, heads + h] = lax.dot_general((kh * wb).astype(BF16), vh, tdims,
                                                  preferred_element_type=F32)


def _ctx_call(ctx, norm1, shift_c, scale_c, w_in, dec, heads, dh):
    b, n_ctx, d = ctx.shape
    d_ret = heads * dh
    nb = CTX_BATCHES
    assert b % nb == 0
    return pl.pallas_call(
        functools.partial(_ctx_kernel, heads=heads, dh=dh),
        grid=(b // nb,),
        in_specs=[pl.BlockSpec((nb, n_ctx, d), lambda i: (i, 0, 0)),
                  pl.BlockSpec((1, d), lambda i: (0, 0)),
                  pl.BlockSpec((1, d), lambda i: (0, 0)),
                  pl.BlockSpec((1, d), lambda i: (0, 0)),
                  pl.BlockSpec((d, d_ret), lambda i: (0, 1)),
                  pl.BlockSpec((d, d_ret), lambda i: (0, 2)),
                  pl.BlockSpec(dec.shape, lambda i: (0, 0))],
        out_specs=pl.BlockSpec((nb, 2 * heads, dh, dh), lambda i: (i, 0, 0, 0)),
        out_shape=jax.ShapeDtypeStruct((b, 2 * heads, dh, dh), F32),
        compiler_params=_cparams(("arbitrary",)),
        name="ctx_state",
    )(ctx, norm1, shift_c, scale_c, w_in, w_in, dec)


def _inproj_kernel(x_ref, xp_ref, xn_ref, mod_ref, n1_ref, w_ref, cq_ref, sq_ref, ck_ref, sk_ref,
                   sw_ref, sb_ref, qkvg_ref, x0_ref, u_ref, *, heads, dh, d_hy, nt):
    t = pl.program_id(1)
    tm = x_ref.shape[0]
    d_ret = heads * dh
    gain = n1_ref[...] * (1.0 + mod_ref[1:2, :])

    def normed(x):
        ms = jnp.mean(x * x, axis=-1, keepdims=True)
        return x * lax.rsqrt(ms + EPS) * gain + mod_ref[0:1, :]

    groups = [slice(r, r + GROUP_ROWS) for r in range(0, tm, GROUP_ROWS)]
    hs = [normed(x_ref[rs, :]) for rs in groups]
    hbs = [h.astype(BF16) for h in hs]
    hy_lhs = list(hbs)
    hy_lhs[0] = jnp.concatenate([normed(xp_ref[...]), hs[0]], axis=0).astype(BF16)
    hy_lhs[-1] = jnp.concatenate([hs[-1], normed(xn_ref[...])], axis=0).astype(BF16)

    def proj(lhs, base, width):
        return [jnp.dot(hb, w_ref[:, base:base + width], preferred_element_type=F32) for hb in lhs]

    def hyena_cols(j, lo, hi):
        sl = slice(j * d_hy + lo, j * d_hy + hi)
        parts = proj(hy_lhs, 4 * d_ret + j * d_hy + lo, hi - lo)
        head, tail = parts[0], parts[-1]
        parts[0] = jnp.concatenate([jnp.where(t == 0, 0.0, head[:HALO]), head[HALO:]], axis=0)
        parts[-1] = jnp.concatenate([tail[:GROUP_ROWS], jnp.where(t == nt - 1, 0.0, tail[GROUP_ROWS:])],
                                    axis=0)
        p = jnp.concatenate(parts, axis=0)
        n = p.shape[0]
        y = (sw_ref[0:1, sl] * pltpu.roll(p, 1, 0) + sw_ref[1:2, sl] * p
             + sw_ref[2:3, sl] * pltpu.roll(p, n - 1, 0) + sb_ref[:, sl])
        return y[HALO:HALO + tm]

    def roped_cols(base, cos_ref, sin_ref):
        for rs, p in zip(groups, proj(hbs, base, d_ret)):
            cc = cos_ref[rs, :]
            ss = sin_ref[rs, :]
            for j in range(heads):
                pj = p[:, j * dh:(j + 1) * dh]
                qkvg_ref[rs, base + j * dh:base + (j + 1) * dh] = (
                    pj * cc + pltpu.roll(pj, dh // 2, 1) * ss).astype(BF16)

    half_hy = d_hy // 2
    roped_cols(0, cq_ref, sq_ref)
    x0_ref[:, :half_hy] = hyena_cols(0, 0, half_hy).astype(x0_ref.dtype)
    roped_cols(d_ret, ck_ref, sk_ref)
    x0_ref[:, half_hy:] = hyena_cols(0, half_hy, d_hy).astype(x0_ref.dtype)
    u_ref[:, :half_hy] = (hyena_cols(1, 0, half_hy) * hyena_cols(2, 0, half_hy)).astype(u_ref.dtype)
    for rs, g in zip(groups, proj(hbs, 3 * d_ret, d_ret)):
        qkvg_ref[rs, 3 * d_ret:4 * d_ret] = _silu(g).astype(BF16)
    u_ref[:, half_hy:] = (hyena_cols(1, half_hy, d_hy) * hyena_cols(2, half_hy, d_hy)).astype(u_ref.dtype)
    for rs, v in zip(groups, proj(hbs, 2 * d_ret, d_ret)):
        qkvg_ref[rs, 2 * d_ret:3 * d_ret] = v.astype(BF16)


def _inproj_call(x, mod, norm1, w_in, rope, short_w, short_b, heads, dh, d_hy):
    b, l, d = x.shape
    tm = IN_ROW_TILE
    nt = l // tm
    d_ret = heads * dh
    per_tile = tm // HALO
    tile = lambda i, t: (i, t, 0)
    return pl.pallas_call(
        functools.partial(_inproj_kernel, heads=heads, dh=dh, d_hy=d_hy, nt=nt),
        grid=(b, nt),
        in_specs=[pl.BlockSpec((None, tm, d), tile),
                  pl.BlockSpec((None, HALO, d), lambda i, t: (i, jnp.maximum(t * per_tile - 1, 0), 0)),
                  pl.BlockSpec((None, HALO, d),
                               lambda i, t: (i, jnp.minimum((t + 1) * per_tile, nt * per_tile - 1), 0)),
                  pl.BlockSpec((None, N_MOD, d), lambda i, t: (i, 0, 0)),
                  pl.BlockSpec((1, d), lambda i, t: (0, 0)),
                  _const_spec(w_in.shape),
                  *[pl.BlockSpec((tm, dh), lambda i, t: (t, 0)) for _ in rope],
                  pl.BlockSpec(short_w.shape, lambda i, t: (0, 0)),
                  pl.BlockSpec(short_b.shape, lambda i, t: (0, 0))],
        out_specs=[pl.BlockSpec((None, tm, 4 * d_ret), tile),
                   pl.BlockSpec((None, tm, d_hy), tile),
                   pl.BlockSpec((None, tm, d_hy), tile)],
        out_shape=[jax.ShapeDtypeStruct((b, l, 4 * d_ret), BF16),
                   jax.ShapeDtypeStruct((b, l, d_hy), BF16),
                   jax.ShapeDtypeStruct((b, l, d_hy), BF16)],
        compiler_params=_cparams(("arbitrary", "arbitrary")),
        name="in_proj",
    )(x, x, x, mod, norm1, w_in, *rope, short_w, short_b)


def _ret_kernel(q_ref, k_ref, v_ref, g_ref, sf0_ref, sb0_ref, dec_ref, gain_ref,
                o_ref, tab_s, dmask_s, *, heads, chunk):
    l, dh = q_ref.shape
    nc = l // chunk
    h = pl.program_id(0)

    @pl.when(pl.program_id(1) == 0)
    def _():
        lgf = jnp.log1p(-jnp.exp(dec_ref[pl.ds(h, 1), :]))
        lgb = jnp.log1p(-jnp.exp(dec_ref[pl.ds(h + heads, 1), :]))
        il = lax.broadcasted_iota(jnp.int32, (chunk, dh), 0).astype(F32)
        tab_s[0] = jnp.exp(lgf * (chunk - 1.0 - il))
        tab_s[1] = jnp.exp(lgb * il)
        tab_s[2] = jnp.exp(lgf * (il + 1.0))
        tab_s[3] = jnp.exp(lgb * (chunk - il))
        tab_s[4] = jnp.exp(jnp.broadcast_to(lgf, (chunk, dh)) * float(chunk))
        tab_s[5] = jnp.exp(jnp.broadcast_to(lgb, (chunk, dh)) * float(chunk))
        reps = chunk // dh
        lgf_c = jnp.concatenate([lgf] * reps, axis=1)
        lgb_c = jnp.concatenate([lgb] * reps, axis=1)
        ii = lax.broadcasted_iota(jnp.int32, (chunk, chunk), 0)
        jj = lax.broadcasted_iota(jnp.int32, (chunk, chunk), 1)
        diff = (ii - jj).astype(F32)
        dmask_s[...] = (jnp.where(diff >= 0, jnp.exp(lgf_c * jnp.maximum(diff, 0.0)), 0.0)
                        + jnp.where(diff <= 0, jnp.exp(lgb_c * jnp.maximum(-diff, 0.0)), 0.0))

    tdims = (((0,), (0,)), ((), ()))
    ntdims = (((1,), (1,)), ((), ()))
    rows = lambda n: pl.ds(n * chunk, chunk)

    kvf, kvb = [], []
    for n in range(nc):
        kr = k_ref[rows(n), :].astype(F32)
        vv = v_ref[rows(n), :]
        kvf.append(lax.dot_general((kr * tab_s[0]).astype(BF16), vv, tdims, preferred_element_type=F32))
        kvb.append(lax.dot_general((kr * tab_s[1]).astype(BF16), vv, tdims, preferred_element_type=F32))

    cd_f = tab_s[4, :dh, :]
    cd_b = tab_s[5, :dh, :]
    sf, sb = [None] * nc, [None] * nc
    s = sf0_ref[...]
    for n in range(nc):
        sf[n] = s.astype(BF16)
        s = s * cd_f + kvf[n]
    s = sb0_ref[...]
    for n in reversed(range(nc)):
        sb[n] = s.astype(BF16)
        s = s * cd_b + kvb[n]

    gain = gain_ref[...]
    for n in range(nc):
        qb = q_ref[rows(n), :]
        qr = qb.astype(F32)
        sc = lax.dot_general(qb, k_ref[rows(n), :], ntdims, preferred_element_type=F32)
        lhs = jnp.concatenate([(qr * tab_s[2]).astype(BF16), (qr * tab_s[3]).astype(BF16),
                               (sc * dmask_s[...]).astype(BF16)], axis=1)
        rhs = jnp.concatenate([sf[n], sb[n], v_ref[rows(n), :]], axis=0)
        o = jnp.dot(lhs, rhs, preferred_element_type=F32)
        mu = jnp.mean(o, axis=-1, keepdims=True)
        d = o - mu
        var = jnp.mean(d * d, axis=-1, keepdims=True)
        gg = g_ref[rows(n), :].astype(F32)
        o_ref[rows(n), :] = (d * lax.rsqrt(var + EPS) * gain * gg).astype(o_ref.dtype)


def _ret_call(qkvg, s0, dec, gn_gain, heads, dh):
    b, l, _ = qkvg.shape
    chunk = RET_CHUNK
    seq = lambda off: pl.BlockSpec((None, l, dh), lambda h, i: (i, 0, off + h))
    return pl.pallas_call(
        functools.partial(_ret_kernel, heads=heads, chunk=chunk),
        grid=(heads, b),
        in_specs=[seq(0), seq(heads), seq(2 * heads), seq(3 * heads),
                  pl.BlockSpec((None, None, dh, dh), lambda h, i: (i, h, 0, 0)),
                  pl.BlockSpec((None, None, dh, dh), lambda h, i: (i, heads + h, 0, 0)),
                  pl.BlockSpec(dec.shape, lambda h, i: (0, 0)),
                  pl.BlockSpec((1, dh), lambda h, i: (0, h))],
        out_specs=pl.BlockSpec((None, l, dh), lambda h, i: (i, 0, h)),
        out_shape=jax.ShapeDtypeStruct((b, l, heads * dh), BF16),
        scratch_shapes=[pltpu.VMEM((6, chunk, dh), F32), pltpu.VMEM((chunk, chunk), F32)],
        compiler_params=_cparams(("arbitrary", "arbitrary")),
        name="retention",
    )(qkvg, qkvg, qkvg, qkvg, s0, s0, dec, gn_gain)


def _filt_time_kernel(emb_ref, w1_ref, b1_ref, w2_ref, b2_ref, w3_ref, b3_ref, w4f_ref, w4b_ref,
                      fr_ref, dl_ref, cast_ref, c_ref, wm_ref, bm_ref,
                      kern_ref, s_ref, cast_out_ref, mod_ref, *, seq_len):
    cast_out_ref[...] = cast_ref[...].astype(cast_out_ref.dtype)
    mod_ref[...] = _dot3(_silu(c_ref[...]), wm_ref[...]) + bm_ref[...]
    i = pl.program_id(0)
    rows, c = kern_ref.shape[1], kern_ref.shape[2]
    fr = fr_ref[...]
    hdot = _dot3
    z = jnp.sin(fr * (hdot(emb_ref[...], w1_ref[...]) + b1_ref[...]))
    z = jnp.sin(fr * (hdot(z, w2_ref[...]) + b2_ref[...]))
    z = jnp.sin(fr * (hdot(z, w3_ref[...]) + b3_ref[...]))
    pos = (i * rows + lax.broadcasted_iota(jnp.int32, (rows, c), 0)).astype(F32)
    inv = 1.0 / (seq_len - 1.0)
    adl = dl_ref[...]
    hf = hdot(z, w4f_ref[...]) * jnp.exp(-(pos * inv) * adl)
    hr = hdot(z, w4b_ref[...]) * jnp.exp(-((seq_len - pos) * inv) * adl)
    hr = jnp.where(pos == 0.0, 0.0, hr)
    kern_ref[0] = hf
    kern_ref[1] = hr
    part = jnp.sum(jnp.abs(hf) + jnp.abs(hr), axis=0, keepdims=True)

    @pl.when(i == 0)
    def _():
        s_ref[...] = part

    @pl.when(i != 0)
    def _():
        s_ref[...] += part


def _filt_time_call(emb2, w1, b1, w2, b2, w3, b3, w4f, w4b, freq, absdelta, seq_len, cast_w,
                    c_rows, w_mod, b_mod):
    c = absdelta.shape[1]
    rows = FILT_ROWS
    steps = seq_len // rows
    assert cast_w.shape[0] % (steps * 2 * SUBLANES) == 0
    slab = pl.BlockSpec((cast_w.shape[0] // steps, cast_w.shape[1]), lambda i: (i, 0))
    mrows, d = c_rows.shape
    n_mod = w_mod.shape[1]
    assert n_mod % (steps * LANES) == 0
    tn = n_mod // steps
    small = lambda a: pl.BlockSpec(a.shape, lambda i: (0,) * a.ndim)
    return pl.pallas_call(
        functools.partial(_filt_time_kernel, seq_len=seq_len),
        grid=(steps,),
        in_specs=[pl.BlockSpec((rows, emb2.shape[1]), lambda i: (i, 0)),
                  small(w1), small(b1), small(w2), small(b2), small(w3), small(b3), small(w4f), small(w4b),
                  small(freq), small(absdelta), slab,
                  small(c_rows),
                  pl.BlockSpec((d, tn), lambda i: (0, i)),
                  pl.BlockSpec((1, tn), lambda i: (0, i))],
        out_specs=[pl.BlockSpec((2, rows, c), lambda i: (0, i, 0)),
                   pl.BlockSpec((1, c), lambda i: (0, 0)), slab,
                   pl.BlockSpec((mrows, tn), lambda i: (0, i))],
        out_shape=[jax.ShapeDtypeStruct((2, seq_len, c), F32),
                   jax.ShapeDtypeStruct((1, c), F32),
                   jax.ShapeDtypeStruct(cast_w.shape, BF16),
                   jax.ShapeDtypeStruct((mrows, n_mod), F32)],
        compiler_params=_cparams(("arbitrary",)),
        name="filt_time",
    )(emb2, w1, b1, w2, b2, w3, b3, w4f, w4b, freq, absdelta, cast_w, c_rows, w_mod, b_mod)


@functools.lru_cache(maxsize=None)
def _fft_tables(seq_len):
    n = 2 * seq_len
    n1s = FFT_N1
    n2s = n // n1s
    half = n1s // 2
    k1n = half + 1
    n1 = np.arange(half)
    k1 = np.arange(k1n)
    th = 2.0 * np.pi * (np.outer(k1, n1) % n1s) / n1s
    herm = np.where((k1 == 0) | (k1 == half), 1.0, 2.0)
    fa = np.concatenate([np.cos(th), -np.sin(th)[1:half]], axis=0)
    sgn = np.concatenate([(-1.0) ** k1, (-1.0) ** k1[1:half]])
    fai = np.concatenate([np.cos(th) * herm[:, None], (-np.sin(th) * herm[:, None])[1:half]], axis=0).T / n
    eye = np.eye(SUBLANES)
    fk = np.kron(fa, eye)
    fks = np.kron(fa * sgn[:, None], eye)
    fki = np.kron(fai, eye)
    k2 = np.arange(n2s)
    n2 = np.arange(n2s)
    m = np.zeros((k1n, 2 * n2s, 2 * n2s))
    for a in range(k1n):
        ang = 2.0 * np.pi * (np.outer(a + n1s * k2, n2) % n) / n
        gr, gi = np.cos(ang), -np.sin(ang)
        m[a] = np.block([[gr, -gi], [gi, gr]])
    groups = np.arange(n2s).reshape(-1, SUBLANES)
    perm = np.concatenate([np.concatenate([g, n2s + g]) for g in groups])
    m = m[:, perm, :]
    mt = np.transpose(m, (0, 2, 1))
    return dict(fk=fk, fkf=np.concatenate([fk, fks], axis=1), fki=fki, m=m, mt=mt,
                k1n=k1n, n2s=n2s, half=half)


def _stage_n1(src_refs, mat, dst_ref, n2s):
    rows = dst_ref.shape[0]
    pair = 2 * SUBLANES
    for jj in range(n2s // pair):
        blks = [r[:, jj * pair:(jj + 1) * pair, :].astype(F32) for r in src_refs]
        for hf in range(2):
            lo = hf * SUBLANES
            parts = [b[:, lo:lo + SUBLANES, :] for b in blks]
            xg = jnp.concatenate([p.reshape(p.shape[0] * SUBLANES, p.shape[2]) for p in parts], axis=0)
            a = jnp.dot(mat, xg.astype(BF16), preferred_element_type=F32)
            sl = slice(jj * pair + lo, jj * pair + lo + SUBLANES)
            dst_ref[:, sl, :] = a.reshape(rows, SUBLANES, a.shape[1])


def _n2_input(a_s, k1, half, n2s):
    if 0 < k1 < half:
        return jnp.concatenate([a_s[k1], a_s[half + k1]], axis=0)
    return a_s[k1]


def _filt_spec_kernel(kern_ref, s_ref, fkf_ref, m_ref, kf_ref, a_s, *, k1n, n2s):
    half = k1n - 1
    _stage_n1([kern_ref.at[0], kern_ref.at[1]], fkf_ref[...], a_s, n2s)
    inv = 1.0 / (s_ref[...] + EPS)
    for k1 in range(k1n):
        a = _n2_input(a_s, k1, half, n2s).astype(BF16)
        kf_ref[k1] = jnp.dot(m_ref[k1, :, :a.shape[0]], a, preferred_element_type=F32) * inv


def _filt_spec_call(kern4, s, tabs):
    _, half, n2s, c = kern4.shape
    ct = HY_CT
    k1n = tabs["k1n"]
    return pl.pallas_call(
        functools.partial(_filt_spec_kernel, k1n=k1n, n2s=n2s),
        grid=(c // ct,),
        in_specs=[pl.BlockSpec((2, half, n2s, ct), lambda j: (0, 0, 0, j)),
                  pl.BlockSpec((1, ct), lambda j: (0, j)),
                  _const_spec(tabs["fkf"].shape), _const_spec(tabs["m"].shape)],
        out_specs=pl.BlockSpec((k1n, 2 * n2s, ct), lambda j: (0, 0, j)),
        out_shape=jax.ShapeDtypeStruct((k1n, 2 * n2s, c), F32),
        scratch_shapes=[pltpu.VMEM((2 * half, n2s, ct), F32)],
        compiler_params=_cparams(("arbitrary",)),
        name="filt_spec",
    )(kern4, s, tabs["fkf"], tabs["m"])


def _hyena_kernel(u_ref, x0_ref, bias_ref, kf_ref, fk_ref, fki_ref, m_ref, mt_ref, *rest, k1n, n2s, n_cast):
    cast_in, (o_ref, *cast_out), (a_s,) = rest[:n_cast], rest[n_cast:2 * n_cast + 1], rest[2 * n_cast + 1:]
    for src, dst in zip(cast_in, cast_out):
        dst[...] = src[...].astype(dst.dtype)
    half = k1n - 1
    ct = u_ref.shape[2]
    _stage_n1([u_ref], fk_ref[...], a_s, n2s)

    for k1 in range(k1n):
        a = _n2_input(a_s, k1, half, n2s).astype(BF16)
        x = jnp.dot(m_ref[k1, :, :a.shape[0]], a, preferred_element_type=F32)
        x = x.reshape(n2s // SUBLANES, 2 * SUBLANES, ct)
        kf = kf_ref[k1].reshape(n2s // SUBLANES, 2 * SUBLANES, ct)
        xr, xi = x[:, :SUBLANES], x[:, SUBLANES:]
        kr, ki = kf[:, :SUBLANES], kf[:, SUBLANES:]
        y = jnp.concatenate([xr * kr - xi * ki, xr * ki + xi * kr], axis=1)
        y = y.reshape(2 * n2s, ct).astype(BF16)
        if 0 < k1 < half:
            z = jnp.dot(mt_ref[k1], y, preferred_element_type=F32)
            a_s[k1] = z[:n2s]
            a_s[half + k1] = z[n2s:]
        else:
            a_s[k1] = jnp.dot(mt_ref[k1, :n2s, :], y, preferred_element_type=F32)

    fki = fki_ref[...]
    bias = bias_ref[...]
    pair = 2 * SUBLANES
    for j in range(n2s // pair):
        ys = []
        for jj in range(2):
            sl = slice(j * pair + jj * SUBLANES, j * pair + (jj + 1) * SUBLANES)
            zg = a_s[:, sl, :].reshape(2 * half * SUBLANES, ct)
            ys.append(jnp.dot(fki, zg.astype(BF16), preferred_element_type=F32).reshape(half, SUBLANES, ct))
        y = jnp.concatenate(ys, axis=1)
        sl = slice(j * pair, (j + 1) * pair)
        o_ref[:, sl, :] = (x0_ref[:, sl, :].astype(F32)
                           * (y + u_ref[:, sl, :].astype(F32) * bias)).astype(o_ref.dtype)


def _hyena_call(u4, x04, bias, kf, tabs, cast_weights):
    b, half, n2s, c = u4.shape
    ct = HY_CT
    k1n = tabs["k1n"]
    nct = c // ct
    steps = nct * b
    seq = pl.BlockSpec((None, half, n2s, ct), lambda j, i: (i, 0, 0, j))
    slabs = [pl.BlockSpec((w.shape[0] // steps, w.shape[1]), lambda j, i: (j * b + i, 0)) for w in cast_weights]
    assert all(w.shape[0] % (steps * 2 * SUBLANES) == 0 for w in cast_weights)
    outs = pl.pallas_call(
        functools.partial(_hyena_kernel, k1n=k1n, n2s=n2s, n_cast=len(cast_weights)),
        grid=(nct, b),
        in_specs=[seq, seq,
                  pl.BlockSpec((1, ct), lambda j, i: (0, j)),
                  pl.BlockSpec((k1n, 2 * n2s, ct), lambda j, i: (0, 0, j)),
                  _const_spec(tabs["fk"].shape), _const_spec(tabs["fki"].shape),
                  _const_spec(tabs["m"].shape), _const_spec(tabs["mt"].shape), *slabs],
        out_specs=[seq, *slabs],
        out_shape=[jax.ShapeDtypeStruct((b, half, n2s, c), BF16),
                   *[jax.ShapeDtypeStruct(w.shape, BF16) for w in cast_weights]],
        scratch_shapes=[pltpu.VMEM((2 * half, n2s, ct), F32)],
        compiler_params=_cparams(("arbitrary", "arbitrary")),
        name="hyena",
    )(u4, x04, bias, kf, tabs["fk"], tabs["fki"], tabs["m"], tabs["mt"], *cast_weights)
    return outs[0], outs[1:]


def _out_ffn_kernel(ret_ref, hy_ref, x_ref, mod_ref, hg_ref, n2_ref, fn_ref, wo_ref, wgu_ref, wd_ref,
                    o_ref, *, d_ret, d_ff, ff_bounds, row_parts):
    rows = x_ref.shape[0] // row_parts
    groups = [slice(r * rows, (r + 1) * rows) for r in range(row_parts)]
    gain2 = n2_ref[...] * (1.0 + mod_ref[4:5, :])
    x1s, h2s = [], []
    for rs in groups:
        hy = hy_ref[rs, :].astype(F32)
        hms = jnp.mean(hy * hy, axis=-1, keepdims=True)
        hyn = (hy * lax.rsqrt(hms + EPS) * hg_ref[...]).astype(BF16)
        mix = jnp.dot(ret_ref[rs, :], wo_ref[:d_ret, :], preferred_element_type=F32)
        mix = mix + jnp.dot(hyn, wo_ref[d_ret:, :], preferred_element_type=F32)
        x1s.append(x_ref[rs, :] + mod_ref[2:3, :] * mix)
    for x1 in x1s:
        ms = jnp.mean(x1 * x1, axis=-1, keepdims=True)
        h2s.append((x1 * lax.rsqrt(ms + EPS) * gain2 + mod_ref[3:4, :]).astype(BF16))
    accs = [None] * row_parts
    for lo, hi in zip(ff_bounds[:-1], ff_bounds[1:]):
        acts = []
        for h2 in h2s:
            g = jnp.dot(h2, wgu_ref[:, lo:hi], preferred_element_type=F32)
            u = jnp.dot(h2, wgu_ref[:, d_ff + lo:d_ff + hi], preferred_element_type=F32)
            acts.append((_silu(g) * u).astype(BF16))
        for r, a in enumerate(acts):
            d = jnp.dot(a, wd_ref[lo:hi, :], preferred_element_type=F32)
            accs[r] = d if accs[r] is None else accs[r] + d
    for rs, x1, acc in zip(groups, x1s, accs):
        x2 = x1 + mod_ref[5:6, :] * acc
        ms2 = jnp.mean(x2 * x2, axis=-1, keepdims=True)
        o_ref[rs, :] = x2 * lax.rsqrt(ms2 + EPS) * fn_ref[...]


def _ff_bounds(d_ff, parts=2):
    tiles = d_ff // MXU_WIDTH
    assert tiles * MXU_WIDTH == d_ff
    cuts = [-(-tiles * p // parts) for p in range(parts + 1)]
    return tuple(c * MXU_WIDTH for c in cuts)


def _out_ffn_call(ret, hy, x, mod, hy_gain, norm2, final_norm, w_out, w_gu, w_down):
    b, l, d = x.shape
    d_ret = ret.shape[2]
    d_hy = hy.shape[2]
    d_ff = w_down.shape[0]
    tm = ROW_TILE
    return pl.pallas_call(
        functools.partial(_out_ffn_kernel, d_ret=d_ret, d_ff=d_ff, ff_bounds=_ff_bounds(d_ff),
                          row_parts=tm // GROUP_ROWS),
        grid=(b, l // tm),
        in_specs=[pl.BlockSpec((None, tm, d_ret), lambda i, t: (i, t, 0)),
                  pl.BlockSpec((None, tm, d_hy), lambda i, t: (i, t, 0)),
                  pl.BlockSpec((None, tm, d), lambda i, t: (i, t, 0)),
                  pl.BlockSpec((None, N_MOD, d), lambda i, t: (i, 0, 0)),
                  pl.BlockSpec((1, d_hy), lambda i, t: (0, 0)),
                  pl.BlockSpec((1, d), lambda i, t: (0, 0)),
                  pl.BlockSpec((1, d), lambda i, t: (0, 0)),
                  _const_spec(w_out.shape), _const_spec(w_gu.shape), _const_spec(w_down.shape)],
        out_specs=pl.BlockSpec((None, tm, d), lambda i, t: (i, t, 0)),
        out_shape=jax.ShapeDtypeStruct((b, l, d), x.dtype),
        compiler_params=_cparams(("arbitrary", "arbitrary")),
        name="out_ffn",
    )(ret, hy, x, mod, hy_gain, norm2, final_norm, w_out, w_gu, w_down)


@functools.lru_cache(maxsize=None)
def _rope_tables(seq_len, dh):
    n = dh // 4
    t = np.arange(seq_len)
    inv = ROPE_BASE ** (-np.arange(n, dtype=np.float64) / n)
    ang = np.concatenate([(t // GRID_W)[:, None] * inv, (t % GRID_W)[:, None] * inv], axis=-1)
    cc = np.concatenate([np.cos(ang), np.cos(ang)], axis=-1)
    ss = np.concatenate([-np.sin(ang), np.sin(ang)], axis=-1)
    k_scale = dh ** -0.5
    return tuple(t.astype(np.float32) for t in (cc, ss, cc * k_scale, ss * k_scale))


@functools.lru_cache(maxsize=None)
def _filter_tables(seq_len, emb_dim, emb_pad, channels):
    t = np.linspace(0.0, 1.0, seq_len)[:, None]
    bands = (emb_dim - 1) // 2
    f = np.linspace(1e-4, bands - 1, bands)[None, :]
    wpos = 2.0 * np.pi * np.arange(seq_len)[:, None] / seq_len
    emb = np.concatenate([t, np.cos(f * wpos), -np.sin(f * wpos)], axis=-1)
    emb = np.pad(emb, ((0, 0), (0, emb_pad - emb_dim)))
    emb_rev = np.concatenate([emb[:1], emb[:0:-1]], axis=0)
    emb2 = np.concatenate([emb, emb_rev], axis=1)
    max_decay = math.log(FILTER_DECAY_TARGET) / FILTER_DECAY_FAST
    min_decay = math.log(FILTER_DECAY_TARGET) / FILTER_DECAY_SLOW
    absdelta = np.abs(np.linspace(min_decay, max_decay, channels))[None, :]
    return emb2.astype(np.float32), absdelta.astype(np.float32)


def _block_diag2(w):
    z = jnp.zeros_like(w)
    return jnp.concatenate([jnp.concatenate([w, z], axis=1), jnp.concatenate([z, w], axis=1)], axis=0)


def kernel(x, c, ctx, c_ctx, w_mod, b_mod, norm1, norm2, w_in, ret_decay, ret_gn_gain, hy_short_w,
           hy_short_b, hy_w1, hy_b1, hy_w2, hy_b2, hy_w3, hy_b3, hy_w4, hy_freq, hy_bias, hy_out_norm,
           w_out, w_gate_up, w_down, final_norm):
    b, seq_len, d = x.shape
    assert w_mod.shape[0] == 1, "single-layer block"
    heads = RET_HEADS
    d_ret = ret_gn_gain.shape[1]
    dh = d_ret // heads
    d_hy = hy_bias.shape[1]
    assert dh == LANES and d_hy % HY_CT == 0 and seq_len % (FFT_N1 // 2) == 0

    rows = -(-(b + 1) // SUBLANES) * SUBLANES
    c_rows = jnp.zeros((rows, d), F32).at[:b].set(c).at[b].set(c_ctx)

    tabs = dict(_fft_tables(seq_len))
    for name in ("fk", "fkf", "fki", "m", "mt"):
        tabs[name] = jnp.asarray(tabs[name], dtype=F32).astype(BF16)
    half, n2s = tabs["half"], tabs["n2s"]
    emb_dim = hy_w1.shape[1]
    emb_pad = -(-emb_dim // SUBLANES) * SUBLANES
    emb2, absdelta = (jnp.asarray(t) for t in _filter_tables(seq_len, emb_dim, emb_pad, d_hy))
    w1p = jnp.pad(hy_w1[0], ((0, emb_pad - emb_dim), (0, 0)))
    two = lambda a: jnp.concatenate([a, a], axis=1)
    w4 = hy_w4[0]
    zero4 = jnp.zeros((w4.shape[0], d_hy), F32)
    kern, ksum, w_in_b, mod_all = _filt_time_call(
        emb2, _block_diag2(w1p), two(hy_b1), _block_diag2(hy_w2[0]), two(hy_b2), _block_diag2(hy_w3[0]),
        two(hy_b3), jnp.concatenate([w4[:, :d_hy], zero4], axis=0), jnp.concatenate([zero4, w4[:, d_hy:]], axis=0),
        two(hy_freq), absdelta, seq_len, w_in[0], c_rows, w_mod[0], b_mod[0][None, :])
    mod = mod_all[:b].reshape(b, N_MOD, d)
    mod_c = mod_all[b].reshape(N_MOD, d)
    kf = _filt_spec_call(kern.reshape(2, half, n2s, d_hy), ksum, tabs)

    dec = jnp.broadcast_to(ret_decay[0].reshape(2 * heads, 1), (2 * heads, LANES))
    n1g = norm1[0][None, :]
    s0 = _ctx_call(ctx, n1g, mod_c[0:1], mod_c[1:2], w_in_b, dec, heads, dh)
    rope = [jnp.asarray(t) for t in _rope_tables(seq_len, dh)]
    qkvg, x0c, u = _inproj_call(x, mod, n1g, w_in_b, rope, hy_short_w[0], hy_short_b, heads, dh, d_hy)
    ret = _ret_call(qkvg, s0, dec, ret_gn_gain, heads, dh)

    hy, (w_out_b, w_gu_b, w_down_b) = _hyena_call(
        u.reshape(b, half, n2s, d_hy), x0c.reshape(b, half, n2s, d_hy), hy_bias, kf, tabs,
        [w_out[0], w_gate_up[0], w_down[0]])
    hy = hy.reshape(b, seq_len, d_hy)

    return _out_ffn_call(ret, hy, x, mod, hy_out_norm, norm2[0][None, :], final_norm[None, :],
                         w_out_b, w_gu_b, w_down_b)
```

```python
import functools
import math

import jax
import jax.numpy as jnp
import numpy as np
from jax import lax
from jax.experimental import pallas as pl
from jax.experimental.pallas import tpu as pltpu

F32 = jnp.float32
BF16 = jnp.bfloat16

RET_HEADS = 4
GRID_W = 64
ROPE_BASE = 10000.0
N_MOD = 6
HYENA_PROJ = 3
FILTER_DECAY_FAST = 0.3
FILTER_DECAY_SLOW = 1.5
FILTER_DECAY_TARGET = 1e-2
EPS = 1e-6

LANES = 128
SUBLANES = 8
MXU_WIDTH = 256
VMEM_LIMIT_BYTES = 56 * 1024 * 1024

RET_CHUNK = 256
RET_HEADS_PER_STEP = 2
ROW_TILE = 1024
IN_ROW_TILE = 1024
GROUP_ROWS = 256
CTX_BATCHES = 4
HALO = 16
HY_CT = 256
FILT_ROWS = 1024
FFT_N1 = 64


def _silu(v):
    return v / (1.0 + jnp.exp(-v))


def _dot3(a, b):
    a_hi = a.astype(BF16)
    b_hi = b.astype(BF16)
    a_lo = (a - a_hi.astype(F32)).astype(BF16)
    b_lo = (b - b_hi.astype(F32)).astype(BF16)
    dot = functools.partial(jnp.dot, preferred_element_type=F32)
    return dot(a_hi, b_hi) + (dot(a_hi, b_lo) + dot(a_lo, b_hi))


def _cparams(sem, vmem=VMEM_LIMIT_BYTES):
    return pltpu.CompilerParams(dimension_semantics=sem, vmem_limit_bytes=vmem)


def _const_spec(shape):
    nd = len(shape)
    return pl.BlockSpec(shape, lambda *_: (0,) * nd, pipeline_mode=pl.Buffered(1))


def _ctx_kernel(ctx_ref, n1_ref, sh_ref, sc_ref, wk_ref, wv_ref, dec_ref, s_ref, *, heads, dh):
    nb, n_ctx, d = ctx_ref.shape
    xc = ctx_ref[...].reshape(nb * n_ctx, d)
    ms = jnp.mean(xc * xc, axis=-1, keepdims=True)
    hc = ((xc * lax.rsqrt(ms + EPS) * n1_ref[...]) * (1.0 + sc_ref[...]) + sh_ref[...]).astype(BF16)
    k = jnp.dot(hc, wk_ref[...], preferred_element_type=F32)
    v = jnp.dot(hc, wv_ref[...], preferred_element_type=F32)
    lg = jnp.log1p(-jnp.exp(dec_ref[...]))
    pos = lax.broadcasted_iota(jnp.int32, (n_ctx, dh), 0).astype(F32)
    k_scale = dh ** -0.5
    tdims = (((0,), (0,)), ((), ()))
    for h in range(heads):
        wf = jnp.exp(lg[h:h + 1, :] * (n_ctx - 1.0 - pos)) * k_scale
        wb = jnp.exp(lg[heads + h:heads + h + 1, :] * pos) * k_scale
        for i in range(nb):
            rows = slice(i * n_ctx, (i + 1) * n_ctx)
            kh = k[rows, h * dh:(h + 1) * dh]
            vh = v[rows, h * dh:(h + 1) * dh].astype(BF16)
            s_ref[i, h] = lax.dot_general((kh * wf).astype(BF16), vh, tdims, preferred_element_type=F32)
            s_ref[i, heads + h] = lax.dot_general((kh * wb).astype(BF16), vh, tdims,
                                                  preferred_element_type=F32)


def _ctx_call(ctx, norm1, shift_c, scale_c, w_in, dec, heads, dh):
    b, n_ctx, d = ctx.shape
    d_ret = heads * dh
    nb = CTX_BATCHES
    assert b % nb == 0
    return pl.pallas_call(
        functools.partial(_ctx_kernel, heads=heads, dh=dh),
        grid=(b // nb,),
        in_specs=[pl.BlockSpec((nb, n_ctx, d), lambda i: (i, 0, 0)),
                  pl.BlockSpec((1, d), lambda i: (0, 0)),
                  pl.BlockSpec((1, d), lambda i: (0, 0)),
                  pl.BlockSpec((1, d), lambda i: (0, 0)),
                  pl.BlockSpec((d, d_ret), lambda i: (0, 1)),
                  pl.BlockSpec((d, d_ret), lambda i: (0, 2)),
                  pl.BlockSpec(dec.shape, lambda i: (0, 0))],
        out_specs=pl.BlockSpec((nb, 2 * heads, dh, dh), lambda i: (i, 0, 0, 0)),
        out_shape=jax.ShapeDtypeStruct((b, 2 * heads, dh, dh), F32),
        compiler_params=_cparams(("arbitrary",)),
        name="ctx_state",
    )(ctx, norm1, shift_c, scale_c, w_in, w_in, dec)


def _inproj_kernel(x_ref, xp_ref, xn_ref, mod_ref, n1_ref, w_ref, cq_ref, sq_ref, ck_ref, sk_ref,
                   sw_ref, sb_ref, qkvg_ref, x0_ref, u_ref, *, heads, dh, d_hy, nt):
    t = pl.program_id(1)
    tm = x_ref.shape[0]
    d_ret = heads * dh
    gain = n1_ref[...] * (1.0 + mod_ref[1:2, :])

    def normed(x):
        ms = jnp.mean(x * x, axis=-1, keepdims=True)
        return x * lax.rsqrt(ms + EPS) * gain + mod_ref[0:1, :]

    groups = [slice(r, r + GROUP_ROWS) for r in range(0, tm, GROUP_ROWS)]
    hs = [normed(x_ref[rs, :]) for rs in groups]
    hbs = [h.astype(BF16) for h in hs]
    hy_lhs = list(hbs)
    hy_lhs[0] = jnp.concatenate([normed(xp_ref[...]), hs[0]], axis=0).astype(BF16)
    hy_lhs[-1] = jnp.concatenate([hs[-1], normed(xn_ref[...])], axis=0).astype(BF16)

    def proj(lhs, base, width):
        return [jnp.dot(hb, w_ref[:, base:base + width], preferred_element_type=F32) for hb in lhs]

    def hyena_cols(j, lo, hi):
        sl = slice(j * d_hy + lo, j * d_hy + hi)
        parts = proj(hy_lhs, 4 * d_ret + j * d_hy + lo, hi - lo)
        head, tail = parts[0], parts[-1]
        parts[0] = jnp.concatenate([jnp.where(t == 0, 0.0, head[:HALO]), head[HALO:]], axis=0)
        parts[-1] = jnp.concatenate([tail[:GROUP_ROWS], jnp.where(t == nt - 1, 0.0, tail[GROUP_ROWS:])],
                                    axis=0)
        p = jnp.concatenate(parts, axis=0)
        n = p.shape[0]
        y = (sw_ref[0:1, sl] * pltpu.roll(p, 1, 0) + sw_ref[1:2, sl] * p
             + sw_ref[2:3, sl] * pltpu.roll(p, n - 1, 0) + sb_ref[:, sl])
        return y[HALO:HALO + tm]

    def roped_cols(base, cos_ref, sin_ref):
        for rs, p in zip(groups, proj(hbs, base, d_ret)):
            cc = cos_ref[rs, :]
            ss = sin_ref[rs, :]
            for j in range(heads):
                pj = p[:, j * dh:(j + 1) * dh]
                qkvg_ref[rs, base + j * dh:base + (j + 1) * dh] = (
                    pj * cc + pltpu.roll(pj, dh // 2, 1) * ss).astype(BF16)

    half_hy = d_hy // 2
    roped_cols(0, cq_ref, sq_ref)
    x0_ref[:, :half_hy] = hyena_cols(0, 0, half_hy).astype(x0_ref.dtype)
    roped_cols(d_ret, ck_ref, sk_ref)
    x0_ref[:, half_hy:] = hyena_cols(0, half_hy, d_hy).astype(x0_ref.dtype)
    u_ref[:, :half_hy] = (hyena_cols(1, 0, half_hy) * hyena_cols(2, 0, half_hy)).astype(u_ref.dtype)
    for rs, g in zip(groups, proj(hbs, 3 * d_ret, d_ret)):
        qkvg_ref[rs, 3 * d_ret:4 * d_ret] = _silu(g).astype(BF16)
    u_ref[:, half_hy:] = (hyena_cols(1, half_hy, d_hy) * hyena_cols(2, half_hy, d_hy)).astype(u_ref.dtype)
    for rs, v in zip(groups, proj(hbs, 2 * d_ret, d_ret)):
        qkvg_ref[rs, 2 * d_ret:3 * d_ret] = v.astype(BF16)


def _inproj_call(x, mod, norm1, w_in, rope, short_w, short_b, heads, dh, d_hy):
    b, l, d = x.shape
    tm = IN_ROW_TILE
    nt = l // tm
    d_ret = heads * dh
    per_tile = tm // HALO
    tile = lambda i, t: (i, t, 0)
    return pl.pallas_call(
        functools.partial(_inproj_kernel, heads=heads, dh=dh, d_hy=d_hy, nt=nt),
        grid=(b, nt),
        in_specs=[pl.BlockSpec((None, tm, d), tile),
                  pl.BlockSpec((None, HALO, d), lambda i, t: (i, jnp.maximum(t * per_tile - 1, 0), 0)),
                  pl.BlockSpec((None, HALO, d),
                               lambda i, t: (i, jnp.minimum((t + 1) * per_tile, nt * per_tile - 1), 0)),
                  pl.BlockSpec((None, N_MOD, d), lambda i, t: (i, 0, 0)),
                  pl.BlockSpec((1, d), lambda i, t: (0, 0)),
                  _const_spec(w_in.shape),
                  *[pl.BlockSpec((tm, dh), lambda i, t: (t, 0)) for _ in rope],
                  pl.BlockSpec(short_w.shape, lambda i, t: (0, 0)),
                  pl.BlockSpec(short_b.shape, lambda i, t: (0, 0))],
        out_specs=[pl.BlockSpec((None, tm, 4 * d_ret), tile),
                   pl.BlockSpec((None, tm, d_hy), tile),
                   pl.BlockSpec((None, tm, d_hy), tile)],
        out_shape=[jax.ShapeDtypeStruct((b, l, 4 * d_ret), BF16),
                   jax.ShapeDtypeStruct((b, l, d_hy), BF16),
                   jax.ShapeDtypeStruct((b, l, d_hy), BF16)],
        compiler_params=_cparams(("arbitrary", "arbitrary")),
        name="in_proj",
    )(x, x, x, mod, norm1, w_in, *rope, short_w, short_b)


def _ret_kernel(q_ref, k_ref, v_ref, g_ref, sf0_ref, sb0_ref, dec_ref, gain_ref,
                o_ref, tab_s, dmask_s, *, heads, chunk, hpg):
    l = q_ref.shape[0]
    dh = q_ref.shape[1] // hpg
    nc = l // chunk
    hp = pl.program_id(0)
    hh_range = range(hpg)

    @pl.when(pl.program_id(1) == 0)
    def _():
        il = lax.broadcasted_iota(jnp.int32, (chunk, dh), 0).astype(F32)
        ii = lax.broadcasted_iota(jnp.int32, (chunk, chunk), 0)
        jj = lax.broadcasted_iota(jnp.int32, (chunk, chunk), 1)
        diff = (ii - jj).astype(F32)
        reps = chunk // dh
        for hh in hh_range:
            h = hp * hpg + hh
            lgf = jnp.log1p(-jnp.exp(dec_ref[pl.ds(h, 1), :]))
            lgb = jnp.log1p(-jnp.exp(dec_ref[pl.ds(h + heads, 1), :]))
            tab_s[hh, 0] = jnp.exp(lgf * (chunk - 1.0 - il))
            tab_s[hh, 1] = jnp.exp(lgb * il)
            tab_s[hh, 2] = jnp.exp(lgf * (il + 1.0))
            tab_s[hh, 3] = jnp.exp(lgb * (chunk - il))
            tab_s[hh, 4] = jnp.exp(jnp.broadcast_to(lgf, (chunk, dh)) * float(chunk))
            tab_s[hh, 5] = jnp.exp(jnp.broadcast_to(lgb, (chunk, dh)) * float(chunk))
            lgf_c = jnp.concatenate([lgf] * reps, axis=1)
            lgb_c = jnp.concatenate([lgb] * reps, axis=1)
            dmask_s[hh] = (jnp.where(diff >= 0, jnp.exp(lgf_c * jnp.maximum(diff, 0.0)), 0.0)
                           + jnp.where(diff <= 0, jnp.exp(lgb_c * jnp.maximum(-diff, 0.0)), 0.0))

    tdims = (((0,), (0,)), ((), ()))
    ntdims = (((1,), (1,)), ((), ()))
    rows = lambda n: pl.ds(n * chunk, chunk)
    cols = lambda hh: slice(hh * dh, (hh + 1) * dh)

    kvf = [[None] * nc for _ in hh_range]
    kvb = [[None] * nc for _ in hh_range]
    for n in range(nc):
        for hh in hh_range:
            kr = k_ref[rows(n), cols(hh)].astype(F32)
            vv = v_ref[rows(n), cols(hh)]
            kvf[hh][n] = lax.dot_general((kr * tab_s[hh, 0]).astype(BF16), vv, tdims,
                                         preferred_element_type=F32)
            kvb[hh][n] = lax.dot_general((kr * tab_s[hh, 1]).astype(BF16), vv, tdims,
                                         preferred_element_type=F32)

    sf = [[None] * nc for _ in hh_range]
    sb = [[None] * nc for _ in hh_range]
    for hh in hh_range:
        cd_f = tab_s[hh, 4, :dh, :]
        cd_b = tab_s[hh, 5, :dh, :]
        s = sf0_ref[hh]
        for n in range(nc):
            sf[hh][n] = s.astype(BF16)
            s = s * cd_f + kvf[hh][n]
        s = sb0_ref[hh]
        for n in reversed(range(nc)):
            sb[hh][n] = s.astype(BF16)
            s = s * cd_b + kvb[hh][n]

    for n in range(nc):
        for hh in hh_range:
            qb = q_ref[rows(n), cols(hh)]
            qr = qb.astype(F32)
            sc = lax.dot_general(qb, k_ref[rows(n), cols(hh)], ntdims, preferred_element_type=F32)
            lhs = jnp.concatenate([(qr * tab_s[hh, 2]).astype(BF16), (qr * tab_s[hh, 3]).astype(BF16),
                                   (sc * dmask_s[hh]).astype(BF16)], axis=1)
            rhs = jnp.concatenate([sf[hh][n], sb[hh][n], v_ref[rows(n), cols(hh)]], axis=0)
            o = jnp.dot(lhs, rhs, preferred_element_type=F32)
            mu = jnp.mean(o, axis=-1, keepdims=True)
            d = o - mu
            var = jnp.mean(d * d, axis=-1, keepdims=True)
            gg = g_ref[rows(n), cols(hh)].astype(F32)
            o_ref[rows(n), cols(hh)] = (d * lax.rsqrt(var + EPS) * gain_ref[:, cols(hh)]
                                        * gg).astype(o_ref.dtype)


def _ret_call(qkvg, s0, dec, gn_gain, heads, dh):
    b, l, _ = qkvg.shape
    chunk = RET_CHUNK
    hpg = RET_HEADS_PER_STEP
    assert heads % hpg == 0
    ng = heads // hpg
    w = hpg * dh
    seq = lambda off: pl.BlockSpec((None, l, w), lambda h, i: (i, 0, off // hpg + h))
    return pl.pallas_call(
        functools.partial(_ret_kernel, heads=heads, chunk=chunk, hpg=hpg),
        grid=(ng, b),
        in_specs=[seq(0), seq(heads), seq(2 * heads), seq(3 * heads),
                  pl.BlockSpec((None, hpg, dh, dh), lambda h, i: (i, h, 0, 0)),
                  pl.BlockSpec((None, hpg, dh, dh), lambda h, i: (i, ng + h, 0, 0)),
                  pl.BlockSpec(dec.shape, lambda h, i: (0, 0)),
                  pl.BlockSpec((1, w), lambda h, i: (0, h))],
        out_specs=pl.BlockSpec((None, l, w), lambda h, i: (i, 0, h)),
        out_shape=jax.ShapeDtypeStruct((b, l, heads * dh), BF16),
        scratch_shapes=[pltpu.VMEM((hpg, 6, chunk, dh), F32), pltpu.VMEM((hpg, chunk, chunk), F32)],
        compiler_params=_cparams(("arbitrary", "arbitrary")),
        name="retention",
    )(qkvg, qkvg, qkvg, qkvg, s0, s0, dec, gn_gain)


def _filt_time_kernel(emb_ref, w1_ref, b1_ref, w2_ref, b2_ref, w3_ref, b3_ref, w4f_ref, w4b_ref,
                      fr_ref, dl_ref, cast_ref, c_ref, wm_ref, bm_ref,
                      kern_ref, s_ref, cast_out_ref, mod_ref, *, seq_len):
    cast_out_ref[...] = cast_ref[...].astype(cast_out_ref.dtype)
    mod_ref[...] = _dot3(_silu(c_ref[...]), wm_ref[...]) + bm_ref[...]
    i = pl.program_id(0)
    rows, c = kern_ref.shape[1], kern_ref.shape[2]
    fr = fr_ref[...]
    hdot = _dot3
    z = jnp.sin(fr * (hdot(emb_ref[...], w1_ref[...]) + b1_ref[...]))
    z = jnp.sin(fr * (hdot(z, w2_ref[...]) + b2_ref[...]))
    z = jnp.sin(fr * (hdot(z, w3_ref[...]) + b3_ref[...]))
    pos = (i * rows + lax.broadcasted_iota(jnp.int32, (rows, c), 0)).astype(F32)
    inv = 1.0 / (seq_len - 1.0)
    adl = dl_ref[...]
    hf = hdot(z, w4f_ref[...]) * jnp.exp(-(pos * inv) * adl)
    hr = hdot(z, w4b_ref[...]) * jnp.exp(-((seq_len - pos) * inv) * adl)
    hr = jnp.where(pos == 0.0, 0.0, hr)
    kern_ref[0] = hf
    kern_ref[1] = hr
    part = jnp.sum(jnp.abs(hf) + jnp.abs(hr), axis=0, keepdims=True)

    @pl.when(i == 0)
    def _():
        s_ref[...] = part

    @pl.when(i != 0)
    def _():
        s_ref[...] += part


def _filt_time_call(emb2, w1, b1, w2, b2, w3, b3, w4f, w4b, freq, absdelta, seq_len, cast_w,
                    c_rows, w_mod, b_mod):
    c = absdelta.shape[1]
    rows = FILT_ROWS
    steps = seq_len // rows
    assert cast_w.shape[0] % (steps * 2 * SUBLANES) == 0
    slab = pl.BlockSpec((cast_w.shape[0] // steps, cast_w.shape[1]), lambda i: (i, 0))
    mrows, d = c_rows.shape
    n_mod = w_mod.shape[1]
    assert n_mod % (steps * LANES) == 0
    tn = n_mod // steps
    small = lambda a: pl.BlockSpec(a.shape, lambda i: (0,) * a.ndim)
    return pl.pallas_call(
        functools.partial(_filt_time_kernel, seq_len=seq_len),
        grid=(steps,),
        in_specs=[pl.BlockSpec((rows, emb2.shape[1]), lambda i: (i, 0)),
                  small(w1), small(b1), small(w2), small(b2), small(w3), small(b3), small(w4f), small(w4b),
                  small(freq), small(absdelta), slab,
                  small(c_rows),
                  pl.BlockSpec((d, tn), lambda i: (0, i)),
                  pl.BlockSpec((1, tn), lambda i: (0, i))],
        out_specs=[pl.BlockSpec((2, rows, c), lambda i: (0, i, 0)),
                   pl.BlockSpec((1, c), lambda i: (0, 0)), slab,
                   pl.BlockSpec((mrows, tn), lambda i: (0, i))],
        out_shape=[jax.ShapeDtypeStruct((2, seq_len, c), F32),
                   jax.ShapeDtypeStruct((1, c), F32),
                   jax.ShapeDtypeStruct(cast_w.shape, BF16),
                   jax.ShapeDtypeStruct((mrows, n_mod), F32)],
        compiler_params=_cparams(("arbitrary",)),
        name="filt_time",
    )(emb2, w1, b1, w2, b2, w3, b3, w4f, w4b, freq, absdelta, cast_w, c_rows, w_mod, b_mod)


@functools.lru_cache(maxsize=None)
def _fft_tables(seq_len):
    n = 2 * seq_len
    n1s = FFT_N1
    n2s = n // n1s
    half = n1s // 2
    k1n = half + 1
    n1 = np.arange(half)
    k1 = np.arange(k1n)
    th = 2.0 * np.pi * (np.outer(k1, n1) % n1s) / n1s
    herm = np.where((k1 == 0) | (k1 == half), 1.0, 2.0)
    fa = np.concatenate([np.cos(th), -np.sin(th)[1:half]], axis=0)
    sgn = np.concatenate([(-1.0) ** k1, (-1.0) ** k1[1:half]])
    fai = np.concatenate([np.cos(th) * herm[:, None], (-np.sin(th) * herm[:, None])[1:half]], axis=0).T / n
    eye = np.eye(SUBLANES)
    fk = np.kron(fa, eye)
    fks = np.kron(fa * sgn[:, None], eye)
    fki = np.kron(fai, eye)
    k2 = np.arange(n2s)
    n2 = np.arange(n2s)
    m = np.zeros((k1n, 2 * n2s, 2 * n2s))
    for a in range(k1n):
        ang = 2.0 * np.pi * (np.outer(a + n1s * k2, n2) % n) / n
        gr, gi = np.cos(ang), -np.sin(ang)
        m[a] = np.block([[gr, -gi], [gi, gr]])
    groups = np.arange(n2s).reshape(-1, SUBLANES)
    perm = np.concatenate([np.concatenate([g, n2s + g]) for g in groups])
    m = m[:, perm, :]
    mt = np.transpose(m, (0, 2, 1))
    return dict(fk=fk, fkf=np.concatenate([fk, fks], axis=1), fki=fki, m=m, mt=mt,
                k1n=k1n, n2s=n2s, half=half)


def _stage_n1(src_refs, mat, dst_ref, n2s):
    rows = dst_ref.shape[0]
    pair = 2 * SUBLANES
    for jj in range(n2s // pair):
        blks = [r[:, jj * pair:(jj + 1) * pair, :].astype(F32) for r in src_refs]
        for hf in range(2):
            lo = hf * SUBLANES
            parts = [b[:, lo:lo + SUBLANES, :] for b in blks]
            xg = jnp.concatenate([p.reshape(p.shape[0] * SUBLANES, p.shape[2]) for p in parts], axis=0)
            a = jnp.dot(mat, xg.astype(BF16), preferred_element_type=F32)
            sl = slice(jj * pair + lo, jj * pair + lo + SUBLANES)
            dst_ref[:, sl, :] = a.reshape(rows, SUBLANES, a.shape[1])


def _n2_input(a_s, k1, half, n2s):
    if 0 < k1 < half:
        return jnp.concatenate([a_s[k1], a_s[half + k1]], axis=0)
    return a_s[k1]


def _filt_spec_kernel(kern_ref, s_ref, fkf_ref, m_ref, kf_ref, a_s, *, k1n, n2s):
    half = k1n - 1
    _stage_n1([kern_ref.at[0], kern_ref.at[1]], fkf_ref[...], a_s, n2s)
    inv = 1.0 / (s_ref[...] + EPS)
    for k1 in range(k1n):
        a = _n2_input(a_s, k1, half, n2s).astype(BF16)
        kf_ref[k1] = jnp.dot(m_ref[k1, :, :a.shape[0]], a, preferred_element_type=F32) * inv


def _filt_spec_call(kern4, s, tabs):
    _, half, n2s, c = kern4.shape
    ct = HY_CT
    k1n = tabs["k1n"]
    return pl.pallas_call(
        functools.partial(_filt_spec_kernel, k1n=k1n, n2s=n2s),
        grid=(c // ct,),
        in_specs=[pl.BlockSpec((2, half, n2s, ct), lambda j: (0, 0, 0, j)),
                  pl.BlockSpec((1, ct), lambda j: (0, j)),
                  _const_spec(tabs["fkf"].shape), _const_spec(tabs["m"].shape)],
        out_specs=pl.BlockSpec((k1n, 2 * n2s, ct), lambda j: (0, 0, j)),
        out_shape=jax.ShapeDtypeStruct((k1n, 2 * n2s, c), F32),
        scratch_shapes=[pltpu.VMEM((2 * half, n2s, ct), F32)],
        compiler_params=_cparams(("arbitrary",)),
        name="filt_spec",
    )(kern4, s, tabs["fkf"], tabs["m"])


def _hyena_kernel(u_ref, x0_ref, bias_ref, kf_ref, fk_ref, fki_ref, m_ref, mt_ref, *rest, k1n, n2s, n_cast):
    cast_in, (o_ref, *cast_out), (a_s,) = rest[:n_cast], rest[n_cast:2 * n_cast + 1], rest[2 * n_cast + 1:]
    for src, dst in zip(cast_in, cast_out):
        dst[...] = src[...].astype(dst.dtype)
    half = k1n - 1
    ct = u_ref.shape[2]
    _stage_n1([u_ref], fk_ref[...], a_s, n2s)

    for k1 in range(k1n):
        a = _n2_input(a_s, k1, half, n2s).astype(BF16)
        x = jnp.dot(m_ref[k1, :, :a.shape[0]], a, preferred_element_type=F32)
        x = x.reshape(n2s // SUBLANES, 2 * SUBLANES, ct)
        kf = kf_ref[k1].reshape(n2s // SUBLANES, 2 * SUBLANES, ct)
        xr, xi = x[:, :SUBLANES], x[:, SUBLANES:]
        kr, ki = kf[:, :SUBLANES], kf[:, SUBLANES:]
        y = jnp.concatenate([xr * kr - xi * ki, xr * ki + xi * kr], axis=1)
        y = y.reshape(2 * n2s, ct).astype(BF16)
        if 0 < k1 < half:
            z = jnp.dot(mt_ref[k1], y, preferred_element_type=F32)
            a_s[k1] = z[:n2s]
            a_s[half + k1] = z[n2s:]
        else:
            a_s[k1] = jnp.dot(mt_ref[k1, :n2s, :], y, preferred_element_type=F32)

    fki = fki_ref[...]
    bias = bias_ref[...]
    pair = 2 * SUBLANES
    for j in range(n2s // pair):
        ys = []
        for jj in range(2):
            sl = slice(j * pair + jj * SUBLANES, j * pair + (jj + 1) * SUBLANES)
            zg = a_s[:, sl, :].reshape(2 * half * SUBLANES, ct)
            ys.append(jnp.dot(fki, zg.astype(BF16), preferred_element_type=F32).reshape(half, SUBLANES, ct))
        y = jnp.concatenate(ys, axis=1)
        sl = slice(j * pair, (j + 1) * pair)
        o_ref[:, sl, :] = (x0_ref[:, sl, :].astype(F32)
                           * (y + u_ref[:, sl, :].astype(F32) * bias)).astype(o_ref.dtype)


def _hyena_call(u4, x04, bias, kf, tabs, cast_weights):
    b, half, n2s, c = u4.shape
    ct = HY_CT
    k1n = tabs["k1n"]
    nct = c // ct
    steps = nct * b
    seq = pl.BlockSpec((None, half, n2s, ct), lambda j, i: (i, 0, 0, j))
    slabs = [pl.BlockSpec((w.shape[0] // steps, w.shape[1]), lambda j, i: (j * b + i, 0)) for w in cast_weights]
    assert all(w.shape[0] % (steps * 2 * SUBLANES) == 0 for w in cast_weights)
    outs = pl.pallas_call(
        functools.partial(_hyena_kernel, k1n=k1n, n2s=n2s, n_cast=len(cast_weights)),
        grid=(nct, b),
        in_specs=[seq, seq,
                  pl.BlockSpec((1, ct), lambda j, i: (0, j)),
                  pl.BlockSpec((k1n, 2 * n2s, ct), lambda j, i: (0, 0, j)),
                  _const_spec(tabs["fk"].shape), _const_spec(tabs["fki"].shape),
                  _const_spec(tabs["m"].shape), _const_spec(tabs["mt"].shape), *slabs],
        out_specs=[seq, *slabs],
        out_shape=[jax.ShapeDtypeStruct((b, half, n2s, c), BF16),
                   *[jax.ShapeDtypeStruct(w.shape, BF16) for w in cast_weights]],
        scratch_shapes=[pltpu.VMEM((2 * half, n2s, ct), F32)],
        compiler_params=_cparams(("arbitrary", "arbitrary")),
        name="hyena",
    )(u4, x04, bias, kf, tabs["fk"], tabs["fki"], tabs["m"], tabs["mt"], *cast_weights)
    return outs[0], outs[1:]


def _out_ffn_kernel(ret_ref, hy_ref, x_ref, mod_ref, hg_ref, n2_ref, fn_ref, wo_ref, wgu_ref, wd_ref,
                    o_ref, *, d_ret, d_ff, ff_bounds, row_parts):
    rows = x_ref.shape[0] // row_parts
    groups = [slice(r * rows, (r + 1) * rows) for r in range(row_parts)]
    gain2 = n2_ref[...] * (1.0 + mod_ref[4:5, :])
    x1s, h2s = [], []
    for rs in groups:
        hy = hy_ref[rs, :].astype(F32)
        hms = jnp.mean(hy * hy, axis=-1, keepdims=True)
        hyn = (hy * lax.rsqrt(hms + EPS) * hg_ref[...]).astype(BF16)
        mix = jnp.dot(ret_ref[rs, :], wo_ref[:d_ret, :], preferred_element_type=F32)
        mix = mix + jnp.dot(hyn, wo_ref[d_ret:, :], preferred_element_type=F32)
        x1s.append(x_ref[rs, :] + mod_ref[2:3, :] * mix)
    for x1 in x1s:
        ms = jnp.mean(x1 * x1, axis=-1, keepdims=True)
        h2s.append((x1 * lax.rsqrt(ms + EPS) * gain2 + mod_ref[3:4, :]).astype(BF16))
    accs = [None] * row_parts
    for lo, hi in zip(ff_bounds[:-1], ff_bounds[1:]):
        acts = []
        for h2 in h2s:
            g = jnp.dot(h2, wgu_ref[:, lo:hi], preferred_element_type=F32)
            u = jnp.dot(h2, wgu_ref[:, d_ff + lo:d_ff + hi], preferred_element_type=F32)
            acts.append((_silu(g) * u).astype(BF16))
        for r, a in enumerate(acts):
            d = jnp.dot(a, wd_ref[lo:hi, :], preferred_element_type=F32)
            accs[r] = d if accs[r] is None else accs[r] + d
    for rs, x1, acc in zip(groups, x1s, accs):
        x2 = x1 + mod_ref[5:6, :] * acc
        ms2 = jnp.mean(x2 * x2, axis=-1, keepdims=True)
        o_ref[rs, :] = x2 * lax.rsqrt(ms2 + EPS) * fn_ref[...]


def _ff_bounds(d_ff, parts=2):
    tiles = d_ff // MXU_WIDTH
    assert tiles * MXU_WIDTH == d_ff
    cuts = [-(-tiles * p // parts) for p in range(parts + 1)]
    return tuple(c * MXU_WIDTH for c in cuts)


def _out_ffn_call(ret, hy, x, mod, hy_gain, norm2, final_norm, w_out, w_gu, w_down):
    b, l, d = x.shape
    d_ret = ret.shape[2]
    d_hy = hy.shape[2]
    d_ff = w_down.shape[0]
    tm = ROW_TILE
    return pl.pallas_call(
        functools.partial(_out_ffn_kernel, d_ret=d_ret, d_ff=d_ff, ff_bounds=_ff_bounds(d_ff),
                          row_parts=tm // GROUP_ROWS),
        grid=(b, l // tm),
        in_specs=[pl.BlockSpec((None, tm, d_ret), lambda i, t: (i, t, 0)),
                  pl.BlockSpec((None, tm, d_hy), lambda i, t: (i, t, 0)),
                  pl.BlockSpec((None, tm, d), lambda i, t: (i, t, 0)),
                  pl.BlockSpec((None, N_MOD, d), lambda i, t: (i, 0, 0)),
                  pl.BlockSpec((1, d_hy), lambda i, t: (0, 0)),
                  pl.BlockSpec((1, d), lambda i, t: (0, 0)),
                  pl.BlockSpec((1, d), lambda i, t: (0, 0)),
                  _const_spec(w_out.shape), _const_spec(w_gu.shape), _const_spec(w_down.shape)],
        out_specs=pl.BlockSpec((None, tm, d), lambda i, t: (i, t, 0)),
        out_shape=jax.ShapeDtypeStruct((b, l, d), x.dtype),
        compiler_params=_cparams(("arbitrary", "arbitrary")),
        name="out_ffn",
    )(ret, hy, x, mod, hy_gain, norm2, final_norm, w_out, w_gu, w_down)


@functools.lru_cache(maxsize=None)
def _rope_tables(seq_len, dh):
    n = dh // 4
    t = np.arange(seq_len)
    inv = ROPE_BASE ** (-np.arange(n, dtype=np.float64) / n)
    ang = np.concatenate([(t // GRID_W)[:, None] * inv, (t % GRID_W)[:, None] * inv], axis=-1)
    cc = np.concatenate([np.cos(ang), np.cos(ang)], axis=-1)
    ss = np.concatenate([-np.sin(ang), np.sin(ang)], axis=-1)
    k_scale = dh ** -0.5
    return tuple(t.astype(np.float32) for t in (cc, ss, cc * k_scale, ss * k_scale))


@functools.lru_cache(maxsize=None)
def _filter_tables(seq_len, emb_dim, emb_pad, channels):
    t = np.linspace(0.0, 1.0, seq_len)[:, None]
    bands = (emb_dim - 1) // 2
    f = np.linspace(1e-4, bands - 1, bands)[None, :]
    wpos = 2.0 * np.pi * np.arange(seq_len)[:, None] / seq_len
    emb = np.concatenate([t, np.cos(f * wpos), -np.sin(f * wpos)], axis=-1)
    emb = np.pad(emb, ((0, 0), (0, emb_pad - emb_dim)))
    emb_rev = np.concatenate([emb[:1], emb[:0:-1]], axis=0)
    emb2 = np.concatenate([emb, emb_rev], axis=1)
    max_decay = math.log(FILTER_DECAY_TARGET) / FILTER_DECAY_FAST
    min_decay = math.log(FILTER_DECAY_TARGET) / FILTER_DECAY_SLOW
    absdelta = np.abs(np.linspace(min_decay, max_decay, channels))[None, :]
    return emb2.astype(np.float32), absdelta.astype(np.float32)


def _block_diag2(w):
    z = jnp.zeros_like(w)
    return jnp.concatenate([jnp.concatenate([w, z], axis=1), jnp.concatenate([z, w], axis=1)], axis=0)


def kernel(x, c, ctx, c_ctx, w_mod, b_mod, norm1, norm2, w_in, ret_decay, ret_gn_gain, hy_short_w,
           hy_short_b, hy_w1, hy_b1, hy_w2, hy_b2, hy_w3, hy_b3, hy_w4, hy_freq, hy_bias, hy_out_norm,
           w_out, w_gate_up, w_down, final_norm):
    b, seq_len, d = x.shape
    assert w_mod.shape[0] == 1, "single-layer block"
    heads = RET_HEADS
    d_ret = ret_gn_gain.shape[1]
    dh = d_ret // heads
    d_hy = hy_bias.shape[1]
    assert dh == LANES and d_hy % HY_CT == 0 and seq_len % (FFT_N1 // 2) == 0

    rows = -(-(b + 1) // SUBLANES) * SUBLANES
    c_rows = jnp.zeros((rows, d), F32).at[:b].set(c).at[b].set(c_ctx)

    tabs = dict(_fft_tables(seq_len))
    for name in ("fk", "fkf", "fki", "m", "mt"):
        tabs[name] = jnp.asarray(tabs[name], dtype=F32).astype(BF16)
    half, n2s = tabs["half"], tabs["n2s"]
    emb_dim = hy_w1.shape[1]
    emb_pad = -(-emb_dim // SUBLANES) * SUBLANES
    emb2, absdelta = (jnp.asarray(t) for t in _filter_tables(seq_len, emb_dim, emb_pad, d_hy))
    w1p = jnp.pad(hy_w1[0], ((0, emb_pad - emb_dim), (0, 0)))
    two = lambda a: jnp.concatenate([a, a], axis=1)
    w4 = hy_w4[0]
    zero4 = jnp.zeros((w4.shape[0], d_hy), F32)
    kern, ksum, w_in_b, mod_all = _filt_time_call(
        emb2, _block_diag2(w1p), two(hy_b1), _block_diag2(hy_w2[0]), two(hy_b2), _block_diag2(hy_w3[0]),
        two(hy_b3), jnp.concatenate([w4[:, :d_hy], zero4], axis=0), jnp.concatenate([zero4, w4[:, d_hy:]], axis=0),
        two(hy_freq), absdelta, seq_len, w_in[0], c_rows, w_mod[0], b_mod[0][None, :])
    mod = mod_all[:b].reshape(b, N_MOD, d)
    mod_c = mod_all[b].reshape(N_MOD, d)
    kf = _filt_spec_call(kern.reshape(2, half, n2s, d_hy), ksum, tabs)

    dec = jnp.broadcast_to(ret_decay[0].reshape(2 * heads, 1), (2 * heads, LANES))
    n1g = norm1[0][None, :]
    s0 = _ctx_call(ctx, n1g, mod_c[0:1], mod_c[1:2], w_in_b, dec, heads, dh)
    rope = [jnp.asarray(t) for t in _rope_tables(seq_len, dh)]
    qkvg, x0c, u = _inproj_call(x, mod, n1g, w_in_b, rope, hy_short_w[0], hy_short_b, heads, dh, d_hy)
    ret = _ret_call(qkvg, s0, dec, ret_gn_gain, heads, dh)

    hy, (w_out_b, w_gu_b, w_down_b) = _hyena_call(
        u.reshape(b, half, n2s, d_hy), x0c.reshape(b, half, n2s, d_hy), hy_bias, kf, tabs,
        [w_out[0], w_gate_up[0], w_down[0]])
    hy = hy.reshape(b, seq_len, d_hy)

    return _out_ffn_call(ret, hy, x, mod, hy_out_norm, norm2[0][None, :], final_norm[None, :],
                         w_out_b, w_gu_b, w_down_b)
```

```python
import functools
import math

import jax
import jax.numpy as jnp
import numpy as np
from jax import lax
from jax.experimental import pallas as pl
from jax.experimental.pallas import tpu as pltpu

F32 = jnp.float32
BF16 = jnp.bfloat16

RET_HEADS = 4
GRID_W = 64
ROPE_BASE = 10000.0
N_MOD = 6
HYENA_PROJ = 3
FILTER_DECAY_FAST = 0.3
FILTER_DECAY_SLOW = 1.5
FILTER_DECAY_TARGET = 1e-2
EPS = 1e-6

LANES = 128
SUBLANES = 8
MXU_WIDTH = 256
VMEM_LIMIT_BYTES = 56 * 1024 * 1024

RET_CHUNK = 256
ROW_TILE = 1024
IN_ROW_TILE = 1024
GROUP_ROWS = 256
CTX_BATCHES = 8
HALO = 16
HY_CT = 256
FILT_ROWS = 1024
FFT_N1 = 64


def _silu(v):
    return v / (1.0 + jnp.exp(-v))


def _dot3(a, b):
    a_hi = a.astype(BF16)
    b_hi = b.astype(BF16)
    a_lo = (a - a_hi.astype(F32)).astype(BF16)
    b_lo = (b - b_hi.astype(F32)).astype(BF16)
    dot = functools.partial(jnp.dot, preferred_element_type=F32)
    return dot(a_hi, b_hi) + (dot(a_hi, b_lo) + dot(a_lo, b_hi))


def _cparams(sem, vmem=VMEM_LIMIT_BYTES):
    return pltpu.CompilerParams(dimension_semantics=sem, vmem_limit_bytes=vmem)


def _const_spec(shape):
    nd = len(shape)
    return pl.BlockSpec(shape, lambda *_: (0,) * nd, pipeline_mode=pl.Buffered(1))


def _ctx_kernel(ctx_ref, n1_ref, sh_ref, sc_ref, wk_ref, wv_ref, dec_ref, s_ref, *, heads, dh):
    nb, n_ctx, d = ctx_ref.shape
    xc = ctx_ref[...].reshape(nb * n_ctx, d)
    ms = jnp.mean(xc * xc, axis=-1, keepdims=True)
    hc = ((xc * lax.rsqrt(ms + EPS) * n1_ref[...]) * (1.0 + sc_ref[...]) + sh_ref[...]).astype(BF16)
    k = jnp.dot(hc, wk_ref[...], preferred_element_type=F32)
    v = jnp.dot(hc, wv_ref[...], preferred_element_type=F32)
    lg = jnp.log1p(-jnp.exp(dec_ref[...]))
    pos = lax.broadcasted_iota(jnp.int32, (n_ctx, dh), 0).astype(F32)
    k_scale = dh ** -0.5
    tdims = (((0,), (0,)), ((), ()))
    for h in range(heads):
        wf = jnp.exp(lg[h:h + 1, :] * (n_ctx - 1.0 - pos)) * k_scale
        wb = jnp.exp(lg[heads + h:heads + h + 1, :] * pos) * k_scale
        for i in range(nb):
            rows = slice(i * n_ctx, (i + 1) * n_ctx)
            kh = k[rows, h * dh:(h + 1) * dh]
            vh = v[rows, h * dh:(h + 1) * dh].astype(BF16)
            kw = jnp.concatenate([(kh * wf).astype(BF16), (kh * wb).astype(BF16)], axis=1)
            st = lax.dot_general(kw, vh, tdims, preferred_element_type=F32)
            s_ref[i, h] = st[:dh]
            s_ref[i, heads + h] = st[dh:]


def _ctx_call(ctx, norm1, shift_c, scale_c, w_in, dec, heads, dh):
    b, n_ctx, d = ctx.shape
    d_ret = heads * dh
    nb = CTX_BATCHES
    assert b % nb == 0
    return pl.pallas_call(
        functools.partial(_ctx_kernel, heads=heads, dh=dh),
        grid=(b // nb,),
        in_specs=[pl.BlockSpec((nb, n_ctx, d), lambda i: (i, 0, 0)),
                  pl.BlockSpec((1, d), lambda i: (0, 0)),
                  pl.BlockSpec((1, d), lambda i: (0, 0)),
                  pl.BlockSpec((1, d), lambda i: (0, 0)),
                  pl.BlockSpec((d, d_ret), lambda i: (0, 1)),
                  pl.BlockSpec((d, d_ret), lambda i: (0, 2)),
                  pl.BlockSpec(dec.shape, lambda i: (0, 0))],
        out_specs=pl.BlockSpec((nb, 2 * heads, dh, dh), lambda i: (i, 0, 0, 0)),
        out_shape=jax.ShapeDtypeStruct((b, 2 * heads, dh, dh), F32),
        compiler_params=_cparams(("arbitrary",)),
        name="ctx_state",
    )(ctx, norm1, shift_c, scale_c, w_in, w_in, dec)


def _inproj_kernel(x_ref, xp_ref, xn_ref, mod_ref, n1_ref, w_ref, cq_ref, sq_ref, ck_ref, sk_ref,
                   sw_ref, sb_ref, qkvg_ref, x0_ref, u_ref, *, heads, dh, d_hy, nt):
    t = pl.program_id(1)
    tm = x_ref.shape[0]
    d_ret = heads * dh
    gain = n1_ref[...] * (1.0 + mod_ref[1:2, :])

    def normed(x):
        ms = jnp.mean(x * x, axis=-1, keepdims=True)
        return x * lax.rsqrt(ms + EPS) * gain + mod_ref[0:1, :]

    groups = [slice(r, r + GROUP_ROWS) for r in range(0, tm, GROUP_ROWS)]
    hs = [normed(x_ref[rs, :]) for rs in groups]
    hbs = [h.astype(BF16) for h in hs]
    hy_lhs = list(hbs)
    hy_lhs[0] = jnp.concatenate([normed(xp_ref[...]), hs[0]], axis=0).astype(BF16)
    hy_lhs[-1] = jnp.concatenate([hs[-1], normed(xn_ref[...])], axis=0).astype(BF16)

    def proj(lhs, base, width):
        return [jnp.dot(hb, w_ref[:, base:base + width], preferred_element_type=F32) for hb in lhs]

    def hyena_cols(j, lo, hi):
        sl = slice(j * d_hy + lo, j * d_hy + hi)
        parts = proj(hy_lhs, 4 * d_ret + j * d_hy + lo, hi - lo)
        head, tail = parts[0], parts[-1]
        parts[0] = jnp.concatenate([jnp.where(t == 0, 0.0, head[:HALO]), head[HALO:]], axis=0)
        parts[-1] = jnp.concatenate([tail[:GROUP_ROWS], jnp.where(t == nt - 1, 0.0, tail[GROUP_ROWS:])],
                                    axis=0)
        p = jnp.concatenate(parts, axis=0)
        n = p.shape[0]
        y = (sw_ref[0:1, sl] * pltpu.roll(p, 1, 0) + sw_ref[1:2, sl] * p
             + sw_ref[2:3, sl] * pltpu.roll(p, n - 1, 0) + sb_ref[:, sl])
        return y[HALO:HALO + tm]

    def roped_cols(base, cos_ref, sin_ref):
        for rs, p in zip(groups, proj(hbs, base, d_ret)):
            cc = cos_ref[rs, :]
            ss = sin_ref[rs, :]
            for j in range(heads):
                pj = p[:, j * dh:(j + 1) * dh]
                qkvg_ref[rs, base + j * dh:base + (j + 1) * dh] = (
                    pj * cc + pltpu.roll(pj, dh // 2, 1) * ss).astype(BF16)

    half_hy = d_hy // 2
    roped_cols(0, cq_ref, sq_ref)
    x0_ref[:, :half_hy] = hyena_cols(0, 0, half_hy).astype(x0_ref.dtype)
    roped_cols(d_ret, ck_ref, sk_ref)
    x0_ref[:, half_hy:] = hyena_cols(0, half_hy, d_hy).astype(x0_ref.dtype)
    u_ref[:, :half_hy] = (hyena_cols(1, 0, half_hy) * hyena_cols(2, 0, half_hy)).astype(u_ref.dtype)
    for rs, g in zip(groups, proj(hbs, 3 * d_ret, d_ret)):
        qkvg_ref[rs, 3 * d_ret:4 * d_ret] = _silu(g).astype(BF16)
    u_ref[:, half_hy:] = (hyena_cols(1, half_hy, d_hy) * hyena_cols(2, half_hy, d_hy)).astype(u_ref.dtype)
    for rs, v in zip(groups, proj(hbs, 2 * d_ret, d_ret)):
        qkvg_ref[rs, 2 * d_ret:3 * d_ret] = v.astype(BF16)


def _inproj_call(x, mod, norm1, w_in, rope, short_w, short_b, heads, dh, d_hy):
    b, l, d = x.shape
    tm = IN_ROW_TILE
    nt = l // tm
    d_ret = heads * dh
    per_tile = tm // HALO
    tile = lambda i, t: (i, t, 0)
    return pl.pallas_call(
        functools.partial(_inproj_kernel, heads=heads, dh=dh, d_hy=d_hy, nt=nt),
        grid=(b, nt),
        in_specs=[pl.BlockSpec((None, tm, d), tile),
                  pl.BlockSpec((None, HALO, d), lambda i, t: (i, jnp.maximum(t * per_tile - 1, 0), 0)),
                  pl.BlockSpec((None, HALO, d),
                               lambda i, t: (i, jnp.minimum((t + 1) * per_tile, nt * per_tile - 1), 0)),
                  pl.BlockSpec((None, N_MOD, d), lambda i, t: (i, 0, 0)),
                  pl.BlockSpec((1, d), lambda i, t: (0, 0)),
                  _const_spec(w_in.shape),
                  *[pl.BlockSpec((tm, dh), lambda i, t: (t, 0)) for _ in rope],
                  pl.BlockSpec(short_w.shape, lambda i, t: (0, 0)),
                  pl.BlockSpec(short_b.shape, lambda i, t: (0, 0))],
        out_specs=[pl.BlockSpec((None, tm, 4 * d_ret), tile),
                   pl.BlockSpec((None, tm, d_hy), tile),
                   pl.BlockSpec((None, tm, d_hy), tile)],
        out_shape=[jax.ShapeDtypeStruct((b, l, 4 * d_ret), BF16),
                   jax.ShapeDtypeStruct((b, l, d_hy), BF16),
                   jax.ShapeDtypeStruct((b, l, d_hy), BF16)],
        compiler_params=_cparams(("arbitrary", "arbitrary")),
        name="in_proj",
    )(x, x, x, mod, norm1, w_in, *rope, short_w, short_b)


def _ret_kernel(q_ref, k_ref, v_ref, g_ref, sf0_ref, sb0_ref, dec_ref, gain_ref,
                o_ref, tab_s, dmask_s, *, heads, chunk):
    l, dh = q_ref.shape
    nc = l // chunk
    h = pl.program_id(0)

    @pl.when(pl.program_id(1) == 0)
    def _():
        lgf = jnp.log1p(-jnp.exp(dec_ref[pl.ds(h, 1), :]))
        lgb = jnp.log1p(-jnp.exp(dec_ref[pl.ds(h + heads, 1), :]))
        il = lax.broadcasted_iota(jnp.int32, (chunk, dh), 0).astype(F32)
        tab_s[0] = jnp.exp(lgf * (chunk - 1.0 - il))
        tab_s[1] = jnp.exp(lgb * il)
        tab_s[2] = jnp.exp(lgf * (il + 1.0))
        tab_s[3] = jnp.exp(lgb * (chunk - il))
        tab_s[4] = jnp.exp(jnp.broadcast_to(lgf, (chunk, dh)) * float(chunk))
        tab_s[5] = jnp.exp(jnp.broadcast_to(lgb, (chunk, dh)) * float(chunk))
        reps = chunk // dh
        lgf_c = jnp.concatenate([lgf] * reps, axis=1)
        lgb_c = jnp.concatenate([lgb] * reps, axis=1)
        ii = lax.broadcasted_iota(jnp.int32, (chunk, chunk), 0)
        jj = lax.broadcasted_iota(jnp.int32, (chunk, chunk), 1)
        diff = (ii - jj).astype(F32)
        dmask_s[...] = (jnp.where(diff >= 0, jnp.exp(lgf_c * jnp.maximum(diff, 0.0)), 0.0)
                        + jnp.where(diff <= 0, jnp.exp(lgb_c * jnp.maximum(-diff, 0.0)), 0.0))

    tdims = (((0,), (0,)), ((), ()))
    ntdims = (((1,), (1,)), ((), ()))
    rows = lambda n: pl.ds(n * chunk, chunk)

    kvf, kvb = [], []
    for n in range(nc):
        kr = k_ref[rows(n), :].astype(F32)
        vv = v_ref[rows(n), :]
        kvf.append(lax.dot_general((kr * tab_s[0]).astype(BF16), vv, tdims, preferred_element_type=F32))
        kvb.append(lax.dot_general((kr * tab_s[1]).astype(BF16), vv, tdims, preferred_element_type=F32))

    cd_f = tab_s[4, :dh, :]
    cd_b = tab_s[5, :dh, :]
    sf, sb = [None] * nc, [None] * nc
    s = sf0_ref[...]
    for n in range(nc):
        sf[n] = s.astype(BF16)
        s = s * cd_f + kvf[n]
    s = sb0_ref[...]
    for n in reversed(range(nc)):
        sb[n] = s.astype(BF16)
        s = s * cd_b + kvb[n]

    gain = gain_ref[...]
    for n in range(nc):
        qb = q_ref[rows(n), :]
        qr = qb.astype(F32)
        sc = lax.dot_general(qb, k_ref[rows(n), :], ntdims, preferred_element_type=F32)
        lhs = jnp.concatenate([(qr * tab_s[2]).astype(BF16), (qr * tab_s[3]).astype(BF16),
                               (sc * dmask_s[...]).astype(BF16)], axis=1)
        rhs = jnp.concatenate([sf[n], sb[n], v_ref[rows(n), :]], axis=0)
        o = jnp.dot(lhs, rhs, preferred_element_type=F32)
        mu = jnp.mean(o, axis=-1, keepdims=True)
        d = o - mu
        var = jnp.mean(d * d, axis=-1, keepdims=True)
        gg = g_ref[rows(n), :].astype(F32)
        o_ref[rows(n), :] = (d * lax.rsqrt(var + EPS) * gain * gg).astype(o_ref.dtype)


def _ret_call(qkvg, s0, dec, gn_gain, heads, dh):
    b, l, _ = qkvg.shape
    chunk = RET_CHUNK
    seq = lambda off: pl.BlockSpec((None, l, dh), lambda h, i: (i, 0, off + h))
    return pl.pallas_call(
        functools.partial(_ret_kernel, heads=heads, chunk=chunk),
        grid=(heads, b),
        in_specs=[seq(0), seq(heads), seq(2 * heads), seq(3 * heads),
                  pl.BlockSpec((None, None, dh, dh), lambda h, i: (i, h, 0, 0)),
                  pl.BlockSpec((None, None, dh, dh), lambda h, i: (i, heads + h, 0, 0)),
                  pl.BlockSpec(dec.shape, lambda h, i: (0, 0)),
                  pl.BlockSpec((1, dh), lambda h, i: (0, h))],
        out_specs=pl.BlockSpec((None, l, dh), lambda h, i: (i, 0, h)),
        out_shape=jax.ShapeDtypeStruct((b, l, heads * dh), BF16),
        scratch_shapes=[pltpu.VMEM((6, chunk, dh), F32), pltpu.VMEM((chunk, chunk), F32)],
        compiler_params=_cparams(("arbitrary", "arbitrary")),
        name="retention",
    )(qkvg, qkvg, qkvg, qkvg, s0, s0, dec, gn_gain)


def _filt_time_kernel(emb_ref, w1_ref, b1_ref, w2_ref, b2_ref, w3_ref, b3_ref, w4f_ref, w4b_ref,
                      fr_ref, dl_ref, cast_ref, c_ref, wm_ref, bm_ref,
                      kern_ref, s_ref, cast_out_ref, mod_ref, *, seq_len):
    cast_out_ref[...] = cast_ref[...].astype(cast_out_ref.dtype)
    mod_ref[...] = _dot3(_silu(c_ref[...]), wm_ref[...]) + bm_ref[...]
    i = pl.program_id(0)
    rows, c = kern_ref.shape[1], kern_ref.shape[2]
    fr = fr_ref[...]
    hdot = _dot3
    z = jnp.sin(fr * (hdot(emb_ref[...], w1_ref[...]) + b1_ref[...]))
    z = jnp.sin(fr * (hdot(z, w2_ref[...]) + b2_ref[...]))
    z = jnp.sin(fr * (hdot(z, w3_ref[...]) + b3_ref[...]))
    pos = (i * rows + lax.broadcasted_iota(jnp.int32, (rows, c), 0)).astype(F32)
    inv = 1.0 / (seq_len - 1.0)
    adl = dl_ref[...]
    hf = hdot(z, w4f_ref[...]) * jnp.exp(-(pos * inv) * adl)
    hr = hdot(z, w4b_ref[...]) * jnp.exp(-((seq_len - pos) * inv) * adl)
    hr = jnp.where(pos == 0.0, 0.0, hr)
    kern_ref[0] = hf
    kern_ref[1] = hr
    part = jnp.sum(jnp.abs(hf) + jnp.abs(hr), axis=0, keepdims=True)

    @pl.when(i == 0)
    def _():
        s_ref[...] = part

    @pl.when(i != 0)
    def _():
        s_ref[...] += part


def _filt_time_call(emb2, w1, b1, w2, b2, w3, b3, w4f, w4b, freq, absdelta, seq_len, cast_w,
                    c_rows, w_mod, b_mod):
    c = absdelta.shape[1]
    rows = FILT_ROWS
    steps = seq_len // rows
    assert cast_w.shape[0] % (steps * 2 * SUBLANES) == 0
    slab = pl.BlockSpec((cast_w.shape[0] // steps, cast_w.shape[1]), lambda i: (i, 0))
    mrows, d = c_rows.shape
    n_mod = w_mod.shape[1]
    assert n_mod % (steps * LANES) == 0
    tn = n_mod // steps
    small = lambda a: pl.BlockSpec(a.shape, lambda i: (0,) * a.ndim)
    return pl.pallas_call(
        functools.partial(_filt_time_kernel, seq_len=seq_len),
        grid=(steps,),
        in_specs=[pl.BlockSpec((rows, emb2.shape[1]), lambda i: (i, 0)),
                  small(w1), small(b1), small(w2), small(b2), small(w3), small(b3), small(w4f), small(w4b),
                  small(freq), small(absdelta), slab,
                  small(c_rows),
                  pl.BlockSpec((d, tn), lambda i: (0, i)),
                  pl.BlockSpec((1, tn), lambda i: (0, i))],
        out_specs=[pl.BlockSpec((2, rows, c), lambda i: (0, i, 0)),
                   pl.BlockSpec((1, c), lambda i: (0, 0)), slab,
                   pl.BlockSpec((mrows, tn), lambda i: (0, i))],
        out_shape=[jax.ShapeDtypeStruct((2, seq_len, c), F32),
                   jax.ShapeDtypeStruct((1, c), F32),
                   jax.ShapeDtypeStruct(cast_w.shape, BF16),
                   jax.ShapeDtypeStruct((mrows, n_mod), F32)],
        compiler_params=_cparams(("arbitrary",)),
        name="filt_time",
    )(emb2, w1, b1, w2, b2, w3, b3, w4f, w4b, freq, absdelta, cast_w, c_rows, w_mod, b_mod)


@functools.lru_cache(maxsize=None)
def _fft_tables(seq_len):
    n = 2 * seq_len
    n1s = FFT_N1
    n2s = n // n1s
    half = n1s // 2
    k1n = half + 1
    n1 = np.arange(half)
    k1 = np.arange(k1n)
    th = 2.0 * np.pi * (np.outer(k1, n1) % n1s) / n1s
    herm = np.where((k1 == 0) | (k1 == half), 1.0, 2.0)
    fa = np.concatenate([np.cos(th), -np.sin(th)[1:half]], axis=0)
    sgn = np.concatenate([(-1.0) ** k1, (-1.0) ** k1[1:half]])
    fai = np.concatenate([np.cos(th) * herm[:, None], (-np.sin(th) * herm[:, None])[1:half]], axis=0).T / n
    eye = np.eye(SUBLANES)
    fk = np.kron(fa, eye)
    fks = np.kron(fa * sgn[:, None], eye)
    fki = np.kron(fai, eye)
    k2 = np.arange(n2s)
    n2 = np.arange(n2s)
    m = np.zeros((k1n, 2 * n2s, 2 * n2s))
    for a in range(k1n):
        ang = 2.0 * np.pi * (np.outer(a + n1s * k2, n2) % n) / n
        gr, gi = np.cos(ang), -np.sin(ang)
        m[a] = np.block([[gr, -gi], [gi, gr]])
    groups = np.arange(n2s).reshape(-1, SUBLANES)
    perm = np.concatenate([np.concatenate([g, n2s + g]) for g in groups])
    m = m[:, perm, :]
    mt = np.transpose(m, (0, 2, 1))
    return dict(fk=fk, fkf=np.concatenate([fk, fks], axis=1), fki=fki, m=m, mt=mt,
                k1n=k1n, n2s=n2s, half=half)


def _stage_n1(src_refs, mat, dst_ref, n2s):
    rows = dst_ref.shape[0]
    pair = 2 * SUBLANES
    for jj in range(n2s // pair):
        blks = [r[:, jj * pair:(jj + 1) * pair, :].astype(F32) for r in src_refs]
        for hf in range(2):
            lo = hf * SUBLANES
            parts = [b[:, lo:lo + SUBLANES, :] for b in blks]
            xg = jnp.concatenate([p.reshape(p.shape[0] * SUBLANES, p.shape[2]) for p in parts], axis=0)
            a = jnp.dot(mat, xg.astype(BF16), preferred_element_type=F32)
            sl = slice(jj * pair + lo, jj * pair + lo + SUBLANES)
            dst_ref[:, sl, :] = a.reshape(rows, SUBLANES, a.shape[1])


def _n2_input(a_s, k1, half, n2s):
    if 0 < k1 < half:
        return jnp.concatenate([a_s[k1], a_s[half + k1]], axis=0)
    return a_s[k1]


def _filt_spec_kernel(kern_ref, s_ref, fkf_ref, m_ref, kf_ref, a_s, *, k1n, n2s):
    half = k1n - 1
    _stage_n1([kern_ref.at[0], kern_ref.at[1]], fkf_ref[...], a_s, n2s)
    inv = 1.0 / (s_ref[...] + EPS)
    for k1 in range(k1n):
        a = _n2_input(a_s, k1, half, n2s).astype(BF16)
        kf_ref[k1] = jnp.dot(m_ref[k1, :, :a.shape[0]], a, preferred_element_type=F32) * inv


def _filt_spec_call(kern4, s, tabs):
    _, half, n2s, c = kern4.shape
    ct = HY_CT
    k1n = tabs["k1n"]
    return pl.pallas_call(
        functools.partial(_filt_spec_kernel, k1n=k1n, n2s=n2s),
        grid=(c // ct,),
        in_specs=[pl.BlockSpec((2, half, n2s, ct), lambda j: (0, 0, 0, j)),
                  pl.BlockSpec((1, ct), lambda j: (0, j)),
                  _const_spec(tabs["fkf"].shape), _const_spec(tabs["m"].shape)],
        out_specs=pl.BlockSpec((k1n, 2 * n2s, ct), lambda j: (0, 0, j)),
        out_shape=jax.ShapeDtypeStruct((k1n, 2 * n2s, c), F32),
        scratch_shapes=[pltpu.VMEM((2 * half, n2s, ct), F32)],
        compiler_params=_cparams(("arbitrary",)),
        name="filt_spec",
    )(kern4, s, tabs["fkf"], tabs["m"])


def _hyena_kernel(u_ref, x0_ref, bias_ref, kf_ref, fk_ref, fki_ref, m_ref, mt_ref, *rest, k1n, n2s, n_cast):
    cast_in, (o_ref, *cast_out), (a_s,) = rest[:n_cast], rest[n_cast:2 * n_cast + 1], rest[2 * n_cast + 1:]
    for src, dst in zip(cast_in, cast_out):
        dst[...] = src[...].astype(dst.dtype)
    half = k1n - 1
    ct = u_ref.shape[2]
    _stage_n1([u_ref], fk_ref[...], a_s, n2s)

    for k1 in range(k1n):
        a = _n2_input(a_s, k1, half, n2s).astype(BF16)
        x = jnp.dot(m_ref[k1, :, :a.shape[0]], a, preferred_element_type=F32)
        x = x.reshape(n2s // SUBLANES, 2 * SUBLANES, ct)
        kf = kf_ref[k1].reshape(n2s // SUBLANES, 2 * SUBLANES, ct)
        xr, xi = x[:, :SUBLANES], x[:, SUBLANES:]
        kr, ki = kf[:, :SUBLANES], kf[:, SUBLANES:]
        y = jnp.concatenate([xr * kr - xi * ki, xr * ki + xi * kr], axis=1)
        y = y.reshape(2 * n2s, ct).astype(BF16)
        if 0 < k1 < half:
            z = jnp.dot(mt_ref[k1], y, preferred_element_type=F32)
            a_s[k1] = z[:n2s]
            a_s[half + k1] = z[n2s:]
        else:
            a_s[k1] = jnp.dot(mt_ref[k1, :n2s, :], y, preferred_element_type=F32)

    fki = fki_ref[...]
    bias = bias_ref[...]
    pair = 2 * SUBLANES
    for j in range(n2s // pair):
        ys = []
        for jj in range(2):
            sl = slice(j * pair + jj * SUBLANES, j * pair + (jj + 1) * SUBLANES)
            zg = a_s[:, sl, :].reshape(2 * half * SUBLANES, ct)
            ys.append(jnp.dot(fki, zg.astype(BF16), preferred_element_type=F32).reshape(half, SUBLANES, ct))
        y = jnp.concatenate(ys, axis=1)
        sl = slice(j * pair, (j + 1) * pair)
        o_ref[:, sl, :] = (x0_ref[:, sl, :].astype(F32)
                           * (y + u_ref[:, sl, :].astype(F32) * bias)).astype(o_ref.dtype)


def _hyena_call(u4, x04, bias, kf, tabs, cast_weights):
    b, half, n2s, c = u4.shape
    ct = HY_CT
    k1n = tabs["k1n"]
    nct = c // ct
    steps = nct * b
    seq = pl.BlockSpec((None, half, n2s, ct), lambda j, i: (i, 0, 0, j))
    slabs = [pl.BlockSpec((w.shape[0] // steps, w.shape[1]), lambda j, i: (j * b + i, 0)) for w in cast_weights]
    assert all(w.shape[0] % (steps * 2 * SUBLANES) == 0 for w in cast_weights)
    outs = pl.pallas_call(
        functools.partial(_hyena_kernel, k1n=k1n, n2s=n2s, n_cast=len(cast_weights)),
        grid=(nct, b),
        in_specs=[seq, seq,
                  pl.BlockSpec((1, ct), lambda j, i: (0, j)),
                  pl.BlockSpec((k1n, 2 * n2s, ct), lambda j, i: (0, 0, j)),
                  _const_spec(tabs["fk"].shape), _const_spec(tabs["fki"].shape),
                  _const_spec(tabs["m"].shape), _const_spec(tabs["mt"].shape), *slabs],
        out_specs=[seq, *slabs],
        out_shape=[jax.ShapeDtypeStruct((b, half, n2s, c), BF16),
                   *[jax.ShapeDtypeStruct(w.shape, BF16) for w in cast_weights]],
        scratch_shapes=[pltpu.VMEM((2 * half, n2s, ct), F32)],
        compiler_params=_cparams(("arbitrary", "arbitrary")),
        name="hyena",
    )(u4, x04, bias, kf, tabs["fk"], tabs["fki"], tabs["m"], tabs["mt"], *cast_weights)
    return outs[0], outs[1:]


def _out_ffn_kernel(ret_ref, hy_ref, x_ref, mod_ref, hg_ref, n2_ref, fn_ref, wo_ref, wgu_ref, wd_ref,
                    o_ref, *, d_ret, d_ff, ff_bounds, row_parts):
    rows = x_ref.shape[0] // row_parts
    groups = [slice(r * rows, (r + 1) * rows) for r in range(row_parts)]
    gain2 = n2_ref[...] * (1.0 + mod_ref[4:5, :])
    x1s, h2s = [], []
    for rs in groups:
        hy = hy_ref[rs, :].astype(F32)
        hms = jnp.mean(hy * hy, axis=-1, keepdims=True)
        hyn = (hy * lax.rsqrt(hms + EPS) * hg_ref[...]).astype(BF16)
        mix = jnp.dot(ret_ref[rs, :], wo_ref[:d_ret, :], preferred_element_type=F32)
        mix = mix + jnp.dot(hyn, wo_ref[d_ret:, :], preferred_element_type=F32)
        x1s.append(x_ref[rs, :] + mod_ref[2:3, :] * mix)
    for x1 in x1s:
        ms = jnp.mean(x1 * x1, axis=-1, keepdims=True)
        h2s.append((x1 * lax.rsqrt(ms + EPS) * gain2 + mod_ref[3:4, :]).astype(BF16))
    accs = [None] * row_parts
    for lo, hi in zip(ff_bounds[:-1], ff_bounds[1:]):
        acts = []
        for h2 in h2s:
            g = jnp.dot(h2, wgu_ref[:, lo:hi], preferred_element_type=F32)
            u = jnp.dot(h2, wgu_ref[:, d_ff + lo:d_ff + hi], preferred_element_type=F32)
            acts.append((_silu(g) * u).astype(BF16))
        for r, a in enumerate(acts):
            d = jnp.dot(a, wd_ref[lo:hi, :], preferred_element_type=F32)
            accs[r] = d if accs[r] is None else accs[r] + d
    for rs, x1, acc in zip(groups, x1s, accs):
        x2 = x1 + mod_ref[5:6, :] * acc
        ms2 = jnp.mean(x2 * x2, axis=-1, keepdims=True)
        o_ref[rs, :] = x2 * lax.rsqrt(ms2 + EPS) * fn_ref[...]


def _ff_bounds(d_ff, parts=2):
    tiles = d_ff // MXU_WIDTH
    assert tiles * MXU_WIDTH == d_ff
    cuts = [-(-tiles * p // parts) for p in range(parts + 1)]
    return tuple(c * MXU_WIDTH for c in cuts)


def _out_ffn_call(ret, hy, x, mod, hy_gain, norm2, final_norm, w_out, w_gu, w_down):
    b, l, d = x.shape
    d_ret = ret.shape[2]
    d_hy = hy.shape[2]
    d_ff = w_down.shape[0]
    tm = ROW_TILE
    return pl.pallas_call(
        functools.partial(_out_ffn_kernel, d_ret=d_ret, d_ff=d_ff, ff_bounds=_ff_bounds(d_ff),
                          row_parts=tm // GROUP_ROWS),
        grid=(b, l // tm),
        in_specs=[pl.BlockSpec((None, tm, d_ret), lambda i, t: (i, t, 0)),
                  pl.BlockSpec((None, tm, d_hy), lambda i, t: (i, t, 0)),
                  pl.BlockSpec((None, tm, d), lambda i, t: (i, t, 0)),
                  pl.BlockSpec((None, N_MOD, d), lambda i, t: (i, 0, 0)),
                  pl.BlockSpec((1, d_hy), lambda i, t: (0, 0)),
                  pl.BlockSpec((1, d), lambda i, t: (0, 0)),
                  pl.BlockSpec((1, d), lambda i, t: (0, 0)),
                  _const_spec(w_out.shape), _const_spec(w_gu.shape), _const_spec(w_down.shape)],
        out_specs=pl.BlockSpec((None, tm, d), lambda i, t: (i, t, 0)),
        out_shape=jax.ShapeDtypeStruct((b, l, d), x.dtype),
        compiler_params=_cparams(("arbitrary", "arbitrary")),
        name="out_ffn",
    )(ret, hy, x, mod, hy_gain, norm2, final_norm, w_out, w_gu, w_down)


@functools.lru_cache(maxsize=None)
def _rope_tables(seq_len, dh):
    n = dh // 4
    t = np.arange(seq_len)
    inv = ROPE_BASE ** (-np.arange(n, dtype=np.float64) / n)
    ang = np.concatenate([(t // GRID_W)[:, None] * inv, (t % GRID_W)[:, None] * inv], axis=-1)
    cc = np.concatenate([np.cos(ang), np.cos(ang)], axis=-1)
    ss = np.concatenate([-np.sin(ang), np.sin(ang)], axis=-1)
    k_scale = dh ** -0.5
    return tuple(t.astype(np.float32) for t in (cc, ss, cc * k_scale, ss * k_scale))


@functools.lru_cache(maxsize=None)
def _filter_tables(seq_len, emb_dim, emb_pad, channels):
    t = np.linspace(0.0, 1.0, seq_len)[:, None]
    bands = (emb_dim - 1) // 2
    f = np.linspace(1e-4, bands - 1, bands)[None, :]
    wpos = 2.0 * np.pi * np.arange(seq_len)[:, None] / seq_len
    emb = np.concatenate([t, np.cos(f * wpos), -np.sin(f * wpos)], axis=-1)
    emb = np.pad(emb, ((0, 0), (0, emb_pad - emb_dim)))
    emb_rev = np.concatenate([emb[:1], emb[:0:-1]], axis=0)
    emb2 = np.concatenate([emb, emb_rev], axis=1)
    max_decay = math.log(FILTER_DECAY_TARGET) / FILTER_DECAY_FAST
    min_decay = math.log(FILTER_DECAY_TARGET) / FILTER_DECAY_SLOW
    absdelta = np.abs(np.linspace(min_decay, max_decay, channels))[None, :]
    return emb2.astype(np.float32), absdelta.astype(np.float32)


def _block_diag2(w):
    z = jnp.zeros_like(w)
    return jnp.concatenate([jnp.concatenate([w, z], axis=1), jnp.concatenate([z, w], axis=1)], axis=0)


def kernel(x, c, ctx, c_ctx, w_mod, b_mod, norm1, norm2, w_in, ret_decay, ret_gn_gain, hy_short_w,
           hy_short_b, hy_w1, hy_b1, hy_w2, hy_b2, hy_w3, hy_b3, hy_w4, hy_freq, hy_bias, hy_out_norm,
           w_out, w_gate_up, w_down, final_norm):
    b, seq_len, d = x.shape
    assert w_mod.shape[0] == 1, "single-layer block"
    heads = RET_HEADS
    d_ret = ret_gn_gain.shape[1]
    dh = d_ret // heads
    d_hy = hy_bias.shape[1]
    assert dh == LANES and d_hy % HY_CT == 0 and seq_len % (FFT_N1 // 2) == 0

    rows = -(-(b + 1) // SUBLANES) * SUBLANES
    c_rows = jnp.zeros((rows, d), F32).at[:b].set(c).at[b].set(c_ctx)

    tabs = dict(_fft_tables(seq_len))
    for name in ("fk", "fkf", "fki", "m", "mt"):
        tabs[name] = jnp.asarray(tabs[name], dtype=F32).astype(BF16)
    half, n2s = tabs["half"], tabs["n2s"]
    emb_dim = hy_w1.shape[1]
    emb_pad = -(-emb_dim // SUBLANES) * SUBLANES
    emb2, absdelta = (jnp.asarray(t) for t in _filter_tables(seq_len, emb_dim, emb_pad, d_hy))
    w1p = jnp.pad(hy_w1[0], ((0, emb_pad - emb_dim), (0, 0)))
    two = lambda a: jnp.concatenate([a, a], axis=1)
    w4 = hy_w4[0]
    zero4 = jnp.zeros((w4.shape[0], d_hy), F32)
    kern, ksum, w_in_b, mod_all = _filt_time_call(
        emb2, _block_diag2(w1p), two(hy_b1), _block_diag2(hy_w2[0]), two(hy_b2), _block_diag2(hy_w3[0]),
        two(hy_b3), jnp.concatenate([w4[:, :d_hy], zero4], axis=0), jnp.concatenate([zero4, w4[:, d_hy:]], axis=0),
        two(hy_freq), absdelta, seq_len, w_in[0], c_rows, w_mod[0], b_mod[0][None, :])
    mod = mod_all[:b].reshape(b, N_MOD, d)
    mod_c = mod_all[b].reshape(N_MOD, d)
    kf = _filt_spec_call(kern.reshape(2, half, n2s, d_hy), ksum, tabs)

    dec = jnp.broadcast_to(ret_decay[0].reshape(2 * heads, 1), (2 * heads, LANES))
    n1g = norm1[0][None, :]
    s0 = _ctx_call(ctx, n1g, mod_c[0:1], mod_c[1:2], w_in_b, dec, heads, dh)
    rope = [jnp.asarray(t) for t in _rope_tables(seq_len, dh)]
    qkvg, x0c, u = _inproj_call(x, mod, n1g, w_in_b, rope, hy_short_w[0], hy_short_b, heads, dh, d_hy)
    ret = _ret_call(qkvg, s0, dec, ret_gn_gain, heads, dh)

    hy, (w_out_b, w_gu_b, w_down_b) = _hyena_call(
        u.reshape(b, half, n2s, d_hy), x0c.reshape(b, half, n2s, d_hy), hy_bias, kf, tabs,
        [w_out[0], w_gate_up[0], w_down[0]])
    hy = hy.reshape(b, seq_len, d_hy)

    return _out_ffn_call(ret, hy, x, mod, hy_out_norm, norm2[0][None, :], final_norm[None, :],
                         w_out_b, w_gu_b, w_down_b)
```
